```python
import math
import jax, jax.numpy as jnp
from jax import lax
import numpy as np

D_MODEL = 1024
BATCH = 4
SEQ = 4096
DEPTH = 2

N_MIXERS = 2
N_HEADS = 16
HEAD_DIM = D_MODEL // N_HEADS
Q_BLOCK = 128
SSM_GROUP = 16
N_GROUPS = D_MODEL // SSM_GROUP
SSM_STATE = 64
D_FF = 4 * D_MODEL
DT_MIN = 0.001
DT_MAX = 0.1
EPS = 1e-6
N_ATTN_LAYERS = (DEPTH + 1) // 2
N_SSM_LAYERS = DEPTH // 2

kernel_name = "fox_s5_interleaved_hybrid"


def rms_norm(x, g):
    xf = x.astype(jnp.float32)
    y = xf * lax.rsqrt(jnp.mean(xf * xf, axis=-1, keepdims=True) + EPS)
    return (y * g.astype(jnp.float32)).astype(x.dtype)


def fox_attention(h, w_in, b_f, q_g, k_g, w_out):
    B, S, D = h.shape
    proj = h @ w_in
    q, k, v, f = jnp.split(proj, [D, 2 * D, 3 * D], axis=-1)
    q = rms_norm(q.reshape(B, S, N_HEADS, HEAD_DIM), q_g)
    k = rms_norm(k.reshape(B, S, N_HEADS, HEAD_DIM), k_g)
    v = v.reshape(B, S, N_HEADS, HEAD_DIM)
    log_f = jax.nn.log_sigmoid((f + b_f).astype(jnp.float32))
    c = jnp.cumsum(log_f, axis=1).transpose(0, 2, 1)
    q = q.transpose(0, 2, 1, 3)
    k = k.transpose(0, 2, 1, 3)
    v = v.transpose(0, 2, 1, 3)
    nb = S // Q_BLOCK
    qb = q.reshape(B, N_HEADS, nb, Q_BLOCK, HEAD_DIM).transpose(2, 0, 1, 3, 4)
    cb = c.reshape(B, N_HEADS, nb, Q_BLOCK).transpose(2, 0, 1, 3)
    k_pos = jnp.arange(S)
    scale = 1.0 / math.sqrt(HEAD_DIM)

    def block(args):
        q_i, c_i, i = args
        q_pos = i * Q_BLOCK + jnp.arange(Q_BLOCK)
        s = jnp.einsum('bhqd,bhkd->bhqk', q_i, k).astype(jnp.float32) * scale
        s = s + c_i[..., :, None] - c[:, :, None, :]
        s = jnp.where(k_pos[None, :] <= q_pos[:, None], s, -jnp.inf)
        p = jax.nn.softmax(s, axis=-1)
        return jnp.einsum('bhqk,bhkd->bhqd', p.astype(v.dtype), v)

    o = lax.map(block, (qb, cb, jnp.arange(nb)))
    o = o.transpose(1, 0, 3, 2, 4).reshape(B, S, D)
    return o @ w_out


def _scan_combine(e1, e2):
    a1r, a1i, b1r, b1i = e1
    a2r, a2i, b2r, b2i = e2
    ar = a1r * a2r - a1i * a2i
    ai = a1r * a2i + a1i * a2r
    br = a2r * b1r - a2i * b1i + b2r
    bi = a2r * b1i + a2i * b1r + b2i
    return (ar, ai, br, bi)


def s5_layer(h, w_in, a_re, a_im, b_re, b_im, c_re, c_im, log_dt, d_skip, w_glu):
    B, S, D = h.shape
    f32 = jnp.float32
    u = (h @ w_in).astype(f32)
    ug = u.reshape(B, S, N_GROUPS, SSM_GROUP)
    a_re = a_re.astype(f32); a_im = a_im.astype(f32)
    dt = jnp.exp(log_dt.astype(f32))[:, None]
    mag = jnp.exp(dt * a_re)
    ab_re = mag * jnp.cos(dt * a_im)
    ab_im = mag * jnp.sin(dt * a_im)
    num_re = ab_re - 1.0
    num_im = ab_im
    den = a_re * a_re + a_im * a_im
    s_re = (num_re * a_re + num_im * a_im) / den
    s_im = (num_im * a_re - num_re * a_im) / den
    b_re = b_re.astype(f32); b_im = b_im.astype(f32)
    bb_re = s_re[..., None] * b_re - s_im[..., None] * b_im
    bb_im = s_re[..., None] * b_im + s_im[..., None] * b_re
    bu_re = jnp.einsum('bsgc,gpc->bsgp', ug, bb_re)
    bu_im = jnp.einsum('bsgc,gpc->bsgp', ug, bb_im)
    ar = jnp.broadcast_to(ab_re, bu_re.shape)
    ai = jnp.broadcast_to(ab_im, bu_re.shape)
    _, _, xr, xi = lax.associative_scan(_scan_combine, (ar, ai, bu_re, bu_im), axis=1)
    y = (jnp.einsum('bsgp,gcp->bsgc', xr, c_re.astype(f32))
         - jnp.einsum('bsgp,gcp->bsgc', xi, c_im.astype(f32)))
    y = y.reshape(B, S, D) + d_skip.astype(f32) * u
    z = jax.nn.gelu(y).astype(h.dtype)
    val, gate = jnp.split(z @ w_glu, 2, axis=-1)
    return val * jax.nn.sigmoid(gate)


def sqrelu_mlp(h, w1, w2):
    return jnp.square(jax.nn.relu(h @ w1)) @ w2


def setup_inputs(seed: int = 0) -> dict:
    key = jax.random.key(seed)
    ks = jax.random.split(key, 20)
    D, H, G, P, C = D_MODEL, N_HEADS, N_GROUPS, SSM_STATE, SSM_GROUP
    nA, nS = N_ATTN_LAYERS, N_SSM_LAYERS
    nrm = jax.random.normal
    x = nrm(ks[0], (BATCH, SEQ, D), jnp.float32)
    norm_mix_g = 1.0 + 0.02 * nrm(ks[1], (DEPTH, D), jnp.float32)
    norm_mlp_g = 1.0 + 0.02 * nrm(ks[2], (DEPTH, D), jnp.float32)
    attn_w_in = nrm(ks[3], (nA, D, 3 * D + H), jnp.float32) * D ** -0.5
    attn_b_f = jax.random.uniform(ks[4], (nA, H), jnp.float32, 1.0, 6.0)
    attn_q_g = 1.0 + 0.02 * nrm(ks[5], (nA, HEAD_DIM), jnp.float32)
    attn_k_g = 1.0 + 0.02 * nrm(ks[6], (nA, HEAD_DIM), jnp.float32)
    attn_w_out = nrm(ks[7], (nA, D, D), jnp.float32) * D ** -0.5
    ssm_w_in = nrm(ks[8], (nS, D, D), jnp.float32) * D ** -0.5
    ssm_a_re = -0.5 * (1.0 + 0.02 * nrm(ks[9], (nS, G, P), jnp.float32))
    ssm_a_im = jnp.broadcast_to(jnp.pi * jnp.arange(P, dtype=jnp.float32), (nS, G, P))
    ssm_b_re = nrm(ks[10], (nS, G, P, C), jnp.float32) * (2.0 * C) ** -0.5
    ssm_b_im = nrm(ks[11], (nS, G, P, C), jnp.float32) * (2.0 * C) ** -0.5
    ssm_c_re = nrm(ks[12], (nS, G, C, P), jnp.float32) * (2.0 * P) ** -0.5
    ssm_c_im = nrm(ks[13], (nS, G, C, P), jnp.float32) * (2.0 * P) ** -0.5
    ssm_log_dt = jax.random.uniform(ks[14], (nS, G), jnp.float32,
                                    math.log(DT_MIN), math.log(DT_MAX))
    ssm_d = nrm(ks[15], (nS, D), jnp.float32)
    ssm_w_glu = nrm(ks[16], (nS, D, 2 * D), jnp.float32) * D ** -0.5
    mlp_w1 = nrm(ks[17], (DEPTH, D, D_FF), jnp.float32) * D ** -0.5
    mlp_w2 = nrm(ks[18], (DEPTH, D_FF, D), jnp.float32) * D_FF ** -0.5
    return {"x": x, "norm_mix_g": norm_mix_g, "norm_mlp_g": norm_mlp_g,
            "attn_w_in": attn_w_in, "attn_b_f": attn_b_f, "attn_q_g": attn_q_g,
            "attn_k_g": attn_k_g, "attn_w_out": attn_w_out,
            "ssm_w_in": ssm_w_in, "ssm_a_re": ssm_a_re, "ssm_a_im": ssm_a_im,
            "ssm_b_re": ssm_b_re, "ssm_b_im": ssm_b_im, "ssm_c_re": ssm_c_re,
            "ssm_c_im": ssm_c_im, "ssm_log_dt": ssm_log_dt, "ssm_d": ssm_d,
            "ssm_w_glu": ssm_w_glu, "mlp_w1": mlp_w1, "mlp_w2": mlp_w2}


def reference(x, norm_mix_g, norm_mlp_g, attn_w_in, attn_b_f, attn_q_g, attn_k_g,
              attn_w_out, ssm_w_in, ssm_a_re, ssm_a_im, ssm_b_re, ssm_b_im,
              ssm_c_re, ssm_c_im, ssm_log_dt, ssm_d, ssm_w_glu, mlp_w1, mlp_w2):
    for i in range(DEPTH):
        h = rms_norm(x, norm_mix_g[i])
        j = i // N_MIXERS
        if i % N_MIXERS == 0:
            mix = fox_attention(h, attn_w_in[j], attn_b_f[j], attn_q_g[j],
                                attn_k_g[j], attn_w_out[j])
        else:
            mix = s5_layer(h, ssm_w_in[j], ssm_a_re[j], ssm_a_im[j], ssm_b_re[j],
                           ssm_b_im[j], ssm_c_re[j], ssm_c_im[j], ssm_log_dt[j],
                           ssm_d[j], ssm_w_glu[j])
        x = x + mix
        h = rms_norm(x, norm_mlp_g[i])
        x = x + sqrelu_mlp(h, mlp_w1[i], mlp_w2[i])
    return x
```

```python
import functools
import math

import jax
import jax.numpy as jnp
from jax import lax
from jax.experimental import pallas as pl
from jax.experimental.pallas import tpu as pltpu

F32 = jnp.float32
BF16 = jnp.bfloat16

N_HEADS = 16
HEAD_DIM = 64
SSM_GROUP = 16
SSM_STATE = 64
SSM_CHUNK = 16
EPS = 1e-6
LOG2E = 1.4426950408889634

LANES = 128
MXU_DIM = 256
HEADS_PER_STEP = LANES // HEAD_DIM
BIAS_PIECES = 3
VMEM_LIMIT = 56 * 1024 * 1024

NT_DIMS = (((1,), (1,)), ((), ()))


def _rms_norm(x, g):
    ms = jnp.mean(x * x, axis=-1, keepdims=True)
    return x * lax.rsqrt(ms + EPS) * g


def _const_spec(shape):
    zeros = (0,) * len(shape)
    return pl.BlockSpec(shape, lambda *_: zeros)


def _qkv_kernel(x_ref, g_ref, wqkf_ref, wvt_ref, bd_ref, gain_ref, bf_ref, tri_ref, place_ref,
                q_ref, k_ref, vt_ref, kb_ref, carry_ref, *, tm, tk, d):
    @pl.when(pl.program_id(1) == 0)
    def _():
        carry_ref[...] = jnp.zeros_like(carry_ref)

    h = _rms_norm(x_ref[0], g_ref[...]).astype(BF16)
    y = jnp.dot(h, wqkf_ref[...], preferred_element_type=F32)
    n_tiles = 2 * d // MXU_DIM
    for t in range(n_tiles):
        sl = slice(MXU_DIM * t, MXU_DIM * (t + 1))
        tile = y[:, sl]
        ss = jnp.dot((tile * tile).astype(BF16), bd_ref[...], preferred_element_type=F32)
        r = lax.rsqrt(ss * (1.0 / HEAD_DIM) + EPS)
        res = (tile * r * gain_ref[:, sl]).astype(BF16)
        if t < n_tiles // 2:
            q_ref[0, :, sl] = res
        else:
            k_ref[0, :, MXU_DIM * t - d:MXU_DIM * (t + 1) - d] = res

    f = y[:, 2 * d:] + bf_ref[...]
    log_f = jnp.minimum(f, 0.0) - jnp.log1p(jnp.exp(-jnp.abs(f)))
    cs = jnp.dot(tri_ref[...], log_f, precision=lax.Precision.HIGHEST,
                 preferred_element_type=F32) + carry_ref[...]
    carry_ref[...] = cs[tm - 1:tm, :]
    lane = lax.broadcasted_iota(jnp.int32, cs.shape, 1)
    rem = jnp.where(lane < N_HEADS, cs * (-LOG2E), 0.0)
    pieces = jnp.zeros_like(rem)
    for i in range(BIAS_PIECES):
        piece = rem.astype(BF16).astype(F32)
        rem = rem - piece
        pieces = pieces + (piece if i == 0 else pltpu.roll(piece, N_HEADS * i, axis=1))
    kb_ref[0] = jnp.dot(pieces.astype(BF16), place_ref[...],
                        preferred_element_type=F32).astype(BF16)

    vt = lax.dot_general(wvt_ref[...], h, NT_DIMS, preferred_element_type=F32)
    for j in range(tm // tk):
        vt_ref[0, j] = vt[:, tk * j:tk * (j + 1)].astype(BF16)


def _qkv_proj(x, g, wqkf, wvt, bd, gain, bfp, tri, place, *, tm, tk):
    b, s, d = x.shape
    nf = wqkf.shape[1]
    row = lambda bi, si: (bi, si, 0)
    return pl.pallas_call(
        functools.partial(_qkv_kernel, tm=tm, tk=tk, d=d),
        grid=(b, s // tm),
        in_specs=[pl.BlockSpec((1, tm, d), row),
                  _const_spec((1, d)), _const_spec((d, nf)), _const_spec((d, d)),
                  _const_spec((MXU_DIM, MXU_DIM)), _const_spec((1, 2 * d)),
                  _const_spec((1, LANES)), _const_spec((tm, tm)), _const_spec((LANES, d))],
        out_specs=[pl.BlockSpec((1, tm, d), row),
                   pl.BlockSpec((1, tm, d), row),
                   pl.BlockSpec((1, tm // tk, d, tk), lambda bi, si: (bi, si, 0, 0)),
                   pl.BlockSpec((1, tm, d), row)],
        out_shape=[jax.ShapeDtypeStruct((b, s, d), BF16),
                   jax.ShapeDtypeStruct((b, s, d), BF16),
                   jax.ShapeDtypeStruct((b, s // tk, d, tk), BF16),
                   jax.ShapeDtypeStruct((b, s, d), BF16)],
        scratch_shapes=[pltpu.VMEM((1, LANES), F32)],
        compiler_params=pltpu.CompilerParams(
            dimension_semantics=("arbitrary", "arbitrary"), vmem_limit_bytes=VMEM_LIMIT),
        name="qkv_proj",
    )(x, g, wqkf, wvt, bd, gain, bfp, tri, place)


def _attn_kernel(q_ref, k_ref, kb_ref, vt_ref, o_ref, *, tq):
    i = pl.program_id(2)
    q = q_ref[0]
    lane = lax.broadcasted_iota(jnp.int32, (tq, LANES), 1)
    key_idx = lax.broadcasted_iota(jnp.int32, (tq, tq), 0)
    qry_idx = lax.broadcasted_iota(jnp.int32, (tq, tq), 1)
    outs = []
    for hh in range(HEADS_PER_STEP):
        q_h = jnp.where((lane >= HEAD_DIM * hh) & (lane < HEAD_DIM * (hh + 1)), q, jnp.zeros_like(q))
        ones_h = jnp.where((lane >= BIAS_PIECES * hh) & (lane < BIAS_PIECES * (hh + 1)),
                           1.0, 0.0).astype(BF16)
        qa = jnp.concatenate([q_h, ones_h], axis=1)

        def block(j, carry, masked):
            m, l, acc = carry
            off = pl.multiple_of(j * tq, tq)
            ka = jnp.concatenate([k_ref[0, pl.ds(off, tq), :],
                                  kb_ref[0, pl.ds(off, tq), :]], axis=1)
            s = lax.dot_general(ka, qa, NT_DIMS, preferred_element_type=F32)
            if masked:
                s = jnp.where(key_idx <= qry_idx, s, -1e30)
            m_new = jnp.maximum(m, jnp.max(s, axis=0, keepdims=True))
            alpha = jnp.exp2(m - m_new)
            p = jnp.exp2(s - m_new)
            l = alpha * l + jnp.sum(p, axis=0, keepdims=True)
            vt = vt_ref[0, j, HEAD_DIM * hh:HEAD_DIM * (hh + 1), :]
            acc = alpha * acc + jnp.dot(vt, p.astype(BF16), preferred_element_type=F32)
            return m_new, l, acc

        init = (jnp.full((1, tq), -1e30, F32), jnp.zeros((1, tq), F32),
                jnp.zeros((HEAD_DIM, tq), F32))
        carry = lax.fori_loop(0, i, functools.partial(block, masked=False), init)
        _, l, acc = block(i, carry, True)
        outs.append(acc / l)
    o_t = jnp.concatenate(outs, axis=0)
    o_ref[0] = o_t.T.astype(BF16)


def _attention(q, k, kb, vt, *, tq):
    b, s, d = q.shape
    n_pairs = d // LANES
    nk = s // tq
    return pl.pallas_call(
        functools.partial(_attn_kernel, tq=tq),
        grid=(b, n_pairs, s // tq),
        in_specs=[pl.BlockSpec((1, tq, LANES), lambda bi, hp, i: (bi, i, hp)),
                  pl.BlockSpec((1, s, LANES), lambda bi, hp, i: (bi, 0, hp)),
                  pl.BlockSpec((1, s, LANES), lambda bi, hp, i: (bi, 0, hp)),
                  pl.BlockSpec((1, nk, LANES, tq), lambda bi, hp, i: (bi, 0, hp, 0))],
        out_specs=pl.BlockSpec((1, tq, LANES), lambda bi, hp, i: (bi, i, hp)),
        out_shape=jax.ShapeDtypeStruct((b, s, d), BF16),
        compiler_params=pltpu.CompilerParams(
            dimension_semantics=("arbitrary", "arbitrary", "arbitrary"),
            vmem_limit_bytes=VMEM_LIMIT),
        name="fox_attention",
    )(q, k, kb, vt)


def _mix_mlp_kernel(x_ref, a_ref, wmix_ref, g_ref, w1_ref, w2_ref, o_ref, *, glu, d, ff_chunk):
    mix = jnp.dot(a_ref[...], wmix_ref[...], preferred_element_type=F32)
    if glu:
        mix = mix[:, :d] * jax.nn.sigmoid(mix[:, d:])
    x1 = x_ref[...] + mix
    h = _rms_norm(x1, g_ref[...]).astype(BF16)
    acc = x1
    for c in range(w1_ref.shape[1] // ff_chunk):
        sl = slice(ff_chunk * c, ff_chunk * (c + 1))
        hid = jnp.maximum(jnp.dot(h, w1_ref[:, sl], preferred_element_type=F32), 0.0)
        acc = acc + jnp.dot((hid * hid).astype(BF16), w2_ref[sl, :], preferred_element_type=F32)
    o_ref[...] = acc


def _mix_mlp(x, a, wmix, g, w1, w2, *, glu, tm, ff_chunk=1024):
    t, d = x.shape
    row = lambda i: (i, 0)
    single = pl.Buffered(1)
    wspec = lambda shape: pl.BlockSpec(shape, lambda i: (0, 0), pipeline_mode=single)
    return pl.pallas_call(
        functools.partial(_mix_mlp_kernel, glu=glu, d=d, ff_chunk=ff_chunk),
        grid=(t // tm,),
        in_specs=[pl.BlockSpec((tm, d), row), pl.BlockSpec((tm, d), row),
                  wspec(wmix.shape), wspec((1, d)), wspec(w1.shape), wspec(w2.shape)],
        out_specs=pl.BlockSpec((tm, d), row),
        out_shape=jax.ShapeDtypeStruct((t, d), F32),
        compiler_params=pltpu.CompilerParams(
            dimension_semantics=("arbitrary",), vmem_limit_bytes=VMEM_LIMIT),
        name="mix_glu_mlp" if glu else "mix_mlp",
    )(x, a, wmix, g, w1, w2)


def _norm_proj_kernel(x_ref, g_ref, w_ref, o_ref):
    h = _rms_norm(x_ref[...], g_ref[...]).astype(BF16)
    o_ref[...] = jnp.dot(h, w_ref[...], preferred_element_type=F32).astype(o_ref.dtype)


def _norm_proj(x, g, w, *, tm):
    t, d = x.shape
    n = w.shape[1]
    return pl.pallas_call(
        _norm_proj_kernel,
        grid=(t // tm,),
        in_specs=[pl.BlockSpec((tm, d), lambda i: (i, 0)), _const_spec((1, d)),
                  _const_spec((d, n))],
        out_specs=pl.BlockSpec((tm, n), lambda i: (i, 0)),
        out_shape=jax.ShapeDtypeStruct((t, n), BF16),
        compiler_params=pltpu.CompilerParams(
            dimension_semantics=("arbitrary",), vmem_limit_bytes=VMEM_LIMIT),
        name="norm_proj",
    )(x, g, w)


def _ssm_kernel(u_ref, mi_ref, ws_ref, wo_ref, a1_ref, a2_ref, dv_ref, z_ref, *, n_chunks):
    u = u_ref[0]
    rows = u.shape[0]
    y = jnp.dot(u, mi_ref[0], preferred_element_type=F32)
    xs = jnp.dot(u, ws_ref[0], preferred_element_type=F32)
    chunk = lax.broadcasted_iota(jnp.int32, (rows, LANES), 0) % n_chunks
    half = SSM_STATE
    step = 1
    j = 0
    while step < n_chunks:
        sh = jnp.where(chunk >= step, pltpu.roll(xs, step, axis=0), 0.0)
        xs = xs + sh * a1_ref[0, j:j + 1, :] + pltpu.roll(sh, half, axis=1) * a2_ref[0, j:j + 1, :]
        step *= 2
        j += 1
    x_in = jnp.where(chunk >= 1, pltpu.roll(xs, 1, axis=0), 0.0)
    y = y + jnp.dot(x_in.astype(BF16), wo_ref[0], preferred_element_type=F32)
    y = y + u.astype(F32) * dv_ref[0]
    z_ref[0] = jax.nn.gelu(y).astype(BF16)


def _ssm_scan(ug, mi, ws, wo, a1, a2, dv, *, n_chunks):
    g, rows, w = ug.shape
    grp = lambda gi: (gi, 0, 0)
    return pl.pallas_call(
        functools.partial(_ssm_kernel, n_chunks=n_chunks),
        grid=(g,),
        in_specs=[pl.BlockSpec((1, rows, w), grp), pl.BlockSpec((1, w, w), grp),
                  pl.BlockSpec((1, w, LANES), grp), pl.BlockSpec((1, LANES, w), grp),
                  pl.BlockSpec((1,) + a1.shape[1:], grp), pl.BlockSpec((1,) + a2.shape[1:], grp),
                  pl.BlockSpec((1, 1, w), grp)],
        out_specs=pl.BlockSpec((1, rows, w), grp),
        out_shape=jax.ShapeDtypeStruct((g, rows, w), BF16),
        compiler_params=pltpu.CompilerParams(
            dimension_semantics=("arbitrary",), vmem_limit_bytes=VMEM_LIMIT),
        name="s5_scan",
    )(ug, mi, ws, wo, a1, a2, dv)


def _ssm_operators(a_re, a_im, b_re, b_im, c_re, c_im, log_dt, d_skip, n_chunks):
    hi = lax.Precision.HIGHEST
    L = SSM_CHUNK
    g, p = a_re.shape
    dt = jnp.exp(log_dt)[:, None]
    lam_re, lam_im = dt * a_re, dt * a_im

    def powers(ks):
        ks = jnp.asarray(ks, F32)[:, None, None]
        mag = jnp.exp(ks * lam_re)
        return mag * jnp.cos(ks * lam_im), mag * jnp.sin(ks * lam_im)

    pr, pi = powers(range(L + 1))
    num_re, num_im = pr[1] - 1.0, pi[1]
    den = a_re * a_re + a_im * a_im
    s_re = (num_re * a_re + num_im * a_im) / den
    s_im = (num_im * a_re - num_re * a_im) / den
    bb_re = s_re[..., None] * b_re - s_im[..., None] * b_im
    bb_im = s_re[..., None] * b_im + s_im[..., None] * b_re
    ca_re = c_re[None] * pr[:, :, None, :] - c_im[None] * pi[:, :, None, :]
    ca_im = c_re[None] * pi[:, :, None, :] + c_im[None] * pr[:, :, None, :]
    kern = (jnp.einsum('kgop,gpi->kgio', ca_re[:L], bb_re, precision=hi)
            - jnp.einsum('kgop,gpi->kgio', ca_im[:L], bb_im, precision=hi))
    s_idx = jnp.arange(L)[:, None]
    t_idx = jnp.arange(L)[None, :]
    lag = t_idx - s_idx
    m_intra = jnp.where((lag >= 0)[:, :, None, None, None], kern[jnp.clip(lag, 0, L - 1)], 0.0)
    m_intra = m_intra.transpose(2, 0, 3, 1, 4).reshape(g, L * SSM_GROUP, L * SSM_GROUP)
    pr_s, pi_s = pr[L - 1::-1][:L], pi[L - 1::-1][:L]
    ws_re = pr_s[..., None] * bb_re[None] - pi_s[..., None] * bb_im[None]
    ws_im = pr_s[..., None] * bb_im[None] + pi_s[..., None] * bb_re[None]
    w_state = jnp.concatenate([ws_re, ws_im], axis=2)
    w_state = w_state.transpose(1, 0, 3, 2).reshape(g, L * SSM_GROUP, 2 * p)
    w_out = jnp.concatenate([ca_re[1:], -ca_im[1:]], axis=3)
    w_out = w_out.transpose(1, 3, 0, 2).reshape(g, 2 * p, L * SSM_GROUP)
    n_steps = max(1, int(math.log2(n_chunks)))
    sr, si = powers([L * 2 ** j for j in range(n_steps)])
    a1 = jnp.concatenate([sr, sr], axis=2).transpose(1, 0, 2)
    a2 = jnp.concatenate([-si, si], axis=2).transpose(1, 0, 2)
    dv = jnp.tile(d_skip.reshape(g, 1, SSM_GROUP), (1, L, 1)).reshape(g, 1, L * SSM_GROUP)
    return (m_intra.astype(BF16), w_state.astype(BF16), w_out.astype(BF16),
            a1.astype(F32), a2.astype(F32), dv.astype(F32))


def kernel(x, norm_mix_g, norm_mlp_g, attn_w_in, attn_b_f, attn_q_g, attn_k_g, attn_w_out,
           ssm_w_in, ssm_a_re, ssm_a_im, ssm_b_re, ssm_b_im, ssm_c_re, ssm_c_im, ssm_log_dt,
           ssm_d, ssm_w_glu, mlp_w1, mlp_w2):
    b, s, d = x.shape
    t = b * s
    tq = 256
    tm_proj = min(512, s)
    tm_mlp = min(512, t)

    w_in = attn_w_in[0]
    w_f = jnp.pad(w_in[:, 3 * d:], ((0, 0), (0, LANES - N_HEADS)))
    wqkf = jnp.concatenate([w_in[:, :2 * d], w_f], axis=1).astype(BF16)
    wvt = w_in[:, 2 * d:3 * d].T.astype(BF16)
    blk = jnp.arange(MXU_DIM) // HEAD_DIM
    bd = (blk[:, None] == blk[None, :]).astype(BF16)
    gain = jnp.concatenate([jnp.tile(attn_q_g[0], N_HEADS) * (LOG2E / math.sqrt(HEAD_DIM)),
                            jnp.tile(attn_k_g[0], N_HEADS)]).reshape(1, 2 * d)
    bfp = jnp.pad(attn_b_f[0], (0, LANES - N_HEADS)).reshape(1, LANES)
    tri = (jnp.arange(tm_proj)[:, None] >= jnp.arange(tm_proj)[None, :]).astype(F32)
    heads = jnp.arange(N_HEADS)
    place = jnp.zeros((LANES, d), F32)
    for i in range(BIAS_PIECES):
        place = place.at[N_HEADS * i + heads,
                         (heads // HEADS_PER_STEP) * LANES
                         + BIAS_PIECES * (heads % HEADS_PER_STEP) + i].set(1.0)
    q, k, vt, kb = _qkv_proj(x, norm_mix_g[0].reshape(1, d), wqkf, wvt, bd, gain, bfp, tri,
                             place.astype(BF16), tm=tm_proj, tk=tq)
    o = _attention(q, k, kb, vt, tq=tq)
    x2 = _mix_mlp(x.reshape(t, d), o.reshape(t, d), attn_w_out[0].astype(BF16),
                  norm_mlp_g[0].reshape(1, d), mlp_w1[0].astype(BF16), mlp_w2[0].astype(BF16),
                  glu=False, tm=tm_mlp)

    u = _norm_proj(x2, norm_mix_g[1].reshape(1, d), ssm_w_in[0].astype(BF16), tm=tm_mlp)
    n_groups = d // SSM_GROUP
    n_chunks = s // SSM_CHUNK
    ug = u.reshape(b * n_chunks, SSM_CHUNK, n_groups, SSM_GROUP).transpose(2, 0, 1, 3)
    ug = ug.reshape(n_groups, b * n_chunks, SSM_CHUNK * SSM_GROUP)
    ops = _ssm_operators(ssm_a_re[0], ssm_a_im[0], ssm_b_re[0], ssm_b_im[0], ssm_c_re[0],
                         ssm_c_im[0], ssm_log_dt[0], ssm_d[0], n_chunks)
    zg = _ssm_scan(ug, *ops, n_chunks=n_chunks)
    z = zg.reshape(n_groups, b * n_chunks, SSM_CHUNK, SSM_GROUP).transpose(1, 2, 0, 3)
    z = z.reshape(t, d)
    x3 = _mix_mlp(x2, z, ssm_w_glu[0].astype(BF16), norm_mlp_g[1].reshape(1, d),
                  mlp_w1[1].astype(BF16), mlp_w2[1].astype(BF16), glu=True, tm=tm_mlp)
    return x3.reshape(b, s, d)
```

```python
import functools
import math

import jax
import jax.numpy as jnp
from jax import lax
from jax.experimental import pallas as pl
from jax.experimental.pallas import tpu as pltpu

F32 = jnp.float32
BF16 = jnp.bfloat16

N_HEADS = 16
HEAD_DIM = 64
SSM_GROUP = 16
SSM_STATE = 64
SSM_CHUNK = 16
EPS = 1e-6
LOG2E = 1.4426950408889634

LANES = 128
MXU_DIM = 256
HEADS_PER_STEP = LANES // HEAD_DIM
BIAS_PIECES = 3
VMEM_LIMIT = 56 * 1024 * 1024

NT_DIMS = (((1,), (1,)), ((), ()))


def _rms_norm(x, g):
    ms = jnp.mean(x * x, axis=-1, keepdims=True)
    return x * lax.rsqrt(ms + EPS) * g


def _const_spec(shape):
    zeros = (0,) * len(shape)
    return pl.BlockSpec(shape, lambda *_: zeros)


def _qkv_kernel(x_ref, g_ref, wqkf_ref, wvt_ref, bd_ref, gain_ref, bf_ref, tri_ref, place_ref,
                q_ref, k_ref, vt_ref, kb_ref, carry_ref, *, tm, tk, d):
    @pl.when(pl.program_id(1) == 0)
    def _():
        carry_ref[...] = jnp.zeros_like(carry_ref)

    h = _rms_norm(x_ref[0], g_ref[...]).astype(BF16)
    y = jnp.dot(h, wqkf_ref[...], preferred_element_type=F32)
    n_tiles = 2 * d // MXU_DIM
    for t in range(n_tiles):
        sl = slice(MXU_DIM * t, MXU_DIM * (t + 1))
        tile = y[:, sl]
        ss = jnp.dot((tile * tile).astype(BF16), bd_ref[...], preferred_element_type=F32)
        r = lax.rsqrt(ss * (1.0 / HEAD_DIM) + EPS)
        res = (tile * r * gain_ref[:, sl]).astype(BF16)
        if t < n_tiles // 2:
            q_ref[0, :, sl] = res
        else:
            k_ref[0, :, MXU_DIM * t - d:MXU_DIM * (t + 1) - d] = res

    f = y[:, 2 * d:] + bf_ref[...]
    log_f = jnp.minimum(f, 0.0) - jnp.log1p(jnp.exp(-jnp.abs(f)))
    cs = jnp.dot(tri_ref[...], log_f, precision=lax.Precision.HIGHEST,
                 preferred_element_type=F32) + carry_ref[...]
    carry_ref[...] = cs[tm - 1:tm, :]
    lane = lax.broadcasted_iota(jnp.int32, cs.shape, 1)
    rem = jnp.where(lane < N_HEADS, cs * (-LOG2E), 0.0)
    pieces = jnp.zeros_like(rem)
    for i in range(BIAS_PIECES):
        piece = rem.astype(BF16).astype(F32)
        rem = rem - piece
        pieces = pieces + (piece if i == 0 else pltpu.roll(piece, N_HEADS * i, axis=1))
    kb_ref[0] = jnp.dot(pieces.astype(BF16), place_ref[...],
                        preferred_element_type=F32).astype(BF16)

    vt = lax.dot_general(wvt_ref[...], h, NT_DIMS, preferred_element_type=F32)
    for j in range(tm // tk):
        vt_ref[0, j] = vt[:, tk * j:tk * (j + 1)].astype(BF16)


def _qkv_proj(x, g, wqkf, wvt, bd, gain, bfp, tri, place, *, tm, tk):
    b, s, d = x.shape
    nf = wqkf.shape[1]
    row = lambda bi, si: (bi, si, 0)
    return pl.pallas_call(
        functools.partial(_qkv_kernel, tm=tm, tk=tk, d=d),
        grid=(b, s // tm),
        in_specs=[pl.BlockSpec((1, tm, d), row),
                  _const_spec((1, d)), _const_spec((d, nf)), _const_spec((d, d)),
                  _const_spec((MXU_DIM, MXU_DIM)), _const_spec((1, 2 * d)),
                  _const_spec((1, LANES)), _const_spec((tm, tm)), _const_spec((LANES, d))],
        out_specs=[pl.BlockSpec((1, tm, d), row),
                   pl.BlockSpec((1, tm, d), row),
                   pl.BlockSpec((1, tm // tk, d, tk), lambda bi, si: (bi, si, 0, 0)),
                   pl.BlockSpec((1, tm, d), row)],
        out_shape=[jax.ShapeDtypeStruct((b, s, d), BF16),
                   jax.ShapeDtypeStruct((b, s, d), BF16),
                   jax.ShapeDtypeStruct((b, s // tk, d, tk), BF16),
                   jax.ShapeDtypeStruct((b, s, d), BF16)],
        scratch_shapes=[pltpu.VMEM((1, LANES), F32)],
        compiler_params=pltpu.CompilerParams(
            dimension_semantics=("arbitrary", "arbitrary"), vmem_limit_bytes=VMEM_LIMIT),
        name="qkv_proj",
    )(x, g, wqkf, wvt, bd, gain, bfp, tri, place)


def _attn_kernel(q_ref, k_ref, kb_ref, vt_ref, o_ref, *, tq, tk):
    i = pl.program_id(2)
    n_sub = tq // tk
    q = q_ref[0]
    lane = lax.broadcasted_iota(jnp.int32, (tq, LANES), 1)
    causal = (lax.broadcasted_iota(jnp.int32, (tk, tk), 0)
              <= lax.broadcasted_iota(jnp.int32, (tk, tk), 1))
    outs = []
    for hh in range(HEADS_PER_STEP):
        q_h = jnp.where((lane >= HEAD_DIM * hh) & (lane < HEAD_DIM * (hh + 1)), q, jnp.zeros_like(q))
        ones_h = jnp.where((lane >= BIAS_PIECES * hh) & (lane < BIAS_PIECES * (hh + 1)),
                           1.0, 0.0).astype(BF16)
        qa = jnp.concatenate([q_h, ones_h], axis=1)

        def update(carry, j, qa_cols, diagonal):
            m, l, acc = carry
            off = pl.multiple_of(j * tk, tk)
            ka = jnp.concatenate([k_ref[0, pl.ds(off, tk), :],
                                  kb_ref[0, pl.ds(off, tk), :]], axis=1)
            s = lax.dot_general(ka, qa_cols, NT_DIMS, preferred_element_type=F32)
            if diagonal:
                s_tri = jnp.where(causal, s[:, :tk], -1e30)
                s = s_tri if s.shape[1] == tk else jnp.concatenate([s_tri, s[:, tk:]], axis=1)
            m_new = jnp.maximum(m, jnp.max(s, axis=0, keepdims=True))
            alpha = jnp.exp2(m - m_new)
            p = jnp.exp2(s - m_new)
            l = alpha * l + jnp.sum(p, axis=0, keepdims=True)
            vt = vt_ref[0, j, HEAD_DIM * hh:HEAD_DIM * (hh + 1), :]
            acc = alpha * acc + jnp.dot(vt, p.astype(BF16), preferred_element_type=F32)
            return m_new, l, acc

        init = (jnp.full((1, tq), -1e30, F32), jnp.zeros((1, tq), F32),
                jnp.zeros((HEAD_DIM, tq), F32))
        m, l, acc = lax.fori_loop(
            0, n_sub * i, lambda j, c: update(c, j, qa, False), init)
        m, l, acc = update((m, l, acc), n_sub * i, qa, True)
        for dd in range(1, n_sub):
            c0 = tk * dd
            m_d, l_d, acc_d = update((m[:, c0:], l[:, c0:], acc[:, c0:]), n_sub * i + dd,
                                     qa[c0:], True)
            m = jnp.concatenate([m[:, :c0], m_d], axis=1)
            l = jnp.concatenate([l[:, :c0], l_d], axis=1)
            acc = jnp.concatenate([acc[:, :c0], acc_d], axis=1)
        outs.append(acc / l)
    o_t = jnp.concatenate(outs, axis=0)
    o_ref[0] = o_t.T.astype(BF16)


def _attention(q, k, kb, vt, *, tq, tk):
    b, s, d = q.shape
    n_pairs = d // LANES
    nk = s // tk
    return pl.pallas_call(
        functools.partial(_attn_kernel, tq=tq, tk=tk),
        grid=(b, n_pairs, s // tq),
        in_specs=[pl.BlockSpec((1, tq, LANES), lambda bi, hp, i: (bi, i, hp)),
                  pl.BlockSpec((1, s, LANES), lambda bi, hp, i: (bi, 0, hp)),
                  pl.BlockSpec((1, s, LANES), lambda bi, hp, i: (bi, 0, hp)),
                  pl.BlockSpec((1, nk, LANES, tk), lambda bi, hp, i: (bi, 0, hp, 0))],
        out_specs=pl.BlockSpec((1, tq, LANES), lambda bi, hp, i: (bi, i, hp)),
        out_shape=jax.ShapeDtypeStruct((b, s, d), BF16),
        compiler_params=pltpu.CompilerParams(
            dimension_semantics=("arbitrary", "arbitrary", "arbitrary"),
            vmem_limit_bytes=VMEM_LIMIT),
        name="fox_attention",
    )(q, k, kb, vt)


def _mix_mlp_kernel(x_ref, a_ref, wmix_ref, g_ref, w1_ref, w2_ref, o_ref, *, glu, d, ff_chunk):
    mix = jnp.dot(a_ref[...], wmix_ref[...], preferred_element_type=F32)
    if glu:
        mix = mix[:, :d] * jax.nn.sigmoid(mix[:, d:])
    x1 = x_ref[...] + mix
    h = _rms_norm(x1, g_ref[...]).astype(BF16)
    acc = x1
    for c in range(w1_ref.shape[1] // ff_chunk):
        sl = slice(ff_chunk * c, ff_chunk * (c + 1))
        hid = jnp.maximum(jnp.dot(h, w1_ref[:, sl], preferred_element_type=F32), 0.0)
        acc = acc + jnp.dot((hid * hid).astype(BF16), w2_ref[sl, :], preferred_element_type=F32)
    o_ref[...] = acc


def _mix_mlp(x, a, wmix, g, w1, w2, *, glu, tm, ff_chunk=1024):
    t, d = x.shape
    row = lambda i: (i, 0)
    single = pl.Buffered(1)
    wspec = lambda shape: pl.BlockSpec(shape, lambda i: (0, 0), pipeline_mode=single)
    return pl.pallas_call(
        functools.partial(_mix_mlp_kernel, glu=glu, d=d, ff_chunk=ff_chunk),
        grid=(t // tm,),
        in_specs=[pl.BlockSpec((tm, d), row), pl.BlockSpec((tm, d), row),
                  wspec(wmix.shape), wspec((1, d)), wspec(w1.shape), wspec(w2.shape)],
        out_specs=pl.BlockSpec((tm, d), row),
        out_shape=jax.ShapeDtypeStruct((t, d), F32),
        compiler_params=pltpu.CompilerParams(
            dimension_semantics=("arbitrary",), vmem_limit_bytes=VMEM_LIMIT),
        name="mix_glu_mlp" if glu else "mix_mlp",
    )(x, a, wmix, g, w1, w2)


def _norm_proj_kernel(x_ref, g_ref, w_ref, o_ref):
    h = _rms_norm(x_ref[...], g_ref[...]).astype(BF16)
    o_ref[...] = jnp.dot(h, w_ref[...], preferred_element_type=F32).astype(o_ref.dtype)


def _norm_proj(x, g, w, *, tm):
    t, d = x.shape
    n = w.shape[1]
    return pl.pallas_call(
        _norm_proj_kernel,
        grid=(t // tm,),
        in_specs=[pl.BlockSpec((tm, d), lambda i: (i, 0)), _const_spec((1, d)),
                  _const_spec((d, n))],
        out_specs=pl.BlockSpec((tm, n), lambda i: (i, 0)),
        out_shape=jax.ShapeDtypeStruct((t, n), BF16),
        compiler_params=pltpu.CompilerParams(
            dimension_semantics=("arbitrary",), vmem_limit_bytes=VMEM_LIMIT),
        name="norm_proj",
    )(x, g, w)


def _ssm_kernel(u_ref, mi_ref, ws_ref, wo_ref, a1_ref, a2_ref, dv_ref, z_ref, *, n_chunks):
    u = u_ref[0]
    rows = u.shape[0]
    y = jnp.dot(u, mi_ref[0], preferred_element_type=F32)
    xs = jnp.dot(u, ws_ref[0], preferred_element_type=F32)
    chunk = lax.broadcasted_iota(jnp.int32, (rows, LANES), 0) % n_chunks
    half = SSM_STATE
    step = 1
    j = 0
    while step < n_chunks:
        sh = jnp.where(chunk >= step, pltpu.roll(xs, step, axis=0), 0.0)
        xs = xs + sh * a1_ref[0, j:j + 1, :] + pltpu.roll(sh, half, axis=1) * a2_ref[0, j:j + 1, :]
        step *= 2
        j += 1
    x_in = jnp.where(chunk >= 1, pltpu.roll(xs, 1, axis=0), 0.0)
    y = y + jnp.dot(x_in.astype(BF16), wo_ref[0], preferred_element_type=F32)
    y = y + u.astype(F32) * dv_ref[0]
    z_ref[0] = jax.nn.gelu(y).astype(BF16)


def _ssm_scan(ug, mi, ws, wo, a1, a2, dv, *, n_chunks):
    g, rows, w = ug.shape
    grp = lambda gi: (gi, 0, 0)
    return pl.pallas_call(
        functools.partial(_ssm_kernel, n_chunks=n_chunks),
        grid=(g,),
        in_specs=[pl.BlockSpec((1, rows, w), grp), pl.BlockSpec((1, w, w), grp),
                  pl.BlockSpec((1, w, LANES), grp), pl.BlockSpec((1, LANES, w), grp),
                  pl.BlockSpec((1,) + a1.shape[1:], grp), pl.BlockSpec((1,) + a2.shape[1:], grp),
                  pl.BlockSpec((1, 1, w), grp)],
        out_specs=pl.BlockSpec((1, rows, w), grp),
        out_shape=jax.ShapeDtypeStruct((g, rows, w), BF16),
        compiler_params=pltpu.CompilerParams(
            dimension_semantics=("arbitrary",), vmem_limit_bytes=VMEM_LIMIT),
        name="s5_scan",
    )(ug, mi, ws, wo, a1, a2, dv)


def _ssm_operators(a_re, a_im, b_re, b_im, c_re, c_im, log_dt, d_skip, n_chunks):
    hi = lax.Precision.HIGHEST
    L = SSM_CHUNK
    g, p = a_re.shape
    dt = jnp.exp(log_dt)[:, None]
    lam_re, lam_im = dt * a_re, dt * a_im

    def powers(ks):
        ks = jnp.asarray(ks, F32)[:, None, None]
        mag = jnp.exp(ks * lam_re)
        return mag * jnp.cos(ks * lam_im), mag * jnp.sin(ks * lam_im)

    pr, pi = powers(range(L + 1))
    num_re, num_im = pr[1] - 1.0, pi[1]
    den = a_re * a_re + a_im * a_im
    s_re = (num_re * a_re + num_im * a_im) / den
    s_im = (num_im * a_re - num_re * a_im) / den
    bb_re = s_re[..., None] * b_re - s_im[..., None] * b_im
    bb_im = s_re[..., None] * b_im + s_im[..., None] * b_re
    ca_re = c_re[None] * pr[:, :, None, :] - c_im[None] * pi[:, :, None, :]
    ca_im = c_re[None] * pi[:, :, None, :] + c_im[None] * pr[:, :, None, :]
    kern = (jnp.einsum('kgop,gpi->kgio', ca_re[:L], bb_re, precision=hi)
            - jnp.einsum('kgop,gpi->kgio', ca_im[:L], bb_im, precision=hi))
    s_idx = jnp.arange(L)[:, None]
    t_idx = jnp.arange(L)[None, :]
    lag = t_idx - s_idx
    m_intra = jnp.where((lag >= 0)[:, :, None, None, None], kern[jnp.clip(lag, 0, L - 1)], 0.0)
    m_intra = m_intra.transpose(2, 0, 3, 1, 4).reshape(g, L * SSM_GROUP, L * SSM_GROUP)
    pr_s, pi_s = pr[L - 1::-1][:L], pi[L - 1::-1][:L]
    ws_re = pr_s[..., None] * bb_re[None] - pi_s[..., None] * bb_im[None]
    ws_im = pr_s[..., None] * bb_im[None] + pi_s[..., None] * bb_re[None]
    w_state = jnp.concatenate([ws_re, ws_im], axis=2)
    w_state = w_state.transpose(1, 0, 3, 2).reshape(g, L * SSM_GROUP, 2 * p)
    w_out = jnp.concatenate([ca_re[1:], -ca_im[1:]], axis=3)
    w_out = w_out.transpose(1, 3, 0, 2).reshape(g, 2 * p, L * SSM_GROUP)
    n_steps = max(1, int(math.log2(n_chunks)))
    sr, si = powers([L * 2 ** j for j in range(n_steps)])
    a1 = jnp.concatenate([sr, sr], axis=2).transpose(1, 0, 2)
    a2 = jnp.concatenate([-si, si], axis=2).transpose(1, 0, 2)
    dv = jnp.tile(d_skip.reshape(g, 1, SSM_GROUP), (1, L, 1)).reshape(g, 1, L * SSM_GROUP)
    return (m_intra.astype(BF16), w_state.astype(BF16), w_out.astype(BF16),
            a1.astype(F32), a2.astype(F32), dv.astype(F32))


def kernel(x, norm_mix_g, norm_mlp_g, attn_w_in, attn_b_f, attn_q_g, attn_k_g, attn_w_out,
           ssm_w_in, ssm_a_re, ssm_a_im, ssm_b_re, ssm_b_im, ssm_c_re, ssm_c_im, ssm_log_dt,
           ssm_d, ssm_w_glu, mlp_w1, mlp_w2):
    b, s, d = x.shape
    t = b * s
    tk = MXU_DIM
    tq = min(4 * tk, s)
    tm_proj = min(512, s)
    tm_mlp = min(512, t)

    w_in = attn_w_in[0]
    w_f = jnp.pad(w_in[:, 3 * d:], ((0, 0), (0, LANES - N_HEADS)))
    wqkf = jnp.concatenate([w_in[:, :2 * d], w_f], axis=1).astype(BF16)
    wvt = w_in[:, 2 * d:3 * d].T.astype(BF16)
    blk = jnp.arange(MXU_DIM) // HEAD_DIM
    bd = (blk[:, None] == blk[None, :]).astype(BF16)
    gain = jnp.concatenate([jnp.tile(attn_q_g[0], N_HEADS) * (LOG2E / math.sqrt(HEAD_DIM)),
                            jnp.tile(attn_k_g[0], N_HEADS)]).reshape(1, 2 * d)
    bfp = jnp.pad(attn_b_f[0], (0, LANES - N_HEADS)).reshape(1, LANES)
    tri = (jnp.arange(tm_proj)[:, None] >= jnp.arange(tm_proj)[None, :]).astype(F32)
    heads = jnp.arange(N_HEADS)
    place = jnp.zeros((LANES, d), F32)
    for i in range(BIAS_PIECES):
        place = place.at[N_HEADS * i + heads,
                         (heads // HEADS_PER_STEP) * LANES
                         + BIAS_PIECES * (heads % HEADS_PER_STEP) + i].set(1.0)
    q, k, vt, kb = _qkv_proj(x, norm_mix_g[0].reshape(1, d), wqkf, wvt, bd, gain, bfp, tri,
                             place.astype(BF16), tm=tm_proj, tk=tk)
    o = _attention(q, k, kb, vt, tq=tq, tk=tk)
    x2 = _mix_mlp(x.reshape(t, d), o.reshape(t, d), attn_w_out[0].astype(BF16),
                  norm_mlp_g[0].reshape(1, d), mlp_w1[0].astype(BF16), mlp_w2[0].astype(BF16),
                  glu=False, tm=tm_mlp)

    u = _norm_proj(x2, norm_mix_g[1].reshape(1, d), ssm_w_in[0].astype(BF16), tm=tm_mlp)
    n_groups = d // SSM_GROUP
    n_chunks = s // SSM_CHUNK
    ug = u.reshape(b * n_chunks, SSM_CHUNK, n_groups, SSM_GROUP).transpose(2, 0, 1, 3)
    ug = ug.reshape(n_groups, b * n_chunks, SSM_CHUNK * SSM_GROUP)
    ops = _ssm_operators(ssm_a_re[0], ssm_a_im[0], ssm_b_re[0], ssm_b_im[0], ssm_c_re[0],
                         ssm_c_im[0], ssm_log_dt[0], ssm_d[0], n_chunks)
    zg = _ssm_scan(ug, *ops, n_chunks=n_chunks)
    z = zg.reshape(n_groups, b * n_chunks, SSM_CHUNK, SSM_GROUP).transpose(1, 2, 0, 3)
    z = z.reshape(t, d)
    x3 = _mix_mlp(x2, z, ssm_w_glu[0].astype(BF16), norm_mlp_g[1].reshape(1, d),
                  mlp_w1[1].astype(BF16), mlp_w2[1].astype(BF16), glu=True, tm=tm_mlp)
    return x3.reshape(b, s, d)
```

```python
import functools
import math

import jax
import jax.numpy as jnp
from jax import lax
from jax.experimental import pallas as pl
from jax.experimental.pallas import tpu as pltpu

F32 = jnp.float32
BF16 = jnp.bfloat16

N_HEADS = 16
HEAD_DIM = 64
SSM_GROUP = 16
SSM_STATE = 64
SSM_CHUNK = 16
EPS = 1e-6
LOG2E = 1.4426950408889634

LANES = 128
MXU_DIM = 256
HEADS_PER_STEP = LANES // HEAD_DIM
BIAS_PIECES = 3
VMEM_LIMIT = 56 * 1024 * 1024

NT_DIMS = (((1,), (1,)), ((), ()))


def _rms_norm(x, g):
    ms = jnp.mean(x * x, axis=-1, keepdims=True)
    return x * lax.rsqrt(ms + EPS) * g


def _const_spec(shape):
    zeros = (0,) * len(shape)
    return pl.BlockSpec(shape, lambda *_: zeros)


def _qkv_kernel(x_ref, g_ref, wqkf_ref, wvt_ref, bd_ref, gain_ref, bf_ref, tri_ref, place_ref,
                q_ref, k_ref, vt_ref, kb_ref, carry_ref, *, tm, tk, d):
    @pl.when(pl.program_id(1) == 0)
    def _():
        carry_ref[...] = jnp.zeros_like(carry_ref)

    h = _rms_norm(x_ref[0], g_ref[...]).astype(BF16)
    y = jnp.dot(h, wqkf_ref[...], preferred_element_type=F32)
    n_tiles = 2 * d // MXU_DIM
    for t in range(n_tiles):
        sl = slice(MXU_DIM * t, MXU_DIM * (t + 1))
        tile = y[:, sl]
        ss = jnp.dot((tile * tile).astype(BF16), bd_ref[...], preferred_element_type=F32)
        r = lax.rsqrt(ss * (1.0 / HEAD_DIM) + EPS)
        res = (tile * r * gain_ref[:, sl]).astype(BF16)
        if t < n_tiles // 2:
            q_ref[0, :, sl] = res
        else:
            k_ref[0, :, MXU_DIM * t - d:MXU_DIM * (t + 1) - d] = res

    f = y[:, 2 * d:] + bf_ref[...]
    log_f = jnp.minimum(f, 0.0) - jnp.log1p(jnp.exp(-jnp.abs(f)))
    cs = jnp.dot(tri_ref[...], log_f, precision=lax.Precision.HIGHEST,
                 preferred_element_type=F32) + carry_ref[...]
    carry_ref[...] = cs[tm - 1:tm, :]
    lane = lax.broadcasted_iota(jnp.int32, cs.shape, 1)
    rem = jnp.where(lane < N_HEADS, cs * (-LOG2E), 0.0)
    pieces = jnp.zeros_like(rem)
    for i in range(BIAS_PIECES):
        piece = rem.astype(BF16).astype(F32)
        rem = rem - piece
        pieces = pieces + (piece if i == 0 else pltpu.roll(piece, N_HEADS * i, axis=1))
    kb_ref[0] = jnp.dot(pieces.astype(BF16), place_ref[...],
                        preferred_element_type=F32).astype(BF16)

    vt = lax.dot_general(wvt_ref[...], h, NT_DIMS, preferred_element_type=F32)
    for j in range(tm // tk):
        vt_ref[0, j] = vt[:, tk * j:tk * (j + 1)].astype(BF16)


def _qkv_proj(x, g, wqkf, wvt, bd, gain, bfp, tri, place, *, tm, tk):
    b, s, d = x.shape
    nf = wqkf.shape[1]
    row = lambda bi, si: (bi, si, 0)
    return pl.pallas_call(
        functools.partial(_qkv_kernel, tm=tm, tk=tk, d=d),
        grid=(b, s // tm),
        in_specs=[pl.BlockSpec((1, tm, d), row),
                  _const_spec((1, d)), _const_spec((d, nf)), _const_spec((d, d)),
                  _const_spec((MXU_DIM, MXU_DIM)), _const_spec((1, 2 * d)),
                  _const_spec((1, LANES)), _const_spec((tm, tm)), _const_spec((LANES, d))],
        out_specs=[pl.BlockSpec((1, tm, d), row),
                   pl.BlockSpec((1, tm, d), row),
                   pl.BlockSpec((1, tm // tk, d, tk), lambda bi, si: (bi, si, 0, 0)),
                   pl.BlockSpec((1, tm, d), row)],
        out_shape=[jax.ShapeDtypeStruct((b, s, d), BF16),
                   jax.ShapeDtypeStruct((b, s, d), BF16),
                   jax.ShapeDtypeStruct((b, s // tk, d, tk), BF16),
                   jax.ShapeDtypeStruct((b, s, d), BF16)],
        scratch_shapes=[pltpu.VMEM((1, LANES), F32)],
        compiler_params=pltpu.CompilerParams(
            dimension_semantics=("arbitrary", "arbitrary"), vmem_limit_bytes=VMEM_LIMIT),
        name="qkv_proj",
    )(x, g, wqkf, wvt, bd, gain, bfp, tri, place)


def _attn_kernel(q_ref, k_ref, kb_ref, vt_ref, o_ref, *, tq, tk):
    i = pl.program_id(2)
    n_sub = tq // tk
    q = q_ref[0]
    lane = lax.broadcasted_iota(jnp.int32, (tq, LANES), 1)
    causal = (lax.broadcasted_iota(jnp.int32, (tk, tk), 0)
              <= lax.broadcasted_iota(jnp.int32, (tk, tk), 1))
    outs = []
    for hh in range(HEADS_PER_STEP):
        q_h = jnp.where((lane >= HEAD_DIM * hh) & (lane < HEAD_DIM * (hh + 1)), q, jnp.zeros_like(q))
        ones_h = jnp.where((lane >= BIAS_PIECES * hh) & (lane < BIAS_PIECES * (hh + 1)),
                           1.0, 0.0).astype(BF16)
        qa = jnp.concatenate([q_h, ones_h], axis=1)

        def update(carry, j, qa_cols, diagonal):
            m, l, acc = carry
            off = pl.multiple_of(j * tk, tk)
            ka = jnp.concatenate([k_ref[0, pl.ds(off, tk), :],
                                  kb_ref[0, pl.ds(off, tk), :]], axis=1)
            s = lax.dot_general(ka, qa_cols, NT_DIMS, preferred_element_type=F32)
            if diagonal:
                s_tri = jnp.where(causal, s[:, :tk], -1e30)
                s = s_tri if s.shape[1] == tk else jnp.concatenate([s_tri, s[:, tk:]], axis=1)
            m_new = jnp.maximum(m, jnp.max(s, axis=0, keepdims=True))
            alpha = jnp.exp2(m - m_new)
            p = jnp.exp2(s - m_new)
            l = alpha * l + jnp.sum(p, axis=0, keepdims=True)
            vt = vt_ref[0, j, HEAD_DIM * hh:HEAD_DIM * (hh + 1), :]
            acc = alpha * acc + jnp.dot(vt, p.astype(BF16), preferred_element_type=F32)
            return m_new, l, acc

        init = (jnp.full((1, tq), -1e30, F32), jnp.zeros((1, tq), F32),
                jnp.zeros((HEAD_DIM, tq), F32))
        m, l, acc = lax.fori_loop(
            0, n_sub * i, lambda j, c: update(c, j, qa, False), init)
        m, l, acc = update((m, l, acc), n_sub * i, qa, True)
        for dd in range(1, n_sub):
            c0 = tk * dd
            m_d, l_d, acc_d = update((m[:, c0:], l[:, c0:], acc[:, c0:]), n_sub * i + dd,
                                     qa[c0:], True)
            m = jnp.concatenate([m[:, :c0], m_d], axis=1)
            l = jnp.concatenate([l[:, :c0], l_d], axis=1)
            acc = jnp.concatenate([acc[:, :c0], acc_d], axis=1)
        outs.append(acc / l)
    o_t = jnp.concatenate(outs, axis=0)
    o_ref[0] = o_t.T.astype(BF16)


def _attention(q, k, kb, vt, *, tq, tk):
    b, s, d = q.shape
    n_pairs = d // LANES
    nk = s // tk
    return pl.pallas_call(
        functools.partial(_attn_kernel, tq=tq, tk=tk),
        grid=(b, n_pairs, s // tq),
        in_specs=[pl.BlockSpec((1, tq, LANES), lambda bi, hp, i: (bi, i, hp)),
                  pl.BlockSpec((1, s, LANES), lambda bi, hp, i: (bi, 0, hp)),
                  pl.BlockSpec((1, s, LANES), lambda bi, hp, i: (bi, 0, hp)),
                  pl.BlockSpec((1, nk, LANES, tk), lambda bi, hp, i: (bi, 0, hp, 0))],
        out_specs=pl.BlockSpec((1, tq, LANES), lambda bi, hp, i: (bi, i, hp)),
        out_shape=jax.ShapeDtypeStruct((b, s, d), BF16),
        compiler_params=pltpu.CompilerParams(
            dimension_semantics=("arbitrary", "arbitrary", "arbitrary"),
            vmem_limit_bytes=VMEM_LIMIT),
        name="fox_attention",
    )(q, k, kb, vt)


def _mix_mlp_kernel(x_ref, a_ref, wmix_ref, g_ref, w1_ref, w2_ref, o_ref, *, glu, d, ff_chunk):
    mix = jnp.dot(a_ref[...], wmix_ref[...], preferred_element_type=F32)
    if glu:
        mix = mix[:, :d] * jax.nn.sigmoid(mix[:, d:])
    x1 = x_ref[...] + mix
    h = _rms_norm(x1, g_ref[...]).astype(BF16)
    acc = x1
    for c in range(w1_ref.shape[1] // ff_chunk):
        sl = slice(ff_chunk * c, ff_chunk * (c + 1))
        hid = jnp.maximum(jnp.dot(h, w1_ref[:, sl], preferred_element_type=F32), 0.0)
        acc = acc + jnp.dot((hid * hid).astype(BF16), w2_ref[sl, :], preferred_element_type=F32)
    o_ref[...] = acc


def _mix_mlp(x, a, wmix, g, w1, w2, *, glu, tm, ff_chunk=1024):
    t, d = x.shape
    row = lambda i: (i, 0)
    single = pl.Buffered(1)
    wspec = lambda shape: pl.BlockSpec(shape, lambda i: (0, 0), pipeline_mode=single)
    return pl.pallas_call(
        functools.partial(_mix_mlp_kernel, glu=glu, d=d, ff_chunk=ff_chunk),
        grid=(t // tm,),
        in_specs=[pl.BlockSpec((tm, d), row), pl.BlockSpec((tm, d), row),
                  wspec(wmix.shape), wspec((1, d)), wspec(w1.shape), wspec(w2.shape)],
        out_specs=pl.BlockSpec((tm, d), row),
        out_shape=jax.ShapeDtypeStruct((t, d), F32),
        compiler_params=pltpu.CompilerParams(
            dimension_semantics=("arbitrary",), vmem_limit_bytes=VMEM_LIMIT),
        name="mix_glu_mlp" if glu else "mix_mlp",
    )(x, a, wmix, g, w1, w2)


def _norm_proj_kernel(x_ref, g_ref, w_ref, o_ref):
    h = _rms_norm(x_ref[...], g_ref[...]).astype(BF16)
    o_ref[...] = jnp.dot(h, w_ref[...], preferred_element_type=F32).astype(o_ref.dtype)


def _norm_proj(x, g, w, *, tm):
    t, d = x.shape
    n = w.shape[1]
    return pl.pallas_call(
        _norm_proj_kernel,
        grid=(t // tm,),
        in_specs=[pl.BlockSpec((tm, d), lambda i: (i, 0)), _const_spec((1, d)),
                  _const_spec((d, n))],
        out_specs=pl.BlockSpec((tm, n), lambda i: (i, 0)),
        out_shape=jax.ShapeDtypeStruct((t, n), BF16),
        compiler_params=pltpu.CompilerParams(
            dimension_semantics=("arbitrary",), vmem_limit_bytes=VMEM_LIMIT),
        name="norm_proj",
    )(x, g, w)


GROUPS_PER_SLAB = LANES // SSM_GROUP


def _piece_transpose(arrs, piece):
    arrs = list(arrs)
    dist = GROUPS_PER_SLAB // 2
    while dist >= 1:
        keep = (piece & dist) == 0
        shift = SSM_GROUP * dist
        for i in range(GROUPS_PER_SLAB):
            if i & dist:
                continue
            a, b = arrs[i], arrs[i + dist]
            arrs[i] = jnp.where(keep, a, pltpu.roll(b, shift, axis=1))
            arrs[i + dist] = jnp.where(keep, pltpu.roll(a, LANES - shift, axis=1), b)
        dist //= 2
    return arrs


def _ssm_kernel(u_ref, perm_ref, cc_ref, bt_ref, pw1_ref, pw2_ref, zoh_ref, a1_ref, a2_ref,
                dv_ref, z_ref, ug_scr, zg_scr, mi_scr, ws_scr, wo_scr, *, n_chunks):
    L = SSM_CHUNK
    rows = u_ref.shape[0] // L
    w = L * SSM_GROUP
    half = SSM_STATE
    piece = lax.broadcasted_iota(jnp.int32, (L, LANES), 1) // SSM_GROUP
    unroll = math.gcd(rows // L, 8)

    def relayout_in(rb, carry):
        t0 = pl.multiple_of(rb * w, w)
        r0 = pl.multiple_of(rb * L, L)
        by_pos = jnp.dot(perm_ref[...], u_ref[pl.ds(t0, w), :], preferred_element_type=F32)
        for hf in range(L // GROUPS_PER_SLAB):
            arrs = [by_pos[L * (GROUPS_PER_SLAB * hf + k):L * (GROUPS_PER_SLAB * hf + k + 1)]
                    for k in range(GROUPS_PER_SLAB)]
            arrs = _piece_transpose(arrs, piece)
            for g in range(GROUPS_PER_SLAB):
                ug_scr[g, pl.ds(r0, L), LANES * hf:LANES * (hf + 1)] = arrs[g].astype(BF16)
        return carry

    lax.fori_loop(0, rows // L, relayout_in, 0, unroll=unroll)

    chunk = lax.broadcasted_iota(jnp.int32, (rows, LANES), 0) % n_chunks
    lane_w = lax.broadcasted_iota(jnp.int32, (SSM_GROUP, w), 1)
    sign = jnp.where(lax.broadcasted_iota(jnp.int32, (1, LANES), 1) < half, 1.0, -1.0)

    def group(g, carry):
        cc = cc_ref[g]
        bt = bt_ref[g]
        pw1 = pw1_ref[g]
        pw2 = pw2_ref[g]
        zoh = zoh_ref[g]
        bbar = bt * zoh[0:1] + pltpu.roll(bt, half, axis=1) * zoh[1:2]
        bbar_sw = pltpu.roll(bbar, half, axis=1)
        cc_sw = pltpu.roll(cc, half, axis=1)
        ca = [cc * pw1[k:k + 1] + cc_sw * pw2[k:k + 1] for k in range(L + 1)]
        k_all = lax.dot_general(bbar * sign, jnp.concatenate(ca[:L], axis=0), NT_DIMS,
                                precision=lax.Precision.HIGHEST, preferred_element_type=F32)
        for s in range(L):
            blk = k_all if s == 0 else jnp.where(
                lane_w >= SSM_GROUP * s, pltpu.roll(k_all, SSM_GROUP * s, axis=1), 0.0)
            mi_scr[SSM_GROUP * s:SSM_GROUP * (s + 1), :] = blk.astype(BF16)
            k = L - 1 - s
            ws_scr[SSM_GROUP * s:SSM_GROUP * (s + 1), :] = (
                bbar * pw1[k:k + 1] + bbar_sw * pw2[k:k + 1]).astype(BF16)
            wo_scr[SSM_GROUP * s:SSM_GROUP * (s + 1), :] = (ca[s + 1] * sign).astype(BF16)

        u = ug_scr[g]
        y = jnp.dot(u, mi_scr[...], preferred_element_type=F32)
        xs = jnp.dot(u, ws_scr[...], preferred_element_type=F32)
        a1 = a1_ref[g]
        a2 = a2_ref[g]
        step = 1
        j = 0
        while step < n_chunks:
            sh = jnp.where(chunk >= step, pltpu.roll(xs, step, axis=0), 0.0)
            xs = xs + sh * a1[j:j + 1] + pltpu.roll(sh, half, axis=1) * a2[j:j + 1]
            step *= 2
            j += 1
        x_in = jnp.where(chunk >= 1, pltpu.roll(xs, 1, axis=0), 0.0)
        y = y + lax.dot_general(x_in.astype(BF16), wo_scr[...], NT_DIMS,
                                preferred_element_type=F32)
        y = y + u.astype(F32) * dv_ref[g]
        zg_scr[g] = jax.nn.gelu(y).astype(BF16)
        return carry

    lax.fori_loop(0, GROUPS_PER_SLAB, group, 0)

    def relayout_out(rb, carry):
        t0 = pl.multiple_of(rb * w, w)
        r0 = pl.multiple_of(rb * L, L)
        by_pos = []
        for hf in range(L // GROUPS_PER_SLAB):
            arrs = [zg_scr[g, pl.ds(r0, L), LANES * hf:LANES * (hf + 1)].astype(F32)
                    for g in range(GROUPS_PER_SLAB)]
            by_pos += _piece_transpose(arrs, piece)
        by_pos = jnp.concatenate(by_pos, axis=0).astype(BF16)
        z_ref[pl.ds(t0, w), :] = jnp.dot(perm_ref[...], by_pos,
                                         preferred_element_type=F32).astype(BF16)
        return carry

    lax.fori_loop(0, rows // L, relayout_out, 0, unroll=unroll)


def _ssm_scan(u, perm, cc, bt, pw1, pw2, zoh, a1, a2, dv, *, n_chunks):
    t, d = u.shape
    L = SSM_CHUNK
    rows = t // L
    w = L * SSM_GROUP
    gps = GROUPS_PER_SLAB
    slab = lambda j: (j, 0, 0)
    pspec = lambda arr: pl.BlockSpec((gps,) + arr.shape[1:], slab)
    return pl.pallas_call(
        functools.partial(_ssm_kernel, n_chunks=n_chunks),
        grid=(d // LANES,),
        in_specs=[pl.BlockSpec((t, LANES), lambda j: (0, j)), _const_spec((w, w)),
                  pspec(cc), pspec(bt), pspec(pw1), pspec(pw2), pspec(zoh), pspec(a1), pspec(a2),
                  pspec(dv)],
        out_specs=pl.BlockSpec((t, LANES), lambda j: (0, j)),
        out_shape=jax.ShapeDtypeStruct((t, d), BF16),
        scratch_shapes=[pltpu.VMEM((gps, rows, w), BF16), pltpu.VMEM((gps, rows, w), BF16),
                        pltpu.VMEM((w, w), BF16), pltpu.VMEM((w, LANES), BF16),
                        pltpu.VMEM((w, LANES), BF16)],
        compiler_params=pltpu.CompilerParams(
            dimension_semantics=("arbitrary",), vmem_limit_bytes=VMEM_LIMIT),
        name="s5_scan",
    )(u, perm, cc, bt, pw1, pw2, zoh, a1, a2, dv)


def _ssm_operators(a_re, a_im, b_re, b_im, c_re, c_im, log_dt, d_skip, n_chunks):
    L = SSM_CHUNK
    g, p = a_re.shape
    dt = jnp.exp(log_dt)[:, None]
    lam_re, lam_im = dt * a_re, dt * a_im

    def powers(ks):
        ks = jnp.asarray(ks, F32)[:, None, None]
        mag = jnp.exp(ks * lam_re)
        return mag * jnp.cos(ks * lam_im), mag * jnp.sin(ks * lam_im)

    def patterns(re, im):
        return (jnp.concatenate([re, re], axis=2).transpose(1, 0, 2),
                jnp.concatenate([-im, im], axis=2).transpose(1, 0, 2))

    pr, pi = powers(range(L + 1))
    num_re, num_im = pr[1] - 1.0, pi[1]
    den = a_re * a_re + a_im * a_im
    s_re = (num_re * a_re + num_im * a_im) / den
    s_im = (num_im * a_re - num_re * a_im) / den
    pw1, pw2 = patterns(pr, pi)
    zoh = jnp.concatenate(patterns(s_re[None], s_im[None]), axis=1)
    n_steps = max(1, int(math.log2(n_chunks)))
    a1, a2 = patterns(*powers([L * 2 ** j for j in range(n_steps)]))
    cc = jnp.concatenate([c_re, c_im], axis=2)
    bt = jnp.concatenate([b_re.transpose(0, 2, 1), b_im.transpose(0, 2, 1)], axis=2)
    dv = jnp.tile(d_skip.reshape(g, 1, SSM_GROUP), (1, L, 1)).reshape(g, 1, L * SSM_GROUP)
    return cc, bt, pw1, pw2, zoh, a1, a2, dv


def kernel(x, norm_mix_g, norm_mlp_g, attn_w_in, attn_b_f, attn_q_g, attn_k_g, attn_w_out,
           ssm_w_in, ssm_a_re, ssm_a_im, ssm_b_re, ssm_b_im, ssm_c_re, ssm_c_im, ssm_log_dt,
           ssm_d, ssm_w_glu, mlp_w1, mlp_w2):
    b, s, d = x.shape
    t = b * s
    tk = MXU_DIM
    tq = min(4 * tk, s)
    tm_proj = min(512, s)
    tm_mlp = min(512, t)

    w_in = attn_w_in[0]
    w_f = jnp.pad(w_in[:, 3 * d:], ((0, 0), (0, LANES - N_HEADS)))
    wqkf = jnp.concatenate([w_in[:, :2 * d], w_f], axis=1).astype(BF16)
    wvt = w_in[:, 2 * d:3 * d].T.astype(BF16)
    blk = jnp.arange(MXU_DIM) // HEAD_DIM
    bd = (blk[:, None] == blk[None, :]).astype(BF16)
    gain = jnp.concatenate([jnp.tile(attn_q_g[0], N_HEADS) * (LOG2E / math.sqrt(HEAD_DIM)),
                            jnp.tile(attn_k_g[0], N_HEADS)]).reshape(1, 2 * d)
    bfp = jnp.pad(attn_b_f[0], (0, LANES - N_HEADS)).reshape(1, LANES)
    tri = (jnp.arange(tm_proj)[:, None] >= jnp.arange(tm_proj)[None, :]).astype(F32)
    heads = jnp.arange(N_HEADS)
    place = jnp.zeros((LANES, d), F32)
    for i in range(BIAS_PIECES):
        place = place.at[N_HEADS * i + heads,
                         (heads // HEADS_PER_STEP) * LANES
                         + BIAS_PIECES * (heads % HEADS_PER_STEP) + i].set(1.0)
    q, k, vt, kb = _qkv_proj(x, norm_mix_g[0].reshape(1, d), wqkf, wvt, bd, gain, bfp, tri,
                             place.astype(BF16), tm=tm_proj, tk=tk)
    o = _attention(q, k, kb, vt, tq=tq, tk=tk)
    x2 = _mix_mlp(x.reshape(t, d), o.reshape(t, d), attn_w_out[0].astype(BF16),
                  norm_mlp_g[0].reshape(1, d), mlp_w1[0].astype(BF16), mlp_w2[0].astype(BF16),
                  glu=False, tm=tm_mlp)

    u = _norm_proj(x2, norm_mix_g[1].reshape(1, d), ssm_w_in[0].astype(BF16), tm=tm_mlp)
    n_chunks = s // SSM_CHUNK
    ops = _ssm_operators(ssm_a_re[0], ssm_a_im[0], ssm_b_re[0], ssm_b_im[0], ssm_c_re[0],
                         ssm_c_im[0], ssm_log_dt[0], ssm_d[0], n_chunks)
    tok = jnp.arange(SSM_CHUNK * SSM_CHUNK)
    perm = (tok[None, :] == (tok[:, None] % SSM_CHUNK) * SSM_CHUNK + tok[:, None] // SSM_CHUNK)
    z = _ssm_scan(u, perm.astype(BF16), *ops, n_chunks=n_chunks)
    x3 = _mix_mlp(x2, z, ssm_w_glu[0].astype(BF16), norm_mlp_g[1].reshape(1, d),
                  mlp_w1[1].astype(BF16), mlp_w2[1].astype(BF16), glu=True, tm=tm_mlp)
    return x3.reshape(b, s, d)
```

```python
import functools
import math

import jax
import jax.numpy as jnp
from jax import lax
from jax.experimental import pallas as pl
from jax.experimental.pallas import tpu as pltpu

F32 = jnp.float32
BF16 = jnp.bfloat16

N_HEADS = 16
HEAD_DIM = 64
SSM_GROUP = 16
SSM_STATE = 64
SSM_CHUNK = 16
EPS = 1e-6
LOG2E = 1.4426950408889634

LANES = 128
MXU_DIM = 256
HEADS_PER_STEP = LANES // HEAD_DIM
BIAS_PIECES = 3
VMEM_LIMIT = 56 * 1024 * 1024

NT_DIMS = (((1,), (1,)), ((), ()))


def _rms_norm(x, g):
    ms = jnp.mean(x * x, axis=-1, keepdims=True)
    return x * lax.rsqrt(ms + EPS) * g


def _const_spec(shape):
    zeros = (0,) * len(shape)
    return pl.BlockSpec(shape, lambda *_: zeros)


def _qkv_kernel(x_ref, g_ref, wqkf_ref, wvt_ref, bd_ref, gain_ref, bf_ref, tri_ref, place_ref,
                shift_ref, q_ref, k_ref, vt_ref, kb_ref, carry_ref, *, tm, tk, d):
    @pl.when(pl.program_id(1) == 0)
    def _():
        carry_ref[...] = jnp.zeros_like(carry_ref)

    h = _rms_norm(x_ref[0], g_ref[...]).astype(BF16)
    y = jnp.dot(h, wqkf_ref[...], preferred_element_type=F32)
    n_tiles = 2 * d // MXU_DIM
    for t in range(n_tiles):
        sl = slice(MXU_DIM * t, MXU_DIM * (t + 1))
        tile = y[:, sl]
        ss = jnp.dot((tile * tile).astype(BF16), bd_ref[...], preferred_element_type=F32)
        r = lax.rsqrt(ss * (1.0 / HEAD_DIM) + EPS)
        res = (tile * r * gain_ref[:, sl]).astype(BF16)
        if t < n_tiles // 2:
            q_ref[0, :, sl] = res
        else:
            k_ref[0, :, MXU_DIM * t - d:MXU_DIM * (t + 1) - d] = res

    f = y[:, 2 * d:] + bf_ref[...]
    log_f = jnp.minimum(f, 0.0) - jnp.log1p(jnp.exp(-jnp.abs(f)))
    cs = jnp.dot(tri_ref[...], log_f, precision=lax.Precision.HIGHEST,
                 preferred_element_type=F32) + carry_ref[...]
    carry_ref[...] = cs[tm - 1:tm, :]
    lane = lax.broadcasted_iota(jnp.int32, cs.shape, 1)
    c2 = cs * LOG2E
    pieces = jnp.zeros_like(cs)
    for role, val in enumerate((-c2, c2 - shift_ref[...])):
        rem = jnp.where(lane < N_HEADS, val, 0.0)
        for i in range(BIAS_PIECES):
            piece = rem.astype(BF16).astype(F32)
            rem = rem - piece
            n = BIAS_PIECES * role + i
            pieces = pieces + (piece if n == 0 else pltpu.roll(piece, N_HEADS * n, axis=1))
    kb_ref[0] = jnp.dot(pieces.astype(BF16), place_ref[...],
                        preferred_element_type=F32).astype(BF16)

    vt = lax.dot_general(wvt_ref[...], h, NT_DIMS, preferred_element_type=F32)
    for j in range(tm // tk):
        vt_ref[0, j] = vt[:, tk * j:tk * (j + 1)].astype(BF16)


def _qkv_proj(x, g, wqkf, wvt, bd, gain, bfp, tri, place, shift, *, tm, tk):
    b, s, d = x.shape
    nf = wqkf.shape[1]
    row = lambda bi, si: (bi, si, 0)
    return pl.pallas_call(
        functools.partial(_qkv_kernel, tm=tm, tk=tk, d=d),
        grid=(b, s // tm),
        in_specs=[pl.BlockSpec((1, tm, d), row),
                  _const_spec((1, d)), _const_spec((d, nf)), _const_spec((d, d)),
                  _const_spec((MXU_DIM, MXU_DIM)), _const_spec((1, 2 * d)),
                  _const_spec((1, LANES)), _const_spec((tm, tm)), _const_spec((LANES, d)),
                  _const_spec((1, LANES))],
        out_specs=[pl.BlockSpec((1, tm, d), row),
                   pl.BlockSpec((1, tm, d), row),
                   pl.BlockSpec((1, tm // tk, d, tk), lambda bi, si: (bi, si, 0, 0)),
                   pl.BlockSpec((1, tm, d), row)],
        out_shape=[jax.ShapeDtypeStruct((b, s, d), BF16),
                   jax.ShapeDtypeStruct((b, s, d), BF16),
                   jax.ShapeDtypeStruct((b, s // tk, d, tk), BF16),
                   jax.ShapeDtypeStruct((b, s, d), BF16)],
        scratch_shapes=[pltpu.VMEM((1, LANES), F32)],
        compiler_params=pltpu.CompilerParams(
            dimension_semantics=("arbitrary", "arbitrary"), vmem_limit_bytes=VMEM_LIMIT),
        name="qkv_proj",
    )(x, g, wqkf, wvt, bd, gain, bfp, tri, place, shift)


BIAS_Q_LANE = 8
MAX_SHIFT = 48.0


def _attn_kernel(q_ref, qb_ref, k_ref, kb_ref, vt_ref, o_ref, *, tq, tk, online):
    i = pl.program_id(2)
    n_sub = tq // tk
    kv_unroll = math.gcd(n_sub, 2)
    q = q_ref[0]
    qb = qb_ref[0]
    lane = lax.broadcasted_iota(jnp.int32, (tq, LANES), 1)
    lane_k = lax.broadcasted_iota(jnp.int32, (tk, LANES), 1)
    q_cols = (lane_k >= BIAS_Q_LANE) & (lane_k < BIAS_Q_LANE + HEADS_PER_STEP * BIAS_PIECES)
    causal = (lax.broadcasted_iota(jnp.int32, (tk, tk), 0)
              <= lax.broadcasted_iota(jnp.int32, (tk, tk), 1))
    one = jnp.ones((), BF16)
    zero = jnp.zeros((), BF16)
    outs = []
    for hh in range(HEADS_PER_STEP):
        q_h = jnp.where((lane >= HEAD_DIM * hh) & (lane < HEAD_DIM * (hh + 1)), q, zero)
        k_side = (lane >= BIAS_PIECES * hh) & (lane < BIAS_PIECES * (hh + 1))
        q_lo = BIAS_Q_LANE + BIAS_PIECES * hh
        q_side = (lane >= q_lo) & (lane < q_lo + BIAS_PIECES)
        qa = jnp.concatenate([q_h, jnp.where(k_side, one, jnp.where(q_side, qb, zero))],
                             axis=1)

        def update(carry, j, qa_cols, diagonal):
            off = pl.multiple_of(j * tk, tk)
            kbias = jnp.where(q_cols, one, kb_ref[0, pl.ds(off, tk), :])
            ka = jnp.concatenate([k_ref[0, pl.ds(off, tk), :], kbias], axis=1)
            s = lax.dot_general(ka, qa_cols, NT_DIMS, preferred_element_type=F32)
            if diagonal:
                s_tri = jnp.where(causal, s[:, :tk], -1e30)
                s = s_tri if s.shape[1] == tk else jnp.concatenate([s_tri, s[:, tk:]], axis=1)
            vt = vt_ref[0, j, HEAD_DIM * hh:HEAD_DIM * (hh + 1), :]
            if online:
                m, l, acc = carry
                m_new = jnp.maximum(m, jnp.max(s, axis=0, keepdims=True))
                alpha = jnp.exp2(m - m_new)
                p = jnp.exp2(s - m_new)
                l = alpha * l + jnp.sum(p, axis=0, keepdims=True)
                acc = alpha * acc + jnp.dot(vt, p.astype(BF16), preferred_element_type=F32)
                return m_new, l, acc
            l, acc = carry
            p = jnp.exp2(s)
            l = l + jnp.sum(p, axis=0, keepdims=True)
            acc = acc + jnp.dot(vt, p.astype(BF16), preferred_element_type=F32)
            return l, acc

        carry = (jnp.zeros((1, tq), F32), jnp.zeros((HEAD_DIM, tq), F32))
        if online:
            carry = (jnp.full((1, tq), -1e30, F32),) + carry
        def blocks(jj, c):
            for r in range(kv_unroll):
                c = update(c, kv_unroll * jj + r, qa, False)
            return c

        carry = lax.fori_loop(0, n_sub * i // kv_unroll, blocks, carry)
        carry = update(carry, n_sub * i, qa, True)
        for dd in range(1, n_sub):
            c0 = tk * dd
            part = update(tuple(c[:, c0:] for c in carry), n_sub * i + dd, qa[c0:], True)
            carry = tuple(jnp.concatenate([c[:, :c0], pc], axis=1) for c, pc in zip(carry, part))
        outs.append(carry[-1] / carry[-2])
    o_t = jnp.concatenate(outs, axis=0)
    o_ref[0] = o_t.T.astype(BF16)


def _attention(q, k, kb, vt, *, tq, tk, online):
    b, s, d = q.shape
    n_pairs = d // LANES
    nk = s // tk
    q_blk = pl.BlockSpec((1, tq, LANES), lambda bi, hp, i: (bi, i, hp))
    k_all = pl.BlockSpec((1, s, LANES), lambda bi, hp, i: (bi, 0, hp))
    return pl.pallas_call(
        functools.partial(_attn_kernel, tq=tq, tk=tk, online=online),
        grid=(b, n_pairs, s // tq),
        in_specs=[q_blk, q_blk, k_all, k_all,
                  pl.BlockSpec((1, nk, LANES, tk), lambda bi, hp, i: (bi, 0, hp, 0))],
        out_specs=q_blk,
        out_shape=jax.ShapeDtypeStruct((b, s, d), BF16),
        compiler_params=pltpu.CompilerParams(
            dimension_semantics=("arbitrary", "arbitrary", "arbitrary"),
            vmem_limit_bytes=VMEM_LIMIT),
        name="fox_attention_online" if online else "fox_attention",
    )(q, kb, k, kb, vt)


def _mix_mlp_kernel(x_ref, a_ref, wmix_ref, g_ref, w1_ref, w2_ref, o_ref, *, glu, d, ff_chunk):
    mix = jnp.dot(a_ref[...], wmix_ref[...], preferred_element_type=F32)
    if glu:
        mix = mix[:, :d] * jax.nn.sigmoid(mix[:, d:])
    x1 = x_ref[...] + mix
    h = _rms_norm(x1, g_ref[...]).astype(BF16)
    acc = x1
    for c in range(w1_ref.shape[1] // ff_chunk):
        sl = slice(ff_chunk * c, ff_chunk * (c + 1))
        hid = jnp.maximum(jnp.dot(h, w1_ref[:, sl], preferred_element_type=F32), 0.0)
        acc = acc + jnp.dot((hid * hid).astype(BF16), w2_ref[sl, :], preferred_element_type=F32)
    o_ref[...] = acc


def _mix_mlp(x, a, wmix, g, w1, w2, *, glu, tm, ff_chunk=1024):
    t, d = x.shape
    row = lambda i: (i, 0)
    single = pl.Buffered(1)
    wspec = lambda shape: pl.BlockSpec(shape, lambda i: (0, 0), pipeline_mode=single)
    return pl.pallas_call(
        functools.partial(_mix_mlp_kernel, glu=glu, d=d, ff_chunk=ff_chunk),
        grid=(t // tm,),
        in_specs=[pl.BlockSpec((tm, d), row), pl.BlockSpec((tm, d), row),
                  wspec(wmix.shape), wspec((1, d)), wspec(w1.shape), wspec(w2.shape)],
        out_specs=pl.BlockSpec((tm, d), row),
        out_shape=jax.ShapeDtypeStruct((t, d), F32),
        compiler_params=pltpu.CompilerParams(
            dimension_semantics=("arbitrary",), vmem_limit_bytes=VMEM_LIMIT),
        name="mix_glu_mlp" if glu else "mix_mlp",
    )(x, a, wmix, g, w1, w2)


def _norm_proj_kernel(x_ref, g_ref, w_ref, o_ref):
    h = _rms_norm(x_ref[...], g_ref[...]).astype(BF16)
    o_ref[...] = jnp.dot(h, w_ref[...], preferred_element_type=F32).astype(o_ref.dtype)


def _norm_proj(x, g, w, *, tm):
    t, d = x.shape
    n = w.shape[1]
    return pl.pallas_call(
        _norm_proj_kernel,
        grid=(t // tm,),
        in_specs=[pl.BlockSpec((tm, d), lambda i: (i, 0)), _const_spec((1, d)),
                  _const_spec((d, n))],
        out_specs=pl.BlockSpec((tm, n), lambda i: (i, 0)),
        out_shape=jax.ShapeDtypeStruct((t, n), BF16),
        compiler_params=pltpu.CompilerParams(
            dimension_semantics=("arbitrary",), vmem_limit_bytes=VMEM_LIMIT),
        name="norm_proj",
    )(x, g, w)


GROUPS_PER_SLAB = LANES // SSM_GROUP


def _piece_transpose(arrs, piece):
    arrs = list(arrs)
    dist = GROUPS_PER_SLAB // 2
    while dist >= 1:
        keep = (piece & dist) == 0
        shift = SSM_GROUP * dist
        for i in range(GROUPS_PER_SLAB):
            if i & dist:
                continue
            a, b = arrs[i], arrs[i + dist]
            arrs[i] = jnp.where(keep, a, pltpu.roll(b, shift, axis=1))
            arrs[i + dist] = jnp.where(keep, pltpu.roll(a, LANES - shift, axis=1), b)
        dist //= 2
    return arrs


def _ssm_kernel(u_ref, perm_ref, cc_ref, bt_ref, pw1_ref, pw2_ref, zoh_ref, a1_ref, a2_ref,
                dv_ref, z_ref, ug_scr, zg_scr, mi_scr, ws_scr, wo_scr, *, n_chunks):
    L = SSM_CHUNK
    rows = u_ref.shape[0] // L
    w = L * SSM_GROUP
    half = SSM_STATE
    piece = lax.broadcasted_iota(jnp.int32, (L, LANES), 1) // SSM_GROUP
    unroll = math.gcd(rows // L, 8)

    def relayout_in(rb, carry):
        t0 = pl.multiple_of(rb * w, w)
        r0 = pl.multiple_of(rb * L, L)
        by_pos = jnp.dot(perm_ref[...], u_ref[pl.ds(t0, w), :], preferred_element_type=F32)
        for hf in range(L // GROUPS_PER_SLAB):
            arrs = [by_pos[L * (GROUPS_PER_SLAB * hf + k):L * (GROUPS_PER_SLAB * hf + k + 1)]
                    for k in range(GROUPS_PER_SLAB)]
            arrs = _piece_transpose(arrs, piece)
            for g in range(GROUPS_PER_SLAB):
                ug_scr[g, pl.ds(r0, L), LANES * hf:LANES * (hf + 1)] = arrs[g].astype(BF16)
        return carry

    lax.fori_loop(0, rows // L, relayout_in, 0, unroll=unroll)

    chunk = lax.broadcasted_iota(jnp.int32, (rows, LANES), 0) % n_chunks
    lane_w = lax.broadcasted_iota(jnp.int32, (SSM_GROUP, w), 1)
    sign = jnp.where(lax.broadcasted_iota(jnp.int32, (1, LANES), 1) < half, 1.0, -1.0)

    def group(g, carry):
        cc = cc_ref[g]
        bt = bt_ref[g]
        pw1 = pw1_ref[g]
        pw2 = pw2_ref[g]
        zoh = zoh_ref[g]
        bbar = bt * zoh[0:1] + pltpu.roll(bt, half, axis=1) * zoh[1:2]
        bbar_sw = pltpu.roll(bbar, half, axis=1)
        cc_sw = pltpu.roll(cc, half, axis=1)
        ca = [cc * pw1[k:k + 1] + cc_sw * pw2[k:k + 1] for k in range(L + 1)]
        k_all = lax.dot_general(bbar * sign, jnp.concatenate(ca[:L], axis=0), NT_DIMS,
                                precision=lax.Precision.HIGHEST, preferred_element_type=F32)
        for s in range(L):
            blk = k_all if s == 0 else jnp.where(
                lane_w >= SSM_GROUP * s, pltpu.roll(k_all, SSM_GROUP * s, axis=1), 0.0)
            mi_scr[SSM_GROUP * s:SSM_GROUP * (s + 1), :] = blk.astype(BF16)
            k = L - 1 - s
            ws_scr[SSM_GROUP * s:SSM_GROUP * (s + 1), :] = (
                bbar * pw1[k:k + 1] + bbar_sw * pw2[k:k + 1]).astype(BF16)
            wo_scr[SSM_GROUP * s:SSM_GROUP * (s + 1), :] = (ca[s + 1] * sign).astype(BF16)

        u = ug_scr[g]
        y = jnp.dot(u, mi_scr[...], preferred_element_type=F32)
        xs = jnp.dot(u, ws_scr[...], preferred_element_type=F32)
        a1 = a1_ref[g]
        a2 = a2_ref[g]
        step = 1
        j = 0
        while step < n_chunks:
            sh = jnp.where(chunk >= step, pltpu.roll(xs, step, axis=0), 0.0)
            xs = xs + sh * a1[j:j + 1] + pltpu.roll(sh, half, axis=1) * a2[j:j + 1]
            step *= 2
            j += 1
        x_in = jnp.where(chunk >= 1, pltpu.roll(xs, 1, axis=0), 0.0)
        y = y + lax.dot_general(x_in.astype(BF16), wo_scr[...], NT_DIMS,
                                preferred_element_type=F32)
        y = y + u.astype(F32) * dv_ref[g]
        zg_scr[g] = jax.nn.gelu(y).astype(BF16)
        return carry

    lax.fori_loop(0, GROUPS_PER_SLAB, group, 0)

    def relayout_out(rb, carry):
        t0 = pl.multiple_of(rb * w, w)
        r0 = pl.multiple_of(rb * L, L)
        by_pos = []
        for hf in range(L // GROUPS_PER_SLAB):
            arrs = [zg_scr[g, pl.ds(r0, L), LANES * hf:LANES * (hf + 1)].astype(F32)
                    for g in range(GROUPS_PER_SLAB)]
            by_pos += _piece_transpose(arrs, piece)
        by_pos = jnp.concatenate(by_pos, axis=0).astype(BF16)
        z_ref[pl.ds(t0, w), :] = jnp.dot(perm_ref[...], by_pos,
                                         preferred_element_type=F32).astype(BF16)
        return carry

    lax.fori_loop(0, rows // L, relayout_out, 0, unroll=unroll)


def _ssm_scan(u, perm, cc, bt, pw1, pw2, zoh, a1, a2, dv, *, n_chunks):
    t, d = u.shape
    L = SSM_CHUNK
    rows = t // L
    w = L * SSM_GROUP
    gps = GROUPS_PER_SLAB
    slab = lambda j: (j, 0, 0)
    pspec = lambda arr: pl.BlockSpec((gps,) + arr.shape[1:], slab)
    return pl.pallas_call(
        functools.partial(_ssm_kernel, n_chunks=n_chunks),
        grid=(d // LANES,),
        in_specs=[pl.BlockSpec((t, LANES), lambda j: (0, j)), _const_spec((w, w)),
                  pspec(cc), pspec(bt), pspec(pw1), pspec(pw2), pspec(zoh), pspec(a1), pspec(a2),
                  pspec(dv)],
        out_specs=pl.BlockSpec((t, LANES), lambda j: (0, j)),
        out_shape=jax.ShapeDtypeStruct((t, d), BF16),
        scratch_shapes=[pltpu.VMEM((gps, rows, w), BF16), pltpu.VMEM((gps, rows, w), BF16),
                        pltpu.VMEM((w, w), BF16), pltpu.VMEM((w, LANES), BF16),
                        pltpu.VMEM((w, LANES), BF16)],
        compiler_params=pltpu.CompilerParams(
            dimension_semantics=("arbitrary",), vmem_limit_bytes=VMEM_LIMIT),
        name="s5_scan",
    )(u, perm, cc, bt, pw1, pw2, zoh, a1, a2, dv)


def _ssm_operators(a_re, a_im, b_re, b_im, c_re, c_im, log_dt, d_skip, n_chunks):
    L = SSM_CHUNK
    g, p = a_re.shape
    dt = jnp.exp(log_dt)[:, None]
    lam_re, lam_im = dt * a_re, dt * a_im

    def powers(ks):
        ks = jnp.asarray(ks, F32)[:, None, None]
        mag = jnp.exp(ks * lam_re)
        return mag * jnp.cos(ks * lam_im), mag * jnp.sin(ks * lam_im)

    def patterns(re, im):
        return (jnp.concatenate([re, re], axis=2).transpose(1, 0, 2),
                jnp.concatenate([-im, im], axis=2).transpose(1, 0, 2))

    pr, pi = powers(range(L + 1))
    num_re, num_im = pr[1] - 1.0, pi[1]
    den = a_re * a_re + a_im * a_im
    s_re = (num_re * a_re + num_im * a_im) / den
    s_im = (num_im * a_re - num_re * a_im) / den
    pw1, pw2 = patterns(pr, pi)
    zoh = jnp.concatenate(patterns(s_re[None], s_im[None]), axis=1)
    n_steps = max(1, int(math.log2(n_chunks)))
    a1, a2 = patterns(*powers([L * 2 ** j for j in range(n_steps)]))
    cc = jnp.concatenate([c_re, c_im], axis=2)
    bt = jnp.concatenate([b_re.transpose(0, 2, 1), b_im.transpose(0, 2, 1)], axis=2)
    dv = jnp.tile(d_skip.reshape(g, 1, SSM_GROUP), (1, L, 1)).reshape(g, 1, L * SSM_GROUP)
    return cc, bt, pw1, pw2, zoh, a1, a2, dv


def kernel(x, norm_mix_g, norm_mlp_g, attn_w_in, attn_b_f, attn_q_g, attn_k_g, attn_w_out,
           ssm_w_in, ssm_a_re, ssm_a_im, ssm_b_re, ssm_b_im, ssm_c_re, ssm_c_im, ssm_log_dt,
           ssm_d, ssm_w_glu, mlp_w1, mlp_w2):
    b, s, d = x.shape
    t = b * s
    tk = MXU_DIM
    tq = min(4 * tk, s)
    tm_proj = min(512, s)
    tm_mlp = min(512, t)

    w_in = attn_w_in[0]
    w_f = jnp.pad(w_in[:, 3 * d:], ((0, 0), (0, LANES - N_HEADS)))
    wqkf = jnp.concatenate([w_in[:, :2 * d], w_f], axis=1).astype(BF16)
    wvt = w_in[:, 2 * d:3 * d].T.astype(BF16)
    blk = jnp.arange(MXU_DIM) // HEAD_DIM
    bd = (blk[:, None] == blk[None, :]).astype(BF16)
    gain = jnp.concatenate([jnp.tile(attn_q_g[0], N_HEADS) * (LOG2E / math.sqrt(HEAD_DIM)),
                            jnp.tile(attn_k_g[0], N_HEADS)]).reshape(1, 2 * d)
    bfp = jnp.pad(attn_b_f[0], (0, LANES - N_HEADS)).reshape(1, LANES)
    tri = (jnp.arange(tm_proj)[:, None] >= jnp.arange(tm_proj)[None, :]).astype(F32)
    heads = jnp.arange(N_HEADS)
    place = jnp.zeros((LANES, d), F32)
    for role in range(2):
        for i in range(BIAS_PIECES):
            place = place.at[N_HEADS * (BIAS_PIECES * role + i) + heads,
                             (heads // HEADS_PER_STEP) * LANES + BIAS_Q_LANE * role
                             + BIAS_PIECES * (heads % HEADS_PER_STEP) + i].set(1.0)
    shift = (LOG2E * math.sqrt(HEAD_DIM) * jnp.max(jnp.abs(attn_q_g[0]))
             * jnp.max(jnp.abs(attn_k_g[0])))
    q, k, vt, kb = _qkv_proj(x, norm_mix_g[0].reshape(1, d), wqkf, wvt, bd, gain, bfp, tri,
                             place.astype(BF16), jnp.full((1, LANES), shift, F32),
                             tm=tm_proj, tk=tk)
    o = lax.cond(shift <= MAX_SHIFT,
                 functools.partial(_attention, tq=tq, tk=tk, online=False),
                 functools.partial(_attention, tq=tq, tk=tk, online=True),
                 q, k, kb, vt)
    x2 = _mix_mlp(x.reshape(t, d), o.reshape(t, d), attn_w_out[0].astype(BF16),
                  norm_mlp_g[0].reshape(1, d), mlp_w1[0].astype(BF16), mlp_w2[0].astype(BF16),
                  glu=False, tm=tm_mlp)

    u = _norm_proj(x2, norm_mix_g[1].reshape(1, d), ssm_w_in[0].astype(BF16), tm=tm_mlp)
    n_chunks = s // SSM_CHUNK
    ops = _ssm_operators(ssm_a_re[0], ssm_a_im[0], ssm_b_re[0], ssm_b_im[0], ssm_c_re[0],
                         ssm_c_im[0], ssm_log_dt[0], ssm_d[0], n_chunks)
    tok = jnp.arange(SSM_CHUNK * SSM_CHUNK)
    perm = (tok[None, :] == (tok[:, None] % SSM_CHUNK) * SSM_CHUNK + tok[:, None] // SSM_CHUNK)
    z = _ssm_scan(u, perm.astype(BF16), *ops, n_chunks=n_chunks)
    x3 = _mix_mlp(x2, z, ssm_w_glu[0].astype(BF16), norm_mlp_g[1].reshape(1, d),
                  mlp_w1[1].astype(BF16), mlp_w2[1].astype(BF16), glu=True, tm=tm_mlp)
    return x3.reshape(b, s, d)
```

```python
import functools
import math

import jax
import jax.numpy as jnp
from jax import lax
from jax.experimental import pallas as pl
from jax.experimental.pallas import tpu as pltpu

F32 = jnp.float32
BF16 = jnp.bfloat16

N_HEADS = 16
HEAD_DIM = 64
SSM_GROUP = 16
SSM_STATE = 64
SSM_CHUNK = 16
EPS = 1e-6
LOG2E = 1.4426950408889634

LANES = 128
MXU_DIM = 256
HEADS_PER_STEP = LANES // HEAD_DIM
BIAS_PIECES = 3
VMEM_LIMIT = 56 * 1024 * 1024

NT_DIMS = (((1,), (1,)), ((), ()))


def _rms_norm(x, g):
    ms = jnp.mean(x * x, axis=-1, keepdims=True)
    return x * lax.rsqrt(ms + EPS) * g


def _const_spec(shape):
    zeros = (0,) * len(shape)
    return pl.BlockSpec(shape, lambda *_: zeros)


def _bf16_pieces(val):
    pieces = jnp.zeros_like(val)
    rem = val
    for n in range(BIAS_PIECES):
        piece = rem.astype(BF16).astype(F32)
        rem = rem - piece
        pieces = pieces + (piece if n == 0 else pltpu.roll(piece, N_HEADS * n, axis=1))
    return pieces


def _qkv_kernel(x_ref, g_ref, wkf_ref, wqvt_ref, bd_ref, gain_k_ref, gain_qt_ref, bf_ref, tri_ref,
                place_k_ref, place_qt_ref, shift_ref,
                qt_ref, k_ref, vt_ref, kb_ref, qbt_ref, carry_ref, *, tm, tk, d):
    @pl.when(pl.program_id(1) == 0)
    def _():
        carry_ref[...] = jnp.zeros_like(carry_ref)

    h = _rms_norm(x_ref[0], g_ref[...]).astype(BF16)
    inv_hd = 1.0 / HEAD_DIM

    y = jnp.dot(h, wkf_ref[...], preferred_element_type=F32)
    for t in range(d // MXU_DIM):
        sl = slice(MXU_DIM * t, MXU_DIM * (t + 1))
        tile = y[:, sl]
        ss = jnp.dot((tile * tile).astype(BF16), bd_ref[...], preferred_element_type=F32)
        k_ref[0, :, sl] = (tile * lax.rsqrt(ss * inv_hd + EPS) * gain_k_ref[:, sl]).astype(BF16)

    yt = lax.dot_general(wqvt_ref[...], h, NT_DIMS, preferred_element_type=F32)
    for t in range(d // MXU_DIM):
        sl = slice(MXU_DIM * t, MXU_DIM * (t + 1))
        tile = yt[sl, :]
        ss = jnp.dot(bd_ref[...], (tile * tile).astype(BF16), preferred_element_type=F32)
        gain = jnp.tile(gain_qt_ref[sl, :], (1, tm // LANES))
        qt_ref[0, sl, :] = (tile * lax.rsqrt(ss * inv_hd + EPS) * gain).astype(BF16)
    for j in range(tm // tk):
        vt_ref[0, j] = yt[d:, tk * j:tk * (j + 1)].astype(BF16)

    f = y[:, d:] + bf_ref[...]
    log_f = jnp.minimum(f, 0.0) - jnp.log1p(jnp.exp(-jnp.abs(f)))
    cs = jnp.dot(tri_ref[...], log_f, precision=lax.Precision.HIGHEST,
                 preferred_element_type=F32) + carry_ref[...]
    carry_ref[...] = cs[tm - 1:tm, :]
    lane = lax.broadcasted_iota(jnp.int32, cs.shape, 1)
    c2 = jnp.where(lane < N_HEADS, cs * LOG2E, 0.0)
    kb_ref[0] = jnp.dot(_bf16_pieces(-c2).astype(BF16), place_k_ref[...],
                        preferred_element_type=F32).astype(BF16)
    q_pieces = _bf16_pieces(jnp.where(lane < N_HEADS, c2 - shift_ref[...], 0.0))
    qbt_ref[0] = jnp.dot(place_qt_ref[...], q_pieces.T.astype(BF16),
                         preferred_element_type=F32).astype(BF16)


def _qkv_proj(x, g, wkf, wqvt, bd, gain_k, gain_qt, bfp, tri, place_k, place_qt, shift, *, tm, tk):
    b, s, d = x.shape
    row = lambda bi, si: (bi, si, 0)
    col = lambda bi, si: (bi, 0, si)
    consts = (g, wkf, wqvt, bd, gain_k, gain_qt, bfp, tri, place_k, place_qt, shift)
    return pl.pallas_call(
        functools.partial(_qkv_kernel, tm=tm, tk=tk, d=d),
        grid=(b, s // tm),
        in_specs=[pl.BlockSpec((1, tm, d), row)] + [_const_spec(c.shape) for c in consts],
        out_specs=[pl.BlockSpec((1, d, tm), col),
                   pl.BlockSpec((1, tm, d), row),
                   pl.BlockSpec((1, tm // tk, d, tk), lambda bi, si: (bi, si, 0, 0)),
                   pl.BlockSpec((1, tm, d), row),
                   pl.BlockSpec((1, d, tm), col)],
        out_shape=[jax.ShapeDtypeStruct((b, d, s), BF16),
                   jax.ShapeDtypeStruct((b, s, d), BF16),
                   jax.ShapeDtypeStruct((b, s // tk, d, tk), BF16),
                   jax.ShapeDtypeStruct((b, s, d), BF16),
                   jax.ShapeDtypeStruct((b, d, s), BF16)],
        scratch_shapes=[pltpu.VMEM((1, LANES), F32)],
        compiler_params=pltpu.CompilerParams(
            dimension_semantics=("arbitrary", "arbitrary"), vmem_limit_bytes=VMEM_LIMIT),
        name="qkv_proj",
    )(x, *consts)


BIAS_Q_LANE = 8
MAX_SHIFT = 48.0


def _attn_kernel(qt_ref, qbt_ref, k_ref, kb_ref, vt_ref, o_ref, s0_scr, s1_scr, *, tq, tk, online):
    i = pl.program_id(2)
    n_sub = tq // tk
    qt = qt_ref[0]
    qbt = qbt_ref[0]
    row = lax.broadcasted_iota(jnp.int32, (LANES, tq), 0)
    lane_k = lax.broadcasted_iota(jnp.int32, (tk, LANES), 1)
    q_cols = (lane_k >= BIAS_Q_LANE) & (lane_k < BIAS_Q_LANE + HEADS_PER_STEP * BIAS_PIECES)
    causal = (lax.broadcasted_iota(jnp.int32, (tk, tk), 0)
              <= lax.broadcasted_iota(jnp.int32, (tk, tk), 1))
    one = jnp.ones((), BF16)
    zero = jnp.zeros((), BF16)
    outs = []
    for hh in range(HEADS_PER_STEP):
        q_h = jnp.where((row >= HEAD_DIM * hh) & (row < HEAD_DIM * (hh + 1)), qt, zero)
        k_side = (row >= BIAS_PIECES * hh) & (row < BIAS_PIECES * (hh + 1))
        q_lo = BIAS_Q_LANE + BIAS_PIECES * hh
        q_side = (row >= q_lo) & (row < q_lo + BIAS_PIECES)
        qat = jnp.concatenate([q_h, jnp.where(k_side, one, jnp.where(q_side, qbt, zero))],
                              axis=0)

        def scores(j, qat_cols):
            off = pl.multiple_of(j * tk, tk)
            kbias = jnp.where(q_cols, one, kb_ref[0, pl.ds(off, tk), :])
            ka = jnp.concatenate([k_ref[0, pl.ds(off, tk), :], kbias], axis=1)
            return jnp.dot(ka, qat_cols, preferred_element_type=F32)

        def consume(carry, s, j, diagonal):
            if diagonal:
                s_tri = jnp.where(causal, s[:, :tk], -1e30)
                s = s_tri if s.shape[1] == tk else jnp.concatenate([s_tri, s[:, tk:]], axis=1)
            vt = vt_ref[0, j, HEAD_DIM * hh:HEAD_DIM * (hh + 1), :]
            if online:
                m, l, acc = carry
                m_new = jnp.maximum(m, jnp.max(s, axis=0, keepdims=True))
                alpha = jnp.exp2(m - m_new)
                p = jnp.exp2(s - m_new)
                l = alpha * l + jnp.sum(p, axis=0, keepdims=True)
                acc = alpha * acc + jnp.dot(vt, p.astype(BF16), preferred_element_type=F32)
                return m_new, l, acc
            l, acc = carry
            p = jnp.exp2(s)
            l = l + jnp.sum(p, axis=0, keepdims=True)
            acc = acc + jnp.dot(vt, p.astype(BF16), preferred_element_type=F32)
            return l, acc

        carry = (jnp.zeros((1, tq), F32), jnp.zeros((HEAD_DIM, tq), F32))
        if online:
            carry = (jnp.full((1, tq), -1e30, F32),) + carry

        s0_scr[...] = scores(0, qat)

        def pair(jj, c):
            s1_scr[...] = scores(2 * jj + 1, qat)
            c = consume(c, s0_scr[...], 2 * jj, False)
            s0_scr[...] = scores(2 * jj + 2, qat)
            return consume(c, s1_scr[...], 2 * jj + 1, False)

        carry = lax.fori_loop(0, n_sub * i // 2, pair, carry)
        carry = consume(carry, s0_scr[...], n_sub * i, True)
        for dd in range(1, n_sub):
            c0 = tk * dd
            j = n_sub * i + dd
            part = consume(tuple(c[:, c0:] for c in carry), scores(j, qat[:, c0:]), j, True)
            carry = tuple(jnp.concatenate([c[:, :c0], pc], axis=1) for c, pc in zip(carry, part))
        outs.append(carry[-1] / carry[-2])
    o_t = jnp.concatenate(outs, axis=0)
    o_ref[0] = o_t.T.astype(BF16)


def _attention(qt, qbt, k, kb, vt, *, tq, tk, online):
    b, s, d = k.shape
    assert (tq // tk) % 2 == 0 and s % tq == 0
    n_pairs = d // LANES
    nk = s // tk
    qt_blk = pl.BlockSpec((1, LANES, tq), lambda bi, hp, i: (bi, hp, i))
    k_all = pl.BlockSpec((1, s, LANES), lambda bi, hp, i: (bi, 0, hp))
    return pl.pallas_call(
        functools.partial(_attn_kernel, tq=tq, tk=tk, online=online),
        grid=(b, n_pairs, s // tq),
        in_specs=[qt_blk, qt_blk, k_all, k_all,
                  pl.BlockSpec((1, nk, LANES, tk), lambda bi, hp, i: (bi, 0, hp, 0))],
        out_specs=pl.BlockSpec((1, tq, LANES), lambda bi, hp, i: (bi, i, hp)),
        out_shape=jax.ShapeDtypeStruct((b, s, d), BF16),
        scratch_shapes=[pltpu.VMEM((tk, tq), F32), pltpu.VMEM((tk, tq), F32)],
        compiler_params=pltpu.CompilerParams(
            dimension_semantics=("arbitrary", "arbitrary", "arbitrary"),
            vmem_limit_bytes=VMEM_LIMIT),
        name="fox_attention_online" if online else "fox_attention",
    )(qt, qbt, k, kb, vt)


def _mix_mlp_kernel(x_ref, a_ref, wmix_ref, g_ref, w1_ref, w2_ref, o_ref, *, glu, d, ff_chunk):
    mix = jnp.dot(a_ref[...], wmix_ref[...], preferred_element_type=F32)
    if glu:
        mix = mix[:, :d] * jax.nn.sigmoid(mix[:, d:])
    x1 = x_ref[...] + mix
    h = _rms_norm(x1, g_ref[...]).astype(BF16)
    acc = x1
    for c in range(w1_ref.shape[1] // ff_chunk):
        sl = slice(ff_chunk * c, ff_chunk * (c + 1))
        hid = jnp.maximum(jnp.dot(h, w1_ref[:, sl], preferred_element_type=F32), 0.0)
        acc = acc + jnp.dot((hid * hid).astype(BF16), w2_ref[sl, :], preferred_element_type=F32)
    o_ref[...] = acc


def _mix_mlp(x, a, wmix, g, w1, w2, *, glu, tm, ff_chunk=1024):
    t, d = x.shape
    row = lambda i: (i, 0)
    single = pl.Buffered(1)
    wspec = lambda shape: pl.BlockSpec(shape, lambda i: (0, 0), pipeline_mode=single)
    return pl.pallas_call(
        functools.partial(_mix_mlp_kernel, glu=glu, d=d, ff_chunk=ff_chunk),
        grid=(t // tm,),
        in_specs=[pl.BlockSpec((tm, d), row), pl.BlockSpec((tm, d), row),
                  wspec(wmix.shape), wspec((1, d)), wspec(w1.shape), wspec(w2.shape)],
        out_specs=pl.BlockSpec((tm, d), row),
        out_shape=jax.ShapeDtypeStruct((t, d), F32),
        compiler_params=pltpu.CompilerParams(
            dimension_semantics=("arbitrary",), vmem_limit_bytes=VMEM_LIMIT),
        name="mix_glu_mlp" if glu else "mix_mlp",
    )(x, a, wmix, g, w1, w2)


def _norm_proj_kernel(x_ref, g_ref, w_ref, o_ref):
    h = _rms_norm(x_ref[...], g_ref[...]).astype(BF16)
    o_ref[...] = jnp.dot(h, w_ref[...], preferred_element_type=F32).astype(o_ref.dtype)


def _norm_proj(x, g, w, *, tm):
    t, d = x.shape
    n = w.shape[1]
    return pl.pallas_call(
        _norm_proj_kernel,
        grid=(t // tm,),
        in_specs=[pl.BlockSpec((tm, d), lambda i: (i, 0)), _const_spec((1, d)),
                  _const_spec((d, n))],
        out_specs=pl.BlockSpec((tm, n), lambda i: (i, 0)),
        out_shape=jax.ShapeDtypeStruct((t, n), BF16),
        compiler_params=pltpu.CompilerParams(
            dimension_semantics=("arbitrary",), vmem_limit_bytes=VMEM_LIMIT),
        name="norm_proj",
    )(x, g, w)


GROUPS_PER_SLAB = LANES // SSM_GROUP


def _piece_transpose(arrs, piece):
    arrs = list(arrs)
    dist = GROUPS_PER_SLAB // 2
    while dist >= 1:
        keep = (piece & dist) == 0
        shift = SSM_GROUP * dist
        for i in range(GROUPS_PER_SLAB):
            if i & dist:
                continue
            a, b = arrs[i], arrs[i + dist]
            arrs[i] = jnp.where(keep, a, pltpu.roll(b, shift, axis=1))
            arrs[i + dist] = jnp.where(keep, pltpu.roll(a, LANES - shift, axis=1), b)
        dist //= 2
    return arrs


def _ssm_kernel(u_ref, perm_ref, cc_ref, bt_ref, pw1_ref, pw2_ref, zoh_ref, a1_ref, a2_ref,
                dv_ref, z_ref, ug_scr, zg_scr, mi_scr, ws_scr, wo_scr, *, n_chunks):
    L = SSM_CHUNK
    rows = u_ref.shape[0] // L
    w = L * SSM_GROUP
    half = SSM_STATE
    piece = lax.broadcasted_iota(jnp.int32, (L, LANES), 1) // SSM_GROUP
    unroll = math.gcd(rows // L, 8)

    def relayout_in(rb, carry):
        t0 = pl.multiple_of(rb * w, w)
        r0 = pl.multiple_of(rb * L, L)
        by_pos = jnp.dot(perm_ref[...], u_ref[pl.ds(t0, w), :], preferred_element_type=F32)
        for hf in range(L // GROUPS_PER_SLAB):
            arrs = [by_pos[L * (GROUPS_PER_SLAB * hf + k):L * (GROUPS_PER_SLAB * hf + k + 1)]
                    for k in range(GROUPS_PER_SLAB)]
            arrs = _piece_transpose(arrs, piece)
            for g in range(GROUPS_PER_SLAB):
                ug_scr[g, pl.ds(r0, L), LANES * hf:LANES * (hf + 1)] = arrs[g].astype(BF16)
        return carry

    lax.fori_loop(0, rows // L, relayout_in, 0, unroll=unroll)

    chunk = lax.broadcasted_iota(jnp.int32, (rows, LANES), 0) % n_chunks
    lane_w = lax.broadcasted_iota(jnp.int32, (SSM_GROUP, w), 1)
    sign = jnp.where(lax.broadcasted_iota(jnp.int32, (1, LANES), 1) < half, 1.0, -1.0)

    def group(g, carry):
        cc = cc_ref[g]
        bt = bt_ref[g]
        pw1 = pw1_ref[g]
        pw2 = pw2_ref[g]
        zoh = zoh_ref[g]
        bbar = bt * zoh[0:1] + pltpu.roll(bt, half, axis=1) * zoh[1:2]
        bbar_sw = pltpu.roll(bbar, half, axis=1)
        cc_sw = pltpu.roll(cc, half, axis=1)
        ca = [cc * pw1[k:k + 1] + cc_sw * pw2[k:k + 1] for k in range(L + 1)]
        k_all = lax.dot_general(bbar * sign, jnp.concatenate(ca[:L], axis=0), NT_DIMS,
                                precision=lax.Precision.HIGHEST, preferred_element_type=F32)
        for s in range(L):
            blk = k_all if s == 0 else jnp.where(
                lane_w >= SSM_GROUP * s, pltpu.roll(k_all, SSM_GROUP * s, axis=1), 0.0)
            mi_scr[SSM_GROUP * s:SSM_GROUP * (s + 1), :] = blk.astype(BF16)
            k = L - 1 - s
            ws_scr[SSM_GROUP * s:SSM_GROUP * (s + 1), :] = (
                bbar * pw1[k:k + 1] + bbar_sw * pw2[k:k + 1]).astype(BF16)
            wo_scr[SSM_GROUP * s:SSM_GROUP * (s + 1), :] = (ca[s + 1] * sign).astype(BF16)

        u = ug_scr[g]
        y = jnp.dot(u, mi_scr[...], preferred_element_type=F32)
        xs = jnp.dot(u, ws_scr[...], preferred_element_type=F32)
        a1 = a1_ref[g]
        a2 = a2_ref[g]
        step = 1
        j = 0
        while step < n_chunks:
            sh = jnp.where(chunk >= step, pltpu.roll(xs, step, axis=0), 0.0)
            xs = xs + sh * a1[j:j + 1] + pltpu.roll(sh, half, axis=1) * a2[j:j + 1]
            step *= 2
            j += 1
        x_in = jnp.where(chunk >= 1, pltpu.roll(xs, 1, axis=0), 0.0)
        y = y + lax.dot_general(x_in.astype(BF16), wo_scr[...], NT_DIMS,
                                preferred_element_type=F32)
        y = y + u.astype(F32) * dv_ref[g]
        zg_scr[g] = jax.nn.gelu(y).astype(BF16)
        return carry

    lax.fori_loop(0, GROUPS_PER_SLAB, group, 0)

    def relayout_out(rb, carry):
        t0 = pl.multiple_of(rb * w, w)
        r0 = pl.multiple_of(rb * L, L)
        by_pos = []
        for hf in range(L // GROUPS_PER_SLAB):
            arrs = [zg_scr[g, pl.ds(r0, L), LANES * hf:LANES * (hf + 1)].astype(F32)
                    for g in range(GROUPS_PER_SLAB)]
            by_pos += _piece_transpose(arrs, piece)
        by_pos = jnp.concatenate(by_pos, axis=0).astype(BF16)
        z_ref[pl.ds(t0, w), :] = jnp.dot(perm_ref[...], by_pos,
                                         preferred_element_type=F32).astype(BF16)
        return carry

    lax.fori_loop(0, rows // L, relayout_out, 0, unroll=unroll)


def _ssm_scan(u, perm, cc, bt, pw1, pw2, zoh, a1, a2, dv, *, n_chunks):
    t, d = u.shape
    L = SSM_CHUNK
    rows = t // L
    w = L * SSM_GROUP
    gps = GROUPS_PER_SLAB
    slab = lambda j: (j, 0, 0)
    pspec = lambda arr: pl.BlockSpec((gps,) + arr.shape[1:], slab)
    return pl.pallas_call(
        functools.partial(_ssm_kernel, n_chunks=n_chunks),
        grid=(d // LANES,),
        in_specs=[pl.BlockSpec((t, LANES), lambda j: (0, j)), _const_spec((w, w)),
                  pspec(cc), pspec(bt), pspec(pw1), pspec(pw2), pspec(zoh), pspec(a1), pspec(a2),
                  pspec(dv)],
        out_specs=pl.BlockSpec((t, LANES), lambda j: (0, j)),
        out_shape=jax.ShapeDtypeStruct((t, d), BF16),
        scratch_shapes=[pltpu.VMEM((gps, rows, w), BF16), pltpu.VMEM((gps, rows, w), BF16),
                        pltpu.VMEM((w, w), BF16), pltpu.VMEM((w, LANES), BF16),
                        pltpu.VMEM((w, LANES), BF16)],
        compiler_params=pltpu.CompilerParams(
            dimension_semantics=("arbitrary",), vmem_limit_bytes=VMEM_LIMIT),
        name="s5_scan",
    )(u, perm, cc, bt, pw1, pw2, zoh, a1, a2, dv)


def _ssm_operators(a_re, a_im, b_re, b_im, c_re, c_im, log_dt, d_skip, n_chunks):
    L = SSM_CHUNK
    g, p = a_re.shape
    dt = jnp.exp(log_dt)[:, None]
    lam_re, lam_im = dt * a_re, dt * a_im

    def powers(ks):
        ks = jnp.asarray(ks, F32)[:, None, None]
        mag = jnp.exp(ks * lam_re)
        return mag * jnp.cos(ks * lam_im), mag * jnp.sin(ks * lam_im)

    def patterns(re, im):
        return (jnp.concatenate([re, re], axis=2).transpose(1, 0, 2),
                jnp.concatenate([-im, im], axis=2).transpose(1, 0, 2))

    pr, pi = powers(range(L + 1))
    num_re, num_im = pr[1] - 1.0, pi[1]
    den = a_re * a_re + a_im * a_im
    s_re = (num_re * a_re + num_im * a_im) / den
    s_im = (num_im * a_re - num_re * a_im) / den
    pw1, pw2 = patterns(pr, pi)
    zoh = jnp.concatenate(patterns(s_re[None], s_im[None]), axis=1)
    n_steps = max(1, int(math.log2(n_chunks)))
    a1, a2 = patterns(*powers([L * 2 ** j for j in range(n_steps)]))
    cc = jnp.concatenate([c_re, c_im], axis=2)
    bt = jnp.concatenate([b_re.transpose(0, 2, 1), b_im.transpose(0, 2, 1)], axis=2)
    dv = jnp.tile(d_skip.reshape(g, 1, SSM_GROUP), (1, L, 1)).reshape(g, 1, L * SSM_GROUP)
    return cc, bt, pw1, pw2, zoh, a1, a2, dv


def kernel(x, norm_mix_g, norm_mlp_g, attn_w_in, attn_b_f, attn_q_g, attn_k_g, attn_w_out,
           ssm_w_in, ssm_a_re, ssm_a_im, ssm_b_re, ssm_b_im, ssm_c_re, ssm_c_im, ssm_log_dt,
           ssm_d, ssm_w_glu, mlp_w1, mlp_w2):
    b, s, d = x.shape
    t = b * s
    tk = min(512, s // 2)
    tq = 2 * tk
    tm_proj = min(512, s)
    tm_mlp = min(512, t)

    w_in = attn_w_in[0]
    w_f = jnp.pad(w_in[:, 3 * d:], ((0, 0), (0, LANES - N_HEADS)))
    wkf = jnp.concatenate([w_in[:, d:2 * d], w_f], axis=1).astype(BF16)
    wqvt = jnp.concatenate([w_in[:, :d], w_in[:, 2 * d:3 * d]], axis=1).T.astype(BF16)
    blk = jnp.arange(MXU_DIM) // HEAD_DIM
    bd = (blk[:, None] == blk[None, :]).astype(BF16)
    gain_k = jnp.tile(attn_k_g[0], N_HEADS).reshape(1, d)
    gain_qt = jnp.broadcast_to(
        (jnp.tile(attn_q_g[0], N_HEADS) * (LOG2E / math.sqrt(HEAD_DIM)))[:, None], (d, LANES))
    bfp = jnp.pad(attn_b_f[0], (0, LANES - N_HEADS)).reshape(1, LANES)
    tri = (jnp.arange(tm_proj)[:, None] >= jnp.arange(tm_proj)[None, :]).astype(F32)
    heads = jnp.arange(N_HEADS)
    place_k = jnp.zeros((LANES, d), F32)
    for i in range(BIAS_PIECES):
        place_k = place_k.at[N_HEADS * i + heads,
                             (heads // HEADS_PER_STEP) * LANES
                             + BIAS_PIECES * (heads % HEADS_PER_STEP) + i].set(1.0)
    place_qt = jnp.roll(place_k, BIAS_Q_LANE, axis=1).T
    shift = (LOG2E * math.sqrt(HEAD_DIM) * jnp.max(jnp.abs(attn_q_g[0]))
             * jnp.max(jnp.abs(attn_k_g[0])))
    qt, k, vt, kb, qbt = _qkv_proj(
        x, norm_mix_g[0].reshape(1, d), wkf, wqvt, bd, gain_k, gain_qt, bfp, tri,
        place_k.astype(BF16), place_qt.astype(BF16), jnp.full((1, LANES), shift, F32),
        tm=tm_proj, tk=tk)
    o = lax.cond(shift <= MAX_SHIFT,
                 functools.partial(_attention, tq=tq, tk=tk, online=False),
                 functools.partial(_attention, tq=tq, tk=tk, online=True),
                 qt, qbt, k, kb, vt)
    x2 = _mix_mlp(x.reshape(t, d), o.reshape(t, d), attn_w_out[0].astype(BF16),
                  norm_mlp_g[0].reshape(1, d), mlp_w1[0].astype(BF16), mlp_w2[0].astype(BF16),
                  glu=False, tm=tm_mlp)

    u = _norm_proj(x2, norm_mix_g[1].reshape(1, d), ssm_w_in[0].astype(BF16), tm=tm_mlp)
    n_chunks = s // SSM_CHUNK
    ops = _ssm_operators(ssm_a_re[0], ssm_a_im[0], ssm_b_re[0], ssm_b_im[0], ssm_c_re[0],
                         ssm_c_im[0], ssm_log_dt[0], ssm_d[0], n_chunks)
    tok = jnp.arange(SSM_CHUNK * SSM_CHUNK)
    perm = (tok[None, :] == (tok[:, None] % SSM_CHUNK) * SSM_CHUNK + tok[:, None] // SSM_CHUNK)
    z = _ssm_scan(u, perm.astype(BF16), *ops, n_chunks=n_chunks)
    x3 = _mix_mlp(x2, z, ssm_w_glu[0].astype(BF16), norm_mlp_g[1].reshape(1, d),
                  mlp_w1[1].astype(BF16), mlp_w2[1].astype(BF16), glu=True, tm=tm_mlp)
    return x3.reshape(b, s, d)
```

```python
import functools
import math

import jax
import jax.numpy as jnp
from jax import lax
from jax.experimental import pallas as pl
from jax.experimental.pallas import tpu as pltpu

F32 = jnp.float32
BF16 = jnp.bfloat16

N_HEADS = 16
HEAD_DIM = 64
SSM_GROUP = 16
SSM_STATE = 64
SSM_CHUNK = 16
EPS = 1e-6
LOG2E = 1.4426950408889634

LANES = 128
MXU_DIM = 256
HEADS_PER_STEP = LANES // HEAD_DIM
BIAS_PIECES = 3
VMEM_LIMIT = 56 * 1024 * 1024

NT_DIMS = (((1,), (1,)), ((), ()))


def _rms_norm(x, g):
    ms = jnp.mean(x * x, axis=-1, keepdims=True)
    return x * lax.rsqrt(ms + EPS) * g


def _const_spec(shape):
    zeros = (0,) * len(shape)
    return pl.BlockSpec(shape, lambda *_: zeros)


def _bf16_pieces(val):
    pieces = jnp.zeros_like(val)
    rem = val
    for n in range(BIAS_PIECES):
        piece = rem.astype(BF16).astype(F32)
        rem = rem - piece
        pieces = pieces + (piece if n == 0 else pltpu.roll(piece, N_HEADS * n, axis=1))
    return pieces


def _qkv_kernel(x_ref, g_ref, wkf_ref, wqvt_ref, bd_ref, gain_k_ref, gain_qt_ref, bf_ref, tri_ref,
                place_k_ref, place_qt_ref, shift_ref,
                qt_ref, k_ref, vt_ref, kb_ref, qbt_ref, carry_ref, *, tm, tk, d):
    @pl.when(pl.program_id(1) == 0)
    def _():
        carry_ref[...] = jnp.zeros_like(carry_ref)

    h = _rms_norm(x_ref[0], g_ref[...]).astype(BF16)
    inv_hd = 1.0 / HEAD_DIM

    for t in range(d // MXU_DIM):
        sl = slice(MXU_DIM * t, MXU_DIM * (t + 1))
        tile = jnp.dot(h, wkf_ref[:, sl], preferred_element_type=F32)
        ss = jnp.dot((tile * tile).astype(BF16), bd_ref[...], preferred_element_type=F32)
        k_ref[0, :, sl] = (tile * lax.rsqrt(ss * inv_hd + EPS) * gain_k_ref[:, sl]).astype(BF16)

    vt = lax.dot_general(wqvt_ref[d:, :], h, NT_DIMS, preferred_element_type=F32).astype(BF16)
    for j in range(tm // tk):
        vt_ref[0, j] = vt[:, tk * j:tk * (j + 1)]
    qt = lax.dot_general(wqvt_ref[:d, :], h, NT_DIMS, preferred_element_type=F32)
    for t in range(d // MXU_DIM):
        sl = slice(MXU_DIM * t, MXU_DIM * (t + 1))
        tile = qt[sl, :]
        ss = jnp.dot(bd_ref[...], (tile * tile).astype(BF16), preferred_element_type=F32)
        gain = jnp.tile(gain_qt_ref[sl, :], (1, tm // LANES))
        qt_ref[0, sl, :] = (tile * lax.rsqrt(ss * inv_hd + EPS) * gain).astype(BF16)

    f = jnp.dot(h, wkf_ref[:, d:], preferred_element_type=F32) + bf_ref[...]
    log_f = jnp.minimum(f, 0.0) - jnp.log1p(jnp.exp(-jnp.abs(f)))
    lane = lax.broadcasted_iota(jnp.int32, log_f.shape, 1)
    part = jnp.dot(tri_ref[...], _bf16_pieces(jnp.where(lane < N_HEADS, log_f, 0.0)).astype(BF16),
                   preferred_element_type=F32)
    cs = carry_ref[...] + part
    for n in range(1, BIAS_PIECES):
        cs = cs + pltpu.roll(part, LANES - N_HEADS * n, axis=1)
    carry_ref[...] = cs[tm - 1:tm, :]
    c2 = jnp.where(lane < N_HEADS, cs * LOG2E, 0.0)
    kb_ref[0] = jnp.dot(_bf16_pieces(-c2).astype(BF16), place_k_ref[...],
                        preferred_element_type=F32).astype(BF16)
    q_pieces = _bf16_pieces(jnp.where(lane < N_HEADS, c2 - shift_ref[...], 0.0))
    qbt_ref[0] = jnp.dot(place_qt_ref[...], q_pieces.T.astype(BF16),
                         preferred_element_type=F32).astype(BF16)


def _qkv_proj(x, g, wkf, wqvt, bd, gain_k, gain_qt, bfp, tri, place_k, place_qt, shift, *, tm, tk):
    b, s, d = x.shape
    row = lambda bi, si: (bi, si, 0)
    col = lambda bi, si: (bi, 0, si)
    consts = (g, wkf, wqvt, bd, gain_k, gain_qt, bfp, tri, place_k, place_qt, shift)
    return pl.pallas_call(
        functools.partial(_qkv_kernel, tm=tm, tk=tk, d=d),
        grid=(b, s // tm),
        in_specs=[pl.BlockSpec((1, tm, d), row)] + [_const_spec(c.shape) for c in consts],
        out_specs=[pl.BlockSpec((1, d, tm), col),
                   pl.BlockSpec((1, tm, d), row),
                   pl.BlockSpec((1, tm // tk, d, tk), lambda bi, si: (bi, si, 0, 0)),
                   pl.BlockSpec((1, tm, d), row),
                   pl.BlockSpec((1, d, tm), col)],
        out_shape=[jax.ShapeDtypeStruct((b, d, s), BF16),
                   jax.ShapeDtypeStruct((b, s, d), BF16),
                   jax.ShapeDtypeStruct((b, s // tk, d, tk), BF16),
                   jax.ShapeDtypeStruct((b, s, d), BF16),
                   jax.ShapeDtypeStruct((b, d, s), BF16)],
        scratch_shapes=[pltpu.VMEM((1, LANES), F32)],
        compiler_params=pltpu.CompilerParams(
            dimension_semantics=("arbitrary", "arbitrary"), vmem_limit_bytes=VMEM_LIMIT),
        name="qkv_proj",
    )(x, *consts)


BIAS_Q_LANE = 8
MAX_SHIFT = 48.0


def _attn_kernel(qt_ref, qbt_ref, k_ref, kb_ref, vt_ref, o_ref, s0_scr, s1_scr, *, tq, tk, online):
    i = pl.program_id(2)
    n_sub = tq // tk
    qt = qt_ref[0]
    qbt = qbt_ref[0]
    row = lax.broadcasted_iota(jnp.int32, (LANES, tq), 0)
    lane_k = lax.broadcasted_iota(jnp.int32, (tk, LANES), 1)
    q_cols = (lane_k >= BIAS_Q_LANE) & (lane_k < BIAS_Q_LANE + HEADS_PER_STEP * BIAS_PIECES)
    causal = (lax.broadcasted_iota(jnp.int32, (tk, tk), 0)
              <= lax.broadcasted_iota(jnp.int32, (tk, tk), 1))
    one = jnp.ones((), BF16)
    zero = jnp.zeros((), BF16)
    heads = range(HEADS_PER_STEP)
    qats = []
    for hh in heads:
        q_h = jnp.where((row >= HEAD_DIM * hh) & (row < HEAD_DIM * (hh + 1)), qt, zero)
        k_side = (row >= BIAS_PIECES * hh) & (row < BIAS_PIECES * (hh + 1))
        q_lo = BIAS_Q_LANE + BIAS_PIECES * hh
        q_side = (row >= q_lo) & (row < q_lo + BIAS_PIECES)
        qats.append(jnp.concatenate(
            [q_h, jnp.where(k_side, one, jnp.where(q_side, qbt, zero))], axis=0))

    def scores(hh, j, c0=0):
        off = pl.multiple_of(j * tk, tk)
        kbias = jnp.where(q_cols, one, kb_ref[0, pl.ds(off, tk), :])
        ka = jnp.concatenate([k_ref[0, pl.ds(off, tk), :], kbias], axis=1)
        return jnp.dot(ka, qats[hh][:, c0:], preferred_element_type=F32)

    def consume(hh, carry, s, j, diagonal):
        if diagonal:
            s_tri = jnp.where(causal, s[:, :tk], -1e30)
            s = s_tri if s.shape[1] == tk else jnp.concatenate([s_tri, s[:, tk:]], axis=1)
        vt = vt_ref[0, j, HEAD_DIM * hh:HEAD_DIM * (hh + 1), :]
        if online:
            m, l, acc = carry
            m_new = jnp.maximum(m, jnp.max(s, axis=0, keepdims=True))
            alpha = jnp.exp2(m - m_new)
            p = jnp.exp2(s - m_new)
            l = alpha * l + jnp.sum(p, axis=0, keepdims=True)
            acc = alpha * acc + jnp.dot(vt, p.astype(BF16), preferred_element_type=F32)
            return m_new, l, acc
        l, acc = carry
        p = jnp.exp2(s)
        l = l + jnp.sum(p, axis=0, keepdims=True)
        acc = acc + jnp.dot(vt, p.astype(BF16), preferred_element_type=F32)
        return l, acc

    init = (jnp.zeros((1, tq), F32), jnp.zeros((HEAD_DIM, tq), F32))
    if online:
        init = (jnp.full((1, tq), -1e30, F32),) + init

    carries = []
    for hh in heads:
        s0_scr[hh] = scores(hh, 0)

        def pair(jj, c, hh=hh):
            s1_scr[hh] = scores(hh, 2 * jj + 1)
            c = consume(hh, c, s0_scr[hh], 2 * jj, False)
            s0_scr[hh] = scores(hh, 2 * jj + 2)
            return consume(hh, c, s1_scr[hh], 2 * jj + 1, False)

        carries.append(lax.fori_loop(0, n_sub * i // 2, pair, init))

    j0 = n_sub * i
    later = [[scores(hh, j0 + dd, tk * dd) for hh in heads] for dd in range(1, n_sub)]
    carries = [consume(hh, carries[hh], s0_scr[hh], j0, True) for hh in heads]
    for dd in range(1, n_sub):
        c0 = tk * dd
        for hh in heads:
            part = consume(hh, tuple(c[:, c0:] for c in carries[hh]), later[dd - 1][hh],
                           j0 + dd, True)
            carries[hh] = tuple(jnp.concatenate([c[:, :c0], pc], axis=1)
                                for c, pc in zip(carries[hh], part))
    o_t = jnp.concatenate([carry[-1] / carry[-2] for carry in carries], axis=0)
    o_ref[0] = o_t.T.astype(BF16)


def _attention(qt, qbt, k, kb, vt, *, tq, tk, online):
    b, s, d = k.shape
    assert (tq // tk) % 2 == 0 and s % tq == 0
    n_pairs = d // LANES
    nk = s // tk
    qt_blk = pl.BlockSpec((1, LANES, tq), lambda bi, hp, i: (bi, hp, i))
    k_all = pl.BlockSpec((1, s, LANES), lambda bi, hp, i: (bi, 0, hp))
    return pl.pallas_call(
        functools.partial(_attn_kernel, tq=tq, tk=tk, online=online),
        grid=(b, n_pairs, s // tq),
        in_specs=[qt_blk, qt_blk, k_all, k_all,
                  pl.BlockSpec((1, nk, LANES, tk), lambda bi, hp, i: (bi, 0, hp, 0))],
        out_specs=pl.BlockSpec((1, tq, LANES), lambda bi, hp, i: (bi, i, hp)),
        out_shape=jax.ShapeDtypeStruct((b, s, d), BF16),
        scratch_shapes=[pltpu.VMEM((HEADS_PER_STEP, tk, tq), F32),
                        pltpu.VMEM((HEADS_PER_STEP, tk, tq), F32)],
        compiler_params=pltpu.CompilerParams(
            dimension_semantics=("arbitrary", "arbitrary", "arbitrary"),
            vmem_limit_bytes=VMEM_LIMIT),
        name="fox_attention_online" if online else "fox_attention",
    )(qt, qbt, k, kb, vt)


def _mix_mlp_kernel(x_ref, a_ref, wmix_ref, g_ref, w1_ref, w2_ref, *rest, glu, d, ff_chunk):
    mix = jnp.dot(a_ref[...], wmix_ref[...], preferred_element_type=F32)
    if glu:
        mix = mix[:, :d] * jax.nn.sigmoid(mix[:, d:])
    x1 = x_ref[...] + mix
    h = _rms_norm(x1, g_ref[...]).astype(BF16)
    acc = x1
    for c in range(w1_ref.shape[1] // ff_chunk):
        sl = slice(ff_chunk * c, ff_chunk * (c + 1))
        hid = jnp.maximum(jnp.dot(h, w1_ref[:, sl], preferred_element_type=F32), 0.0)
        acc = acc + jnp.dot((hid * hid).astype(BF16), w2_ref[sl, :], preferred_element_type=F32)
    if len(rest) == 1:
        (o_ref,) = rest
    else:
        g_next_ref, w_next_ref, o_ref, u_ref = rest
        h_next = _rms_norm(acc, g_next_ref[...]).astype(BF16)
        u_ref[...] = jnp.dot(h_next, w_next_ref[...], preferred_element_type=F32).astype(BF16)
    o_ref[...] = acc


def _mix_mlp(x, a, wmix, g, w1, w2, next_proj=None, *, glu, tm, ff_chunk=1024):
    t, d = x.shape
    row = lambda i: (i, 0)
    single = pl.Buffered(1)
    wspec = lambda shape: pl.BlockSpec(shape, lambda i: (0, 0), pipeline_mode=single)
    operands = [x, a, wmix, g, w1, w2]
    in_specs = [pl.BlockSpec((tm, d), row), pl.BlockSpec((tm, d), row),
                wspec(wmix.shape), wspec((1, d)), wspec(w1.shape), wspec(w2.shape)]
    out_specs = pl.BlockSpec((tm, d), row)
    out_shape = jax.ShapeDtypeStruct((t, d), F32)
    if next_proj is not None:
        operands += list(next_proj)
        in_specs += [wspec(w.shape) for w in next_proj]
        n = next_proj[1].shape[1]
        out_specs = [out_specs, pl.BlockSpec((tm, n), row)]
        out_shape = [out_shape, jax.ShapeDtypeStruct((t, n), BF16)]
    return pl.pallas_call(
        functools.partial(_mix_mlp_kernel, glu=glu, d=d, ff_chunk=ff_chunk),
        grid=(t // tm,),
        in_specs=in_specs,
        out_specs=out_specs,
        out_shape=out_shape,
        compiler_params=pltpu.CompilerParams(
            dimension_semantics=("arbitrary",), vmem_limit_bytes=VMEM_LIMIT),
        name="mix_glu_mlp" if glu else "mix_mlp",
    )(*operands)


GROUPS_PER_SLAB = LANES // SSM_GROUP


def _piece_transpose(arrs, piece):
    arrs = list(arrs)
    dist = GROUPS_PER_SLAB // 2
    while dist >= 1:
        keep = (piece & dist) == 0
        shift = SSM_GROUP * dist
        for i in range(GROUPS_PER_SLAB):
            if i & dist:
                continue
            a, b = arrs[i], arrs[i + dist]
            arrs[i] = jnp.where(keep, a, pltpu.roll(b, shift, axis=1))
            arrs[i + dist] = jnp.where(keep, pltpu.roll(a, LANES - shift, axis=1), b)
        dist //= 2
    return arrs


def _ssm_kernel(u_ref, perm_ref, cc_ref, bt_ref, pw1_ref, pw2_ref, zoh_ref, a1_ref, a2_ref,
                dv_ref, z_ref, ug_scr, zg_scr, mi_scr, ws_scr, wo_scr, *, n_chunks):
    L = SSM_CHUNK
    rows = u_ref.shape[0] // L
    w = L * SSM_GROUP
    half = SSM_STATE
    piece = lax.broadcasted_iota(jnp.int32, (L, LANES), 1) // SSM_GROUP
    unroll = math.gcd(rows // L, 8)

    def relayout_in(rb, carry):
        t0 = pl.multiple_of(rb * w, w)
        r0 = pl.multiple_of(rb * L, L)
        by_pos = jnp.dot(perm_ref[...], u_ref[pl.ds(t0, w), :], preferred_element_type=F32)
        for hf in range(L // GROUPS_PER_SLAB):
            arrs = [by_pos[L * (GROUPS_PER_SLAB * hf + k):L * (GROUPS_PER_SLAB * hf + k + 1)]
                    for k in range(GROUPS_PER_SLAB)]
            arrs = _piece_transpose(arrs, piece)
            for g in range(GROUPS_PER_SLAB):
                ug_scr[g, pl.ds(r0, L), LANES * hf:LANES * (hf + 1)] = arrs[g].astype(BF16)
        return carry

    lax.fori_loop(0, rows // L, relayout_in, 0, unroll=unroll)

    chunk = lax.broadcasted_iota(jnp.int32, (rows, LANES), 0) % n_chunks
    lane_w = lax.broadcasted_iota(jnp.int32, (SSM_GROUP, w), 1)
    sign = jnp.where(lax.broadcasted_iota(jnp.int32, (1, LANES), 1) < half, 1.0, -1.0)

    def group(g, carry):
        cc = cc_ref[g]
        bt = bt_ref[g]
        pw1 = pw1_ref[g]
        pw2 = pw2_ref[g]
        zoh = zoh_ref[g]
        bbar = bt * zoh[0:1] + pltpu.roll(bt, half, axis=1) * zoh[1:2]
        bbar_sw = pltpu.roll(bbar, half, axis=1)
        cc_sw = pltpu.roll(cc, half, axis=1)
        ca = [cc * pw1[k:k + 1] + cc_sw * pw2[k:k + 1] for k in range(L + 1)]
        k_all = lax.dot_general(bbar * sign, jnp.concatenate(ca[:L], axis=0), NT_DIMS,
                                precision=lax.Precision.HIGHEST, preferred_element_type=F32)
        for s in range(L):
            blk = k_all if s == 0 else jnp.where(
                lane_w >= SSM_GROUP * s, pltpu.roll(k_all, SSM_GROUP * s, axis=1), 0.0)
            mi_scr[SSM_GROUP * s:SSM_GROUP * (s + 1), :] = blk.astype(BF16)
            k = L - 1 - s
            ws_scr[SSM_GROUP * s:SSM_GROUP * (s + 1), :] = (
                bbar * pw1[k:k + 1] + bbar_sw * pw2[k:k + 1]).astype(BF16)
            wo_scr[SSM_GROUP * s:SSM_GROUP * (s + 1), :] = (ca[s + 1] * sign).astype(BF16)

        u = ug_scr[g]
        y = jnp.dot(u, mi_scr[...], preferred_element_type=F32)
        xs = jnp.dot(u, ws_scr[...], preferred_element_type=F32)
        a1 = a1_ref[g]
        a2 = a2_ref[g]
        step = 1
        j = 0
        while step < n_chunks:
            sh = jnp.where(chunk >= step, pltpu.roll(xs, step, axis=0), 0.0)
            xs = xs + sh * a1[j:j + 1] + pltpu.roll(sh, half, axis=1) * a2[j:j + 1]
            step *= 2
            j += 1
        x_in = jnp.where(chunk >= 1, pltpu.roll(xs, 1, axis=0), 0.0)
        y = y + lax.dot_general(x_in.astype(BF16), wo_scr[...], NT_DIMS,
                                preferred_element_type=F32)
        y = y + u.astype(F32) * dv_ref[g]
        zg_scr[g] = jax.nn.gelu(y).astype(BF16)
        return carry

    lax.fori_loop(0, GROUPS_PER_SLAB, group, 0)

    def relayout_out(rb, carry):
        t0 = pl.multiple_of(rb * w, w)
        r0 = pl.multiple_of(rb * L, L)
        by_pos = []
        for hf in range(L // GROUPS_PER_SLAB):
            arrs = [zg_scr[g, pl.ds(r0, L), LANES * hf:LANES * (hf + 1)].astype(F32)
                    for g in range(GROUPS_PER_SLAB)]
            by_pos += _piece_transpose(arrs, piece)
        by_pos = jnp.concatenate(by_pos, axis=0).astype(BF16)
        z_ref[pl.ds(t0, w), :] = jnp.dot(perm_ref[...], by_pos,
                                         preferred_element_type=F32).astype(BF16)
        return carry

    lax.fori_loop(0, rows // L, relayout_out, 0, unroll=unroll)


def _ssm_scan(u, perm, cc, bt, pw1, pw2, zoh, a1, a2, dv, *, n_chunks):
    t, d = u.shape
    L = SSM_CHUNK
    rows = t // L
    w = L * SSM_GROUP
    gps = GROUPS_PER_SLAB
    slab = lambda j: (j, 0, 0)
    pspec = lambda arr: pl.BlockSpec((gps,) + arr.shape[1:], slab)
    return pl.pallas_call(
        functools.partial(_ssm_kernel, n_chunks=n_chunks),
        grid=(d // LANES,),
        in_specs=[pl.BlockSpec((t, LANES), lambda j: (0, j)), _const_spec((w, w)),
                  pspec(cc), pspec(bt), pspec(pw1), pspec(pw2), pspec(zoh), pspec(a1), pspec(a2),
                  pspec(dv)],
        out_specs=pl.BlockSpec((t, LANES), lambda j: (0, j)),
        out_shape=jax.ShapeDtypeStruct((t, d), BF16),
        scratch_shapes=[pltpu.VMEM((gps, rows, w), BF16), pltpu.VMEM((gps, rows, w), BF16),
                        pltpu.VMEM((w, w), BF16), pltpu.VMEM((w, LANES), BF16),
                        pltpu.VMEM((w, LANES), BF16)],
        compiler_params=pltpu.CompilerParams(
            dimension_semantics=("arbitrary",), vmem_limit_bytes=VMEM_LIMIT),
        name="s5_scan",
    )(u, perm, cc, bt, pw1, pw2, zoh, a1, a2, dv)


def _ssm_operators(a_re, a_im, b_re, b_im, c_re, c_im, log_dt, d_skip, n_chunks):
    L = SSM_CHUNK
    g, p = a_re.shape
    dt = jnp.exp(log_dt)[:, None]
    lam_re, lam_im = dt * a_re, dt * a_im

    def powers(ks):
        ks = jnp.asarray(ks, F32)[:, None, None]
        mag = jnp.exp(ks * lam_re)
        return mag * jnp.cos(ks * lam_im), mag * jnp.sin(ks * lam_im)

    def patterns(re, im):
        return (jnp.concatenate([re, re], axis=2).transpose(1, 0, 2),
                jnp.concatenate([-im, im], axis=2).transpose(1, 0, 2))

    pr, pi = powers(range(L + 1))
    num_re, num_im = pr[1] - 1.0, pi[1]
    den = a_re * a_re + a_im * a_im
    s_re = (num_re * a_re + num_im * a_im) / den
    s_im = (num_im * a_re - num_re * a_im) / den
    pw1, pw2 = patterns(pr, pi)
    zoh = jnp.concatenate(patterns(s_re[None], s_im[None]), axis=1)
    n_steps = max(1, int(math.log2(n_chunks)))
    a1, a2 = patterns(*powers([L * 2 ** j for j in range(n_steps)]))
    cc = jnp.concatenate([c_re, c_im], axis=2)
    bt = jnp.concatenate([b_re.transpose(0, 2, 1), b_im.transpose(0, 2, 1)], axis=2)
    dv = jnp.tile(d_skip.reshape(g, 1, SSM_GROUP), (1, L, 1)).reshape(g, 1, L * SSM_GROUP)
    return cc, bt, pw1, pw2, zoh, a1, a2, dv


def kernel(x, norm_mix_g, norm_mlp_g, attn_w_in, attn_b_f, attn_q_g, attn_k_g, attn_w_out,
           ssm_w_in, ssm_a_re, ssm_a_im, ssm_b_re, ssm_b_im, ssm_c_re, ssm_c_im, ssm_log_dt,
           ssm_d, ssm_w_glu, mlp_w1, mlp_w2):
    b, s, d = x.shape
    t = b * s
    tk = min(512, s // 2)
    tq = 2 * tk
    tm_proj = min(512, s)
    tm_mlp = min(512, t)

    w_in = attn_w_in[0]
    w_f = jnp.pad(w_in[:, 3 * d:], ((0, 0), (0, LANES - N_HEADS)))
    wkf = jnp.concatenate([w_in[:, d:2 * d], w_f], axis=1).astype(BF16)
    wqvt = jnp.concatenate([w_in[:, :d], w_in[:, 2 * d:3 * d]], axis=1).T.astype(BF16)
    blk = jnp.arange(MXU_DIM) // HEAD_DIM
    bd = (blk[:, None] == blk[None, :]).astype(BF16)
    gain_k = jnp.tile(attn_k_g[0], N_HEADS).reshape(1, d)
    gain_qt = jnp.broadcast_to(
        (jnp.tile(attn_q_g[0], N_HEADS) * (LOG2E / math.sqrt(HEAD_DIM)))[:, None], (d, LANES))
    bfp = jnp.pad(attn_b_f[0], (0, LANES - N_HEADS)).reshape(1, LANES)
    tri = (jnp.arange(tm_proj)[:, None] >= jnp.arange(tm_proj)[None, :]).astype(BF16)
    heads = jnp.arange(N_HEADS)
    place_k = jnp.zeros((LANES, d), F32)
    for i in range(BIAS_PIECES):
        place_k = place_k.at[N_HEADS * i + heads,
                             (heads // HEADS_PER_STEP) * LANES
                             + BIAS_PIECES * (heads % HEADS_PER_STEP) + i].set(1.0)
    place_qt = jnp.roll(place_k, BIAS_Q_LANE, axis=1).T
    shift = (LOG2E * math.sqrt(HEAD_DIM) * jnp.max(jnp.abs(attn_q_g[0]))
             * jnp.max(jnp.abs(attn_k_g[0])))
    qt, k, vt, kb, qbt = _qkv_proj(
        x, norm_mix_g[0].reshape(1, d), wkf, wqvt, bd, gain_k, gain_qt, bfp, tri,
        place_k.astype(BF16), place_qt.astype(BF16), jnp.full((1, LANES), shift, F32),
        tm=tm_proj, tk=tk)
    o = lax.cond(shift <= MAX_SHIFT,
                 functools.partial(_attention, tq=tq, tk=tk, online=False),
                 functools.partial(_attention, tq=tq, tk=tk, online=True),
                 qt, qbt, k, kb, vt)
    x2, u = _mix_mlp(x.reshape(t, d), o.reshape(t, d), attn_w_out[0].astype(BF16),
                     norm_mlp_g[0].reshape(1, d), mlp_w1[0].astype(BF16), mlp_w2[0].astype(BF16),
                     (norm_mix_g[1].reshape(1, d), ssm_w_in[0].astype(BF16)),
                     glu=False, tm=tm_mlp)

    n_chunks = s // SSM_CHUNK
    ops = _ssm_operators(ssm_a_re[0], ssm_a_im[0], ssm_b_re[0], ssm_b_im[0], ssm_c_re[0],
                         ssm_c_im[0], ssm_log_dt[0], ssm_d[0], n_chunks)
    tok = jnp.arange(SSM_CHUNK * SSM_CHUNK)
    perm = (tok[None, :] == (tok[:, None] % SSM_CHUNK) * SSM_CHUNK + tok[:, None] // SSM_CHUNK)
    z = _ssm_scan(u, perm.astype(BF16), *ops, n_chunks=n_chunks)
    x3 = _mix_mlp(x2, z, ssm_w_glu[0].astype(BF16), norm_mlp_g[1].reshape(1, d),
                  mlp_w1[1].astype(BF16), mlp_w2[1].astype(BF16), glu=True, tm=tm_mlp)
    return x3.reshape(b, s, d)
```

```python
import functools
import math

import jax
import jax.numpy as jnp
from jax import lax
from jax.experimental import pallas as pl
from jax.experimental.pallas import tpu as pltpu

F32 = jnp.float32
BF16 = jnp.bfloat16

N_HEADS = 16
HEAD_DIM = 64
SSM_GROUP = 16
SSM_STATE = 64
SSM_CHUNK = 16
EPS = 1e-6
LOG2E = 1.4426950408889634

LANES = 128
MXU_DIM = 256
HEADS_PER_STEP = LANES // HEAD_DIM
BIAS_PIECES = 3
VMEM_LIMIT = 56 * 1024 * 1024

NT_DIMS = (((1,), (1,)), ((), ()))


def _rms_norm(x, g):
    ms = jnp.mean(x * x, axis=-1, keepdims=True)
    return x * lax.rsqrt(ms + EPS) * g


def _const_spec(shape):
    zeros = (0,) * len(shape)
    return pl.BlockSpec(shape, lambda *_: zeros)


def _bf16_pieces(val):
    pieces = jnp.zeros_like(val)
    rem = val
    for n in range(BIAS_PIECES):
        piece = rem.astype(BF16).astype(F32)
        rem = rem - piece
        pieces = pieces + (piece if n == 0 else pltpu.roll(piece, N_HEADS * n, axis=1))
    return pieces


def _qkv_kernel(x_ref, g_ref, wkf_ref, wqvt_ref, bd_ref, gain_k_ref, gain_qt_ref, bf_ref, tri_ref,
                place_k_ref, place_qt_ref, shift_ref,
                qt_ref, k_ref, vt_ref, kb_ref, qbt_ref, carry_ref, *, tm, tk, d):
    @pl.when(pl.program_id(1) == 0)
    def _():
        carry_ref[...] = jnp.zeros_like(carry_ref)

    h = _rms_norm(x_ref[0], g_ref[...]).astype(BF16)
    inv_hd = 1.0 / HEAD_DIM

    y = jnp.dot(h, wkf_ref[...], preferred_element_type=F32)
    vt = lax.dot_general(wqvt_ref[d:, :], h, NT_DIMS, preferred_element_type=F32).astype(BF16)
    for j in range(tm // tk):
        vt_ref[0, j] = vt[:, tk * j:tk * (j + 1)]
    qt = lax.dot_general(wqvt_ref[:d, :], h, NT_DIMS, preferred_element_type=F32)

    for t in range(d // MXU_DIM):
        sl = slice(MXU_DIM * t, MXU_DIM * (t + 1))
        tile = y[:, sl]
        ss = jnp.dot((tile * tile).astype(BF16), bd_ref[...], preferred_element_type=F32)
        k_ref[0, :, sl] = (tile * lax.rsqrt(ss * inv_hd + EPS) * gain_k_ref[:, sl]).astype(BF16)
        tile = qt[sl, :]
        ss = jnp.dot(bd_ref[...], (tile * tile).astype(BF16), preferred_element_type=F32)
        gain = jnp.tile(gain_qt_ref[sl, :], (1, tm // LANES))
        qt_ref[0, sl, :] = (tile * lax.rsqrt(ss * inv_hd + EPS) * gain).astype(BF16)

    f = y[:, d:] + bf_ref[...]
    log_f = jnp.minimum(f, 0.0) - jnp.log1p(jnp.exp(-jnp.abs(f)))
    lane = lax.broadcasted_iota(jnp.int32, log_f.shape, 1)
    part = jnp.dot(tri_ref[...], _bf16_pieces(jnp.where(lane < N_HEADS, log_f, 0.0)).astype(BF16),
                   preferred_element_type=F32)
    cs = carry_ref[...] + part
    for n in range(1, BIAS_PIECES):
        cs = cs + pltpu.roll(part, LANES - N_HEADS * n, axis=1)
    carry_ref[...] = cs[tm - 1:tm, :]
    c2 = jnp.where(lane < N_HEADS, cs * LOG2E, 0.0)
    kb_ref[0] = jnp.dot(_bf16_pieces(-c2).astype(BF16), place_k_ref[...],
                        preferred_element_type=F32).astype(BF16)
    q_pieces = _bf16_pieces(jnp.where(lane < N_HEADS, c2 - shift_ref[...], 0.0))
    qbt_ref[0] = jnp.dot(place_qt_ref[...], q_pieces.T.astype(BF16),
                         preferred_element_type=F32).astype(BF16)


def _qkv_proj(x, g, wkf, wqvt, bd, gain_k, gain_qt, bfp, tri, place_k, place_qt, shift, *, tm, tk):
    b, s, d = x.shape
    row = lambda bi, si: (bi, si, 0)
    col = lambda bi, si: (bi, 0, si)
    consts = (g, wkf, wqvt, bd, gain_k, gain_qt, bfp, tri, place_k, place_qt, shift)
    return pl.pallas_call(
        functools.partial(_qkv_kernel, tm=tm, tk=tk, d=d),
        grid=(b, s // tm),
        in_specs=[pl.BlockSpec((1, tm, d), row)] + [_const_spec(c.shape) for c in consts],
        out_specs=[pl.BlockSpec((1, d, tm), col),
                   pl.BlockSpec((1, tm, d), row),
                   pl.BlockSpec((1, tm // tk, d, tk), lambda bi, si: (bi, si, 0, 0)),
                   pl.BlockSpec((1, tm, d), row),
                   pl.BlockSpec((1, d, tm), col)],
        out_shape=[jax.ShapeDtypeStruct((b, d, s), BF16),
                   jax.ShapeDtypeStruct((b, s, d), BF16),
                   jax.ShapeDtypeStruct((b, s // tk, d, tk), BF16),
                   jax.ShapeDtypeStruct((b, s, d), BF16),
                   jax.ShapeDtypeStruct((b, d, s), BF16)],
        scratch_shapes=[pltpu.VMEM((1, LANES), F32)],
        compiler_params=pltpu.CompilerParams(
            dimension_semantics=("arbitrary", "arbitrary"), vmem_limit_bytes=VMEM_LIMIT),
        name="qkv_proj",
    )(x, *consts)


BIAS_Q_LANE = 8
MAX_SHIFT = 48.0


def _attn_kernel(qt_ref, qbt_ref, k_ref, kb_ref, vt_ref, o_ref, s0_scr, s1_scr, *, tq, tk, online):
    i = pl.program_id(2)
    n_sub = tq // tk
    qt = qt_ref[0]
    qbt = qbt_ref[0]
    row = lax.broadcasted_iota(jnp.int32, (LANES, tq), 0)
    lane_k = lax.broadcasted_iota(jnp.int32, (tk, LANES), 1)
    q_cols = (lane_k >= BIAS_Q_LANE) & (lane_k < BIAS_Q_LANE + HEADS_PER_STEP * BIAS_PIECES)
    causal = (lax.broadcasted_iota(jnp.int32, (tk, tk), 0)
              <= lax.broadcasted_iota(jnp.int32, (tk, tk), 1))
    one = jnp.ones((), BF16)
    zero = jnp.zeros((), BF16)
    heads = range(HEADS_PER_STEP)
    qats = []
    for hh in heads:
        q_h = jnp.where((row >= HEAD_DIM * hh) & (row < HEAD_DIM * (hh + 1)), qt, zero)
        k_side = (row >= BIAS_PIECES * hh) & (row < BIAS_PIECES * (hh + 1))
        q_lo = BIAS_Q_LANE + BIAS_PIECES * hh
        q_side = (row >= q_lo) & (row < q_lo + BIAS_PIECES)
        qats.append(jnp.concatenate(
            [q_h, jnp.where(k_side, one, jnp.where(q_side, qbt, zero))], axis=0))

    def scores(hh, j, c0=0):
        off = pl.multiple_of(j * tk, tk)
        kbias = jnp.where(q_cols, one, kb_ref[0, pl.ds(off, tk), :])
        ka = jnp.concatenate([k_ref[0, pl.ds(off, tk), :], kbias], axis=1)
        return jnp.dot(ka, qats[hh][:, c0:], preferred_element_type=F32)

    def consume(hh, carry, s, j, diagonal):
        if diagonal:
            s_tri = jnp.where(causal, s[:, :tk], -1e30)
            s = s_tri if s.shape[1] == tk else jnp.concatenate([s_tri, s[:, tk:]], axis=1)
        vt = vt_ref[0, j, HEAD_DIM * hh:HEAD_DIM * (hh + 1), :]
        if online:
            m, l, acc = carry
            m_new = jnp.maximum(m, jnp.max(s, axis=0, keepdims=True))
            alpha = jnp.exp2(m - m_new)
            p = jnp.exp2(s - m_new)
            l = alpha * l + jnp.sum(p, axis=0, keepdims=True)
            acc = alpha * acc + jnp.dot(vt, p.astype(BF16), preferred_element_type=F32)
            return m_new, l, acc
        l, acc = carry
        p = jnp.exp2(s)
        l = l + jnp.sum(p, axis=0, keepdims=True)
        acc = acc + jnp.dot(vt, p.astype(BF16), preferred_element_type=F32)
        return l, acc

    init = (jnp.zeros((1, tq), F32), jnp.zeros((HEAD_DIM, tq), F32))
    if online:
        init = (jnp.full((1, tq), -1e30, F32),) + init

    carries = []
    for hh in heads:
        s0_scr[hh] = scores(hh, 0)

        def pair(jj, c, hh=hh):
            s1_scr[hh] = scores(hh, 2 * jj + 1)
            c = consume(hh, c, s0_scr[hh], 2 * jj, False)
            s0_scr[hh] = scores(hh, 2 * jj + 2)
            return consume(hh, c, s1_scr[hh], 2 * jj + 1, False)

        carries.append(lax.fori_loop(0, n_sub * i // 2, pair, init))

    j0 = n_sub * i
    later = [[scores(hh, j0 + dd, tk * dd) for hh in heads] for dd in range(1, n_sub)]
    carries = [consume(hh, carries[hh], s0_scr[hh], j0, True) for hh in heads]
    for dd in range(1, n_sub):
        c0 = tk * dd
        for hh in heads:
            part = consume(hh, tuple(c[:, c0:] for c in carries[hh]), later[dd - 1][hh],
                           j0 + dd, True)
            carries[hh] = tuple(jnp.concatenate([c[:, :c0], pc], axis=1)
                                for c, pc in zip(carries[hh], part))
    o_t = jnp.concatenate([carry[-1] / carry[-2] for carry in carries], axis=0)
    o_ref[0] = o_t.T.astype(BF16)


def _attention(qt, qbt, k, kb, vt, *, tq, tk, online):
    b, s, d = k.shape
    assert (tq // tk) % 2 == 0 and s % tq == 0
    n_pairs = d // LANES
    nk = s // tk
    qt_blk = pl.BlockSpec((1, LANES, tq), lambda bi, hp, i: (bi, hp, i))
    k_all = pl.BlockSpec((1, s, LANES), lambda bi, hp, i: (bi, 0, hp))
    return pl.pallas_call(
        functools.partial(_attn_kernel, tq=tq, tk=tk, online=online),
        grid=(b, n_pairs, s // tq),
        in_specs=[qt_blk, qt_blk, k_all, k_all,
                  pl.BlockSpec((1, nk, LANES, tk), lambda bi, hp, i: (bi, 0, hp, 0))],
        out_specs=pl.BlockSpec((1, tq, LANES), lambda bi, hp, i: (bi, i, hp)),
        out_shape=jax.ShapeDtypeStruct((b, s, d), BF16),
        scratch_shapes=[pltpu.VMEM((HEADS_PER_STEP, tk, tq), F32),
                        pltpu.VMEM((HEADS_PER_STEP, tk, tq), F32)],
        compiler_params=pltpu.CompilerParams(
            dimension_semantics=("arbitrary", "arbitrary", "arbitrary"),
            vmem_limit_bytes=VMEM_LIMIT),
        name="fox_attention_online" if online else "fox_attention",
    )(qt, qbt, k, kb, vt)


def _permute_token_blocks(perm, a):
    w = perm.shape[0]
    return jnp.concatenate(
        [jnp.dot(perm, a[w * i:w * (i + 1)], preferred_element_type=F32).astype(BF16)
         for i in range(a.shape[0] // w)], axis=0)


def _mix_mlp_kernel(x_ref, a_ref, wmix_ref, g_ref, w1_ref, w2_ref, *rest, glu, d, ff_chunk):
    if glu:
        a = _permute_token_blocks(rest[0][...], a_ref[...])
        mix = jnp.dot(a, wmix_ref[...], preferred_element_type=F32)
        mix = mix[:, :d] * jax.nn.sigmoid(mix[:, d:])
    else:
        mix = jnp.dot(a_ref[...], wmix_ref[...], preferred_element_type=F32)
    x1 = x_ref[...] + mix
    h = _rms_norm(x1, g_ref[...]).astype(BF16)
    acc = x1
    for c in range(w1_ref.shape[1] // ff_chunk):
        sl = slice(ff_chunk * c, ff_chunk * (c + 1))
        hid = jnp.maximum(jnp.dot(h, w1_ref[:, sl], preferred_element_type=F32), 0.0)
        acc = acc + jnp.dot((hid * hid).astype(BF16), w2_ref[sl, :], preferred_element_type=F32)
    if glu:
        _, o_ref = rest
    else:
        perm_ref, g_next_ref, w_next_ref, o_ref, u_ref = rest
        h_next = _rms_norm(acc, g_next_ref[...]).astype(BF16)
        u = jnp.dot(h_next, w_next_ref[...], preferred_element_type=F32).astype(BF16)
        u_ref[...] = _permute_token_blocks(perm_ref[...], u)
    o_ref[...] = acc


def _mix_mlp(x, a, wmix, g, w1, w2, perm, next_proj=None, *, glu, tm, ff_chunk=1024):
    t, d = x.shape
    row = lambda i: (i, 0)
    single = pl.Buffered(1)
    wspec = lambda shape: pl.BlockSpec(shape, lambda i: (0, 0), pipeline_mode=single)
    operands = [x, a, wmix, g, w1, w2, perm]
    in_specs = [pl.BlockSpec((tm, d), row), pl.BlockSpec((tm, d), row),
                wspec(wmix.shape), wspec((1, d)), wspec(w1.shape), wspec(w2.shape),
                wspec(perm.shape)]
    out_specs = pl.BlockSpec((tm, d), row)
    out_shape = jax.ShapeDtypeStruct((t, d), F32)
    if next_proj is not None:
        operands += list(next_proj)
        in_specs += [wspec(w.shape) for w in next_proj]
        n = next_proj[1].shape[1]
        out_specs = [out_specs, pl.BlockSpec((tm, n), row)]
        out_shape = [out_shape, jax.ShapeDtypeStruct((t, n), BF16)]
    return pl.pallas_call(
        functools.partial(_mix_mlp_kernel, glu=glu, d=d, ff_chunk=ff_chunk),
        grid=(t // tm,),
        in_specs=in_specs,
        out_specs=out_specs,
        out_shape=out_shape,
        compiler_params=pltpu.CompilerParams(
            dimension_semantics=("arbitrary",), vmem_limit_bytes=VMEM_LIMIT),
        name="mix_glu_mlp" if glu else "mix_mlp",
    )(*operands)


GROUPS_PER_SLAB = LANES // SSM_GROUP


def _piece_transpose(arrs, piece):
    arrs = list(arrs)
    dist = GROUPS_PER_SLAB // 2
    while dist >= 1:
        keep = (piece & dist) == 0
        shift = SSM_GROUP * dist
        for i in range(GROUPS_PER_SLAB):
            if i & dist:
                continue
            a, b = arrs[i], arrs[i + dist]
            arrs[i] = jnp.where(keep, a, pltpu.roll(b, shift, axis=1))
            arrs[i + dist] = jnp.where(keep, pltpu.roll(a, LANES - shift, axis=1), b)
        dist //= 2
    return arrs


def _ssm_kernel(u_ref, cc_ref, bt_ref, pw1_ref, pw2_ref, zoh_ref, a1_ref, a2_ref,
                dv_ref, z_ref, ug_scr, zg_scr, mi_scr, ws_scr, wo_scr, *, n_chunks):
    L = SSM_CHUNK
    rows = u_ref.shape[0] // L
    w = L * SSM_GROUP
    half = SSM_STATE
    nb = math.gcd(rows // L, 8)
    piece = lax.broadcasted_iota(jnp.int32, (nb * L, LANES), 1) // SSM_GROUP

    def relayout_in(rb, carry):
        t0 = pl.multiple_of(rb * nb * w, nb * w)
        r0 = pl.multiple_of(rb * nb * L, nb * L)
        by_pos = u_ref[pl.ds(t0, nb * w), :].astype(F32)
        for hf in range(L // GROUPS_PER_SLAB):
            arrs = []
            for k in range(GROUPS_PER_SLAB):
                pos = GROUPS_PER_SLAB * hf + k
                arrs.append(jnp.concatenate(
                    [by_pos[w * blk + L * pos:w * blk + L * (pos + 1)] for blk in range(nb)],
                    axis=0))
            arrs = _piece_transpose(arrs, piece)
            for g in range(GROUPS_PER_SLAB):
                ug_scr[g, pl.ds(r0, nb * L), LANES * hf:LANES * (hf + 1)] = arrs[g].astype(BF16)
        return carry

    lax.fori_loop(0, rows // (nb * L), relayout_in, 0)

    chunk = lax.broadcasted_iota(jnp.int32, (rows, LANES), 0) % n_chunks
    lane_w = lax.broadcasted_iota(jnp.int32, (SSM_GROUP, w), 1)
    sign = jnp.where(lax.broadcasted_iota(jnp.int32, (1, LANES), 1) < half, 1.0, -1.0)

    def group(g, carry):
        cc = cc_ref[g]
        bt = bt_ref[g]
        pw1 = pw1_ref[g]
        pw2 = pw2_ref[g]
        zoh = zoh_ref[g]
        bbar = bt * zoh[0:1] + pltpu.roll(bt, half, axis=1) * zoh[1:2]
        bbar_sw = pltpu.roll(bbar, half, axis=1)
        cc_sw = pltpu.roll(cc, half, axis=1)
        ca = [cc * pw1[k:k + 1] + cc_sw * pw2[k:k + 1] for k in range(L + 1)]
        k_all = lax.dot_general(bbar * sign, jnp.concatenate(ca[:L], axis=0), NT_DIMS,
                                precision=lax.Precision.HIGHEST, preferred_element_type=F32)
        for s in range(L):
            blk = k_all if s == 0 else jnp.where(
                lane_w >= SSM_GROUP * s, pltpu.roll(k_all, SSM_GROUP * s, axis=1), 0.0)
            mi_scr[SSM_GROUP * s:SSM_GROUP * (s + 1), :] = blk.astype(BF16)
            k = L - 1 - s
            ws_scr[SSM_GROUP * s:SSM_GROUP * (s + 1), :] = (
                bbar * pw1[k:k + 1] + bbar_sw * pw2[k:k + 1]).astype(BF16)
            wo_scr[SSM_GROUP * s:SSM_GROUP * (s + 1), :] = (ca[s + 1] * sign).astype(BF16)

        u = ug_scr[g]
        y = jnp.dot(u, mi_scr[...], preferred_element_type=F32)
        xs = jnp.dot(u, ws_scr[...], preferred_element_type=F32)
        a1 = a1_ref[g]
        a2 = a2_ref[g]
        step = 1
        j = 0
        while step < n_chunks:
            sh = jnp.where(chunk >= step, pltpu.roll(xs, step, axis=0), 0.0)
            xs = xs + sh * a1[j:j + 1] + pltpu.roll(sh, half, axis=1) * a2[j:j + 1]
            step *= 2
            j += 1
        x_in = jnp.where(chunk >= 1, pltpu.roll(xs, 1, axis=0), 0.0)
        y = y + lax.dot_general(x_in.astype(BF16), wo_scr[...], NT_DIMS,
                                preferred_element_type=F32)
        y = y + u.astype(F32) * dv_ref[g]
        zg_scr[g] = jax.nn.gelu(y).astype(BF16)
        return carry

    lax.fori_loop(0, GROUPS_PER_SLAB, group, 0)

    def relayout_out(rb, carry):
        t0 = pl.multiple_of(rb * nb * w, nb * w)
        r0 = pl.multiple_of(rb * nb * L, nb * L)
        by_pos = []
        for hf in range(L // GROUPS_PER_SLAB):
            arrs = [zg_scr[g, pl.ds(r0, nb * L), LANES * hf:LANES * (hf + 1)].astype(F32)
                    for g in range(GROUPS_PER_SLAB)]
            by_pos += _piece_transpose(arrs, piece)
        for blk in range(nb):
            z_ref[pl.ds(t0 + w * blk, w), :] = jnp.concatenate(
                [arr[L * blk:L * (blk + 1)] for arr in by_pos], axis=0).astype(BF16)
        return carry

    lax.fori_loop(0, rows // (nb * L), relayout_out, 0)


def _ssm_scan(u, cc, bt, pw1, pw2, zoh, a1, a2, dv, *, n_chunks):
    t, d = u.shape
    L = SSM_CHUNK
    rows = t // L
    w = L * SSM_GROUP
    gps = GROUPS_PER_SLAB
    slab = lambda j: (j, 0, 0)
    pspec = lambda arr: pl.BlockSpec((gps,) + arr.shape[1:], slab)
    return pl.pallas_call(
        functools.partial(_ssm_kernel, n_chunks=n_chunks),
        grid=(d // LANES,),
        in_specs=[pl.BlockSpec((t, LANES), lambda j: (0, j)),
                  pspec(cc), pspec(bt), pspec(pw1), pspec(pw2), pspec(zoh), pspec(a1), pspec(a2),
                  pspec(dv)],
        out_specs=pl.BlockSpec((t, LANES), lambda j: (0, j)),
        out_shape=jax.ShapeDtypeStruct((t, d), BF16),
        scratch_shapes=[pltpu.VMEM((gps, rows, w), BF16), pltpu.VMEM((gps, rows, w), BF16),
                        pltpu.VMEM((w, w), BF16), pltpu.VMEM((w, LANES), BF16),
                        pltpu.VMEM((w, LANES), BF16)],
        compiler_params=pltpu.CompilerParams(
            dimension_semantics=("arbitrary",), vmem_limit_bytes=VMEM_LIMIT),
        name="s5_scan",
    )(u, cc, bt, pw1, pw2, zoh, a1, a2, dv)


def _ssm_operators(a_re, a_im, b_re, b_im, c_re, c_im, log_dt, d_skip, n_chunks):
    L = SSM_CHUNK
    g, p = a_re.shape
    dt = jnp.exp(log_dt)[:, None]
    lam_re, lam_im = dt * a_re, dt * a_im

    def powers(ks):
        ks = jnp.asarray(ks, F32)[:, None, None]
        mag = jnp.exp(ks * lam_re)
        return mag * jnp.cos(ks * lam_im), mag * jnp.sin(ks * lam_im)

    def patterns(re, im):
        return (jnp.concatenate([re, re], axis=2).transpose(1, 0, 2),
                jnp.concatenate([-im, im], axis=2).transpose(1, 0, 2))

    pr, pi = powers(range(L + 1))
    num_re, num_im = pr[1] - 1.0, pi[1]
    den = a_re * a_re + a_im * a_im
    s_re = (num_re * a_re + num_im * a_im) / den
    s_im = (num_im * a_re - num_re * a_im) / den
    pw1, pw2 = patterns(pr, pi)
    zoh = jnp.concatenate(patterns(s_re[None], s_im[None]), axis=1)
    n_steps = max(1, int(math.log2(n_chunks)))
    a1, a2 = patterns(*powers([L * 2 ** j for j in range(n_steps)]))
    cc = jnp.concatenate([c_re, c_im], axis=2)
    bt = jnp.concatenate([b_re.transpose(0, 2, 1), b_im.transpose(0, 2, 1)], axis=2)
    dv = jnp.tile(d_skip.reshape(g, 1, SSM_GROUP), (1, L, 1)).reshape(g, 1, L * SSM_GROUP)
    return cc, bt, pw1, pw2, zoh, a1, a2, dv


def kernel(x, norm_mix_g, norm_mlp_g, attn_w_in, attn_b_f, attn_q_g, attn_k_g, attn_w_out,
           ssm_w_in, ssm_a_re, ssm_a_im, ssm_b_re, ssm_b_im, ssm_c_re, ssm_c_im, ssm_log_dt,
           ssm_d, ssm_w_glu, mlp_w1, mlp_w2):
    b, s, d = x.shape
    t = b * s
    tk = min(512, s // 2)
    tq = 2 * tk
    tm_proj = min(512, s)
    tm_mlp = min(512, t)

    w_in = attn_w_in[0]
    w_f = jnp.pad(w_in[:, 3 * d:], ((0, 0), (0, LANES - N_HEADS)))
    wkf = jnp.concatenate([w_in[:, d:2 * d], w_f], axis=1).astype(BF16)
    wqvt = jnp.concatenate([w_in[:, :d], w_in[:, 2 * d:3 * d]], axis=1).T.astype(BF16)
    blk = jnp.arange(MXU_DIM) // HEAD_DIM
    bd = (blk[:, None] == blk[None, :]).astype(BF16)
    gain_k = jnp.tile(attn_k_g[0], N_HEADS).reshape(1, d)
    gain_qt = jnp.broadcast_to(
        (jnp.tile(attn_q_g[0], N_HEADS) * (LOG2E / math.sqrt(HEAD_DIM)))[:, None], (d, LANES))
    bfp = jnp.pad(attn_b_f[0], (0, LANES - N_HEADS)).reshape(1, LANES)
    tri = (jnp.arange(tm_proj)[:, None] >= jnp.arange(tm_proj)[None, :]).astype(BF16)
    heads = jnp.arange(N_HEADS)
    place_k = jnp.zeros((LANES, d), F32)
    for i in range(BIAS_PIECES):
        place_k = place_k.at[N_HEADS * i + heads,
                             (heads // HEADS_PER_STEP) * LANES
                             + BIAS_PIECES * (heads % HEADS_PER_STEP) + i].set(1.0)
    place_qt = jnp.roll(place_k, BIAS_Q_LANE, axis=1).T
    shift = (LOG2E * math.sqrt(HEAD_DIM) * jnp.max(jnp.abs(attn_q_g[0]))
             * jnp.max(jnp.abs(attn_k_g[0])))
    qt, k, vt, kb, qbt = _qkv_proj(
        x, norm_mix_g[0].reshape(1, d), wkf, wqvt, bd, gain_k, gain_qt, bfp, tri,
        place_k.astype(BF16), place_qt.astype(BF16), jnp.full((1, LANES), shift, F32),
        tm=tm_proj, tk=tk)
    o = lax.cond(shift <= MAX_SHIFT,
                 functools.partial(_attention, tq=tq, tk=tk, online=False),
                 functools.partial(_attention, tq=tq, tk=tk, online=True),
                 qt, qbt, k, kb, vt)
    tok = jnp.arange(SSM_CHUNK * SSM_CHUNK)
    perm = (tok[None, :] == (tok[:, None] % SSM_CHUNK) * SSM_CHUNK + tok[:, None] // SSM_CHUNK)
    perm = perm.astype(BF16)
    x2, u = _mix_mlp(x.reshape(t, d), o.reshape(t, d), attn_w_out[0].astype(BF16),
                     norm_mlp_g[0].reshape(1, d), mlp_w1[0].astype(BF16), mlp_w2[0].astype(BF16),
                     perm, (norm_mix_g[1].reshape(1, d), ssm_w_in[0].astype(BF16)),
                     glu=False, tm=tm_mlp)

    n_chunks = s // SSM_CHUNK
    ops = _ssm_operators(ssm_a_re[0], ssm_a_im[0], ssm_b_re[0], ssm_b_im[0], ssm_c_re[0],
                         ssm_c_im[0], ssm_log_dt[0], ssm_d[0], n_chunks)
    z = _ssm_scan(u, *ops, n_chunks=n_chunks)
    x3 = _mix_mlp(x2, z, ssm_w_glu[0].astype(BF16), norm_mlp_g[1].reshape(1, d),
                  mlp_w1[1].astype(BF16), mlp_w2[1].astype(BF16), perm, glu=True, tm=tm_mlp)
    return x3.reshape(b, s, d)
```

```python
import functools
import math

import jax
import jax.numpy as jnp
from jax import lax
from jax.experimental import pallas as pl
from jax.experimental.pallas import tpu as pltpu

F32 = jnp.float32
BF16 = jnp.bfloat16

N_HEADS = 16
HEAD_DIM = 64
SSM_GROUP = 16
SSM_STATE = 64
SSM_CHUNK = 16
EPS = 1e-6
LOG2E = 1.4426950408889634

LANES = 128
MXU_DIM = 256
HEADS_PER_STEP = LANES // HEAD_DIM
BIAS_PIECES = 3
VMEM_LIMIT = 56 * 1024 * 1024

NT_DIMS = (((1,), (1,)), ((), ()))


def _rms_norm(x, g):
    ms = jnp.mean(x * x, axis=-1, keepdims=True)
    return x * lax.rsqrt(ms + EPS) * g


def _const_spec(shape):
    zeros = (0,) * len(shape)
    return pl.BlockSpec(shape, lambda *_: zeros)


def _bf16_pieces(val):
    pieces = jnp.zeros_like(val)
    rem = val
    for n in range(BIAS_PIECES):
        piece = rem.astype(BF16).astype(F32)
        rem = rem - piece
        pieces = pieces + (piece if n == 0 else pltpu.roll(piece, N_HEADS * n, axis=1))
    return pieces


def _qkv_kernel(x_ref, g_ref, wkf_ref, wqvt_ref, bd_ref, gain_k_ref, gain_qt_ref, bf_ref, tri_ref,
                place_k_ref, place_qt_ref, shift_ref,
                qt_ref, k_ref, vt_ref, kb_ref, qbt_ref, edge_ref, carry_ref, *, tm, tk, d):
    @pl.when(pl.program_id(1) == 0)
    def _():
        carry_ref[...] = jnp.zeros_like(carry_ref)

    h = _rms_norm(x_ref[0], g_ref[...]).astype(BF16)
    inv_hd = 1.0 / HEAD_DIM

    y = jnp.dot(h, wkf_ref[...], preferred_element_type=F32)
    vt = lax.dot_general(wqvt_ref[d:, :], h, NT_DIMS, preferred_element_type=F32).astype(BF16)
    for j in range(tm // tk):
        vt_ref[0, j] = vt[:, tk * j:tk * (j + 1)]
    qt = lax.dot_general(wqvt_ref[:d, :], h, NT_DIMS, preferred_element_type=F32)

    for t in range(d // MXU_DIM):
        sl = slice(MXU_DIM * t, MXU_DIM * (t + 1))
        tile = y[:, sl]
        ss = jnp.dot((tile * tile).astype(BF16), bd_ref[...], preferred_element_type=F32)
        k_ref[0, :, sl] = (tile * lax.rsqrt(ss * inv_hd + EPS) * gain_k_ref[:, sl]).astype(BF16)
        tile = qt[sl, :]
        ss = jnp.dot(bd_ref[...], (tile * tile).astype(BF16), preferred_element_type=F32)
        gain = jnp.tile(gain_qt_ref[sl, :], (1, tm // LANES))
        qt_ref[0, sl, :] = (tile * lax.rsqrt(ss * inv_hd + EPS) * gain).astype(BF16)

    f = y[:, d:] + bf_ref[...]
    log_f = jnp.minimum(f, 0.0) - jnp.log1p(jnp.exp(-jnp.abs(f)))
    lane = lax.broadcasted_iota(jnp.int32, log_f.shape, 1)
    part = jnp.dot(tri_ref[...], _bf16_pieces(jnp.where(lane < N_HEADS, log_f, 0.0)).astype(BF16),
                   preferred_element_type=F32)
    cs = carry_ref[...] + part
    for n in range(1, BIAS_PIECES):
        cs = cs + pltpu.roll(part, LANES - N_HEADS * n, axis=1)
    carry_ref[...] = cs[tm - 1:tm, :]
    edge_ref[0, 0] = jnp.concatenate(
        [cs[0:1, :], cs[tm - 1:tm, :], jnp.zeros((6, LANES), F32)], axis=0)
    c2 = jnp.where(lane < N_HEADS, cs * LOG2E, 0.0)
    kb_ref[0] = jnp.dot(_bf16_pieces(-c2).astype(BF16), place_k_ref[...],
                        preferred_element_type=F32).astype(BF16)
    q_pieces = _bf16_pieces(jnp.where(lane < N_HEADS, c2 - shift_ref[...], 0.0))
    qbt_ref[0] = jnp.dot(place_qt_ref[...], q_pieces.T.astype(BF16),
                         preferred_element_type=F32).astype(BF16)


def _qkv_proj(x, g, wkf, wqvt, bd, gain_k, gain_qt, bfp, tri, place_k, place_qt, shift, *, tm, tk):
    b, s, d = x.shape
    row = lambda bi, si: (bi, si, 0)
    col = lambda bi, si: (bi, 0, si)
    consts = (g, wkf, wqvt, bd, gain_k, gain_qt, bfp, tri, place_k, place_qt, shift)
    return pl.pallas_call(
        functools.partial(_qkv_kernel, tm=tm, tk=tk, d=d),
        grid=(b, s // tm),
        in_specs=[pl.BlockSpec((1, tm, d), row)] + [_const_spec(c.shape) for c in consts],
        out_specs=[pl.BlockSpec((1, d, tm), col),
                   pl.BlockSpec((1, tm, d), row),
                   pl.BlockSpec((1, tm // tk, d, tk), lambda bi, si: (bi, si, 0, 0)),
                   pl.BlockSpec((1, tm, d), row),
                   pl.BlockSpec((1, d, tm), col),
                   pl.BlockSpec((1, 1, 8, LANES), lambda bi, si: (bi, si, 0, 0))],
        out_shape=[jax.ShapeDtypeStruct((b, d, s), BF16),
                   jax.ShapeDtypeStruct((b, s, d), BF16),
                   jax.ShapeDtypeStruct((b, s // tk, d, tk), BF16),
                   jax.ShapeDtypeStruct((b, s, d), BF16),
                   jax.ShapeDtypeStruct((b, d, s), BF16),
                   jax.ShapeDtypeStruct((b, s // tm, 8, LANES), F32)],
        scratch_shapes=[pltpu.VMEM((1, LANES), F32)],
        compiler_params=pltpu.CompilerParams(
            dimension_semantics=("arbitrary", "arbitrary"), vmem_limit_bytes=VMEM_LIMIT),
        name="qkv_proj",
    )(x, *consts)


BIAS_Q_LANE = 8
MAX_SHIFT = 48.0
SKIP_LOGIT = -200.0


def _attn_kernel(first_ref, qt_ref, qbt_ref, k_ref, kb_ref, vt_ref, o_ref, s0_scr, s1_scr, *,
                 tq, tk, online):
    i = pl.program_id(2)
    n_sub = tq // tk
    qt = qt_ref[0]
    qbt = qbt_ref[0]
    row = lax.broadcasted_iota(jnp.int32, (LANES, tq), 0)
    lane_k = lax.broadcasted_iota(jnp.int32, (tk, LANES), 1)
    q_cols = (lane_k >= BIAS_Q_LANE) & (lane_k < BIAS_Q_LANE + HEADS_PER_STEP * BIAS_PIECES)
    causal = (lax.broadcasted_iota(jnp.int32, (tk, tk), 0)
              <= lax.broadcasted_iota(jnp.int32, (tk, tk), 1))
    one = jnp.ones((), BF16)
    zero = jnp.zeros((), BF16)
    heads = range(HEADS_PER_STEP)
    qats = []
    for hh in heads:
        q_h = jnp.where((row >= HEAD_DIM * hh) & (row < HEAD_DIM * (hh + 1)), qt, zero)
        k_side = (row >= BIAS_PIECES * hh) & (row < BIAS_PIECES * (hh + 1))
        q_lo = BIAS_Q_LANE + BIAS_PIECES * hh
        q_side = (row >= q_lo) & (row < q_lo + BIAS_PIECES)
        qats.append(jnp.concatenate(
            [q_h, jnp.where(k_side, one, jnp.where(q_side, qbt, zero))], axis=0))

    def scores(hh, j, c0=0):
        off = pl.multiple_of(j * tk, tk)
        kbias = jnp.where(q_cols, one, kb_ref[0, pl.ds(off, tk), :])
        ka = jnp.concatenate([k_ref[0, pl.ds(off, tk), :], kbias], axis=1)
        return jnp.dot(ka, qats[hh][:, c0:], preferred_element_type=F32)

    def consume(hh, carry, s, j, diagonal):
        if diagonal:
            s_tri = jnp.where(causal, s[:, :tk], -1e30)
            s = s_tri if s.shape[1] == tk else jnp.concatenate([s_tri, s[:, tk:]], axis=1)
        vt = vt_ref[0, j, HEAD_DIM * hh:HEAD_DIM * (hh + 1), :]
        if online:
            m, l, acc = carry
            m_new = jnp.maximum(m, jnp.max(s, axis=0, keepdims=True))
            alpha = jnp.exp2(m - m_new)
            p = jnp.exp2(s - m_new)
            l = alpha * l + jnp.sum(p, axis=0, keepdims=True)
            acc = alpha * acc + jnp.dot(vt, p.astype(BF16), preferred_element_type=F32)
            return m_new, l, acc
        l, acc = carry
        p = jnp.exp2(s)
        l = l + jnp.sum(p, axis=0, keepdims=True)
        acc = acc + jnp.dot(vt, p.astype(BF16), preferred_element_type=F32)
        return l, acc

    init = (jnp.zeros((1, tq), F32), jnp.zeros((HEAD_DIM, tq), F32))
    if online:
        init = (jnp.full((1, tq), -1e30, F32),) + init

    carries = []
    for hh in heads:
        if online:
            jj0 = 0
        else:
            head = HEADS_PER_STEP * pl.program_id(1) + hh
            jj0 = first_ref[(pl.program_id(0) * N_HEADS + head) * pl.num_programs(2) + i] // 2
        s0_scr[hh] = scores(hh, 2 * jj0)

        def pair(jj, c, hh=hh):
            s1_scr[hh] = scores(hh, 2 * jj + 1)
            c = consume(hh, c, s0_scr[hh], 2 * jj, False)
            s0_scr[hh] = scores(hh, 2 * jj + 2)
            return consume(hh, c, s1_scr[hh], 2 * jj + 1, False)

        carries.append(lax.fori_loop(jj0, n_sub * i // 2, pair, init))

    j0 = n_sub * i
    later = [[scores(hh, j0 + dd, tk * dd) for hh in heads] for dd in range(1, n_sub)]
    carries = [consume(hh, carries[hh], s0_scr[hh], j0, True) for hh in heads]
    for dd in range(1, n_sub):
        c0 = tk * dd
        for hh in heads:
            part = consume(hh, tuple(c[:, c0:] for c in carries[hh]), later[dd - 1][hh],
                           j0 + dd, True)
            carries[hh] = tuple(jnp.concatenate([c[:, :c0], pc], axis=1)
                                for c, pc in zip(carries[hh], part))
    o_t = jnp.concatenate([carry[-1] / carry[-2] for carry in carries], axis=0)
    o_ref[0] = o_t.T.astype(BF16)


def _attention(first, qt, qbt, k, kb, vt, *, tq, tk, online):
    b, s, d = k.shape
    assert (tq // tk) % 2 == 0 and s % tq == 0
    n_pairs = d // LANES
    nk = s // tk
    qt_blk = pl.BlockSpec((1, LANES, tq), lambda bi, hp, i, first: (bi, hp, i))
    k_all = pl.BlockSpec((1, s, LANES), lambda bi, hp, i, first: (bi, 0, hp))
    return pl.pallas_call(
        functools.partial(_attn_kernel, tq=tq, tk=tk, online=online),
        grid_spec=pltpu.PrefetchScalarGridSpec(
            num_scalar_prefetch=1,
            grid=(b, n_pairs, s // tq),
            in_specs=[qt_blk, qt_blk, k_all, k_all,
                      pl.BlockSpec((1, nk, LANES, tk), lambda bi, hp, i, first: (bi, 0, hp, 0))],
            out_specs=pl.BlockSpec((1, tq, LANES), lambda bi, hp, i, first: (bi, i, hp)),
            scratch_shapes=[pltpu.VMEM((HEADS_PER_STEP, tk, tq), F32),
                            pltpu.VMEM((HEADS_PER_STEP, tk, tq), F32)]),
        out_shape=jax.ShapeDtypeStruct((b, s, d), BF16),
        compiler_params=pltpu.CompilerParams(
            dimension_semantics=("arbitrary", "arbitrary", "arbitrary"),
            vmem_limit_bytes=VMEM_LIMIT),
        name="fox_attention_online" if online else "fox_attention",
    )(first, qt, qbt, k, kb, vt)


def _permute_token_blocks(perm, a):
    w = perm.shape[0]
    return jnp.concatenate(
        [jnp.dot(perm, a[w * i:w * (i + 1)], preferred_element_type=F32).astype(BF16)
         for i in range(a.shape[0] // w)], axis=0)


def _mix_mlp_kernel(x_ref, a_ref, wmix_ref, g_ref, w1_ref, w2_ref, *rest, glu, d, ff_chunk):
    if glu:
        a = _permute_token_blocks(rest[0][...], a_ref[...])
        mix = jnp.dot(a, wmix_ref[...], preferred_element_type=F32)
        mix = mix[:, :d] * jax.nn.sigmoid(mix[:, d:])
    else:
        mix = jnp.dot(a_ref[...], wmix_ref[...], preferred_element_type=F32)
    x1 = x_ref[...] + mix
    h = _rms_norm(x1, g_ref[...]).astype(BF16)
    acc = x1
    for c in range(w1_ref.shape[1] // ff_chunk):
        sl = slice(ff_chunk * c, ff_chunk * (c + 1))
        hid = jnp.maximum(jnp.dot(h, w1_ref[:, sl], preferred_element_type=F32), 0.0)
        acc = acc + jnp.dot((hid * hid).astype(BF16), w2_ref[sl, :], preferred_element_type=F32)
    if glu:
        _, o_ref = rest
    else:
        perm_ref, g_next_ref, w_next_ref, o_ref, u_ref = rest
        h_next = _rms_norm(acc, g_next_ref[...]).astype(BF16)
        u = jnp.dot(h_next, w_next_ref[...], preferred_element_type=F32).astype(BF16)
        u_ref[...] = _permute_token_blocks(perm_ref[...], u)
    o_ref[...] = acc


def _mix_mlp(x, a, wmix, g, w1, w2, perm, next_proj=None, *, glu, tm, ff_chunk=1024):
    t, d = x.shape
    row = lambda i: (i, 0)
    single = pl.Buffered(1)
    wspec = lambda shape: pl.BlockSpec(shape, lambda i: (0, 0), pipeline_mode=single)
    operands = [x, a, wmix, g, w1, w2, perm]
    in_specs = [pl.BlockSpec((tm, d), row), pl.BlockSpec((tm, d), row),
                wspec(wmix.shape), wspec((1, d)), wspec(w1.shape), wspec(w2.shape),
                wspec(perm.shape)]
    out_specs = pl.BlockSpec((tm, d), row)
    out_shape = jax.ShapeDtypeStruct((t, d), F32)
    if next_proj is not None:
        operands += list(next_proj)
        in_specs += [wspec(w.shape) for w in next_proj]
        n = next_proj[1].shape[1]
        out_specs = [out_specs, pl.BlockSpec((tm, n), row)]
        out_shape = [out_shape, jax.ShapeDtypeStruct((t, n), BF16)]
    return pl.pallas_call(
        functools.partial(_mix_mlp_kernel, glu=glu, d=d, ff_chunk=ff_chunk),
        grid=(t // tm,),
        in_specs=in_specs,
        out_specs=out_specs,
        out_shape=out_shape,
        compiler_params=pltpu.CompilerParams(
            dimension_semantics=("arbitrary",), vmem_limit_bytes=VMEM_LIMIT),
        name="mix_glu_mlp" if glu else "mix_mlp",
    )(*operands)


GROUPS_PER_SLAB = LANES // SSM_GROUP


def _piece_transpose(arrs, piece):
    arrs = list(arrs)
    dist = GROUPS_PER_SLAB // 2
    while dist >= 1:
        keep = (piece & dist) == 0
        shift = SSM_GROUP * dist
        for i in range(GROUPS_PER_SLAB):
            if i & dist:
                continue
            a, b = arrs[i], arrs[i + dist]
            arrs[i] = jnp.where(keep, a, pltpu.roll(b, shift, axis=1))
            arrs[i + dist] = jnp.where(keep, pltpu.roll(a, LANES - shift, axis=1), b)
        dist //= 2
    return arrs


def _ssm_kernel(u_ref, cc_ref, bt_ref, pw1_ref, pw2_ref, zoh_ref, a1_ref, a2_ref,
                dv_ref, z_ref, ug_scr, zg_scr, mi_scr, ws_scr, wo_scr, *, n_chunks):
    L = SSM_CHUNK
    rows = u_ref.shape[0] // L
    w = L * SSM_GROUP
    half = SSM_STATE
    nb = math.gcd(rows // L, 8)
    piece = lax.broadcasted_iota(jnp.int32, (nb * L, LANES), 1) // SSM_GROUP

    def relayout_in(rb, carry):
        t0 = pl.multiple_of(rb * nb * w, nb * w)
        r0 = pl.multiple_of(rb * nb * L, nb * L)
        by_pos = u_ref[pl.ds(t0, nb * w), :].astype(F32)
        for hf in range(L // GROUPS_PER_SLAB):
            arrs = []
            for k in range(GROUPS_PER_SLAB):
                pos = GROUPS_PER_SLAB * hf + k
                arrs.append(jnp.concatenate(
                    [by_pos[w * blk + L * pos:w * blk + L * (pos + 1)] for blk in range(nb)],
                    axis=0))
            arrs = _piece_transpose(arrs, piece)
            for g in range(GROUPS_PER_SLAB):
                ug_scr[g, pl.ds(r0, nb * L), LANES * hf:LANES * (hf + 1)] = arrs[g].astype(BF16)
        return carry

    lax.fori_loop(0, rows // (nb * L), relayout_in, 0)

    chunk = lax.broadcasted_iota(jnp.int32, (rows, LANES), 0) % n_chunks
    lane_w = lax.broadcasted_iota(jnp.int32, (SSM_GROUP, w), 1)
    sign = jnp.where(lax.broadcasted_iota(jnp.int32, (1, LANES), 1) < half, 1.0, -1.0)

    def group(g, carry):
        cc = cc_ref[g]
        bt = bt_ref[g]
        pw1 = pw1_ref[g]
        pw2 = pw2_ref[g]
        zoh = zoh_ref[g]
        bbar = bt * zoh[0:1] + pltpu.roll(bt, half, axis=1) * zoh[1:2]
        bbar_sw = pltpu.roll(bbar, half, axis=1)
        cc_sw = pltpu.roll(cc, half, axis=1)
        ca = [cc * pw1[k:k + 1] + cc_sw * pw2[k:k + 1] for k in range(L + 1)]
        k_all = lax.dot_general(bbar * sign, jnp.concatenate(ca[:L], axis=0), NT_DIMS,
                                precision=lax.Precision.HIGHEST, preferred_element_type=F32)
        for s in range(L):
            blk = k_all if s == 0 else jnp.where(
                lane_w >= SSM_GROUP * s, pltpu.roll(k_all, SSM_GROUP * s, axis=1), 0.0)
            mi_scr[SSM_GROUP * s:SSM_GROUP * (s + 1), :] = blk.astype(BF16)
            k = L - 1 - s
            ws_scr[SSM_GROUP * s:SSM_GROUP * (s + 1), :] = (
                bbar * pw1[k:k + 1] + bbar_sw * pw2[k:k + 1]).astype(BF16)
            wo_scr[SSM_GROUP * s:SSM_GROUP * (s + 1), :] = (ca[s + 1] * sign).astype(BF16)

        u = ug_scr[g]
        y = jnp.dot(u, mi_scr[...], preferred_element_type=F32)
        xs = jnp.dot(u, ws_scr[...], preferred_element_type=F32)
        a1 = a1_ref[g]
        a2 = a2_ref[g]
        step = 1
        j = 0
        while step < n_chunks:
            sh = jnp.where(chunk >= step, pltpu.roll(xs, step, axis=0), 0.0)
            xs = xs + sh * a1[j:j + 1] + pltpu.roll(sh, half, axis=1) * a2[j:j + 1]
            step *= 2
            j += 1
        x_in = jnp.where(chunk >= 1, pltpu.roll(xs, 1, axis=0), 0.0)
        y = y + lax.dot_general(x_in.astype(BF16), wo_scr[...], NT_DIMS,
                                preferred_element_type=F32)
        y = y + u.astype(F32) * dv_ref[g]
        zg_scr[g] = jax.nn.gelu(y).astype(BF16)
        return carry

    lax.fori_loop(0, GROUPS_PER_SLAB, group, 0)

    def relayout_out(rb, carry):
        t0 = pl.multiple_of(rb * nb * w, nb * w)
        r0 = pl.multiple_of(rb * nb * L, nb * L)
        by_pos = []
        for hf in range(L // GROUPS_PER_SLAB):
            arrs = [zg_scr[g, pl.ds(r0, nb * L), LANES * hf:LANES * (hf + 1)].astype(F32)
                    for g in range(GROUPS_PER_SLAB)]
            by_pos += _piece_transpose(arrs, piece)
        for blk in range(nb):
            z_ref[pl.ds(t0 + w * blk, w), :] = jnp.concatenate(
                [arr[L * blk:L * (blk + 1)] for arr in by_pos], axis=0).astype(BF16)
        return carry

    lax.fori_loop(0, rows // (nb * L), relayout_out, 0)


def _ssm_scan(u, cc, bt, pw1, pw2, zoh, a1, a2, dv, *, n_chunks):
    t, d = u.shape
    L = SSM_CHUNK
    rows = t // L
    w = L * SSM_GROUP
    gps = GROUPS_PER_SLAB
    slab = lambda j: (j, 0, 0)
    pspec = lambda arr: pl.BlockSpec((gps,) + arr.shape[1:], slab)
    return pl.pallas_call(
        functools.partial(_ssm_kernel, n_chunks=n_chunks),
        grid=(d // LANES,),
        in_specs=[pl.BlockSpec((t, LANES), lambda j: (0, j)),
                  pspec(cc), pspec(bt), pspec(pw1), pspec(pw2), pspec(zoh), pspec(a1), pspec(a2),
                  pspec(dv)],
        out_specs=pl.BlockSpec((t, LANES), lambda j: (0, j)),
        out_shape=jax.ShapeDtypeStruct((t, d), BF16),
        scratch_shapes=[pltpu.VMEM((gps, rows, w), BF16), pltpu.VMEM((gps, rows, w), BF16),
                        pltpu.VMEM((w, w), BF16), pltpu.VMEM((w, LANES), BF16),
                        pltpu.VMEM((w, LANES), BF16)],
        compiler_params=pltpu.CompilerParams(
            dimension_semantics=("arbitrary",), vmem_limit_bytes=VMEM_LIMIT),
        name="s5_scan",
    )(u, cc, bt, pw1, pw2, zoh, a1, a2, dv)


def _ssm_operators(a_re, a_im, b_re, b_im, c_re, c_im, log_dt, d_skip, n_chunks):
    L = SSM_CHUNK
    g, p = a_re.shape
    dt = jnp.exp(log_dt)[:, None]
    lam_re, lam_im = dt * a_re, dt * a_im

    def powers(ks):
        ks = jnp.asarray(ks, F32)[:, None, None]
        mag = jnp.exp(ks * lam_re)
        return mag * jnp.cos(ks * lam_im), mag * jnp.sin(ks * lam_im)

    def patterns(re, im):
        return (jnp.concatenate([re, re], axis=2).transpose(1, 0, 2),
                jnp.concatenate([-im, im], axis=2).transpose(1, 0, 2))

    pr, pi = powers(range(L + 1))
    num_re, num_im = pr[1] - 1.0, pi[1]
    den = a_re * a_re + a_im * a_im
    s_re = (num_re * a_re + num_im * a_im) / den
    s_im = (num_im * a_re - num_re * a_im) / den
    pw1, pw2 = patterns(pr, pi)
    zoh = jnp.concatenate(patterns(s_re[None], s_im[None]), axis=1)
    n_steps = max(1, int(math.log2(n_chunks)))
    a1, a2 = patterns(*powers([L * 2 ** j for j in range(n_steps)]))
    cc = jnp.concatenate([c_re, c_im], axis=2)
    bt = jnp.concatenate([b_re.transpose(0, 2, 1), b_im.transpose(0, 2, 1)], axis=2)
    dv = jnp.tile(d_skip.reshape(g, 1, SSM_GROUP), (1, L, 1)).reshape(g, 1, L * SSM_GROUP)
    return cc, bt, pw1, pw2, zoh, a1, a2, dv


def kernel(x, norm_mix_g, norm_mlp_g, attn_w_in, attn_b_f, attn_q_g, attn_k_g, attn_w_out,
           ssm_w_in, ssm_a_re, ssm_a_im, ssm_b_re, ssm_b_im, ssm_c_re, ssm_c_im, ssm_log_dt,
           ssm_d, ssm_w_glu, mlp_w1, mlp_w2):
    b, s, d = x.shape
    t = b * s
    tk = min(512, s // 2)
    tq = 2 * tk
    tm_proj = min(512, s)
    tm_mlp = min(512, t)

    w_in = attn_w_in[0]
    w_f = jnp.pad(w_in[:, 3 * d:], ((0, 0), (0, LANES - N_HEADS)))
    wkf = jnp.concatenate([w_in[:, d:2 * d], w_f], axis=1).astype(BF16)
    wqvt = jnp.concatenate([w_in[:, :d], w_in[:, 2 * d:3 * d]], axis=1).T.astype(BF16)
    blk = jnp.arange(MXU_DIM) // HEAD_DIM
    bd = (blk[:, None] == blk[None, :]).astype(BF16)
    gain_k = jnp.tile(attn_k_g[0], N_HEADS).reshape(1, d)
    gain_qt = jnp.broadcast_to(
        (jnp.tile(attn_q_g[0], N_HEADS) * (LOG2E / math.sqrt(HEAD_DIM)))[:, None], (d, LANES))
    bfp = jnp.pad(attn_b_f[0], (0, LANES - N_HEADS)).reshape(1, LANES)
    tri = (jnp.arange(tm_proj)[:, None] >= jnp.arange(tm_proj)[None, :]).astype(BF16)
    heads = jnp.arange(N_HEADS)
    place_k = jnp.zeros((LANES, d), F32)
    for i in range(BIAS_PIECES):
        place_k = place_k.at[N_HEADS * i + heads,
                             (heads // HEADS_PER_STEP) * LANES
                             + BIAS_PIECES * (heads % HEADS_PER_STEP) + i].set(1.0)
    place_qt = jnp.roll(place_k, BIAS_Q_LANE, axis=1).T
    shift = (LOG2E * math.sqrt(HEAD_DIM) * jnp.max(jnp.abs(attn_q_g[0]))
             * jnp.max(jnp.abs(attn_k_g[0])))
    qt, k, vt, kb, qbt, edge = _qkv_proj(
        x, norm_mix_g[0].reshape(1, d), wkf, wqvt, bd, gain_k, gain_qt, bfp, tri,
        place_k.astype(BF16), place_qt.astype(BF16), jnp.full((1, LANES), shift, F32),
        tm=tm_proj, tk=tk)
    assert tm_proj == tk
    c_first = edge[:, ::tq // tk, 0, :N_HEADS]
    c_last = edge[:, :, 1, :N_HEADS]
    bound = (c_first[:, :, None, :] - c_last[:, None, :, :]) * LOG2E
    dead = (bound < SKIP_LOGIT).astype(jnp.int32)
    first = jnp.sum(jnp.cumprod(dead, axis=2), axis=2)
    first = jnp.minimum(first, (tq // tk) * jnp.arange(s // tq)[None, :, None]) // 2 * 2
    first = first.transpose(0, 2, 1).reshape(-1).astype(jnp.int32)
    o = lax.cond(shift <= MAX_SHIFT,
                 functools.partial(_attention, tq=tq, tk=tk, online=False),
                 functools.partial(_attention, tq=tq, tk=tk, online=True),
                 first, qt, qbt, k, kb, vt)
    tok = jnp.arange(SSM_CHUNK * SSM_CHUNK)
    perm = (tok[None, :] == (tok[:, None] % SSM_CHUNK) * SSM_CHUNK + tok[:, None] // SSM_CHUNK)
    perm = perm.astype(BF16)
    x2, u = _mix_mlp(x.reshape(t, d), o.reshape(t, d), attn_w_out[0].astype(BF16),
                     norm_mlp_g[0].reshape(1, d), mlp_w1[0].astype(BF16), mlp_w2[0].astype(BF16),
                     perm, (norm_mix_g[1].reshape(1, d), ssm_w_in[0].astype(BF16)),
                     glu=False, tm=tm_mlp)

    n_chunks = s // SSM_CHUNK
    ops = _ssm_operators(ssm_a_re[0], ssm_a_im[0], ssm_b_re[0], ssm_b_im[0], ssm_c_re[0],
                         ssm_c_im[0], ssm_log_dt[0], ssm_d[0], n_chunks)
    z = _ssm_scan(u, *ops, n_chunks=n_chunks)
    x3 = _mix_mlp(x2, z, ssm_w_glu[0].astype(BF16), norm_mlp_g[1].reshape(1, d),
                  mlp_w1[1].astype(BF16), mlp_w2[1].astype(BF16), perm, glu=True, tm=tm_mlp)
    return x3.reshape(b, s, d)
```

```python
import functools
import math

import jax
import jax.numpy as jnp
from jax import lax
from jax.experimental import pallas as pl
from jax.experimental.pallas import tpu as pltpu

F32 = jnp.float32
BF16 = jnp.bfloat16

N_HEADS = 16
HEAD_DIM = 64
SSM_GROUP = 16
SSM_STATE = 64
SSM_CHUNK = 16
EPS = 1e-6
LOG2E = 1.4426950408889634

LANES = 128
MXU_DIM = 256
HEADS_PER_STEP = LANES // HEAD_DIM
BIAS_PIECES = 3
VMEM_LIMIT = 56 * 1024 * 1024

NT_DIMS = (((1,), (1,)), ((), ()))


def _rms_norm(x, g):
    ms = jnp.mean(x * x, axis=-1, keepdims=True)
    return x * lax.rsqrt(ms + EPS) * g


def _const_spec(shape):
    zeros = (0,) * len(shape)
    return pl.BlockSpec(shape, lambda *_: zeros)


def _bf16_pieces(val):
    pieces = jnp.zeros_like(val)
    rem = val
    for n in range(BIAS_PIECES):
        piece = rem.astype(BF16).astype(F32)
        rem = rem - piece
        pieces = pieces + (piece if n == 0 else pltpu.roll(piece, N_HEADS * n, axis=1))
    return pieces


def _qkv_kernel(x_ref, g_ref, wkf_ref, wqvt_ref, bd_ref, gain_k_ref, gain_qt_ref, bf_ref, tri_ref,
                place_k_ref, place_qt_ref, shift_ref,
                qt_ref, k_ref, vt_ref, kb_ref, qbt_ref, edge_ref, carry_ref, *, tm, tk, d):
    @pl.when(pl.program_id(1) == 0)
    def _():
        carry_ref[...] = jnp.zeros_like(carry_ref)

    h = _rms_norm(x_ref[0], g_ref[...]).astype(BF16)

    y = jnp.dot(h, wkf_ref[...], preferred_element_type=F32)
    vt = lax.dot_general(wqvt_ref[d:, :], h, NT_DIMS, preferred_element_type=F32).astype(BF16)
    for j in range(tm // tk):
        vt_ref[0, j] = vt[:, tk * j:tk * (j + 1)]
    qt = lax.dot_general(wqvt_ref[:d, :], h, NT_DIMS, preferred_element_type=F32)

    for t in range(d // MXU_DIM):
        sl = slice(MXU_DIM * t, MXU_DIM * (t + 1))
        tile = y[:, sl]
        ms = jnp.dot((tile * tile).astype(BF16), bd_ref[...], preferred_element_type=F32)
        k_ref[0, :, sl] = (tile * lax.rsqrt(ms + EPS) * gain_k_ref[:, sl]).astype(BF16)
        tile = qt[sl, :]
        ms = jnp.dot(bd_ref[...], (tile * tile).astype(BF16), preferred_element_type=F32)
        gain = jnp.tile(gain_qt_ref[sl, :], (1, tm // LANES))
        qt_ref[0, sl, :] = (tile * lax.rsqrt(ms + EPS) * gain).astype(BF16)

    f = y[:, d:] + bf_ref[...]
    log_f = jnp.minimum(f, 0.0) - jnp.log1p(jnp.exp(-jnp.abs(f)))
    lane = lax.broadcasted_iota(jnp.int32, log_f.shape, 1)
    part = jnp.dot(tri_ref[...], _bf16_pieces(jnp.where(lane < N_HEADS, log_f, 0.0)).astype(BF16),
                   preferred_element_type=F32)
    cs = carry_ref[...] + part
    for n in range(1, BIAS_PIECES):
        cs = cs + pltpu.roll(part, LANES - N_HEADS * n, axis=1)
    carry_ref[...] = cs[tm - 1:tm, :]
    edge_ref[0, 0] = jnp.concatenate(
        [cs[0:1, :], cs[tm - 1:tm, :], jnp.zeros((6, LANES), F32)], axis=0)
    c2 = jnp.where(lane < N_HEADS, cs * LOG2E, 0.0)
    kb_ref[0] = jnp.dot(_bf16_pieces(-c2).astype(BF16), place_k_ref[...],
                        preferred_element_type=F32).astype(BF16)
    q_pieces = _bf16_pieces(jnp.where(lane < N_HEADS, c2 - shift_ref[...], 0.0))
    qbt_ref[0] = jnp.dot(place_qt_ref[...], q_pieces.T.astype(BF16),
                         preferred_element_type=F32).astype(BF16)


def _qkv_proj(x, g, wkf, wqvt, bd, gain_k, gain_qt, bfp, tri, place_k, place_qt, shift, *, tm, tk):
    b, s, d = x.shape
    row = lambda bi, si: (bi, si, 0)
    col = lambda bi, si: (bi, 0, si)
    consts = (g, wkf, wqvt, bd, gain_k, gain_qt, bfp, tri, place_k, place_qt, shift)
    return pl.pallas_call(
        functools.partial(_qkv_kernel, tm=tm, tk=tk, d=d),
        grid=(b, s // tm),
        in_specs=[pl.BlockSpec((1, tm, d), row)] + [_const_spec(c.shape) for c in consts],
        out_specs=[pl.BlockSpec((1, d, tm), col),
                   pl.BlockSpec((1, tm, d), row),
                   pl.BlockSpec((1, tm // tk, d, tk), lambda bi, si: (bi, si, 0, 0)),
                   pl.BlockSpec((1, tm, d), row),
                   pl.BlockSpec((1, d, tm), col),
                   pl.BlockSpec((1, 1, 8, LANES), lambda bi, si: (bi, si, 0, 0))],
        out_shape=[jax.ShapeDtypeStruct((b, d, s), BF16),
                   jax.ShapeDtypeStruct((b, s, d), BF16),
                   jax.ShapeDtypeStruct((b, s // tk, d, tk), BF16),
                   jax.ShapeDtypeStruct((b, s, d), BF16),
                   jax.ShapeDtypeStruct((b, d, s), BF16),
                   jax.ShapeDtypeStruct((b, s // tm, 8, LANES), F32)],
        scratch_shapes=[pltpu.VMEM((1, LANES), F32)],
        compiler_params=pltpu.CompilerParams(
            dimension_semantics=("arbitrary", "arbitrary"), vmem_limit_bytes=VMEM_LIMIT),
        name="qkv_proj",
    )(x, *consts)


BIAS_Q_LANE = 8
MAX_SHIFT = 48.0
SKIP_LOGIT = -200.0


def _attn_kernel(first_ref, qt_ref, qbt_ref, k_ref, kb_ref, vt_ref, o_ref, s0_scr, s1_scr, *,
                 tq, tk, online):
    i = pl.program_id(2)
    n_sub = tq // tk
    qt = qt_ref[0]
    qbt = qbt_ref[0]
    row = lax.broadcasted_iota(jnp.int32, (LANES, tq), 0)
    lane_k = lax.broadcasted_iota(jnp.int32, (tk, LANES), 1)
    q_cols = (lane_k >= BIAS_Q_LANE) & (lane_k < BIAS_Q_LANE + HEADS_PER_STEP * BIAS_PIECES)
    causal = (lax.broadcasted_iota(jnp.int32, (tk, tk), 0)
              <= lax.broadcasted_iota(jnp.int32, (tk, tk), 1))
    one = jnp.ones((), BF16)
    zero = jnp.zeros((), BF16)
    heads = range(HEADS_PER_STEP)
    qats = []
    for hh in heads:
        q_h = jnp.where((row >= HEAD_DIM * hh) & (row < HEAD_DIM * (hh + 1)), qt, zero)
        k_side = (row >= BIAS_PIECES * hh) & (row < BIAS_PIECES * (hh + 1))
        q_lo = BIAS_Q_LANE + BIAS_PIECES * hh
        q_side = (row >= q_lo) & (row < q_lo + BIAS_PIECES)
        qats.append(jnp.concatenate(
            [q_h, jnp.where(k_side, one, jnp.where(q_side, qbt, zero))], axis=0))

    def scores(hh, j, c0=0):
        off = pl.multiple_of(j * tk, tk)
        kbias = jnp.where(q_cols, one, kb_ref[0, pl.ds(off, tk), :])
        ka = jnp.concatenate([k_ref[0, pl.ds(off, tk), :], kbias], axis=1)
        return jnp.dot(ka, qats[hh][:, c0:], preferred_element_type=F32)

    def consume(hh, carry, s, j, diagonal):
        if diagonal:
            s_tri = jnp.where(causal, s[:, :tk], -1e30)
            s = s_tri if s.shape[1] == tk else jnp.concatenate([s_tri, s[:, tk:]], axis=1)
        vt = vt_ref[0, j, HEAD_DIM * hh:HEAD_DIM * (hh + 1), :]
        if online:
            m, l, acc = carry
            m_new = jnp.maximum(m, jnp.max(s, axis=0, keepdims=True))
            alpha = jnp.exp2(m - m_new)
            p = jnp.exp2(s - m_new)
            l = alpha * l + jnp.sum(p, axis=0, keepdims=True)
            acc = alpha * acc + jnp.dot(vt, p.astype(BF16), preferred_element_type=F32)
            return m_new, l, acc
        l, acc = carry
        p = jnp.exp2(s)
        l = l + jnp.sum(p, axis=0, keepdims=True)
        acc = acc + jnp.dot(vt, p.astype(BF16), preferred_element_type=F32)
        return l, acc

    init = (jnp.zeros((1, tq), F32), jnp.zeros((HEAD_DIM, tq), F32))
    if online:
        init = (jnp.full((1, tq), -1e30, F32),) + init

    carries = []
    for hh in heads:
        if online:
            jj0 = 0
        else:
            head = HEADS_PER_STEP * pl.program_id(1) + hh
            jj0 = first_ref[(pl.program_id(0) * N_HEADS + head) * pl.num_programs(2) + i] // 2
        s0_scr[hh] = scores(hh, 2 * jj0)

        def pair(jj, c, hh=hh):
            s1_scr[hh] = scores(hh, 2 * jj + 1)
            c = consume(hh, c, s0_scr[hh], 2 * jj, False)
            s0_scr[hh] = scores(hh, 2 * jj + 2)
            return consume(hh, c, s1_scr[hh], 2 * jj + 1, False)

        carries.append(lax.fori_loop(jj0, n_sub * i // 2, pair, init))

    j0 = n_sub * i
    later = [[scores(hh, j0 + dd, tk * dd) for hh in heads] for dd in range(1, n_sub)]
    carries = [consume(hh, carries[hh], s0_scr[hh], j0, True) for hh in heads]
    for dd in range(1, n_sub):
        c0 = tk * dd
        for hh in heads:
            part = consume(hh, tuple(c[:, c0:] for c in carries[hh]), later[dd - 1][hh],
                           j0 + dd, True)
            carries[hh] = tuple(jnp.concatenate([c[:, :c0], pc], axis=1)
                                for c, pc in zip(carries[hh], part))
    o_t = jnp.concatenate([carry[-1] / carry[-2] for carry in carries], axis=0)
    o_ref[0] = o_t.T.astype(BF16)


def _attention(first, qt, qbt, k, kb, vt, *, tq, tk, online):
    b, s, d = k.shape
    assert (tq // tk) % 2 == 0 and s % tq == 0
    n_pairs = d // LANES
    nk = s // tk
    qt_blk = pl.BlockSpec((1, LANES, tq), lambda bi, hp, i, first: (bi, hp, i))
    k_all = pl.BlockSpec((1, s, LANES), lambda bi, hp, i, first: (bi, 0, hp))
    return pl.pallas_call(
        functools.partial(_attn_kernel, tq=tq, tk=tk, online=online),
        grid_spec=pltpu.PrefetchScalarGridSpec(
            num_scalar_prefetch=1,
            grid=(b, n_pairs, s // tq),
            in_specs=[qt_blk, qt_blk, k_all, k_all,
                      pl.BlockSpec((1, nk, LANES, tk), lambda bi, hp, i, first: (bi, 0, hp, 0))],
            out_specs=pl.BlockSpec((1, tq, LANES), lambda bi, hp, i, first: (bi, i, hp)),
            scratch_shapes=[pltpu.VMEM((HEADS_PER_STEP, tk, tq), F32),
                            pltpu.VMEM((HEADS_PER_STEP, tk, tq), F32)]),
        out_shape=jax.ShapeDtypeStruct((b, s, d), BF16),
        compiler_params=pltpu.CompilerParams(
            dimension_semantics=("arbitrary", "arbitrary", "arbitrary"),
            vmem_limit_bytes=VMEM_LIMIT),
        name="fox_attention_online" if online else "fox_attention",
    )(first, qt, qbt, k, kb, vt)


def _permute_token_blocks(perm, a):
    w = perm.shape[0]
    return jnp.concatenate(
        [jnp.dot(perm, a[w * i:w * (i + 1)], preferred_element_type=F32).astype(BF16)
         for i in range(a.shape[0] // w)], axis=0)


def _mix_mlp_kernel(x_ref, a_ref, wmix_ref, g_ref, w1_ref, w2_ref, *rest, glu, d, ff_chunk):
    if glu:
        a = _permute_token_blocks(rest[0][...], a_ref[...])
        mix = jnp.dot(a, wmix_ref[...], preferred_element_type=F32)
        mix = mix[:, :d] * jax.nn.sigmoid(mix[:, d:])
    else:
        mix = jnp.dot(a_ref[...], wmix_ref[...], preferred_element_type=F32)
    x1 = x_ref[...] + mix
    h = _rms_norm(x1, g_ref[...]).astype(BF16)
    acc = x1
    for c in range(w1_ref.shape[1] // ff_chunk):
        sl = slice(ff_chunk * c, ff_chunk * (c + 1))
        hid = jnp.maximum(jnp.dot(h, w1_ref[:, sl], preferred_element_type=F32), 0.0)
        acc = acc + jnp.dot((hid * hid).astype(BF16), w2_ref[sl, :], preferred_element_type=F32)
    if glu:
        _, o_ref = rest
    else:
        perm_ref, g_next_ref, w_next_ref, o_ref, u_ref = rest
        h_next = _rms_norm(acc, g_next_ref[...]).astype(BF16)
        u = jnp.dot(h_next, w_next_ref[...], preferred_element_type=F32).astype(BF16)
        u_ref[...] = _permute_token_blocks(perm_ref[...], u)
    o_ref[...] = acc


def _mix_mlp(x, a, wmix, g, w1, w2, perm, next_proj=None, *, glu, tm, ff_chunk=1024):
    t, d = x.shape
    row = lambda i: (i, 0)
    single = pl.Buffered(1)
    wspec = lambda shape: pl.BlockSpec(shape, lambda i: (0, 0), pipeline_mode=single)
    operands = [x, a, wmix, g, w1, w2, perm]
    in_specs = [pl.BlockSpec((tm, d), row), pl.BlockSpec((tm, d), row),
                wspec(wmix.shape), wspec((1, d)), wspec(w1.shape), wspec(w2.shape),
                wspec(perm.shape)]
    out_specs = pl.BlockSpec((tm, d), row)
    out_shape = jax.ShapeDtypeStruct((t, d), F32)
    if next_proj is not None:
        operands += list(next_proj)
        in_specs += [wspec(w.shape) for w in next_proj]
        n = next_proj[1].shape[1]
        out_specs = [out_specs, pl.BlockSpec((tm, n), row)]
        out_shape = [out_shape, jax.ShapeDtypeStruct((t, n), BF16)]
    return pl.pallas_call(
        functools.partial(_mix_mlp_kernel, glu=glu, d=d, ff_chunk=ff_chunk),
        grid=(t // tm,),
        in_specs=in_specs,
        out_specs=out_specs,
        out_shape=out_shape,
        compiler_params=pltpu.CompilerParams(
            dimension_semantics=("arbitrary",), vmem_limit_bytes=VMEM_LIMIT),
        name="mix_glu_mlp" if glu else "mix_mlp",
    )(*operands)


GROUPS_PER_SLAB = LANES // SSM_GROUP


def _piece_transpose(arrs, piece):
    arrs = list(arrs)
    dist = GROUPS_PER_SLAB // 2
    while dist >= 1:
        keep = (piece & dist) == 0
        shift = SSM_GROUP * dist
        for i in range(GROUPS_PER_SLAB):
            if i & dist:
                continue
            a, b = arrs[i], arrs[i + dist]
            arrs[i] = jnp.where(keep, a, pltpu.roll(b, shift, axis=1))
            arrs[i + dist] = jnp.where(keep, pltpu.roll(a, LANES - shift, axis=1), b)
        dist //= 2
    return arrs


def _ssm_kernel(u_ref, cc_ref, bt_ref, pw1_ref, pw2_ref, zoh_ref, a1_ref, a2_ref,
                dv_ref, z_ref, ug_scr, zg_scr, *, n_chunks):
    L = SSM_CHUNK
    rows = u_ref.shape[0] // L
    w = L * SSM_GROUP
    half = SSM_STATE
    nb = math.gcd(rows // L, 8)
    piece = lax.broadcasted_iota(jnp.int32, (nb * L, LANES), 1) // SSM_GROUP

    def relayout_in(rb, carry):
        t0 = pl.multiple_of(rb * nb * w, nb * w)
        r0 = pl.multiple_of(rb * nb * L, nb * L)
        by_pos = u_ref[pl.ds(t0, nb * w), :].astype(F32)
        for hf in range(L // GROUPS_PER_SLAB):
            arrs = []
            for k in range(GROUPS_PER_SLAB):
                pos = GROUPS_PER_SLAB * hf + k
                arrs.append(jnp.concatenate(
                    [by_pos[w * blk + L * pos:w * blk + L * (pos + 1)] for blk in range(nb)],
                    axis=0))
            arrs = _piece_transpose(arrs, piece)
            for g in range(GROUPS_PER_SLAB):
                ug_scr[g, pl.ds(r0, nb * L), LANES * hf:LANES * (hf + 1)] = arrs[g].astype(BF16)
        return carry

    lax.fori_loop(0, rows // (nb * L), relayout_in, 0)

    chunk = lax.broadcasted_iota(jnp.int32, (rows, LANES), 0) % n_chunks
    lane_w = lax.broadcasted_iota(jnp.int32, (SSM_GROUP, w), 1)
    sign = jnp.where(lax.broadcasted_iota(jnp.int32, (1, LANES), 1) < half, 1.0, -1.0)

    def group(g, carry):
        cc = cc_ref[g]
        bt = bt_ref[g]
        pw1 = pw1_ref[g]
        pw2 = pw2_ref[g]
        zoh = zoh_ref[g]
        bbar = bt * zoh[0:1] + pltpu.roll(bt, half, axis=1) * zoh[1:2]
        bbar_sw = pltpu.roll(bbar, half, axis=1)
        cc_sw = pltpu.roll(cc, half, axis=1)
        ca = [cc * pw1[k:k + 1] + cc_sw * pw2[k:k + 1] for k in range(L + 1)]
        k_all = lax.dot_general(bbar * sign, jnp.concatenate(ca[:L], axis=0), NT_DIMS,
                                precision=lax.Precision.HIGHEST, preferred_element_type=F32)
        m_intra, w_state, w_out_t = [], [], []
        for s in range(L):
            blk = k_all if s == 0 else jnp.where(
                lane_w >= SSM_GROUP * s, pltpu.roll(k_all, SSM_GROUP * s, axis=1), 0.0)
            m_intra.append(blk.astype(BF16))
            k = L - 1 - s
            w_state.append((bbar * pw1[k:k + 1] + bbar_sw * pw2[k:k + 1]).astype(BF16))
            w_out_t.append((ca[s + 1] * sign).astype(BF16))

        u = ug_scr[g]
        y = jnp.dot(u, jnp.concatenate(m_intra, axis=0), preferred_element_type=F32)
        xs = jnp.dot(u, jnp.concatenate(w_state, axis=0),
                     preferred_element_type=F32)
        a1 = a1_ref[g]
        a2 = a2_ref[g]
        step = 1
        j = 0
        while step < n_chunks:
            sh = jnp.where(chunk >= step, pltpu.roll(xs, step, axis=0), 0.0)
            xs = xs + sh * a1[j:j + 1] + pltpu.roll(sh, half, axis=1) * a2[j:j + 1]
            step *= 2
            j += 1
        x_in = jnp.where(chunk >= 1, pltpu.roll(xs, 1, axis=0), 0.0)
        y = y + lax.dot_general(x_in.astype(BF16), jnp.concatenate(w_out_t, axis=0), NT_DIMS,
                                preferred_element_type=F32)
        y = y + u.astype(F32) * dv_ref[g]
        zg_scr[g] = jax.nn.gelu(y).astype(BF16)
        return carry

    lax.fori_loop(0, GROUPS_PER_SLAB, group, 0)

    def relayout_out(rb, carry):
        t0 = pl.multiple_of(rb * nb * w, nb * w)
        r0 = pl.multiple_of(rb * nb * L, nb * L)
        by_pos = []
        for hf in range(L // GROUPS_PER_SLAB):
            arrs = [zg_scr[g, pl.ds(r0, nb * L), LANES * hf:LANES * (hf + 1)].astype(F32)
                    for g in range(GROUPS_PER_SLAB)]
            by_pos += _piece_transpose(arrs, piece)
        for blk in range(nb):
            z_ref[pl.ds(t0 + w * blk, w), :] = jnp.concatenate(
                [arr[L * blk:L * (blk + 1)] for arr in by_pos], axis=0).astype(BF16)
        return carry

    lax.fori_loop(0, rows // (nb * L), relayout_out, 0)


def _ssm_scan(u, cc, bt, pw1, pw2, zoh, a1, a2, dv, *, n_chunks):
    t, d = u.shape
    L = SSM_CHUNK
    rows = t // L
    w = L * SSM_GROUP
    gps = GROUPS_PER_SLAB
    slab = lambda j: (j, 0, 0)
    pspec = lambda arr: pl.BlockSpec((gps,) + arr.shape[1:], slab)
    return pl.pallas_call(
        functools.partial(_ssm_kernel, n_chunks=n_chunks),
        grid=(d // LANES,),
        in_specs=[pl.BlockSpec((t, LANES), lambda j: (0, j)),
                  pspec(cc), pspec(bt), pspec(pw1), pspec(pw2), pspec(zoh), pspec(a1), pspec(a2),
                  pspec(dv)],
        out_specs=pl.BlockSpec((t, LANES), lambda j: (0, j)),
        out_shape=jax.ShapeDtypeStruct((t, d), BF16),
        scratch_shapes=[pltpu.VMEM((gps, rows, w), BF16), pltpu.VMEM((gps, rows, w), BF16)],
        compiler_params=pltpu.CompilerParams(
            dimension_semantics=("arbitrary",), vmem_limit_bytes=VMEM_LIMIT),
        name="s5_scan",
    )(u, cc, bt, pw1, pw2, zoh, a1, a2, dv)


def _ssm_operators(a_re, a_im, b_re, b_im, c_re, c_im, log_dt, d_skip, n_chunks):
    L = SSM_CHUNK
    g, p = a_re.shape
    dt = jnp.exp(log_dt)[:, None]
    lam_re, lam_im = dt * a_re, dt * a_im

    def powers(ks):
        ks = jnp.asarray(ks, F32)[:, None, None]
        mag = jnp.exp(ks * lam_re)
        return mag * jnp.cos(ks * lam_im), mag * jnp.sin(ks * lam_im)

    def patterns(re, im):
        return (jnp.concatenate([re, re], axis=2).transpose(1, 0, 2),
                jnp.concatenate([-im, im], axis=2).transpose(1, 0, 2))

    pr, pi = powers(range(L + 1))
    num_re, num_im = pr[1] - 1.0, pi[1]
    den = a_re * a_re + a_im * a_im
    s_re = (num_re * a_re + num_im * a_im) / den
    s_im = (num_im * a_re - num_re * a_im) / den
    pw1, pw2 = patterns(pr, pi)
    zoh = jnp.concatenate(patterns(s_re[None], s_im[None]), axis=1)
    n_steps = max(1, int(math.log2(n_chunks)))
    a1, a2 = patterns(*powers([L * 2 ** j for j in range(n_steps)]))
    cc = jnp.concatenate([c_re, c_im], axis=2)
    bt = jnp.concatenate([b_re.transpose(0, 2, 1), b_im.transpose(0, 2, 1)], axis=2)
    dv = jnp.tile(d_skip.reshape(g, 1, SSM_GROUP), (1, L, 1)).reshape(g, 1, L * SSM_GROUP)
    return cc, bt, pw1, pw2, zoh, a1, a2, dv


def kernel(x, norm_mix_g, norm_mlp_g, attn_w_in, attn_b_f, attn_q_g, attn_k_g, attn_w_out,
           ssm_w_in, ssm_a_re, ssm_a_im, ssm_b_re, ssm_b_im, ssm_c_re, ssm_c_im, ssm_log_dt,
           ssm_d, ssm_w_glu, mlp_w1, mlp_w2):
    b, s, d = x.shape
    t = b * s
    tk = min(512, s // 2)
    tq = 2 * tk
    tm_proj = min(512, s)
    tm_mlp = min(512, t)

    w_in = attn_w_in[0]
    w_f = jnp.pad(w_in[:, 3 * d:], ((0, 0), (0, LANES - N_HEADS)))
    wkf = jnp.concatenate([w_in[:, d:2 * d], w_f], axis=1).astype(BF16)
    wqvt = jnp.concatenate([w_in[:, :d], w_in[:, 2 * d:3 * d]], axis=1).T.astype(BF16)
    blk = jnp.arange(MXU_DIM) // HEAD_DIM
    bd = ((blk[:, None] == blk[None, :]) * (1.0 / HEAD_DIM)).astype(BF16)
    gain_k = jnp.tile(attn_k_g[0], N_HEADS).reshape(1, d)
    gain_qt = jnp.broadcast_to(
        (jnp.tile(attn_q_g[0], N_HEADS) * (LOG2E / math.sqrt(HEAD_DIM)))[:, None], (d, LANES))
    bfp = jnp.pad(attn_b_f[0], (0, LANES - N_HEADS)).reshape(1, LANES)
    tri = (jnp.arange(tm_proj)[:, None] >= jnp.arange(tm_proj)[None, :]).astype(BF16)
    heads = jnp.arange(N_HEADS)
    place_k = jnp.zeros((LANES, d), F32)
    for i in range(BIAS_PIECES):
        place_k = place_k.at[N_HEADS * i + heads,
                             (heads // HEADS_PER_STEP) * LANES
                             + BIAS_PIECES * (heads % HEADS_PER_STEP) + i].set(1.0)
    place_qt = jnp.roll(place_k, BIAS_Q_LANE, axis=1).T
    shift = (LOG2E * math.sqrt(HEAD_DIM) * jnp.max(jnp.abs(attn_q_g[0]))
             * jnp.max(jnp.abs(attn_k_g[0])))
    qt, k, vt, kb, qbt, edge = _qkv_proj(
        x, norm_mix_g[0].reshape(1, d), wkf, wqvt, bd, gain_k, gain_qt, bfp, tri,
        place_k.astype(BF16), place_qt.astype(BF16), jnp.full((1, LANES), shift, F32),
        tm=tm_proj, tk=tk)
    assert tm_proj == tk
    c_first = edge[:, ::tq // tk, 0, :N_HEADS]
    c_last = edge[:, :, 1, :N_HEADS]
    bound = (c_first[:, :, None, :] - c_last[:, None, :, :]) * LOG2E
    key_blk = jnp.arange(s // tk)[None, None, :, None]
    first = jnp.min(jnp.where(bound < SKIP_LOGIT, s // tk, key_blk), axis=2)
    first = jnp.minimum(first, (tq // tk) * jnp.arange(s // tq)[None, :, None]) // 2 * 2
    first = first.transpose(0, 2, 1).reshape(-1).astype(jnp.int32)
    o = lax.cond(shift <= MAX_SHIFT,
                 functools.partial(_attention, tq=tq, tk=tk, online=False),
                 functools.partial(_attention, tq=tq, tk=tk, online=True),
                 first, qt, qbt, k, kb, vt)
    tok = jnp.arange(SSM_CHUNK * SSM_CHUNK)
    perm = (tok[None, :] == (tok[:, None] % SSM_CHUNK) * SSM_CHUNK + tok[:, None] // SSM_CHUNK)
    perm = perm.astype(BF16)
    x2, u = _mix_mlp(x.reshape(t, d), o.reshape(t, d), attn_w_out[0].astype(BF16),
                     norm_mlp_g[0].reshape(1, d), mlp_w1[0].astype(BF16), mlp_w2[0].astype(BF16),
                     perm, (norm_mix_g[1].reshape(1, d), ssm_w_in[0].astype(BF16)),
                     glu=False, tm=tm_mlp)

    n_chunks = s // SSM_CHUNK
    ops = _ssm_operators(ssm_a_re[0], ssm_a_im[0], ssm_b_re[0], ssm_b_im[0], ssm_c_re[0],
                         ssm_c_im[0], ssm_log_dt[0], ssm_d[0], n_chunks)
    z = _ssm_scan(u, *ops, n_chunks=n_chunks)
    x3 = _mix_mlp(x2, z, ssm_w_glu[0].astype(BF16), norm_mlp_g[1].reshape(1, d),
                  mlp_w1[1].astype(BF16), mlp_w2[1].astype(BF16), perm, glu=True, tm=tm_mlp)
    return x3.reshape(b, s, d)
```

```python
import functools
import math

import jax
import jax.numpy as jnp
from jax import lax
from jax.experimental import pallas as pl
from jax.experimental.pallas import tpu as pltpu

F32 = jnp.float32
BF16 = jnp.bfloat16

N_HEADS = 16
HEAD_DIM = 64
SSM_GROUP = 16
SSM_STATE = 64
SSM_CHUNK = 16
EPS = 1e-6
LOG2E = 1.4426950408889634

LANES = 128
MXU_DIM = 256
HEADS_PER_STEP = LANES // HEAD_DIM
BIAS_PIECES = 3
VMEM_LIMIT = 56 * 1024 * 1024

NT_DIMS = (((1,), (1,)), ((), ()))


def _rms_norm(x, g):
    ms = jnp.mean(x * x, axis=-1, keepdims=True)
    return x * lax.rsqrt(ms + EPS) * g


def _const_spec(shape):
    zeros = (0,) * len(shape)
    return pl.BlockSpec(shape, lambda *_: zeros)


def _bf16_pieces(val):
    pieces = jnp.zeros_like(val)
    rem = val
    for n in range(BIAS_PIECES):
        piece = rem.astype(BF16).astype(F32)
        rem = rem - piece
        pieces = pieces + (piece if n == 0 else pltpu.roll(piece, N_HEADS * n, axis=1))
    return pieces


def _qkv_kernel(x_ref, g_ref, wkf_ref, wqvt_ref, bd_ref, gain_k_ref, gain_qt_ref, bf_ref, tri_ref,
                place_k_ref, place_qt_ref, shift_ref,
                qt_ref, k_ref, vt_ref, kb_ref, qbt_ref, edge_ref, carry_ref, *, tm, tk, d):
    @pl.when(pl.program_id(1) == 0)
    def _():
        carry_ref[...] = jnp.zeros_like(carry_ref)

    h = _rms_norm(x_ref[0], g_ref[...]).astype(BF16)

    y = jnp.dot(h, wkf_ref[...], preferred_element_type=F32)
    vt = lax.dot_general(wqvt_ref[d:, :], h, NT_DIMS, preferred_element_type=F32).astype(BF16)
    for j in range(tm // tk):
        vt_ref[0, j] = vt[:, tk * j:tk * (j + 1)]
    qt = lax.dot_general(wqvt_ref[:d, :], h, NT_DIMS, preferred_element_type=F32)

    for t in range(d // MXU_DIM):
        sl = slice(MXU_DIM * t, MXU_DIM * (t + 1))
        tile = y[:, sl]
        ms = jnp.dot((tile * tile).astype(BF16), bd_ref[...], preferred_element_type=F32)
        k_ref[0, :, sl] = (tile * lax.rsqrt(ms + EPS) * gain_k_ref[:, sl]).astype(BF16)
        tile = qt[sl, :]
        ms = jnp.dot(bd_ref[...], (tile * tile).astype(BF16), preferred_element_type=F32)
        gain = jnp.tile(gain_qt_ref[sl, :], (1, tm // LANES))
        qt_ref[0, sl, :] = (tile * lax.rsqrt(ms + EPS) * gain).astype(BF16)

    f = y[:, d:] + bf_ref[...]
    log_f = jnp.minimum(f, 0.0) - jnp.log1p(jnp.exp(-jnp.abs(f)))
    lane = lax.broadcasted_iota(jnp.int32, log_f.shape, 1)
    part = jnp.dot(tri_ref[...], _bf16_pieces(jnp.where(lane < N_HEADS, log_f, 0.0)).astype(BF16),
                   preferred_element_type=F32)
    cs = carry_ref[...] + part
    for n in range(1, BIAS_PIECES):
        cs = cs + pltpu.roll(part, LANES - N_HEADS * n, axis=1)
    carry_ref[...] = cs[tm - 1:tm, :]
    edge_ref[0, 0] = jnp.concatenate(
        [cs[0:1, :], cs[tm - 1:tm, :], jnp.zeros((6, LANES), F32)], axis=0)
    c2 = jnp.where(lane < N_HEADS, cs * LOG2E, 0.0)
    kb_ref[0] = jnp.dot(_bf16_pieces(-c2).astype(BF16), place_k_ref[...],
                        preferred_element_type=F32).astype(BF16)
    q_pieces = _bf16_pieces(jnp.where(lane < N_HEADS, c2 - shift_ref[...], 0.0))
    qbt_ref[0] = jnp.dot(place_qt_ref[...], q_pieces.T.astype(BF16),
                         preferred_element_type=F32).astype(BF16)


def _qkv_proj(x, g, wkf, wqvt, bd, gain_k, gain_qt, bfp, tri, place_k, place_qt, shift, *, tm, tk):
    b, s, d = x.shape
    row = lambda bi, si: (bi, si, 0)
    col = lambda bi, si: (bi, 0, si)
    consts = (g, wkf, wqvt, bd, gain_k, gain_qt, bfp, tri, place_k, place_qt, shift)
    return pl.pallas_call(
        functools.partial(_qkv_kernel, tm=tm, tk=tk, d=d),
        grid=(b, s // tm),
        in_specs=[pl.BlockSpec((1, tm, d), row)] + [_const_spec(c.shape) for c in consts],
        out_specs=[pl.BlockSpec((1, d, tm), col),
                   pl.BlockSpec((1, tm, d), row),
                   pl.BlockSpec((1, tm // tk, d, tk), lambda bi, si: (bi, si, 0, 0)),
                   pl.BlockSpec((1, tm, d), row),
                   pl.BlockSpec((1, d, tm), col),
                   pl.BlockSpec((1, 1, 8, LANES), lambda bi, si: (bi, si, 0, 0))],
        out_shape=[jax.ShapeDtypeStruct((b, d, s), BF16),
                   jax.ShapeDtypeStruct((b, s, d), BF16),
                   jax.ShapeDtypeStruct((b, s // tk, d, tk), BF16),
                   jax.ShapeDtypeStruct((b, s, d), BF16),
                   jax.ShapeDtypeStruct((b, d, s), BF16),
                   jax.ShapeDtypeStruct((b, s // tm, 8, LANES), F32)],
        scratch_shapes=[pltpu.VMEM((1, LANES), F32)],
        compiler_params=pltpu.CompilerParams(
            dimension_semantics=("arbitrary", "arbitrary"), vmem_limit_bytes=VMEM_LIMIT),
        name="qkv_proj",
    )(x, *consts)


BIAS_Q_LANE = 8
MAX_SHIFT = 48.0
SKIP_LOGIT = -200.0


def _attn_kernel(first_ref, qt_ref, qbt_ref, k_ref, kb_ref, vt_ref, o_ref, s0_scr, s1_scr, *,
                 tq, tk, online):
    i = pl.program_id(2)
    n_sub = tq // tk
    qt = qt_ref[0]
    qbt = qbt_ref[0]
    row = lax.broadcasted_iota(jnp.int32, (LANES, tq), 0)
    lane_k = lax.broadcasted_iota(jnp.int32, (tk, LANES), 1)
    q_cols = (lane_k >= BIAS_Q_LANE) & (lane_k < BIAS_Q_LANE + HEADS_PER_STEP * BIAS_PIECES)
    causal = (lax.broadcasted_iota(jnp.int32, (tk, tk), 0)
              <= lax.broadcasted_iota(jnp.int32, (tk, tk), 1))
    one = jnp.ones((), BF16)
    zero = jnp.zeros((), BF16)
    heads = range(HEADS_PER_STEP)
    qats = []
    for hh in heads:
        q_h = jnp.where((row >= HEAD_DIM * hh) & (row < HEAD_DIM * (hh + 1)), qt, zero)
        k_side = (row >= BIAS_PIECES * hh) & (row < BIAS_PIECES * (hh + 1))
        q_lo = BIAS_Q_LANE + BIAS_PIECES * hh
        q_side = (row >= q_lo) & (row < q_lo + BIAS_PIECES)
        qats.append(jnp.concatenate(
            [q_h, jnp.where(k_side, one, jnp.where(q_side, qbt, zero))], axis=0))

    def scores(hh, j, c0=0):
        off = pl.multiple_of(j * tk, tk)
        kbias = jnp.where(q_cols, one, kb_ref[0, pl.ds(off, tk), :])
        ka = jnp.concatenate([k_ref[0, pl.ds(off, tk), :], kbias], axis=1)
        return jnp.dot(ka, qats[hh][:, c0:], preferred_element_type=F32)

    def consume(hh, carry, s, j, diagonal):
        if diagonal:
            s_tri = jnp.where(causal, s[:, :tk], -1e30)
            s = s_tri if s.shape[1] == tk else jnp.concatenate([s_tri, s[:, tk:]], axis=1)
        vt = vt_ref[0, j, HEAD_DIM * hh:HEAD_DIM * (hh + 1), :]
        if online:
            m, l, acc = carry
            m_new = jnp.maximum(m, jnp.max(s, axis=0, keepdims=True))
            alpha = jnp.exp2(m - m_new)
            p = jnp.exp2(s - m_new)
            l = alpha * l + jnp.sum(p, axis=0, keepdims=True)
            acc = alpha * acc + jnp.dot(vt, p.astype(BF16), preferred_element_type=F32)
            return m_new, l, acc
        l, acc = carry
        p = jnp.exp2(s)
        l = l + jnp.sum(p, axis=0, keepdims=True)
        acc = acc + jnp.dot(vt, p.astype(BF16), preferred_element_type=F32)
        return l, acc

    init = (jnp.zeros((1, tq), F32), jnp.zeros((HEAD_DIM, tq), F32))
    if online:
        init = (jnp.full((1, tq), -1e30, F32),) + init

    j0 = n_sub * i

    def first_scores(hh):
        if online:
            jj0 = 0
        else:
            head = HEADS_PER_STEP * pl.program_id(1) + hh
            jj0 = first_ref[(pl.program_id(0) * N_HEADS + head) * pl.num_programs(2) + i] // 2
        s0_scr[hh] = scores(hh, 2 * jj0)
        return jj0

    def off_diagonal(hh, jj0):
        def pairs(jj, c, n_pairs):
            for r in range(n_pairs):
                s1_scr[hh] = scores(hh, 2 * (jj + r) + 1)
                c = consume(hh, c, s0_scr[hh], 2 * (jj + r), False)
                s0_scr[hh] = scores(hh, 2 * (jj + r) + 2)
                c = consume(hh, c, s1_scr[hh], 2 * (jj + r) + 1, False)
            return c

        n_long = (j0 // 2 - jj0) // 2
        c = lax.fori_loop(0, n_long, lambda t, c: pairs(jj0 + 2 * t, c, 2), init)
        return lax.fori_loop(jj0 + 2 * n_long, j0 // 2, lambda jj, c: pairs(jj, c, 1), c)

    def diagonal(hh, carry):
        later = [scores(hh, j0 + dd, tk * dd) for dd in range(1, n_sub)]
        carry = consume(hh, carry, s0_scr[hh], j0, True)
        for dd in range(1, n_sub):
            c0 = tk * dd
            part = consume(hh, tuple(c[:, c0:] for c in carry), later[dd - 1], j0 + dd, True)
            carry = tuple(jnp.concatenate([c[:, :c0], pc], axis=1) for c, pc in zip(carry, part))
        return carry[-1] / carry[-2]

    outs = []
    jj0 = first_scores(0)
    for hh in heads:
        carry = off_diagonal(hh, jj0)
        if hh + 1 < HEADS_PER_STEP:
            jj0 = first_scores(hh + 1)
        outs.append(diagonal(hh, carry))
    o_t = jnp.concatenate(outs, axis=0)
    o_ref[0] = o_t.T.astype(BF16)


def _attention(first, qt, qbt, k, kb, vt, *, tq, tk, online):
    b, s, d = k.shape
    assert (tq // tk) % 2 == 0 and s % tq == 0
    n_pairs = d // LANES
    nk = s // tk
    qt_blk = pl.BlockSpec((1, LANES, tq), lambda bi, hp, i, first: (bi, hp, i))
    k_all = pl.BlockSpec((1, s, LANES), lambda bi, hp, i, first: (bi, 0, hp))
    return pl.pallas_call(
        functools.partial(_attn_kernel, tq=tq, tk=tk, online=online),
        grid_spec=pltpu.PrefetchScalarGridSpec(
            num_scalar_prefetch=1,
            grid=(b, n_pairs, s // tq),
            in_specs=[qt_blk, qt_blk, k_all, k_all,
                      pl.BlockSpec((1, nk, LANES, tk), lambda bi, hp, i, first: (bi, 0, hp, 0))],
            out_specs=pl.BlockSpec((1, tq, LANES), lambda bi, hp, i, first: (bi, i, hp)),
            scratch_shapes=[pltpu.VMEM((HEADS_PER_STEP, tk, tq), F32),
                            pltpu.VMEM((HEADS_PER_STEP, tk, tq), F32)]),
        out_shape=jax.ShapeDtypeStruct((b, s, d), BF16),
        compiler_params=pltpu.CompilerParams(
            dimension_semantics=("arbitrary", "arbitrary", "arbitrary"),
            vmem_limit_bytes=VMEM_LIMIT),
        name="fox_attention_online" if online else "fox_attention",
    )(first, qt, qbt, k, kb, vt)


def _permute_token_blocks(perm, a):
    w = perm.shape[0]
    return jnp.concatenate(
        [jnp.dot(perm, a[w * i:w * (i + 1)], preferred_element_type=F32).astype(BF16)
         for i in range(a.shape[0] // w)], axis=0)


def _mix_mlp_kernel(x_ref, a_ref, wmix_ref, g_ref, w1_ref, w2_ref, *rest, glu, d, ff_chunk):
    if glu:
        a = _permute_token_blocks(rest[0][...], a_ref[...])
        mix = jnp.dot(a, wmix_ref[...], preferred_element_type=F32)
        mix = mix[:, :d] * jax.nn.sigmoid(mix[:, d:])
    else:
        mix = jnp.dot(a_ref[...], wmix_ref[...], preferred_element_type=F32)
    x1 = x_ref[...] + mix
    h = _rms_norm(x1, g_ref[...]).astype(BF16)
    acc = x1
    for c in range(w1_ref.shape[1] // ff_chunk):
        sl = slice(ff_chunk * c, ff_chunk * (c + 1))
        hid = jnp.maximum(jnp.dot(h, w1_ref[:, sl], preferred_element_type=F32), 0.0)
        acc = acc + jnp.dot((hid * hid).astype(BF16), w2_ref[sl, :], preferred_element_type=F32)
    if glu:
        _, o_ref = rest
    else:
        perm_ref, g_next_ref, w_next_ref, o_ref, u_ref = rest
        h_next = _rms_norm(acc, g_next_ref[...]).astype(BF16)
        u = jnp.dot(h_next, w_next_ref[...], preferred_element_type=F32).astype(BF16)
        u_ref[...] = _permute_token_blocks(perm_ref[...], u)
    o_ref[...] = acc


def _mix_mlp(x, a, wmix, g, w1, w2, perm, next_proj=None, *, glu, tm, ff_chunk=1024):
    t, d = x.shape
    row = lambda i: (i, 0)
    single = pl.Buffered(1)
    wspec = lambda shape: pl.BlockSpec(shape, lambda i: (0, 0), pipeline_mode=single)
    operands = [x, a, wmix, g, w1, w2, perm]
    in_specs = [pl.BlockSpec((tm, d), row), pl.BlockSpec((tm, d), row),
                wspec(wmix.shape), wspec((1, d)), wspec(w1.shape), wspec(w2.shape),
                wspec(perm.shape)]
    out_specs = pl.BlockSpec((tm, d), row)
    out_shape = jax.ShapeDtypeStruct((t, d), F32)
    if next_proj is not None:
        operands += list(next_proj)
        in_specs += [wspec(w.shape) for w in next_proj]
        n = next_proj[1].shape[1]
        out_specs = [out_specs, pl.BlockSpec((tm, n), row)]
        out_shape = [out_shape, jax.ShapeDtypeStruct((t, n), BF16)]
    return pl.pallas_call(
        functools.partial(_mix_mlp_kernel, glu=glu, d=d, ff_chunk=ff_chunk),
        grid=(t // tm,),
        in_specs=in_specs,
        out_specs=out_specs,
        out_shape=out_shape,
        compiler_params=pltpu.CompilerParams(
            dimension_semantics=("arbitrary",), vmem_limit_bytes=VMEM_LIMIT),
        name="mix_glu_mlp" if glu else "mix_mlp",
    )(*operands)


GROUPS_PER_SLAB = LANES // SSM_GROUP


def _piece_transpose(arrs, piece):
    arrs = list(arrs)
    dist = GROUPS_PER_SLAB // 2
    while dist >= 1:
        keep = (piece & dist) == 0
        shift = SSM_GROUP * dist
        for i in range(GROUPS_PER_SLAB):
            if i & dist:
                continue
            a, b = arrs[i], arrs[i + dist]
            arrs[i] = jnp.where(keep, a, pltpu.roll(b, shift, axis=1))
            arrs[i + dist] = jnp.where(keep, pltpu.roll(a, LANES - shift, axis=1), b)
        dist //= 2
    return arrs


def _ssm_kernel(u_ref, cc_ref, bt_ref, pw1_ref, pw2_ref, zoh_ref, a1_ref, a2_ref,
                dv_ref, z_ref, ug_scr, zg_scr, *, n_chunks):
    L = SSM_CHUNK
    rows = u_ref.shape[0] // L
    w = L * SSM_GROUP
    half = SSM_STATE
    nb = math.gcd(rows // L, 8)
    piece = lax.broadcasted_iota(jnp.int32, (nb * L, LANES), 1) // SSM_GROUP

    def relayout_in(rb, carry):
        t0 = pl.multiple_of(rb * nb * w, nb * w)
        r0 = pl.multiple_of(rb * nb * L, nb * L)
        by_pos = u_ref[pl.ds(t0, nb * w), :].astype(F32)
        for hf in range(L // GROUPS_PER_SLAB):
            arrs = []
            for k in range(GROUPS_PER_SLAB):
                pos = GROUPS_PER_SLAB * hf + k
                arrs.append(jnp.concatenate(
                    [by_pos[w * blk + L * pos:w * blk + L * (pos + 1)] for blk in range(nb)],
                    axis=0))
            arrs = _piece_transpose(arrs, piece)
            for g in range(GROUPS_PER_SLAB):
                ug_scr[g, pl.ds(r0, nb * L), LANES * hf:LANES * (hf + 1)] = arrs[g].astype(BF16)
        return carry

    lax.fori_loop(0, rows // (nb * L), relayout_in, 0)

    chunk = lax.broadcasted_iota(jnp.int32, (rows, LANES), 0) % n_chunks
    lane_w = lax.broadcasted_iota(jnp.int32, (SSM_GROUP, w), 1)
    sign = jnp.where(lax.broadcasted_iota(jnp.int32, (1, LANES), 1) < half, 1.0, -1.0)

    def group(g, carry):
        cc = cc_ref[g]
        bt = bt_ref[g]
        pw1 = pw1_ref[g]
        pw2 = pw2_ref[g]
        zoh = zoh_ref[g]
        bbar = bt * zoh[0:1] + pltpu.roll(bt, half, axis=1) * zoh[1:2]
        bbar_sw = pltpu.roll(bbar, half, axis=1)
        cc_sw = pltpu.roll(cc, half, axis=1)
        ca = [cc * pw1[k:k + 1] + cc_sw * pw2[k:k + 1] for k in range(L + 1)]
        k_all = lax.dot_general(bbar * sign, jnp.concatenate(ca[:L], axis=0), NT_DIMS,
                                precision=lax.Precision.HIGHEST, preferred_element_type=F32)
        m_intra, w_state, w_out_t = [], [], []
        for s in range(L):
            blk = k_all if s == 0 else jnp.where(
                lane_w >= SSM_GROUP * s, pltpu.roll(k_all, SSM_GROUP * s, axis=1), 0.0)
            m_intra.append(blk.astype(BF16))
            k = L - 1 - s
            w_state.append((bbar * pw1[k:k + 1] + bbar_sw * pw2[k:k + 1]).astype(BF16))
            w_out_t.append((ca[s + 1] * sign).astype(BF16))

        u = ug_scr[g]
        y = jnp.dot(u, jnp.concatenate(m_intra, axis=0), preferred_element_type=F32)
        xs = jnp.dot(u, jnp.concatenate(w_state, axis=0),
                     preferred_element_type=F32)
        a1 = a1_ref[g]
        a2 = a2_ref[g]
        step = 1
        j = 0
        while step < n_chunks:
            sh = jnp.where(chunk >= step, pltpu.roll(xs, step, axis=0), 0.0)
            xs = xs + sh * a1[j:j + 1] + pltpu.roll(sh, half, axis=1) * a2[j:j + 1]
            step *= 2
            j += 1
        x_in = jnp.where(chunk >= 1, pltpu.roll(xs, 1, axis=0), 0.0)
        y = y + lax.dot_general(x_in.astype(BF16), jnp.concatenate(w_out_t, axis=0), NT_DIMS,
                                preferred_element_type=F32)
        y = y + u.astype(F32) * dv_ref[g]
        zg_scr[g] = jax.nn.gelu(y).astype(BF16)
        return carry

    lax.fori_loop(0, GROUPS_PER_SLAB, group, 0)

    def relayout_out(rb, carry):
        t0 = pl.multiple_of(rb * nb * w, nb * w)
        r0 = pl.multiple_of(rb * nb * L, nb * L)
        by_pos = []
        for hf in range(L // GROUPS_PER_SLAB):
            arrs = [zg_scr[g, pl.ds(r0, nb * L), LANES * hf:LANES * (hf + 1)].astype(F32)
                    for g in range(GROUPS_PER_SLAB)]
            by_pos += _piece_transpose(arrs, piece)
        for blk in range(nb):
            z_ref[pl.ds(t0 + w * blk, w), :] = jnp.concatenate(
                [arr[L * blk:L * (blk + 1)] for arr in by_pos], axis=0).astype(BF16)
        return carry

    lax.fori_loop(0, rows // (nb * L), relayout_out, 0)


def _ssm_scan(u, cc, bt, pw1, pw2, zoh, a1, a2, dv, *, n_chunks):
    t, d = u.shape
    L = SSM_CHUNK
    rows = t // L
    w = L * SSM_GROUP
    gps = GROUPS_PER_SLAB
    slab = lambda j: (j, 0, 0)
    pspec = lambda arr: pl.BlockSpec((gps,) + arr.shape[1:], slab)
    return pl.pallas_call(
        functools.partial(_ssm_kernel, n_chunks=n_chunks),
        grid=(d // LANES,),
        in_specs=[pl.BlockSpec((t, LANES), lambda j: (0, j)),
                  pspec(cc), pspec(bt), pspec(pw1), pspec(pw2), pspec(zoh), pspec(a1), pspec(a2),
                  pspec(dv)],
        out_specs=pl.BlockSpec((t, LANES), lambda j: (0, j)),
        out_shape=jax.ShapeDtypeStruct((t, d), BF16),
        scratch_shapes=[pltpu.VMEM((gps, rows, w), BF16), pltpu.VMEM((gps, rows, w), BF16)],
        compiler_params=pltpu.CompilerParams(
            dimension_semantics=("arbitrary",), vmem_limit_bytes=VMEM_LIMIT),
        name="s5_scan",
    )(u, cc, bt, pw1, pw2, zoh, a1, a2, dv)


def _ssm_operators(a_re, a_im, b_re, b_im, c_re, c_im, log_dt, d_skip, n_chunks):
    L = SSM_CHUNK
    g, p = a_re.shape
    dt = jnp.exp(log_dt)[:, None]
    lam_re, lam_im = dt * a_re, dt * a_im

    def powers(ks):
        ks = jnp.asarray(ks, F32)[:, None, None]
        mag = jnp.exp(ks * lam_re)
        return mag * jnp.cos(ks * lam_im), mag * jnp.sin(ks * lam_im)

    def patterns(re, im):
        return (jnp.concatenate([re, re], axis=2).transpose(1, 0, 2),
                jnp.concatenate([-im, im], axis=2).transpose(1, 0, 2))

    pr, pi = powers(range(L + 1))
    num_re, num_im = pr[1] - 1.0, pi[1]
    den = a_re * a_re + a_im * a_im
    s_re = (num_re * a_re + num_im * a_im) / den
    s_im = (num_im * a_re - num_re * a_im) / den
    pw1, pw2 = patterns(pr, pi)
    zoh = jnp.concatenate(patterns(s_re[None], s_im[None]), axis=1)
    n_steps = max(1, int(math.log2(n_chunks)))
    a1, a2 = patterns(*powers([L * 2 ** j for j in range(n_steps)]))
    cc = jnp.concatenate([c_re, c_im], axis=2)
    bt = jnp.concatenate([b_re.transpose(0, 2, 1), b_im.transpose(0, 2, 1)], axis=2)
    dv = jnp.tile(d_skip.reshape(g, 1, SSM_GROUP), (1, L, 1)).reshape(g, 1, L * SSM_GROUP)
    return cc, bt, pw1, pw2, zoh, a1, a2, dv


def kernel(x, norm_mix_g, norm_mlp_g, attn_w_in, attn_b_f, attn_q_g, attn_k_g, attn_w_out,
           ssm_w_in, ssm_a_re, ssm_a_im, ssm_b_re, ssm_b_im, ssm_c_re, ssm_c_im, ssm_log_dt,
           ssm_d, ssm_w_glu, mlp_w1, mlp_w2):
    b, s, d = x.shape
    t = b * s
    tk = min(512, s // 2)
    tq = 2 * tk
    tm_proj = min(512, s)
    tm_mlp = min(512, t)

    w_in = attn_w_in[0]
    w_f = jnp.pad(w_in[:, 3 * d:], ((0, 0), (0, LANES - N_HEADS)))
    wkf = jnp.concatenate([w_in[:, d:2 * d], w_f], axis=1).astype(BF16)
    wqvt = jnp.concatenate([w_in[:, :d], w_in[:, 2 * d:3 * d]], axis=1).T.astype(BF16)
    blk = jnp.arange(MXU_DIM) // HEAD_DIM
    bd = ((blk[:, None] == blk[None, :]) * (1.0 / HEAD_DIM)).astype(BF16)
    gain_k = jnp.tile(attn_k_g[0], N_HEADS).reshape(1, d)
    gain_qt = jnp.broadcast_to(
        (jnp.tile(attn_q_g[0], N_HEADS) * (LOG2E / math.sqrt(HEAD_DIM)))[:, None], (d, LANES))
    bfp = jnp.pad(attn_b_f[0], (0, LANES - N_HEADS)).reshape(1, LANES)
    tri = (jnp.arange(tm_proj)[:, None] >= jnp.arange(tm_proj)[None, :]).astype(BF16)
    heads = jnp.arange(N_HEADS)
    place_k = jnp.zeros((LANES, d), F32)
    for i in range(BIAS_PIECES):
        place_k = place_k.at[N_HEADS * i + heads,
                             (heads // HEADS_PER_STEP) * LANES
                             + BIAS_PIECES * (heads % HEADS_PER_STEP) + i].set(1.0)
    place_qt = jnp.roll(place_k, BIAS_Q_LANE, axis=1).T
    shift = (LOG2E * math.sqrt(HEAD_DIM) * jnp.max(jnp.abs(attn_q_g[0]))
             * jnp.max(jnp.abs(attn_k_g[0])))
    qt, k, vt, kb, qbt, edge = _qkv_proj(
        x, norm_mix_g[0].reshape(1, d), wkf, wqvt, bd, gain_k, gain_qt, bfp, tri,
        place_k.astype(BF16), place_qt.astype(BF16), jnp.full((1, LANES), shift, F32),
        tm=tm_proj, tk=tk)
    assert tm_proj == tk
    c_first = edge[:, ::tq // tk, 0, :N_HEADS]
    c_last = edge[:, :, 1, :N_HEADS]
    bound = (c_first[:, :, None, :] - c_last[:, None, :, :]) * LOG2E
    key_blk = jnp.arange(s // tk)[None, None, :, None]
    first = jnp.min(jnp.where(bound < SKIP_LOGIT, s // tk, key_blk), axis=2)
    first = jnp.minimum(first, (tq // tk) * jnp.arange(s // tq)[None, :, None]) // 2 * 2
    first = first.transpose(0, 2, 1).reshape(-1).astype(jnp.int32)
    o = lax.cond(shift <= MAX_SHIFT,
                 functools.partial(_attention, tq=tq, tk=tk, online=False),
                 functools.partial(_attention, tq=tq, tk=tk, online=True),
                 first, qt, qbt, k, kb, vt)
    tok = jnp.arange(SSM_CHUNK * SSM_CHUNK)
    perm = (tok[None, :] == (tok[:, None] % SSM_CHUNK) * SSM_CHUNK + tok[:, None] // SSM_CHUNK)
    perm = perm.astype(BF16)
    x2, u = _mix_mlp(x.reshape(t, d), o.reshape(t, d), attn_w_out[0].astype(BF16),
                     norm_mlp_g[0].reshape(1, d), mlp_w1[0].astype(BF16), mlp_w2[0].astype(BF16),
                     perm, (norm_mix_g[1].reshape(1, d), ssm_w_in[0].astype(BF16)),
                     glu=False, tm=tm_mlp)

    n_chunks = s // SSM_CHUNK
    ops = _ssm_operators(ssm_a_re[0], ssm_a_im[0], ssm_b_re[0], ssm_b_im[0], ssm_c_re[0],
                         ssm_c_im[0], ssm_log_dt[0], ssm_d[0], n_chunks)
    z = _ssm_scan(u, *ops, n_chunks=n_chunks)
    x3 = _mix_mlp(x2, z, ssm_w_glu[0].astype(BF16), norm_mlp_g[1].reshape(1, d),
                  mlp_w1[1].astype(BF16), mlp_w2[1].astype(BF16), perm, glu=True, tm=tm_mlp)
    return x3.reshape(b, s, d)
```

```python
import functools
import math

import jax
import jax.numpy as jnp
from jax import lax
from jax.experimental import pallas as pl
from jax.experimental.pallas import tpu as pltpu

F32 = jnp.float32
BF16 = jnp.bfloat16

N_HEADS = 16
HEAD_DIM = 64
SSM_GROUP = 16
SSM_STATE = 64
SSM_CHUNK = 16
EPS = 1e-6
LOG2E = 1.4426950408889634

LANES = 128
MXU_DIM = 256
HEADS_PER_STEP = LANES // HEAD_DIM
BIAS_PIECES = 3
VMEM_LIMIT = 56 * 1024 * 1024

NT_DIMS = (((1,), (1,)), ((), ()))


def _rms_norm(x, g):
    ms = jnp.mean(x * x, axis=-1, keepdims=True)
    return x * lax.rsqrt(ms + EPS) * g


def _const_spec(shape):
    zeros = (0,) * len(shape)
    return pl.BlockSpec(shape, lambda *_: zeros)


def _bf16_pieces(val):
    pieces = jnp.zeros_like(val)
    rem = val
    for n in range(BIAS_PIECES):
        piece = rem.astype(BF16).astype(F32)
        rem = rem - piece
        pieces = pieces + (piece if n == 0 else pltpu.roll(piece, N_HEADS * n, axis=1))
    return pieces


def _qkv_kernel(x_ref, g_ref, wkf_ref, wqvt_ref, bd_ref, gain_k_ref, gain_qt_ref, bf_ref, tri_ref,
                place_k_ref, place_qt_ref, shift_ref,
                qt_ref, k_ref, vt_ref, kb_ref, qbt_ref, edge_ref, carry_ref, *, tm, tk, d):
    @pl.when(pl.program_id(1) == 0)
    def _():
        carry_ref[...] = jnp.zeros_like(carry_ref)

    h = _rms_norm(x_ref[0], g_ref[...]).astype(BF16)

    y = jnp.dot(h, wkf_ref[...], preferred_element_type=F32)
    vt = lax.dot_general(wqvt_ref[d:, :], h, NT_DIMS, preferred_element_type=F32).astype(BF16)
    for j in range(tm // tk):
        vt_ref[0, j] = vt[:, tk * j:tk * (j + 1)]
    qt = lax.dot_general(wqvt_ref[:d, :], h, NT_DIMS, preferred_element_type=F32)

    for t in range(d // MXU_DIM):
        sl = slice(MXU_DIM * t, MXU_DIM * (t + 1))
        tile = y[:, sl]
        ms = jnp.dot((tile * tile).astype(BF16), bd_ref[...], preferred_element_type=F32)
        k_ref[0, :, sl] = (tile * lax.rsqrt(ms + EPS) * gain_k_ref[:, sl]).astype(BF16)
        tile = qt[sl, :]
        ms = jnp.dot(bd_ref[...], (tile * tile).astype(BF16), preferred_element_type=F32)
        gain = jnp.tile(gain_qt_ref[sl, :], (1, tm // LANES))
        qt_ref[0, sl, :] = (tile * lax.rsqrt(ms + EPS) * gain).astype(BF16)

    f = y[:, d:] + bf_ref[...]
    log_f = jnp.minimum(f, 0.0) - jnp.log1p(jnp.exp(-jnp.abs(f)))
    lane = lax.broadcasted_iota(jnp.int32, log_f.shape, 1)
    part = jnp.dot(tri_ref[...], _bf16_pieces(jnp.where(lane < N_HEADS, log_f, 0.0)).astype(BF16),
                   preferred_element_type=F32)
    cs = carry_ref[...] + part
    for n in range(1, BIAS_PIECES):
        cs = cs + pltpu.roll(part, LANES - N_HEADS * n, axis=1)
    carry_ref[...] = cs[tm - 1:tm, :]
    edge_ref[0, 0] = jnp.concatenate(
        [cs[0:1, :], cs[tm - 1:tm, :], jnp.zeros((6, LANES), F32)], axis=0)
    c2 = jnp.where(lane < N_HEADS, cs * LOG2E, 0.0)
    kb_ref[0] = jnp.dot(_bf16_pieces(-c2).astype(BF16), place_k_ref[...],
                        preferred_element_type=F32).astype(BF16)
    q_pieces = _bf16_pieces(jnp.where(lane < N_HEADS, c2 - shift_ref[...], 0.0))
    qbt_ref[0] = jnp.dot(place_qt_ref[...], q_pieces.T.astype(BF16),
                         preferred_element_type=F32).astype(BF16)


def _qkv_proj(x, g, wkf, wqvt, bd, gain_k, gain_qt, bfp, tri, place_k, place_qt, shift, *, tm, tk):
    b, s, d = x.shape
    row = lambda bi, si: (bi, si, 0)
    col = lambda bi, si: (bi, 0, si)
    consts = (g, wkf, wqvt, bd, gain_k, gain_qt, bfp, tri, place_k, place_qt, shift)
    return pl.pallas_call(
        functools.partial(_qkv_kernel, tm=tm, tk=tk, d=d),
        grid=(b, s // tm),
        in_specs=[pl.BlockSpec((1, tm, d), row)] + [_const_spec(c.shape) for c in consts],
        out_specs=[pl.BlockSpec((1, d, tm), col),
                   pl.BlockSpec((1, tm, d), row),
                   pl.BlockSpec((1, tm // tk, d, tk), lambda bi, si: (bi, si, 0, 0)),
                   pl.BlockSpec((1, tm, d), row),
                   pl.BlockSpec((1, d, tm), col),
                   pl.BlockSpec((1, 1, 8, LANES), lambda bi, si: (bi, si, 0, 0))],
        out_shape=[jax.ShapeDtypeStruct((b, d, s), BF16),
                   jax.ShapeDtypeStruct((b, s, d), BF16),
                   jax.ShapeDtypeStruct((b, s // tk, d, tk), BF16),
                   jax.ShapeDtypeStruct((b, s, d), BF16),
                   jax.ShapeDtypeStruct((b, d, s), BF16),
                   jax.ShapeDtypeStruct((b, s // tm, 8, LANES), F32)],
        scratch_shapes=[pltpu.VMEM((1, LANES), F32)],
        compiler_params=pltpu.CompilerParams(
            dimension_semantics=("arbitrary", "arbitrary"), vmem_limit_bytes=VMEM_LIMIT),
        name="qkv_proj",
    )(x, *consts)


BIAS_Q_LANE = 8
MAX_SHIFT = 48.0
SKIP_LOGIT = -200.0


def _attn_kernel(first_ref, qt_ref, qbt_ref, k_ref, kb_ref, vt_ref, o_ref, s0_scr, s1_scr, *,
                 tq, tk, online):
    i = pl.program_id(2)
    n_sub = tq // tk
    qt = qt_ref[0]
    qbt = qbt_ref[0]
    row = lax.broadcasted_iota(jnp.int32, (LANES, tq), 0)
    lane_k = lax.broadcasted_iota(jnp.int32, (tk, LANES), 1)
    q_cols = (lane_k >= BIAS_Q_LANE) & (lane_k < BIAS_Q_LANE + HEADS_PER_STEP * BIAS_PIECES)
    causal = (lax.broadcasted_iota(jnp.int32, (tk, tk), 0)
              <= lax.broadcasted_iota(jnp.int32, (tk, tk), 1))
    one = jnp.ones((), BF16)
    zero = jnp.zeros((), BF16)
    heads = range(HEADS_PER_STEP)
    qats = []
    for hh in heads:
        q_h = jnp.where((row >= HEAD_DIM * hh) & (row < HEAD_DIM * (hh + 1)), qt, zero)
        k_side = (row >= BIAS_PIECES * hh) & (row < BIAS_PIECES * (hh + 1))
        q_lo = BIAS_Q_LANE + BIAS_PIECES * hh
        q_side = (row >= q_lo) & (row < q_lo + BIAS_PIECES)
        qats.append(jnp.concatenate(
            [q_h, jnp.where(k_side, one, jnp.where(q_side, qbt, zero))], axis=0))

    def scores(hh, j, c0=0):
        off = pl.multiple_of(j * tk, tk)
        kbias = jnp.where(q_cols, one, kb_ref[0, pl.ds(off, tk), :])
        ka = jnp.concatenate([k_ref[0, pl.ds(off, tk), :], kbias], axis=1)
        return jnp.dot(ka, qats[hh][:, c0:], preferred_element_type=F32)

    def consume(hh, carry, s, j, diagonal):
        if diagonal:
            s_tri = jnp.where(causal, s[:, :tk], -1e30)
            s = s_tri if s.shape[1] == tk else jnp.concatenate([s_tri, s[:, tk:]], axis=1)
        vt = vt_ref[0, j, HEAD_DIM * hh:HEAD_DIM * (hh + 1), :]
        if online:
            m, l, acc = carry
            m_new = jnp.maximum(m, jnp.max(s, axis=0, keepdims=True))
            alpha = jnp.exp2(m - m_new)
            p = jnp.exp2(s - m_new)
            l = alpha * l + jnp.sum(p, axis=0, keepdims=True)
            acc = alpha * acc + jnp.dot(vt, p.astype(BF16), preferred_element_type=F32)
            return m_new, l, acc
        l, acc = carry
        p = jnp.exp2(s)
        l = l + jnp.sum(p, axis=0, keepdims=True)
        acc = acc + jnp.dot(vt, p.astype(BF16), preferred_element_type=F32)
        return l, acc

    init = (jnp.zeros((1, tq), F32), jnp.zeros((HEAD_DIM, tq), F32))
    if online:
        init = (jnp.full((1, tq), -1e30, F32),) + init

    j0 = n_sub * i

    def first_scores(hh):
        if online:
            jj0 = 0
        else:
            head = HEADS_PER_STEP * pl.program_id(1) + hh
            jj0 = first_ref[(pl.program_id(0) * N_HEADS + head) * pl.num_programs(2) + i] // 2
        s0_scr[hh] = scores(hh, 2 * jj0)
        return jj0

    def off_diagonal(hh, jj0):
        def pairs(jj, c, n_pairs):
            for r in range(n_pairs):
                s1_scr[hh] = scores(hh, 2 * (jj + r) + 1)
                c = consume(hh, c, s0_scr[hh], 2 * (jj + r), False)
                s0_scr[hh] = scores(hh, 2 * (jj + r) + 2)
                c = consume(hh, c, s1_scr[hh], 2 * (jj + r) + 1, False)
            return c

        n_long = (j0 // 2 - jj0) // 2
        c = lax.fori_loop(0, n_long, lambda t, c: pairs(jj0 + 2 * t, c, 2), init)
        return lax.fori_loop(jj0 + 2 * n_long, j0 // 2, lambda jj, c: pairs(jj, c, 1), c)

    def diagonal(hh, carry):
        later = [scores(hh, j0 + dd, tk * dd) for dd in range(1, n_sub)]
        carry = consume(hh, carry, s0_scr[hh], j0, True)
        for dd in range(1, n_sub):
            c0 = tk * dd
            part = consume(hh, tuple(c[:, c0:] for c in carry), later[dd - 1], j0 + dd, True)
            carry = tuple(jnp.concatenate([c[:, :c0], pc], axis=1) for c, pc in zip(carry, part))
        return carry[-1] / carry[-2]

    outs = []
    jj0 = first_scores(0)
    for hh in heads:
        carry = off_diagonal(hh, jj0)
        if hh + 1 < HEADS_PER_STEP:
            jj0 = first_scores(hh + 1)
        outs.append(diagonal(hh, carry))
    o_t = jnp.concatenate(outs, axis=0)
    o_ref[0] = o_t.T.astype(BF16)


def _attention(first, qt, qbt, k, kb, vt, *, tq, tk, online):
    b, s, d = k.shape
    assert (tq // tk) % 2 == 0 and s % tq == 0
    n_pairs = d // LANES
    nk = s // tk
    qt_blk = pl.BlockSpec((1, LANES, tq), lambda bi, hp, i, first: (bi, hp, i))
    k_all = pl.BlockSpec((1, s, LANES), lambda bi, hp, i, first: (bi, 0, hp))
    return pl.pallas_call(
        functools.partial(_attn_kernel, tq=tq, tk=tk, online=online),
        grid_spec=pltpu.PrefetchScalarGridSpec(
            num_scalar_prefetch=1,
            grid=(b, n_pairs, s // tq),
            in_specs=[qt_blk, qt_blk, k_all, k_all,
                      pl.BlockSpec((1, nk, LANES, tk), lambda bi, hp, i, first: (bi, 0, hp, 0))],
            out_specs=pl.BlockSpec((1, tq, LANES), lambda bi, hp, i, first: (bi, i, hp)),
            scratch_shapes=[pltpu.VMEM((HEADS_PER_STEP, tk, tq), F32),
                            pltpu.VMEM((HEADS_PER_STEP, tk, tq), F32)]),
        out_shape=jax.ShapeDtypeStruct((b, s, d), BF16),
        compiler_params=pltpu.CompilerParams(
            dimension_semantics=("arbitrary", "arbitrary", "arbitrary"),
            vmem_limit_bytes=VMEM_LIMIT),
        name="fox_attention_online" if online else "fox_attention",
    )(first, qt, qbt, k, kb, vt)


def _permute_token_blocks(perm, a):
    w = perm.shape[0]
    return jnp.concatenate(
        [jnp.dot(perm, a[w * i:w * (i + 1)], preferred_element_type=F32).astype(BF16)
         for i in range(a.shape[0] // w)], axis=0)


def _mix_mlp_kernel(x_ref, a_ref, wmix_ref, g_ref, w1_ref, w2_ref, *rest, glu, d, ff_chunk):
    if glu:
        a = _permute_token_blocks(rest[0][...], a_ref[...])
        mix = jnp.dot(a, wmix_ref[...], preferred_element_type=F32)
        mix = mix[:, :d] * jax.nn.sigmoid(mix[:, d:])
    else:
        mix = jnp.dot(a_ref[...], wmix_ref[...], preferred_element_type=F32)
    x1 = x_ref[...] + mix
    h = _rms_norm(x1, g_ref[...]).astype(BF16)
    acc = x1
    for c in range(w1_ref.shape[1] // ff_chunk):
        sl = slice(ff_chunk * c, ff_chunk * (c + 1))
        hid = jnp.maximum(jnp.dot(h, w1_ref[:, sl], preferred_element_type=F32), 0.0)
        acc = acc + jnp.dot((hid * hid).astype(BF16), w2_ref[sl, :], preferred_element_type=F32)
    if glu:
        _, o_ref = rest
    else:
        perm_ref, g_next_ref, w_next_ref, o_ref, u_ref = rest
        h_next = _rms_norm(acc, g_next_ref[...]).astype(BF16)
        u = jnp.dot(h_next, w_next_ref[...], preferred_element_type=F32).astype(BF16)
        u_ref[...] = _permute_token_blocks(perm_ref[...], u)
    o_ref[...] = acc


def _mix_mlp(x, a, wmix, g, w1, w2, perm, next_proj=None, *, glu, tm, ff_chunk=1024):
    t, d = x.shape
    row = lambda i: (i, 0)
    single = pl.Buffered(1)
    wspec = lambda shape: pl.BlockSpec(shape, lambda i: (0, 0), pipeline_mode=single)
    operands = [x, a, wmix, g, w1, w2, perm]
    in_specs = [pl.BlockSpec((tm, d), row), pl.BlockSpec((tm, d), row),
                wspec(wmix.shape), wspec((1, d)), wspec(w1.shape), wspec(w2.shape),
                wspec(perm.shape)]
    out_specs = pl.BlockSpec((tm, d), row)
    out_shape = jax.ShapeDtypeStruct((t, d), F32)
    if next_proj is not None:
        operands += list(next_proj)
        in_specs += [wspec(w.shape) for w in next_proj]
        n = next_proj[1].shape[1]
        out_specs = [out_specs, pl.BlockSpec((tm, n), row)]
        out_shape = [out_shape, jax.ShapeDtypeStruct((t, n), BF16)]
    return pl.pallas_call(
        functools.partial(_mix_mlp_kernel, glu=glu, d=d, ff_chunk=ff_chunk),
        grid=(t // tm,),
        in_specs=in_specs,
        out_specs=out_specs,
        out_shape=out_shape,
        compiler_params=pltpu.CompilerParams(
            dimension_semantics=("arbitrary",), vmem_limit_bytes=VMEM_LIMIT),
        name="mix_glu_mlp" if glu else "mix_mlp",
    )(*operands)


GROUPS_PER_SLAB = LANES // SSM_GROUP


def _piece_transpose(arrs, piece):
    arrs = list(arrs)
    dist = GROUPS_PER_SLAB // 2
    while dist >= 1:
        keep = (piece & dist) == 0
        shift = SSM_GROUP * dist
        for i in range(GROUPS_PER_SLAB):
            if i & dist:
                continue
            a, b = arrs[i], arrs[i + dist]
            arrs[i] = jnp.where(keep, a, pltpu.roll(b, shift, axis=1))
            arrs[i + dist] = jnp.where(keep, pltpu.roll(a, LANES - shift, axis=1), b)
        dist //= 2
    return arrs


def _ssm_kernel(u_ref, cc_ref, bt_ref, pw1_ref, pw2_ref, zoh_ref, a1_ref, a2_ref,
                dv_ref, z_ref, ug_scr, zg_scr, *, n_chunks):
    L = SSM_CHUNK
    rows = u_ref.shape[0] // L
    w = L * SSM_GROUP
    half = SSM_STATE
    nb = math.gcd(rows // L, 8)
    piece = lax.broadcasted_iota(jnp.int32, (nb * L, LANES), 1) // SSM_GROUP

    def relayout_in(rb, carry):
        t0 = pl.multiple_of(rb * nb * w, nb * w)
        r0 = pl.multiple_of(rb * nb * L, nb * L)
        by_pos = u_ref[pl.ds(t0, nb * w), :].astype(F32)
        for hf in range(L // GROUPS_PER_SLAB):
            arrs = []
            for k in range(GROUPS_PER_SLAB):
                pos = GROUPS_PER_SLAB * hf + k
                arrs.append(jnp.concatenate(
                    [by_pos[w * blk + L * pos:w * blk + L * (pos + 1)] for blk in range(nb)],
                    axis=0))
            arrs = _piece_transpose(arrs, piece)
            for g in range(GROUPS_PER_SLAB):
                ug_scr[g, pl.ds(r0, nb * L), LANES * hf:LANES * (hf + 1)] = arrs[g].astype(BF16)
        return carry

    lax.fori_loop(0, rows // (nb * L), relayout_in, 0)

    chunk = lax.broadcasted_iota(jnp.int32, (rows, LANES), 0) % n_chunks
    lane_w = lax.broadcasted_iota(jnp.int32, (SSM_GROUP, w), 1)
    low = lax.broadcasted_iota(jnp.int32, (1, LANES), 1) < half
    sign = jnp.where(low, 1.0, -1.0)

    def operators(g):
        cc = cc_ref[g]
        bt = bt_ref[g]
        pw1 = pw1_ref[g]
        pw2 = pw2_ref[g]
        zoh = zoh_ref[g]
        bbar = bt * zoh[0:1] + pltpu.roll(bt, half, axis=1) * zoh[1:2]
        bbar_sw = pltpu.roll(bbar, half, axis=1)
        cc_sw = pltpu.roll(cc, half, axis=1)
        ca = [cc * pw1[k:k + 1] + cc_sw * pw2[k:k + 1] for k in range(L + 1)]
        k_all = lax.dot_general(bbar * sign, jnp.concatenate(ca[:L], axis=0), NT_DIMS,
                                precision=lax.Precision.HIGHEST, preferred_element_type=F32)
        m_intra, w_state, w_out_t = [], [], []
        for s in range(L):
            blk = k_all if s == 0 else jnp.where(
                lane_w >= SSM_GROUP * s, pltpu.roll(k_all, SSM_GROUP * s, axis=1), 0.0)
            m_intra.append(blk.astype(BF16))
            k = L - 1 - s
            w_state.append(bbar * pw1[k:k + 1] + bbar_sw * pw2[k:k + 1])
            w_out_t.append(ca[s + 1] * sign)
        return jnp.concatenate(m_intra, axis=0), w_state, w_out_t

    def halves(pieces, second):
        firsts, seconds = [], []
        for piece in pieces:
            swapped = pltpu.roll(piece, half, axis=1)
            if second:
                firsts.append(jnp.where(low, 0.0, swapped))
                seconds.append(jnp.where(low, 0.0, piece))
            else:
                firsts.append(jnp.where(low, piece, 0.0))
                seconds.append(jnp.where(low, swapped, 0.0))
        return firsts, seconds

    def group_pair(pr, carry):
        gs = (2 * pr, 2 * pr + 1)
        m_intra, ws_re, ws_im, wo_t = [], [], [], []
        for second, g in enumerate(gs):
            mi, w_state, w_out_t = operators(g)
            m_intra.append(mi)
            re, im = halves(w_state, second)
            ws_re += re
            ws_im += im
            re, im = halves(w_out_t, second)
            wo_t.append(jnp.concatenate(
                [jnp.concatenate([r, m], axis=1) for r, m in zip(re, im)], axis=0).astype(BF16))
        us = [ug_scr[g] for g in gs]
        uu = jnp.concatenate(us, axis=1)
        xr = jnp.dot(uu, jnp.concatenate(ws_re, axis=0).astype(BF16), preferred_element_type=F32)
        xi = jnp.dot(uu, jnp.concatenate(ws_im, axis=0).astype(BF16), preferred_element_type=F32)
        a1 = [a1_ref[g] for g in gs]
        a2 = [a2_ref[g] for g in gs]
        step = 1
        j = 0
        while step < n_chunks:
            ar = jnp.where(low, a1[0][j:j + 1], a1[1][j:j + 1])
            ai = jnp.where(low, -a2[0][j:j + 1], a2[1][j:j + 1])
            sr = jnp.where(chunk >= step, pltpu.roll(xr, step, axis=0), 0.0)
            si = jnp.where(chunk >= step, pltpu.roll(xi, step, axis=0), 0.0)
            xr = xr + sr * ar - si * ai
            xi = xi + si * ar + sr * ai
            step *= 2
            j += 1
        x_in = jnp.concatenate([jnp.where(chunk >= 1, pltpu.roll(x, 1, axis=0), 0.0)
                                for x in (xr, xi)], axis=1).astype(BF16)
        for second, g in enumerate(gs):
            y = jnp.dot(us[second], m_intra[second], preferred_element_type=F32)
            y = y + lax.dot_general(x_in, wo_t[second], NT_DIMS, preferred_element_type=F32)
            y = y + us[second].astype(F32) * dv_ref[g]
            zg_scr[g] = jax.nn.gelu(y).astype(BF16)
        return carry

    lax.fori_loop(0, GROUPS_PER_SLAB // 2, group_pair, 0)

    def relayout_out(rb, carry):
        t0 = pl.multiple_of(rb * nb * w, nb * w)
        r0 = pl.multiple_of(rb * nb * L, nb * L)
        by_pos = []
        for hf in range(L // GROUPS_PER_SLAB):
            arrs = [zg_scr[g, pl.ds(r0, nb * L), LANES * hf:LANES * (hf + 1)].astype(F32)
                    for g in range(GROUPS_PER_SLAB)]
            by_pos += _piece_transpose(arrs, piece)
        for blk in range(nb):
            z_ref[pl.ds(t0 + w * blk, w), :] = jnp.concatenate(
                [arr[L * blk:L * (blk + 1)] for arr in by_pos], axis=0).astype(BF16)
        return carry

    lax.fori_loop(0, rows // (nb * L), relayout_out, 0)


def _ssm_scan(u, cc, bt, pw1, pw2, zoh, a1, a2, dv, *, n_chunks):
    t, d = u.shape
    L = SSM_CHUNK
    rows = t // L
    w = L * SSM_GROUP
    gps = GROUPS_PER_SLAB
    slab = lambda j: (j, 0, 0)
    pspec = lambda arr: pl.BlockSpec((gps,) + arr.shape[1:], slab)
    return pl.pallas_call(
        functools.partial(_ssm_kernel, n_chunks=n_chunks),
        grid=(d // LANES,),
        in_specs=[pl.BlockSpec((t, LANES), lambda j: (0, j)),
                  pspec(cc), pspec(bt), pspec(pw1), pspec(pw2), pspec(zoh), pspec(a1), pspec(a2),
                  pspec(dv)],
        out_specs=pl.BlockSpec((t, LANES), lambda j: (0, j)),
        out_shape=jax.ShapeDtypeStruct((t, d), BF16),
        scratch_shapes=[pltpu.VMEM((gps, rows, w), BF16), pltpu.VMEM((gps, rows, w), BF16)],
        compiler_params=pltpu.CompilerParams(
            dimension_semantics=("arbitrary",), vmem_limit_bytes=VMEM_LIMIT),
        name="s5_scan",
    )(u, cc, bt, pw1, pw2, zoh, a1, a2, dv)


def _ssm_operators(a_re, a_im, b_re, b_im, c_re, c_im, log_dt, d_skip, n_chunks):
    L = SSM_CHUNK
    g, p = a_re.shape
    dt = jnp.exp(log_dt)[:, None]
    lam_re, lam_im = dt * a_re, dt * a_im

    def powers(ks):
        ks = jnp.asarray(ks, F32)[:, None, None]
        mag = jnp.exp(ks * lam_re)
        return mag * jnp.cos(ks * lam_im), mag * jnp.sin(ks * lam_im)

    def patterns(re, im):
        return (jnp.concatenate([re, re], axis=2).transpose(1, 0, 2),
                jnp.concatenate([-im, im], axis=2).transpose(1, 0, 2))

    pr, pi = powers(range(L + 1))
    num_re, num_im = pr[1] - 1.0, pi[1]
    den = a_re * a_re + a_im * a_im
    s_re = (num_re * a_re + num_im * a_im) / den
    s_im = (num_im * a_re - num_re * a_im) / den
    pw1, pw2 = patterns(pr, pi)
    zoh = jnp.concatenate(patterns(s_re[None], s_im[None]), axis=1)
    n_steps = max(1, int(math.log2(n_chunks)))
    a1, a2 = patterns(*powers([L * 2 ** j for j in range(n_steps)]))
    cc = jnp.concatenate([c_re, c_im], axis=2)
    bt = jnp.concatenate([b_re.transpose(0, 2, 1), b_im.transpose(0, 2, 1)], axis=2)
    dv = jnp.tile(d_skip.reshape(g, 1, SSM_GROUP), (1, L, 1)).reshape(g, 1, L * SSM_GROUP)
    return cc, bt, pw1, pw2, zoh, a1, a2, dv


def kernel(x, norm_mix_g, norm_mlp_g, attn_w_in, attn_b_f, attn_q_g, attn_k_g, attn_w_out,
           ssm_w_in, ssm_a_re, ssm_a_im, ssm_b_re, ssm_b_im, ssm_c_re, ssm_c_im, ssm_log_dt,
           ssm_d, ssm_w_glu, mlp_w1, mlp_w2):
    b, s, d = x.shape
    t = b * s
    tk = min(512, s // 2)
    tq = 2 * tk
    tm_proj = min(512, s)
    tm_mlp = min(512, t)

    w_in = attn_w_in[0]
    w_f = jnp.pad(w_in[:, 3 * d:], ((0, 0), (0, LANES - N_HEADS)))
    wkf = jnp.concatenate([w_in[:, d:2 * d], w_f], axis=1).astype(BF16)
    wqvt = jnp.concatenate([w_in[:, :d], w_in[:, 2 * d:3 * d]], axis=1).T.astype(BF16)
    blk = jnp.arange(MXU_DIM) // HEAD_DIM
    bd = ((blk[:, None] == blk[None, :]) * (1.0 / HEAD_DIM)).astype(BF16)
    gain_k = jnp.tile(attn_k_g[0], N_HEADS).reshape(1, d)
    gain_qt = jnp.broadcast_to(
        (jnp.tile(attn_q_g[0], N_HEADS) * (LOG2E / math.sqrt(HEAD_DIM)))[:, None], (d, LANES))
    bfp = jnp.pad(attn_b_f[0], (0, LANES - N_HEADS)).reshape(1, LANES)
    tri = (jnp.arange(tm_proj)[:, None] >= jnp.arange(tm_proj)[None, :]).astype(BF16)
    heads = jnp.arange(N_HEADS)
    place_k = jnp.zeros((LANES, d), F32)
    for i in range(BIAS_PIECES):
        place_k = place_k.at[N_HEADS * i + heads,
                             (heads // HEADS_PER_STEP) * LANES
                             + BIAS_PIECES * (heads % HEADS_PER_STEP) + i].set(1.0)
    place_qt = jnp.roll(place_k, BIAS_Q_LANE, axis=1).T
    shift = (LOG2E * math.sqrt(HEAD_DIM) * jnp.max(jnp.abs(attn_q_g[0]))
             * jnp.max(jnp.abs(attn_k_g[0])))
    qt, k, vt, kb, qbt, edge = _qkv_proj(
        x, norm_mix_g[0].reshape(1, d), wkf, wqvt, bd, gain_k, gain_qt, bfp, tri,
        place_k.astype(BF16), place_qt.astype(BF16), jnp.full((1, LANES), shift, F32),
        tm=tm_proj, tk=tk)
    assert tm_proj == tk
    c_first = edge[:, ::tq // tk, 0, :N_HEADS]
    c_last = edge[:, :, 1, :N_HEADS]
    bound = (c_first[:, :, None, :] - c_last[:, None, :, :]) * LOG2E
    key_blk = jnp.arange(s // tk)[None, None, :, None]
    first = jnp.min(jnp.where(bound < SKIP_LOGIT, s // tk, key_blk), axis=2)
    first = jnp.minimum(first, (tq // tk) * jnp.arange(s // tq)[None, :, None]) // 2 * 2
    first = first.transpose(0, 2, 1).reshape(-1).astype(jnp.int32)
    o = lax.cond(shift <= MAX_SHIFT,
                 functools.partial(_attention, tq=tq, tk=tk, online=False),
                 functools.partial(_attention, tq=tq, tk=tk, online=True),
                 first, qt, qbt, k, kb, vt)
    tok = jnp.arange(SSM_CHUNK * SSM_CHUNK)
    perm = (tok[None, :] == (tok[:, None] % SSM_CHUNK) * SSM_CHUNK + tok[:, None] // SSM_CHUNK)
    perm = perm.astype(BF16)
    x2, u = _mix_mlp(x.reshape(t, d), o.reshape(t, d), attn_w_out[0].astype(BF16),
                     norm_mlp_g[0].reshape(1, d), mlp_w1[0].astype(BF16), mlp_w2[0].astype(BF16),
                     perm, (norm_mix_g[1].reshape(1, d), ssm_w_in[0].astype(BF16)),
                     glu=False, tm=tm_mlp)

    n_chunks = s // SSM_CHUNK
    ops = _ssm_operators(ssm_a_re[0], ssm_a_im[0], ssm_b_re[0], ssm_b_im[0], ssm_c_re[0],
                         ssm_c_im[0], ssm_log_dt[0], ssm_d[0], n_chunks)
    z = _ssm_scan(u, *ops, n_chunks=n_chunks)
    x3 = _mix_mlp(x2, z, ssm_w_glu[0].astype(BF16), norm_mlp_g[1].reshape(1, d),
                  mlp_w1[1].astype(BF16), mlp_w2[1].astype(BF16), perm, glu=True, tm=tm_mlp)
    return x3.reshape(b, s, d)
```

```python
import functools
import math

import jax
import jax.numpy as jnp
import numpy as np
from jax import lax
from jax.experimental import pallas as pl
from jax.experimental.pallas import tpu as pltpu

F32 = jnp.float32
BF16 = jnp.bfloat16

N_HEADS = 16
HEAD_DIM = 64
SSM_GROUP = 16
SSM_STATE = 64
SSM_CHUNK = 16
EPS = 1e-6
LOG2E = 1.4426950408889634

LANES = 128
MXU_DIM = 256
HEADS_PER_STEP = LANES // HEAD_DIM
BIAS_PIECES = 3
VMEM_LIMIT = 56 * 1024 * 1024

NT_DIMS = (((1,), (1,)), ((), ()))


def _rms_norm(x, g):
    ms = jnp.mean(x * x, axis=-1, keepdims=True)
    return x * lax.rsqrt(ms + EPS) * g


def _const_spec(shape):
    zeros = (0,) * len(shape)
    return pl.BlockSpec(shape, lambda *_: zeros)


def _bf16_pieces(val):
    pieces = jnp.zeros_like(val)
    rem = val
    for n in range(BIAS_PIECES):
        piece = rem.astype(BF16).astype(F32)
        rem = rem - piece
        pieces = pieces + (piece if n == 0 else pltpu.roll(piece, N_HEADS * n, axis=1))
    return pieces


def _qkv_kernel(x_ref, g_ref, wkf_ref, wqvt_ref, bd_ref, gain_k_ref, gain_qt_ref, bf_ref, tri_ref,
                place_k_ref, place_qt_ref, shift_ref,
                qt_ref, k_ref, vt_ref, kb_ref, qbt_ref, edge_ref, carry_ref, *, tm, tk, d):
    @pl.when(pl.program_id(1) == 0)
    def _():
        carry_ref[...] = jnp.zeros_like(carry_ref)

    h = _rms_norm(x_ref[0], g_ref[...]).astype(BF16)

    y = jnp.dot(h, wkf_ref[...], preferred_element_type=F32)
    vt = lax.dot_general(wqvt_ref[d:, :], h, NT_DIMS, preferred_element_type=F32).astype(BF16)
    for j in range(tm // tk):
        vt_ref[0, j] = vt[:, tk * j:tk * (j + 1)]
    qt = lax.dot_general(wqvt_ref[:d, :], h, NT_DIMS, preferred_element_type=F32)

    for t in range(d // MXU_DIM):
        sl = slice(MXU_DIM * t, MXU_DIM * (t + 1))
        tile = y[:, sl]
        ms = jnp.dot((tile * tile).astype(BF16), bd_ref[...], preferred_element_type=F32)
        k_ref[0, :, sl] = (tile * lax.rsqrt(ms + EPS) * gain_k_ref[:, sl]).astype(BF16)
        tile = qt[sl, :]
        ms = jnp.dot(bd_ref[...], (tile * tile).astype(BF16), preferred_element_type=F32)
        gain = jnp.tile(gain_qt_ref[sl, :], (1, tm // LANES))
        qt_ref[0, sl, :] = (tile * lax.rsqrt(ms + EPS) * gain).astype(BF16)

    f = y[:, d:] + bf_ref[...]
    log_f = jnp.minimum(f, 0.0) - jnp.log1p(jnp.exp(-jnp.abs(f)))
    lane = lax.broadcasted_iota(jnp.int32, log_f.shape, 1)
    part = jnp.dot(tri_ref[...], _bf16_pieces(jnp.where(lane < N_HEADS, log_f, 0.0)).astype(BF16),
                   preferred_element_type=F32)
    cs = carry_ref[...] + part
    for n in range(1, BIAS_PIECES):
        cs = cs + pltpu.roll(part, LANES - N_HEADS * n, axis=1)
    carry_ref[...] = cs[tm - 1:tm, :]
    edge_ref[0, 0] = jnp.concatenate(
        [cs[0:1, :], cs[tm - 1:tm, :], jnp.zeros((6, LANES), F32)], axis=0)
    c2 = jnp.where(lane < N_HEADS, cs * LOG2E, 0.0)
    kb_ref[0] = jnp.dot(_bf16_pieces(-c2).astype(BF16), place_k_ref[...],
                        preferred_element_type=F32).astype(BF16)
    q_pieces = _bf16_pieces(jnp.where(lane < N_HEADS, c2 - shift_ref[...], 0.0))
    qbt_ref[0] = jnp.dot(place_qt_ref[...], q_pieces.T.astype(BF16),
                         preferred_element_type=F32).astype(BF16)


def _qkv_proj(x, g, wkf, wqvt, bd, gain_k, gain_qt, bfp, tri, place_k, place_qt, shift, *, tm, tk):
    b, s, d = x.shape
    row = lambda bi, si: (bi, si, 0)
    col = lambda bi, si: (bi, 0, si)
    consts = (g, wkf, wqvt, bd, gain_k, gain_qt, bfp, tri, place_k, place_qt, shift)
    return pl.pallas_call(
        functools.partial(_qkv_kernel, tm=tm, tk=tk, d=d),
        grid=(b, s // tm),
        in_specs=[pl.BlockSpec((1, tm, d), row)] + [_const_spec(c.shape) for c in consts],
        out_specs=[pl.BlockSpec((1, d, tm), col),
                   pl.BlockSpec((1, tm, d), row),
                   pl.BlockSpec((1, tm // tk, d, tk), lambda bi, si: (bi, si, 0, 0)),
                   pl.BlockSpec((1, tm, d), row),
                   pl.BlockSpec((1, d, tm), col),
                   pl.BlockSpec((1, 1, 8, LANES), lambda bi, si: (bi, si, 0, 0))],
        out_shape=[jax.ShapeDtypeStruct((b, d, s), BF16),
                   jax.ShapeDtypeStruct((b, s, d), BF16),
                   jax.ShapeDtypeStruct((b, s // tk, d, tk), BF16),
                   jax.ShapeDtypeStruct((b, s, d), BF16),
                   jax.ShapeDtypeStruct((b, d, s), BF16),
                   jax.ShapeDtypeStruct((b, s // tm, 8, LANES), F32)],
        scratch_shapes=[pltpu.VMEM((1, LANES), F32)],
        compiler_params=pltpu.CompilerParams(
            dimension_semantics=("arbitrary", "arbitrary"), vmem_limit_bytes=VMEM_LIMIT),
        name="qkv_proj",
    )(x, *consts)


BIAS_Q_LANE = 8
MAX_SHIFT = 48.0
SKIP_LOGIT = -200.0


def _attn_kernel(first_ref, *refs, tq, tk):
    use_online = first_ref[first_ref.shape[0] - 1]

    @pl.when(use_online == 0)
    def _():
        _attn_body(first_ref, *refs, tq=tq, tk=tk, online=False)

    @pl.when(use_online != 0)
    def _():
        _attn_body(first_ref, *refs, tq=tq, tk=tk, online=True)


def _attn_body(first_ref, qt_ref, qbt_ref, k_ref, kb_ref, vt_ref, o_ref, s0_scr, s1_scr, *,
               tq, tk, online):
    i = pl.program_id(2)
    n_sub = tq // tk
    qt = qt_ref[0]
    qbt = qbt_ref[0]
    row = lax.broadcasted_iota(jnp.int32, (LANES, tq), 0)
    lane_k = lax.broadcasted_iota(jnp.int32, (tk, LANES), 1)
    q_cols = (lane_k >= BIAS_Q_LANE) & (lane_k < BIAS_Q_LANE + HEADS_PER_STEP * BIAS_PIECES)
    causal = (lax.broadcasted_iota(jnp.int32, (tk, tk), 0)
              <= lax.broadcasted_iota(jnp.int32, (tk, tk), 1))
    one = jnp.ones((), BF16)
    zero = jnp.zeros((), BF16)
    heads = range(HEADS_PER_STEP)
    qats = []
    for hh in heads:
        q_h = jnp.where((row >= HEAD_DIM * hh) & (row < HEAD_DIM * (hh + 1)), qt, zero)
        k_side = (row >= BIAS_PIECES * hh) & (row < BIAS_PIECES * (hh + 1))
        q_lo = BIAS_Q_LANE + BIAS_PIECES * hh
        q_side = (row >= q_lo) & (row < q_lo + BIAS_PIECES)
        qats.append(jnp.concatenate(
            [q_h, jnp.where(k_side, one, jnp.where(q_side, qbt, zero))], axis=0))

    def scores(hh, j, c0=0):
        off = pl.multiple_of(j * tk, tk)
        kbias = jnp.where(q_cols, one, kb_ref[0, pl.ds(off, tk), :])
        ka = jnp.concatenate([k_ref[0, pl.ds(off, tk), :], kbias], axis=1)
        return jnp.dot(ka, qats[hh][:, c0:], preferred_element_type=F32)

    def consume(hh, carry, s, j, diagonal):
        if diagonal:
            s_tri = jnp.where(causal, s[:, :tk], -1e30)
            s = s_tri if s.shape[1] == tk else jnp.concatenate([s_tri, s[:, tk:]], axis=1)
        vt = vt_ref[0, j, HEAD_DIM * hh:HEAD_DIM * (hh + 1), :]
        if online:
            m, l, acc = carry
            m_new = jnp.maximum(m, jnp.max(s, axis=0, keepdims=True))
            alpha = jnp.exp2(m - m_new)
            p = jnp.exp2(s - m_new)
            l = alpha * l + jnp.sum(p, axis=0, keepdims=True)
            acc = alpha * acc + jnp.dot(vt, p.astype(BF16), preferred_element_type=F32)
            return m_new, l, acc
        l, acc = carry
        p = jnp.exp2(s)
        l = l + jnp.sum(p, axis=0, keepdims=True)
        acc = acc + jnp.dot(vt, p.astype(BF16), preferred_element_type=F32)
        return l, acc

    init = (jnp.zeros((1, tq), F32), jnp.zeros((HEAD_DIM, tq), F32))
    if online:
        init = (jnp.full((1, tq), -1e30, F32),) + init

    j0 = n_sub * i

    def first_scores(hh):
        if online:
            jj0 = 0
        else:
            head = HEADS_PER_STEP * pl.program_id(1) + hh
            jj0 = first_ref[(pl.program_id(0) * N_HEADS + head) * pl.num_programs(2) + i] // 2
        s0_scr[hh] = scores(hh, 2 * jj0)
        return jj0

    def off_diagonal(hh, jj0):
        def pairs(jj, c, n_pairs):
            for r in range(n_pairs):
                s1_scr[hh] = scores(hh, 2 * (jj + r) + 1)
                c = consume(hh, c, s0_scr[hh], 2 * (jj + r), False)
                s0_scr[hh] = scores(hh, 2 * (jj + r) + 2)
                c = consume(hh, c, s1_scr[hh], 2 * (jj + r) + 1, False)
            return c

        n_long = (j0 // 2 - jj0) // 2
        c = lax.fori_loop(0, n_long, lambda t, c: pairs(jj0 + 2 * t, c, 2), init)
        return lax.fori_loop(jj0 + 2 * n_long, j0 // 2, lambda jj, c: pairs(jj, c, 1), c)

    def diagonal(hh, carry):
        later = [scores(hh, j0 + dd, tk * dd) for dd in range(1, n_sub)]
        carry = consume(hh, carry, s0_scr[hh], j0, True)
        for dd in range(1, n_sub):
            c0 = tk * dd
            part = consume(hh, tuple(c[:, c0:] for c in carry), later[dd - 1], j0 + dd, True)
            carry = tuple(jnp.concatenate([c[:, :c0], pc], axis=1) for c, pc in zip(carry, part))
        return carry[-1] / carry[-2]

    outs = []
    jj0 = first_scores(0)
    for hh in heads:
        carry = off_diagonal(hh, jj0)
        if hh + 1 < HEADS_PER_STEP:
            jj0 = first_scores(hh + 1)
        outs.append(diagonal(hh, carry))
    o_t = jnp.concatenate(outs, axis=0)
    o_ref[0] = o_t.T.astype(BF16)


def _attention(first, qt, qbt, k, kb, vt, *, tq, tk):
    b, s, d = k.shape
    assert (tq // tk) % 2 == 0 and s % tq == 0
    n_pairs = d // LANES
    nk = s // tk
    qt_blk = pl.BlockSpec((1, LANES, tq), lambda bi, hp, i, first: (bi, hp, i))
    k_all = pl.BlockSpec((1, s, LANES), lambda bi, hp, i, first: (bi, 0, hp))
    return pl.pallas_call(
        functools.partial(_attn_kernel, tq=tq, tk=tk),
        grid_spec=pltpu.PrefetchScalarGridSpec(
            num_scalar_prefetch=1,
            grid=(b, n_pairs, s // tq),
            in_specs=[qt_blk, qt_blk, k_all, k_all,
                      pl.BlockSpec((1, nk, LANES, tk), lambda bi, hp, i, first: (bi, 0, hp, 0))],
            out_specs=pl.BlockSpec((1, tq, LANES), lambda bi, hp, i, first: (bi, i, hp)),
            scratch_shapes=[pltpu.VMEM((HEADS_PER_STEP, tk, tq), F32),
                            pltpu.VMEM((HEADS_PER_STEP, tk, tq), F32)]),
        out_shape=jax.ShapeDtypeStruct((b, s, d), BF16),
        compiler_params=pltpu.CompilerParams(
            dimension_semantics=("arbitrary", "arbitrary", "arbitrary"),
            vmem_limit_bytes=VMEM_LIMIT),
        name="fox_attention",
    )(first, qt, qbt, k, kb, vt)


def _permute_token_blocks(perm, a):
    w = perm.shape[0]
    return jnp.concatenate(
        [jnp.dot(perm, a[w * i:w * (i + 1)], preferred_element_type=F32).astype(BF16)
         for i in range(a.shape[0] // w)], axis=0)


def _mix_mlp_kernel(x_ref, a_ref, wmix_ref, g_ref, w1_ref, w2_ref, *rest, glu, d, ff_chunk):
    if glu:
        a = _permute_token_blocks(rest[0][...], a_ref[...])
        mix = jnp.dot(a, wmix_ref[...], preferred_element_type=F32)
        mix = mix[:, :d] * jax.nn.sigmoid(mix[:, d:])
    else:
        mix = jnp.dot(a_ref[...], wmix_ref[...], preferred_element_type=F32)
    x1 = x_ref[...] + mix
    h = _rms_norm(x1, g_ref[...]).astype(BF16)
    acc = x1
    for c in range(w1_ref.shape[1] // ff_chunk):
        sl = slice(ff_chunk * c, ff_chunk * (c + 1))
        hid = jnp.maximum(jnp.dot(h, w1_ref[:, sl], preferred_element_type=F32), 0.0)
        acc = acc + jnp.dot((hid * hid).astype(BF16), w2_ref[sl, :], preferred_element_type=F32)
    if glu:
        _, o_ref = rest
    else:
        perm_ref, g_next_ref, w_next_ref, o_ref, u_ref = rest
        h_next = _rms_norm(acc, g_next_ref[...]).astype(BF16)
        u = jnp.dot(h_next, w_next_ref[...], preferred_element_type=F32).astype(BF16)
        u_ref[...] = _permute_token_blocks(perm_ref[...], u)
    o_ref[...] = acc


def _mix_mlp(x, a, wmix, g, w1, w2, perm, next_proj=None, *, glu, tm, ff_chunk=1024):
    t, d = x.shape
    row = lambda i: (i, 0)
    single = pl.Buffered(1)
    wspec = lambda shape: pl.BlockSpec(shape, lambda i: (0, 0), pipeline_mode=single)
    operands = [x, a, wmix, g, w1, w2, perm]
    in_specs = [pl.BlockSpec((tm, d), row), pl.BlockSpec((tm, d), row),
                wspec(wmix.shape), wspec((1, d)), wspec(w1.shape), wspec(w2.shape),
                wspec(perm.shape)]
    out_specs = pl.BlockSpec((tm, d), row)
    out_shape = jax.ShapeDtypeStruct((t, d), F32)
    if next_proj is not None:
        operands += list(next_proj)
        in_specs += [wspec(w.shape) for w in next_proj]
        n = next_proj[1].shape[1]
        out_specs = [out_specs, pl.BlockSpec((tm, n), row)]
        out_shape = [out_shape, jax.ShapeDtypeStruct((t, n), BF16)]
    return pl.pallas_call(
        functools.partial(_mix_mlp_kernel, glu=glu, d=d, ff_chunk=ff_chunk),
        grid=(t // tm,),
        in_specs=in_specs,
        out_specs=out_specs,
        out_shape=out_shape,
        compiler_params=pltpu.CompilerParams(
            dimension_semantics=("arbitrary",), vmem_limit_bytes=VMEM_LIMIT),
        name="mix_glu_mlp" if glu else "mix_mlp",
    )(*operands)


GROUPS_PER_SLAB = LANES // SSM_GROUP


def _piece_transpose(arrs, piece):
    arrs = list(arrs)
    dist = GROUPS_PER_SLAB // 2
    while dist >= 1:
        keep = (piece & dist) == 0
        shift = SSM_GROUP * dist
        for i in range(GROUPS_PER_SLAB):
            if i & dist:
                continue
            a, b = arrs[i], arrs[i + dist]
            arrs[i] = jnp.where(keep, a, pltpu.roll(b, shift, axis=1))
            arrs[i + dist] = jnp.where(keep, pltpu.roll(a, LANES - shift, axis=1), b)
        dist //= 2
    return arrs


def _ssm_kernel(u_ref, cc_ref, bt_ref, pw1_ref, pw2_ref, zoh_ref, a1_ref, a2_ref,
                dv_ref, z_ref, ug_scr, zg_scr, *, n_chunks):
    L = SSM_CHUNK
    rows = u_ref.shape[0] // L
    w = L * SSM_GROUP
    half = SSM_STATE
    nb = math.gcd(rows // L, 8)
    piece = lax.broadcasted_iota(jnp.int32, (nb * L, LANES), 1) // SSM_GROUP

    def relayout_in(rb, carry):
        t0 = pl.multiple_of(rb * nb * w, nb * w)
        r0 = pl.multiple_of(rb * nb * L, nb * L)
        by_pos = u_ref[pl.ds(t0, nb * w), :].astype(F32)
        for hf in range(L // GROUPS_PER_SLAB):
            arrs = []
            for k in range(GROUPS_PER_SLAB):
                pos = GROUPS_PER_SLAB * hf + k
                arrs.append(jnp.concatenate(
                    [by_pos[w * blk + L * pos:w * blk + L * (pos + 1)] for blk in range(nb)],
                    axis=0))
            arrs = _piece_transpose(arrs, piece)
            for g in range(GROUPS_PER_SLAB):
                ug_scr[g, pl.ds(r0, nb * L), LANES * hf:LANES * (hf + 1)] = arrs[g].astype(BF16)
        return carry

    lax.fori_loop(0, rows // (nb * L), relayout_in, 0)

    chunk = lax.broadcasted_iota(jnp.int32, (rows, LANES), 0) % n_chunks
    lane_w = lax.broadcasted_iota(jnp.int32, (SSM_GROUP, w), 1)
    low = lax.broadcasted_iota(jnp.int32, (1, LANES), 1) < half
    sign = jnp.where(low, 1.0, -1.0)

    def operators(g):
        cc = cc_ref[g]
        bt = bt_ref[g]
        pw1 = pw1_ref[g]
        pw2 = pw2_ref[g]
        zoh = zoh_ref[g]
        bbar = bt * zoh[0:1] + pltpu.roll(bt, half, axis=1) * zoh[1:2]
        bbar_sw = pltpu.roll(bbar, half, axis=1)
        cc_sw = pltpu.roll(cc, half, axis=1)
        ca = [cc * pw1[k:k + 1] + cc_sw * pw2[k:k + 1] for k in range(L + 1)]
        k_all = lax.dot_general(bbar * sign, jnp.concatenate(ca[:L], axis=0), NT_DIMS,
                                precision=lax.Precision.HIGHEST, preferred_element_type=F32)
        m_intra, w_state, w_out_t = [], [], []
        for s in range(L):
            blk = k_all if s == 0 else jnp.where(
                lane_w >= SSM_GROUP * s, pltpu.roll(k_all, SSM_GROUP * s, axis=1), 0.0)
            m_intra.append(blk.astype(BF16))
            k = L - 1 - s
            w_state.append(bbar * pw1[k:k + 1] + bbar_sw * pw2[k:k + 1])
            w_out_t.append(ca[s + 1] * sign)
        return jnp.concatenate(m_intra, axis=0), w_state, w_out_t

    def halves(pieces, second):
        firsts, seconds = [], []
        for piece in pieces:
            swapped = pltpu.roll(piece, half, axis=1)
            if second:
                firsts.append(jnp.where(low, 0.0, swapped))
                seconds.append(jnp.where(low, 0.0, piece))
            else:
                firsts.append(jnp.where(low, piece, 0.0))
                seconds.append(jnp.where(low, swapped, 0.0))
        return firsts, seconds

    def group_pair(pr, carry):
        gs = (2 * pr, 2 * pr + 1)
        m_intra, ws_re, ws_im, wo_t = [], [], [], []
        for second, g in enumerate(gs):
            mi, w_state, w_out_t = operators(g)
            m_intra.append(mi)
            re, im = halves(w_state, second)
            ws_re += re
            ws_im += im
            re, im = halves(w_out_t, second)
            wo_t.append(jnp.concatenate(
                [jnp.concatenate([r, m], axis=1) for r, m in zip(re, im)], axis=0).astype(BF16))
        us = [ug_scr[g] for g in gs]
        uu = jnp.concatenate(us, axis=1)
        xr = jnp.dot(uu, jnp.concatenate(ws_re, axis=0).astype(BF16), preferred_element_type=F32)
        xi = jnp.dot(uu, jnp.concatenate(ws_im, axis=0).astype(BF16), preferred_element_type=F32)
        a1 = [a1_ref[g] for g in gs]
        a2 = [a2_ref[g] for g in gs]
        step = 1
        j = 0
        while step < n_chunks:
            ar = jnp.where(low, a1[0][j:j + 1], a1[1][j:j + 1])
            ai = jnp.where(low, -a2[0][j:j + 1], a2[1][j:j + 1])
            sr = jnp.where(chunk >= step, pltpu.roll(xr, step, axis=0), 0.0)
            si = jnp.where(chunk >= step, pltpu.roll(xi, step, axis=0), 0.0)
            xr = xr + sr * ar - si * ai
            xi = xi + si * ar + sr * ai
            step *= 2
            j += 1
        x_in = jnp.concatenate([jnp.where(chunk >= 1, pltpu.roll(x, 1, axis=0), 0.0)
                                for x in (xr, xi)], axis=1).astype(BF16)
        for second, g in enumerate(gs):
            y = jnp.dot(us[second], m_intra[second], preferred_element_type=F32)
            y = y + lax.dot_general(x_in, wo_t[second], NT_DIMS, preferred_element_type=F32)
            y = y + us[second].astype(F32) * dv_ref[g]
            zg_scr[g] = jax.nn.gelu(y).astype(BF16)
        return carry

    lax.fori_loop(0, GROUPS_PER_SLAB // 2, group_pair, 0)

    def relayout_out(rb, carry):
        t0 = pl.multiple_of(rb * nb * w, nb * w)
        r0 = pl.multiple_of(rb * nb * L, nb * L)
        by_pos = []
        for hf in range(L // GROUPS_PER_SLAB):
            arrs = [zg_scr[g, pl.ds(r0, nb * L), LANES * hf:LANES * (hf + 1)].astype(F32)
                    for g in range(GROUPS_PER_SLAB)]
            by_pos += _piece_transpose(arrs, piece)
        for blk in range(nb):
            z_ref[pl.ds(t0 + w * blk, w), :] = jnp.concatenate(
                [arr[L * blk:L * (blk + 1)] for arr in by_pos], axis=0).astype(BF16)
        return carry

    lax.fori_loop(0, rows // (nb * L), relayout_out, 0)


def _ssm_scan(u, cc, bt, pw1, pw2, zoh, a1, a2, dv, *, n_chunks):
    t, d = u.shape
    L = SSM_CHUNK
    rows = t // L
    w = L * SSM_GROUP
    gps = GROUPS_PER_SLAB
    slab = lambda j: (j, 0, 0)
    pspec = lambda arr: pl.BlockSpec((gps,) + arr.shape[1:], slab)
    return pl.pallas_call(
        functools.partial(_ssm_kernel, n_chunks=n_chunks),
        grid=(d // LANES,),
        in_specs=[pl.BlockSpec((t, LANES), lambda j: (0, j)),
                  pspec(cc), pspec(bt), pspec(pw1), pspec(pw2), pspec(zoh), pspec(a1), pspec(a2),
                  pspec(dv)],
        out_specs=pl.BlockSpec((t, LANES), lambda j: (0, j)),
        out_shape=jax.ShapeDtypeStruct((t, d), BF16),
        scratch_shapes=[pltpu.VMEM((gps, rows, w), BF16), pltpu.VMEM((gps, rows, w), BF16)],
        compiler_params=pltpu.CompilerParams(
            dimension_semantics=("arbitrary",), vmem_limit_bytes=VMEM_LIMIT),
        name="s5_scan",
    )(u, cc, bt, pw1, pw2, zoh, a1, a2, dv)


def _ssm_operators(a_re, a_im, b_re, b_im, c_re, c_im, log_dt, d_skip, n_chunks):
    L = SSM_CHUNK
    g, p = a_re.shape
    dt = jnp.exp(log_dt)[:, None]
    lam_re, lam_im = dt * a_re, dt * a_im

    def powers(ks):
        ks = jnp.asarray(ks, F32)[:, None, None]
        mag = jnp.exp(ks * lam_re)
        return mag * jnp.cos(ks * lam_im), mag * jnp.sin(ks * lam_im)

    def patterns(re, im):
        return (jnp.concatenate([re, re], axis=2).transpose(1, 0, 2),
                jnp.concatenate([-im, im], axis=2).transpose(1, 0, 2))

    pr, pi = powers(range(L + 1))
    num_re, num_im = pr[1] - 1.0, pi[1]
    den = a_re * a_re + a_im * a_im
    s_re = (num_re * a_re + num_im * a_im) / den
    s_im = (num_im * a_re - num_re * a_im) / den
    pw1, pw2 = patterns(pr, pi)
    zoh = jnp.concatenate(patterns(s_re[None], s_im[None]), axis=1)
    n_steps = max(1, int(math.log2(n_chunks)))
    a1, a2 = patterns(*powers([L * 2 ** j for j in range(n_steps)]))
    cc = jnp.concatenate([c_re, c_im], axis=2)
    bt = jnp.concatenate([b_re.transpose(0, 2, 1), b_im.transpose(0, 2, 1)], axis=2)
    dv = jnp.tile(d_skip.reshape(g, 1, SSM_GROUP), (1, L, 1)).reshape(g, 1, L * SSM_GROUP)
    return cc, bt, pw1, pw2, zoh, a1, a2, dv


def kernel(x, norm_mix_g, norm_mlp_g, attn_w_in, attn_b_f, attn_q_g, attn_k_g, attn_w_out,
           ssm_w_in, ssm_a_re, ssm_a_im, ssm_b_re, ssm_b_im, ssm_c_re, ssm_c_im, ssm_log_dt,
           ssm_d, ssm_w_glu, mlp_w1, mlp_w2):
    b, s, d = x.shape
    t = b * s
    tk = min(512, s // 2)
    tq = 2 * tk
    tm_proj = min(512, s)
    tm_mlp = min(512, t)

    w_in = attn_w_in[0]
    w_f = jnp.pad(w_in[:, 3 * d:], ((0, 0), (0, LANES - N_HEADS)))
    wkf = jnp.concatenate([w_in[:, d:2 * d], w_f], axis=1).astype(BF16)
    wqvt = jnp.concatenate([w_in[:, :d], w_in[:, 2 * d:3 * d]], axis=1).T.astype(BF16)
    blk = np.arange(MXU_DIM) // HEAD_DIM
    bd = jnp.asarray((blk[:, None] == blk[None, :]) * (1.0 / HEAD_DIM), BF16)
    gain_k = jnp.tile(attn_k_g[0], N_HEADS).reshape(1, d)
    gain_qt = jnp.broadcast_to(
        (jnp.tile(attn_q_g[0], N_HEADS) * (LOG2E / math.sqrt(HEAD_DIM)))[:, None], (d, LANES))
    bfp = jnp.pad(attn_b_f[0], (0, LANES - N_HEADS)).reshape(1, LANES)
    tri = jnp.asarray(np.arange(tm_proj)[:, None] >= np.arange(tm_proj)[None, :], BF16)
    heads = np.arange(N_HEADS)
    place_k = np.zeros((LANES, d), np.float32)
    for i in range(BIAS_PIECES):
        place_k[N_HEADS * i + heads, (heads // HEADS_PER_STEP) * LANES
                + BIAS_PIECES * (heads % HEADS_PER_STEP) + i] = 1.0
    place_qt = np.roll(place_k, BIAS_Q_LANE, axis=1).T
    shift = (LOG2E * math.sqrt(HEAD_DIM) * jnp.max(jnp.abs(attn_q_g[0]))
             * jnp.max(jnp.abs(attn_k_g[0])))
    qt, k, vt, kb, qbt, edge = _qkv_proj(
        x, norm_mix_g[0].reshape(1, d), wkf, wqvt, bd, gain_k, gain_qt, bfp, tri,
        jnp.asarray(place_k, BF16), jnp.asarray(place_qt, BF16), jnp.full((1, LANES), shift, F32),
        tm=tm_proj, tk=tk)
    assert tm_proj == tk
    c_first = edge[:, ::tq // tk, 0, :N_HEADS]
    c_last = edge[:, :, 1, :N_HEADS]
    bound = (c_first[:, :, None, :] - c_last[:, None, :, :]) * LOG2E
    key_blk = jnp.arange(s // tk)[None, None, :, None]
    first = jnp.min(jnp.where(bound < SKIP_LOGIT, s // tk, key_blk), axis=2)
    first = jnp.minimum(first, (tq // tk) * jnp.arange(s // tq)[None, :, None]) // 2 * 2
    first = jnp.concatenate([first.transpose(0, 2, 1).reshape(-1), (shift > MAX_SHIFT)[None]])
    o = _attention(first.astype(jnp.int32), qt, qbt, k, kb, vt, tq=tq, tk=tk)
    tok = np.arange(SSM_CHUNK * SSM_CHUNK)
    perm = jnp.asarray(
        tok[None, :] == (tok[:, None] % SSM_CHUNK) * SSM_CHUNK + tok[:, None] // SSM_CHUNK, BF16)
    x2, u = _mix_mlp(x.reshape(t, d), o.reshape(t, d), attn_w_out[0].astype(BF16),
                     norm_mlp_g[0].reshape(1, d), mlp_w1[0].astype(BF16), mlp_w2[0].astype(BF16),
                     perm, (norm_mix_g[1].reshape(1, d), ssm_w_in[0].astype(BF16)),
                     glu=False, tm=tm_mlp)

    n_chunks = s // SSM_CHUNK
    ops = _ssm_operators(ssm_a_re[0], ssm_a_im[0], ssm_b_re[0], ssm_b_im[0], ssm_c_re[0],
                         ssm_c_im[0], ssm_log_dt[0], ssm_d[0], n_chunks)
    z = _ssm_scan(u, *ops, n_chunks=n_chunks)
    x3 = _mix_mlp(x2, z, ssm_w_glu[0].astype(BF16), norm_mlp_g[1].reshape(1, d),
                  mlp_w1[1].astype(BF16), mlp_w2[1].astype(BF16), perm, glu=True, tm=tm_mlp)
    return x3.reshape(b, s, d)
```

```python
import functools
import math

import jax
import jax.numpy as jnp
import numpy as np
from jax import lax
from jax.experimental import pallas as pl
from jax.experimental.pallas import tpu as pltpu

F32 = jnp.float32
BF16 = jnp.bfloat16

N_HEADS = 16
HEAD_DIM = 64
SSM_GROUP = 16
SSM_STATE = 64
SSM_CHUNK = 16
EPS = 1e-6
LOG2E = 1.4426950408889634

LANES = 128
MXU_DIM = 256
HEADS_PER_STEP = LANES // HEAD_DIM
BIAS_PIECES = 3
VMEM_LIMIT = 56 * 1024 * 1024

NT_DIMS = (((1,), (1,)), ((), ()))


def _rms_norm(x, g):
    ms = jnp.mean(x * x, axis=-1, keepdims=True)
    return x * lax.rsqrt(ms + EPS) * g


def _const_spec(shape):
    zeros = (0,) * len(shape)
    return pl.BlockSpec(shape, lambda *_: zeros)


def _bf16_pieces(val):
    pieces = jnp.zeros_like(val)
    rem = val
    for n in range(BIAS_PIECES):
        piece = rem.astype(BF16).astype(F32)
        rem = rem - piece
        pieces = pieces + (piece if n == 0 else pltpu.roll(piece, N_HEADS * n, axis=1))
    return pieces


def _qkv_kernel(x_ref, g_ref, wkf_ref, wqvt_ref, bd_ref, gain_k_ref, gain_qt_ref, bf_ref, tri_ref,
                place_k_ref, place_qt_ref, shift_ref,
                qt_ref, k_ref, vt_ref, kb_ref, qbt_ref, edge_ref, carry_ref, *, tm, tk, d):
    @pl.when(pl.program_id(1) == 0)
    def _():
        carry_ref[...] = jnp.zeros_like(carry_ref)

    h = _rms_norm(x_ref[0], g_ref[...]).astype(BF16)

    y = jnp.dot(h, wkf_ref[...], preferred_element_type=F32)
    vt = lax.dot_general(wqvt_ref[d:, :], h, NT_DIMS, preferred_element_type=F32).astype(BF16)
    for j in range(tm // tk):
        vt_ref[0, j] = vt[:, tk * j:tk * (j + 1)]
    qt = lax.dot_general(wqvt_ref[:d, :], h, NT_DIMS, preferred_element_type=F32)

    for t in range(d // MXU_DIM):
        sl = slice(MXU_DIM * t, MXU_DIM * (t + 1))
        tile = y[:, sl]
        ms = jnp.dot((tile * tile).astype(BF16), bd_ref[...], preferred_element_type=F32)
        k_ref[0, :, sl] = (tile * lax.rsqrt(ms + EPS) * gain_k_ref[:, sl]).astype(BF16)
        tile = qt[sl, :]
        ms = jnp.dot(bd_ref[...], (tile * tile).astype(BF16), preferred_element_type=F32)
        gain = jnp.tile(gain_qt_ref[sl, :], (1, tm // LANES))
        qt_ref[0, sl, :] = (tile * lax.rsqrt(ms + EPS) * gain).astype(BF16)

    f = y[:, d:] + bf_ref[...]
    log_f = jnp.minimum(f, 0.0) - jnp.log1p(jnp.exp(-jnp.abs(f)))
    lane = lax.broadcasted_iota(jnp.int32, log_f.shape, 1)
    part = jnp.dot(tri_ref[...], _bf16_pieces(jnp.where(lane < N_HEADS, log_f, 0.0)).astype(BF16),
                   preferred_element_type=F32)
    cs = carry_ref[...] + part
    for n in range(1, BIAS_PIECES):
        cs = cs + pltpu.roll(part, LANES - N_HEADS * n, axis=1)
    carry_ref[...] = cs[tm - 1:tm, :]
    edge_ref[0, 0] = jnp.concatenate(
        [cs[0:1, :], cs[tm - 1:tm, :], jnp.zeros((6, LANES), F32)], axis=0)
    c2 = jnp.where(lane < N_HEADS, cs * LOG2E, 0.0)
    kb_ref[0] = jnp.dot(_bf16_pieces(-c2).astype(BF16), place_k_ref[...],
                        preferred_element_type=F32).astype(BF16)
    q_pieces = _bf16_pieces(jnp.where(lane < N_HEADS, c2 - shift_ref[...], 0.0))
    qbt_ref[0] = jnp.dot(place_qt_ref[...], q_pieces.T.astype(BF16),
                         preferred_element_type=F32).astype(BF16)


def _qkv_proj(x, g, wkf, wqvt, bd, gain_k, gain_qt, bfp, tri, place_k, place_qt, shift, *, tm, tk):
    b, s, d = x.shape
    row = lambda bi, si: (bi, si, 0)
    col = lambda bi, si: (bi, 0, si)
    consts = (g, wkf, wqvt, bd, gain_k, gain_qt, bfp, tri, place_k, place_qt, shift)
    return pl.pallas_call(
        functools.partial(_qkv_kernel, tm=tm, tk=tk, d=d),
        grid=(b, s // tm),
        in_specs=[pl.BlockSpec((1, tm, d), row)] + [_const_spec(c.shape) for c in consts],
        out_specs=[pl.BlockSpec((1, d, tm), col),
                   pl.BlockSpec((1, tm, d), row),
                   pl.BlockSpec((1, tm // tk, d, tk), lambda bi, si: (bi, si, 0, 0)),
                   pl.BlockSpec((1, tm, d), row),
                   pl.BlockSpec((1, d, tm), col),
                   pl.BlockSpec((1, 1, 8, LANES), lambda bi, si: (bi, si, 0, 0))],
        out_shape=[jax.ShapeDtypeStruct((b, d, s), BF16),
                   jax.ShapeDtypeStruct((b, s, d), BF16),
                   jax.ShapeDtypeStruct((b, s // tk, d, tk), BF16),
                   jax.ShapeDtypeStruct((b, s, d), BF16),
                   jax.ShapeDtypeStruct((b, d, s), BF16),
                   jax.ShapeDtypeStruct((b, s // tm, 8, LANES), F32)],
        scratch_shapes=[pltpu.VMEM((1, LANES), F32)],
        compiler_params=pltpu.CompilerParams(
            dimension_semantics=("arbitrary", "arbitrary"), vmem_limit_bytes=VMEM_LIMIT),
        name="qkv_proj",
    )(x, *consts)


PAIRS_PER_STEP = 2
BIAS_Q_LANE = 8
MAX_SHIFT = 48.0
SKIP_LOGIT = -200.0


def _attn_kernel(first_ref, *refs, tq, tk):
    use_online = first_ref[first_ref.shape[0] - 1]

    @pl.when(use_online == 0)
    def _():
        for pair in range(PAIRS_PER_STEP):
            _attn_body(first_ref, *refs, tq=tq, tk=tk, online=False, pair=pair)

    @pl.when(use_online != 0)
    def _():
        for pair in range(PAIRS_PER_STEP):
            _attn_body(first_ref, *refs, tq=tq, tk=tk, online=True, pair=pair)


def _attn_body(first_ref, qt_ref, qbt_ref, k_ref, kb_ref, vt_ref, o_ref, s0_scr, s1_scr, *,
               tq, tk, online, pair):
    lanes = slice(LANES * pair, LANES * (pair + 1))
    i = pl.program_id(2)
    n_sub = tq // tk
    qt = qt_ref[0, lanes, :]
    qbt = qbt_ref[0, lanes, :]
    row = lax.broadcasted_iota(jnp.int32, (LANES, tq), 0)
    lane_k = lax.broadcasted_iota(jnp.int32, (tk, LANES), 1)
    q_cols = (lane_k >= BIAS_Q_LANE) & (lane_k < BIAS_Q_LANE + HEADS_PER_STEP * BIAS_PIECES)
    causal = (lax.broadcasted_iota(jnp.int32, (tk, tk), 0)
              <= lax.broadcasted_iota(jnp.int32, (tk, tk), 1))
    one = jnp.ones((), BF16)
    zero = jnp.zeros((), BF16)
    heads = range(HEADS_PER_STEP)
    qats = []
    for hh in heads:
        q_h = jnp.where((row >= HEAD_DIM * hh) & (row < HEAD_DIM * (hh + 1)), qt, zero)
        k_side = (row >= BIAS_PIECES * hh) & (row < BIAS_PIECES * (hh + 1))
        q_lo = BIAS_Q_LANE + BIAS_PIECES * hh
        q_side = (row >= q_lo) & (row < q_lo + BIAS_PIECES)
        qats.append(jnp.concatenate(
            [q_h, jnp.where(k_side, one, jnp.where(q_side, qbt, zero))], axis=0))

    def scores(hh, j, c0=0):
        off = pl.multiple_of(j * tk, tk)
        kbias = jnp.where(q_cols, one, kb_ref[0, pl.ds(off, tk), lanes])
        ka = jnp.concatenate([k_ref[0, pl.ds(off, tk), lanes], kbias], axis=1)
        return jnp.dot(ka, qats[hh][:, c0:], preferred_element_type=F32)

    def consume(hh, carry, s, j, diagonal):
        if diagonal:
            s_tri = jnp.where(causal, s[:, :tk], -1e30)
            s = s_tri if s.shape[1] == tk else jnp.concatenate([s_tri, s[:, tk:]], axis=1)
        v0 = LANES * pair + HEAD_DIM * hh
        vt = vt_ref[0, j, v0:v0 + HEAD_DIM, :]
        if online:
            m, l, acc = carry
            m_new = jnp.maximum(m, jnp.max(s, axis=0, keepdims=True))
            alpha = jnp.exp2(m - m_new)
            p = jnp.exp2(s - m_new)
            l = alpha * l + jnp.sum(p, axis=0, keepdims=True)
            acc = alpha * acc + jnp.dot(vt, p.astype(BF16), preferred_element_type=F32)
            return m_new, l, acc
        l, acc = carry
        p = jnp.exp2(s)
        l = l + jnp.sum(p, axis=0, keepdims=True)
        acc = acc + jnp.dot(vt, p.astype(BF16), preferred_element_type=F32)
        return l, acc

    init = (jnp.zeros((1, tq), F32), jnp.zeros((HEAD_DIM, tq), F32))
    if online:
        init = (jnp.full((1, tq), -1e30, F32),) + init

    j0 = n_sub * i

    def first_scores(hh):
        if online:
            jj0 = 0
        else:
            head = HEADS_PER_STEP * (PAIRS_PER_STEP * pl.program_id(1) + pair) + hh
            jj0 = first_ref[(pl.program_id(0) * N_HEADS + head) * pl.num_programs(2) + i] // 2
        s0_scr[hh] = scores(hh, 2 * jj0)
        return jj0

    def off_diagonal(hh, jj0):
        def pairs(jj, c, n_pairs):
            for r in range(n_pairs):
                s1_scr[hh] = scores(hh, 2 * (jj + r) + 1)
                c = consume(hh, c, s0_scr[hh], 2 * (jj + r), False)
                s0_scr[hh] = scores(hh, 2 * (jj + r) + 2)
                c = consume(hh, c, s1_scr[hh], 2 * (jj + r) + 1, False)
            return c

        n_long = (j0 // 2 - jj0) // 2
        c = lax.fori_loop(0, n_long, lambda t, c: pairs(jj0 + 2 * t, c, 2), init)
        return lax.fori_loop(jj0 + 2 * n_long, j0 // 2, lambda jj, c: pairs(jj, c, 1), c)

    def diagonal(hh, carry):
        later = [scores(hh, j0 + dd, tk * dd) for dd in range(1, n_sub)]
        carry = consume(hh, carry, s0_scr[hh], j0, True)
        for dd in range(1, n_sub):
            c0 = tk * dd
            part = consume(hh, tuple(c[:, c0:] for c in carry), later[dd - 1], j0 + dd, True)
            carry = tuple(jnp.concatenate([c[:, :c0], pc], axis=1) for c, pc in zip(carry, part))
        return carry[-1] / carry[-2]

    outs = []
    jj0 = first_scores(0)
    for hh in heads:
        carry = off_diagonal(hh, jj0)
        if hh + 1 < HEADS_PER_STEP:
            jj0 = first_scores(hh + 1)
        outs.append(diagonal(hh, carry))
    o_t = jnp.concatenate(outs, axis=0)
    o_ref[0, :, lanes] = o_t.T.astype(BF16)


def _attention(first, qt, qbt, k, kb, vt, *, tq, tk):
    b, s, d = k.shape
    assert (tq // tk) % 2 == 0 and s % tq == 0
    wide = LANES * PAIRS_PER_STEP
    nk = s // tk
    qt_blk = pl.BlockSpec((1, wide, tq), lambda bi, hp, i, first: (bi, hp, i))
    k_all = pl.BlockSpec((1, s, wide), lambda bi, hp, i, first: (bi, 0, hp))
    return pl.pallas_call(
        functools.partial(_attn_kernel, tq=tq, tk=tk),
        grid_spec=pltpu.PrefetchScalarGridSpec(
            num_scalar_prefetch=1,
            grid=(b, d // wide, s // tq),
            in_specs=[qt_blk, qt_blk, k_all, k_all,
                      pl.BlockSpec((1, nk, wide, tk), lambda bi, hp, i, first: (bi, 0, hp, 0))],
            out_specs=pl.BlockSpec((1, tq, wide), lambda bi, hp, i, first: (bi, i, hp)),
            scratch_shapes=[pltpu.VMEM((HEADS_PER_STEP, tk, tq), F32),
                            pltpu.VMEM((HEADS_PER_STEP, tk, tq), F32)]),
        out_shape=jax.ShapeDtypeStruct((b, s, d), BF16),
        compiler_params=pltpu.CompilerParams(
            dimension_semantics=("arbitrary", "arbitrary", "arbitrary"),
            vmem_limit_bytes=VMEM_LIMIT),
        name="fox_attention",
    )(first, qt, qbt, k, kb, vt)


def _permute_token_blocks(perm, a):
    w = perm.shape[0]
    return jnp.concatenate(
        [jnp.dot(perm, a[w * i:w * (i + 1)], preferred_element_type=F32).astype(BF16)
         for i in range(a.shape[0] // w)], axis=0)


def _mix_mlp_kernel(x_ref, a_ref, wmix_ref, g_ref, w1_ref, w2_ref, *rest, glu, d, ff_chunk):
    if glu:
        a = _permute_token_blocks(rest[0][...], a_ref[...])
        mix = jnp.dot(a, wmix_ref[...], preferred_element_type=F32)
        mix = mix[:, :d] * jax.nn.sigmoid(mix[:, d:])
    else:
        mix = jnp.dot(a_ref[...], wmix_ref[...], preferred_element_type=F32)
    x1 = x_ref[...] + mix
    h = _rms_norm(x1, g_ref[...]).astype(BF16)
    acc = x1
    for c in range(w1_ref.shape[1] // ff_chunk):
        sl = slice(ff_chunk * c, ff_chunk * (c + 1))
        hid = jnp.maximum(jnp.dot(h, w1_ref[:, sl], preferred_element_type=F32), 0.0)
        acc = acc + jnp.dot((hid * hid).astype(BF16), w2_ref[sl, :], preferred_element_type=F32)
    if glu:
        _, o_ref = rest
    else:
        perm_ref, g_next_ref, w_next_ref, o_ref, u_ref = rest
        h_next = _rms_norm(acc, g_next_ref[...]).astype(BF16)
        u = jnp.dot(h_next, w_next_ref[...], preferred_element_type=F32).astype(BF16)
        u_ref[...] = _permute_token_blocks(perm_ref[...], u)
    o_ref[...] = acc


def _mix_mlp(x, a, wmix, g, w1, w2, perm, next_proj=None, *, glu, tm, ff_chunk=1024):
    t, d = x.shape
    row = lambda i: (i, 0)
    single = pl.Buffered(1)
    wspec = lambda shape: pl.BlockSpec(shape, lambda i: (0, 0), pipeline_mode=single)
    operands = [x, a, wmix, g, w1, w2, perm]
    in_specs = [pl.BlockSpec((tm, d), row), pl.BlockSpec((tm, d), row),
                wspec(wmix.shape), wspec((1, d)), wspec(w1.shape), wspec(w2.shape),
                wspec(perm.shape)]
    out_specs = pl.BlockSpec((tm, d), row)
    out_shape = jax.ShapeDtypeStruct((t, d), F32)
    if next_proj is not None:
        operands += list(next_proj)
        in_specs += [wspec(w.shape) for w in next_proj]
        n = next_proj[1].shape[1]
        out_specs = [out_specs, pl.BlockSpec((tm, n), row)]
        out_shape = [out_shape, jax.ShapeDtypeStruct((t, n), BF16)]
    return pl.pallas_call(
        functools.partial(_mix_mlp_kernel, glu=glu, d=d, ff_chunk=ff_chunk),
        grid=(t // tm,),
        in_specs=in_specs,
        out_specs=out_specs,
        out_shape=out_shape,
        compiler_params=pltpu.CompilerParams(
            dimension_semantics=("arbitrary",), vmem_limit_bytes=VMEM_LIMIT),
        name="mix_glu_mlp" if glu else "mix_mlp",
    )(*operands)


GROUPS_PER_SLAB = LANES // SSM_GROUP


def _piece_transpose(arrs, piece):
    arrs = list(arrs)
    dist = GROUPS_PER_SLAB // 2
    while dist >= 1:
        keep = (piece & dist) == 0
        shift = SSM_GROUP * dist
        for i in range(GROUPS_PER_SLAB):
            if i & dist:
                continue
            a, b = arrs[i], arrs[i + dist]
            arrs[i] = jnp.where(keep, a, pltpu.roll(b, shift, axis=1))
            arrs[i + dist] = jnp.where(keep, pltpu.roll(a, LANES - shift, axis=1), b)
        dist //= 2
    return arrs


def _ssm_kernel(u_ref, cc_ref, bt_ref, pw1_ref, pw2_ref, zoh_ref, a1_ref, a2_ref,
                dv_ref, z_ref, ug_scr, zg_scr, *, n_chunks):
    L = SSM_CHUNK
    rows = u_ref.shape[0] // L
    w = L * SSM_GROUP
    half = SSM_STATE
    nb = math.gcd(rows // L, 8)
    piece = lax.broadcasted_iota(jnp.int32, (nb * L, LANES), 1) // SSM_GROUP

    def relayout_in(rb, carry):
        t0 = pl.multiple_of(rb * nb * w, nb * w)
        r0 = pl.multiple_of(rb * nb * L, nb * L)
        by_pos = u_ref[pl.ds(t0, nb * w), :].astype(F32)
        for hf in range(L // GROUPS_PER_SLAB):
            arrs = []
            for k in range(GROUPS_PER_SLAB):
                pos = GROUPS_PER_SLAB * hf + k
                arrs.append(jnp.concatenate(
                    [by_pos[w * blk + L * pos:w * blk + L * (pos + 1)] for blk in range(nb)],
                    axis=0))
            arrs = _piece_transpose(arrs, piece)
            for g in range(GROUPS_PER_SLAB):
                ug_scr[g, pl.ds(r0, nb * L), LANES * hf:LANES * (hf + 1)] = arrs[g].astype(BF16)
        return carry

    lax.fori_loop(0, rows // (nb * L), relayout_in, 0)

    chunk = lax.broadcasted_iota(jnp.int32, (rows, LANES), 0) % n_chunks
    lane_w = lax.broadcasted_iota(jnp.int32, (SSM_GROUP, w), 1)
    low = lax.broadcasted_iota(jnp.int32, (1, LANES), 1) < half
    sign = jnp.where(low, 1.0, -1.0)

    def operators(g):
        cc = cc_ref[g]
        bt = bt_ref[g]
        pw1 = pw1_ref[g]
        pw2 = pw2_ref[g]
        zoh = zoh_ref[g]
        bbar = bt * zoh[0:1] + pltpu.roll(bt, half, axis=1) * zoh[1:2]
        bbar_sw = pltpu.roll(bbar, half, axis=1)
        cc_sw = pltpu.roll(cc, half, axis=1)
        ca = [cc * pw1[k:k + 1] + cc_sw * pw2[k:k + 1] for k in range(L + 1)]
        k_all = lax.dot_general(bbar * sign, jnp.concatenate(ca[:L], axis=0), NT_DIMS,
                                precision=lax.Precision.HIGHEST, preferred_element_type=F32)
        m_intra, w_state, w_out_t = [], [], []
        for s in range(L):
            blk = k_all if s == 0 else jnp.where(
                lane_w >= SSM_GROUP * s, pltpu.roll(k_all, SSM_GROUP * s, axis=1), 0.0)
            m_intra.append(blk.astype(BF16))
            k = L - 1 - s
            w_state.append(bbar * pw1[k:k + 1] + bbar_sw * pw2[k:k + 1])
            w_out_t.append(ca[s + 1] * sign)
        return jnp.concatenate(m_intra, axis=0), w_state, w_out_t

    def halves(pieces, second):
        firsts, seconds = [], []
        for piece in pieces:
            swapped = pltpu.roll(piece, half, axis=1)
            if second:
                firsts.append(jnp.where(low, 0.0, swapped))
                seconds.append(jnp.where(low, 0.0, piece))
            else:
                firsts.append(jnp.where(low, piece, 0.0))
                seconds.append(jnp.where(low, swapped, 0.0))
        return firsts, seconds

    def group_pair(pr, carry):
        gs = (2 * pr, 2 * pr + 1)
        m_intra, ws_re, ws_im, wo_t = [], [], [], []
        for second, g in enumerate(gs):
            mi, w_state, w_out_t = operators(g)
            m_intra.append(mi)
            re, im = halves(w_state, second)
            ws_re += re
            ws_im += im
            re, im = halves(w_out_t, second)
            wo_t.append(jnp.concatenate(
                [jnp.concatenate([r, m], axis=1) for r, m in zip(re, im)], axis=0).astype(BF16))
        us = [ug_scr[g] for g in gs]
        uu = jnp.concatenate(us, axis=1)
        xr = jnp.dot(uu, jnp.concatenate(ws_re, axis=0).astype(BF16), preferred_element_type=F32)
        xi = jnp.dot(uu, jnp.concatenate(ws_im, axis=0).astype(BF16), preferred_element_type=F32)
        a1 = [a1_ref[g] for g in gs]
        a2 = [a2_ref[g] for g in gs]
        step = 1
        j = 0
        while step < n_chunks:
            ar = jnp.where(low, a1[0][j:j + 1], a1[1][j:j + 1])
            ai = jnp.where(low, -a2[0][j:j + 1], a2[1][j:j + 1])
            sr = jnp.where(chunk >= step, pltpu.roll(xr, step, axis=0), 0.0)
            si = jnp.where(chunk >= step, pltpu.roll(xi, step, axis=0), 0.0)
            xr = xr + sr * ar - si * ai
            xi = xi + si * ar + sr * ai
            step *= 2
            j += 1
        x_in = jnp.concatenate([jnp.where(chunk >= 1, pltpu.roll(x, 1, axis=0), 0.0)
                                for x in (xr, xi)], axis=1).astype(BF16)
        for second, g in enumerate(gs):
            y = jnp.dot(us[second], m_intra[second], preferred_element_type=F32)
            y = y + lax.dot_general(x_in, wo_t[second], NT_DIMS, preferred_element_type=F32)
            y = y + us[second].astype(F32) * dv_ref[g]
            zg_scr[g] = jax.nn.gelu(y).astype(BF16)
        return carry

    lax.fori_loop(0, GROUPS_PER_SLAB // 2, group_pair, 0)

    def relayout_out(rb, carry):
        t0 = pl.multiple_of(rb * nb * w, nb * w)
        r0 = pl.multiple_of(rb * nb * L, nb * L)
        by_pos = []
        for hf in range(L // GROUPS_PER_SLAB):
            arrs = [zg_scr[g, pl.ds(r0, nb * L), LANES * hf:LANES * (hf + 1)].astype(F32)
                    for g in range(GROUPS_PER_SLAB)]
            by_pos += _piece_transpose(arrs, piece)
        for blk in range(nb):
            z_ref[pl.ds(t0 + w * blk, w), :] = jnp.concatenate(
                [arr[L * blk:L * (blk + 1)] for arr in by_pos], axis=0).astype(BF16)
        return carry

    lax.fori_loop(0, rows // (nb * L), relayout_out, 0)


def _ssm_scan(u, cc, bt, pw1, pw2, zoh, a1, a2, dv, *, n_chunks):
    t, d = u.shape
    L = SSM_CHUNK
    rows = t // L
    w = L * SSM_GROUP
    gps = GROUPS_PER_SLAB
    slab = lambda j: (j, 0, 0)
    pspec = lambda arr: pl.BlockSpec((gps,) + arr.shape[1:], slab)
    return pl.pallas_call(
        functools.partial(_ssm_kernel, n_chunks=n_chunks),
        grid=(d // LANES,),
        in_specs=[pl.BlockSpec((t, LANES), lambda j: (0, j)),
                  pspec(cc), pspec(bt), pspec(pw1), pspec(pw2), pspec(zoh), pspec(a1), pspec(a2),
                  pspec(dv)],
        out_specs=pl.BlockSpec((t, LANES), lambda j: (0, j)),
        out_shape=jax.ShapeDtypeStruct((t, d), BF16),
        scratch_shapes=[pltpu.VMEM((gps, rows, w), BF16), pltpu.VMEM((gps, rows, w), BF16)],
        compiler_params=pltpu.CompilerParams(
            dimension_semantics=("arbitrary",), vmem_limit_bytes=VMEM_LIMIT),
        name="s5_scan",
    )(u, cc, bt, pw1, pw2, zoh, a1, a2, dv)


def _ssm_operators(a_re, a_im, b_re, b_im, c_re, c_im, log_dt, d_skip, n_chunks):
    L = SSM_CHUNK
    g, p = a_re.shape
    dt = jnp.exp(log_dt)[:, None]
    lam_re, lam_im = dt * a_re, dt * a_im

    def powers(ks):
        ks = jnp.asarray(ks, F32)[:, None, None]
        mag = jnp.exp(ks * lam_re)
        return mag * jnp.cos(ks * lam_im), mag * jnp.sin(ks * lam_im)

    def patterns(re, im):
        return (jnp.concatenate([re, re], axis=2).transpose(1, 0, 2),
                jnp.concatenate([-im, im], axis=2).transpose(1, 0, 2))

    pr, pi = powers(range(L + 1))
    num_re, num_im = pr[1] - 1.0, pi[1]
    den = a_re * a_re + a_im * a_im
    s_re = (num_re * a_re + num_im * a_im) / den
    s_im = (num_im * a_re - num_re * a_im) / den
    pw1, pw2 = patterns(pr, pi)
    zoh = jnp.concatenate(patterns(s_re[None], s_im[None]), axis=1)
    n_steps = max(1, int(math.log2(n_chunks)))
    a1, a2 = patterns(*powers([L * 2 ** j for j in range(n_steps)]))
    cc = jnp.concatenate([c_re, c_im], axis=2)
    bt = jnp.concatenate([b_re.transpose(0, 2, 1), b_im.transpose(0, 2, 1)], axis=2)
    dv = jnp.tile(d_skip.reshape(g, 1, SSM_GROUP), (1, L, 1)).reshape(g, 1, L * SSM_GROUP)
    return cc, bt, pw1, pw2, zoh, a1, a2, dv


def kernel(x, norm_mix_g, norm_mlp_g, attn_w_in, attn_b_f, attn_q_g, attn_k_g, attn_w_out,
           ssm_w_in, ssm_a_re, ssm_a_im, ssm_b_re, ssm_b_im, ssm_c_re, ssm_c_im, ssm_log_dt,
           ssm_d, ssm_w_glu, mlp_w1, mlp_w2):
    b, s, d = x.shape
    t = b * s
    tk = min(512, s // 2)
    tq = 2 * tk
    tm_proj = min(512, s)
    tm_mlp = min(512, t)

    w_in = attn_w_in[0]
    w_f = jnp.pad(w_in[:, 3 * d:], ((0, 0), (0, LANES - N_HEADS)))
    wkf = jnp.concatenate([w_in[:, d:2 * d], w_f], axis=1).astype(BF16)
    wqvt = jnp.concatenate([w_in[:, :d], w_in[:, 2 * d:3 * d]], axis=1).T.astype(BF16)
    blk = np.arange(MXU_DIM) // HEAD_DIM
    bd = jnp.asarray((blk[:, None] == blk[None, :]) * (1.0 / HEAD_DIM), BF16)
    gain_k = jnp.tile(attn_k_g[0], N_HEADS).reshape(1, d)
    gain_qt = jnp.broadcast_to(
        (jnp.tile(attn_q_g[0], N_HEADS) * (LOG2E / math.sqrt(HEAD_DIM)))[:, None], (d, LANES))
    bfp = jnp.pad(attn_b_f[0], (0, LANES - N_HEADS)).reshape(1, LANES)
    tri = jnp.asarray(np.arange(tm_proj)[:, None] >= np.arange(tm_proj)[None, :], BF16)
    heads = np.arange(N_HEADS)
    place_k = np.zeros((LANES, d), np.float32)
    for i in range(BIAS_PIECES):
        place_k[N_HEADS * i + heads, (heads // HEADS_PER_STEP) * LANES
                + BIAS_PIECES * (heads % HEADS_PER_STEP) + i] = 1.0
    place_qt = np.roll(place_k, BIAS_Q_LANE, axis=1).T
    shift = (LOG2E * math.sqrt(HEAD_DIM) * jnp.max(jnp.abs(attn_q_g[0]))
             * jnp.max(jnp.abs(attn_k_g[0])))
    qt, k, vt, kb, qbt, edge = _qkv_proj(
        x, norm_mix_g[0].reshape(1, d), wkf, wqvt, bd, gain_k, gain_qt, bfp, tri,
        jnp.asarray(place_k, BF16), jnp.asarray(place_qt, BF16), jnp.full((1, LANES), shift, F32),
        tm=tm_proj, tk=tk)
    assert tm_proj == tk
    c_first = edge[:, ::tq // tk, 0, :N_HEADS]
    c_last = edge[:, :, 1, :N_HEADS]
    bound = (c_first[:, :, None, :] - c_last[:, None, :, :]) * LOG2E
    key_blk = jnp.arange(s // tk)[None, None, :, None]
    first = jnp.min(jnp.where(bound < SKIP_LOGIT, s // tk, key_blk), axis=2)
    first = jnp.minimum(first, (tq // tk) * jnp.arange(s // tq)[None, :, None]) // 2 * 2
    first = jnp.concatenate([first.transpose(0, 2, 1).reshape(-1), (shift > MAX_SHIFT)[None]])
    o = _attention(first.astype(jnp.int32), qt, qbt, k, kb, vt, tq=tq, tk=tk)
    tok = np.arange(SSM_CHUNK * SSM_CHUNK)
    perm = jnp.asarray(
        tok[None, :] == (tok[:, None] % SSM_CHUNK) * SSM_CHUNK + tok[:, None] // SSM_CHUNK, BF16)
    x2, u = _mix_mlp(x.reshape(t, d), o.reshape(t, d), attn_w_out[0].astype(BF16),
                     norm_mlp_g[0].reshape(1, d), mlp_w1[0].astype(BF16), mlp_w2[0].astype(BF16),
                     perm, (norm_mix_g[1].reshape(1, d), ssm_w_in[0].astype(BF16)),
                     glu=False, tm=tm_mlp)

    n_chunks = s // SSM_CHUNK
    ops = _ssm_operators(ssm_a_re[0], ssm_a_im[0], ssm_b_re[0], ssm_b_im[0], ssm_c_re[0],
                         ssm_c_im[0], ssm_log_dt[0], ssm_d[0], n_chunks)
    z = _ssm_scan(u, *ops, n_chunks=n_chunks)
    x3 = _mix_mlp(x2, z, ssm_w_glu[0].astype(BF16), norm_mlp_g[1].reshape(1, d),
                  mlp_w1[1].astype(BF16), mlp_w2[1].astype(BF16), perm, glu=True, tm=tm_mlp)
    return x3.reshape(b, s, d)
```

```python
import functools
import math

import jax
import jax.numpy as jnp
import numpy as np
from jax import lax
from jax.experimental import pallas as pl
from jax.experimental.pallas import tpu as pltpu

F32 = jnp.float32
BF16 = jnp.bfloat16

N_HEADS = 16
HEAD_DIM = 64
SSM_GROUP = 16
SSM_STATE = 64
SSM_CHUNK = 16
EPS = 1e-6
LOG2E = 1.4426950408889634

LANES = 128
MXU_DIM = 256
HEADS_PER_STEP = LANES // HEAD_DIM
BIAS_PIECES = 3
VMEM_LIMIT = 56 * 1024 * 1024

NT_DIMS = (((1,), (1,)), ((), ()))


def _rms_norm(x, g):
    ms = jnp.mean(x * x, axis=-1, keepdims=True)
    return x * lax.rsqrt(ms + EPS) * g


def _const_spec(shape):
    zeros = (0,) * len(shape)
    return pl.BlockSpec(shape, lambda *_: zeros)


def _bf16_pieces(val):
    pieces = jnp.zeros_like(val)
    rem = val
    for n in range(BIAS_PIECES):
        piece = rem.astype(BF16).astype(F32)
        rem = rem - piece
        pieces = pieces + (piece if n == 0 else pltpu.roll(piece, N_HEADS * n, axis=1))
    return pieces


def _qkv_kernel(x_ref, g_ref, wkf_ref, wqvt_ref, bd_ref, gain_k_ref, gain_qt_ref, bf_ref, tri_ref,
                place_k_ref, place_qt_ref, shift_ref,
                qt_ref, k_ref, vt_ref, kb_ref, qbt_ref, edge_ref, carry_ref, *, tm, tk, d):
    @pl.when(pl.program_id(1) == 0)
    def _():
        carry_ref[...] = jnp.zeros_like(carry_ref)

    h = _rms_norm(x_ref[0], g_ref[...]).astype(BF16)

    y = jnp.dot(h, wkf_ref[...], preferred_element_type=F32)
    vt = lax.dot_general(wqvt_ref[d:, :], h, NT_DIMS, preferred_element_type=F32).astype(BF16)
    for j in range(tm // tk):
        vt_ref[0, j] = vt[:, tk * j:tk * (j + 1)]
    qt = lax.dot_general(wqvt_ref[:d, :], h, NT_DIMS, preferred_element_type=F32)

    for t in range(d // MXU_DIM):
        sl = slice(MXU_DIM * t, MXU_DIM * (t + 1))
        tile = y[:, sl]
        ms = jnp.dot((tile * tile).astype(BF16), bd_ref[...], preferred_element_type=F32)
        k_ref[0, :, sl] = (tile * lax.rsqrt(ms + EPS) * gain_k_ref[:, sl]).astype(BF16)
        tile = qt[sl, :]
        ms = jnp.dot(bd_ref[...], (tile * tile).astype(BF16), preferred_element_type=F32)
        gain = jnp.tile(gain_qt_ref[sl, :], (1, tm // LANES))
        qt_ref[0, sl, :] = (tile * lax.rsqrt(ms + EPS) * gain).astype(BF16)

    f = y[:, d:] + bf_ref[...]
    log_f = jnp.minimum(f, 0.0) - jnp.log1p(jnp.exp(-jnp.abs(f)))
    lane = lax.broadcasted_iota(jnp.int32, log_f.shape, 1)
    part = jnp.dot(tri_ref[...], _bf16_pieces(jnp.where(lane < N_HEADS, log_f, 0.0)).astype(BF16),
                   preferred_element_type=F32)
    cs = carry_ref[...] + part
    for n in range(1, BIAS_PIECES):
        cs = cs + pltpu.roll(part, LANES - N_HEADS * n, axis=1)
    carry_ref[...] = cs[tm - 1:tm, :]
    edge_ref[0, 0] = jnp.concatenate(
        [cs[0:1, :], cs[tm - 1:tm, :], jnp.zeros((6, LANES), F32)], axis=0)
    c2 = jnp.where(lane < N_HEADS, cs * LOG2E, 0.0)
    kb_ref[0] = jnp.dot(_bf16_pieces(-c2).astype(BF16), place_k_ref[...],
                        preferred_element_type=F32).astype(BF16)
    q_pieces = _bf16_pieces(jnp.where(lane < N_HEADS, c2 - shift_ref[...], 0.0))
    qbt_ref[0] = jnp.dot(place_qt_ref[...], q_pieces.T.astype(BF16),
                         preferred_element_type=F32).astype(BF16)


def _qkv_proj(x, g, wkf, wqvt, bd, gain_k, gain_qt, bfp, tri, place_k, place_qt, shift, *, tm, tk):
    b, s, d = x.shape
    row = lambda bi, si: (bi, si, 0)
    col = lambda bi, si: (bi, 0, si)
    consts = (g, wkf, wqvt, bd, gain_k, gain_qt, bfp, tri, place_k, place_qt, shift)
    return pl.pallas_call(
        functools.partial(_qkv_kernel, tm=tm, tk=tk, d=d),
        grid=(b, s // tm),
        in_specs=[pl.BlockSpec((1, tm, d), row)] + [_const_spec(c.shape) for c in consts],
        out_specs=[pl.BlockSpec((1, d, tm), col),
                   pl.BlockSpec((1, tm, d), row),
                   pl.BlockSpec((1, tm // tk, d, tk), lambda bi, si: (bi, si, 0, 0)),
                   pl.BlockSpec((1, tm, d), row),
                   pl.BlockSpec((1, d, tm), col),
                   pl.BlockSpec((1, 1, 8, LANES), lambda bi, si: (bi, si, 0, 0))],
        out_shape=[jax.ShapeDtypeStruct((b, d, s), BF16),
                   jax.ShapeDtypeStruct((b, s, d), BF16),
                   jax.ShapeDtypeStruct((b, s // tk, d, tk), BF16),
                   jax.ShapeDtypeStruct((b, s, d), BF16),
                   jax.ShapeDtypeStruct((b, d, s), BF16),
                   jax.ShapeDtypeStruct((b, s // tm, 8, LANES), F32)],
        scratch_shapes=[pltpu.VMEM((1, LANES), F32)],
        compiler_params=pltpu.CompilerParams(
            dimension_semantics=("arbitrary", "arbitrary"), vmem_limit_bytes=VMEM_LIMIT),
        name="qkv_proj",
    )(x, *consts)


BIAS_Q_LANE = 8
MAX_SHIFT = 48.0
SKIP_LOGIT = -200.0


def _attn_kernel(first_ref, *refs, tq, tk):
    use_online = first_ref[first_ref.shape[0] - 1]

    @pl.when(use_online == 0)
    def _():
        _attn_body(first_ref, *refs, tq=tq, tk=tk, online=False)

    @pl.when(use_online != 0)
    def _():
        _attn_body(first_ref, *refs, tq=tq, tk=tk, online=True)


def _attn_body(first_ref, qt_ref, qbt_ref, k_ref, kb_ref, vt_ref, o_ref, s0_scr, s1_scr, p0_scr,
               p1_scr, *, tq, tk, online):
    i = pl.program_id(2)
    n_sub = tq // tk
    qt = qt_ref[0]
    qbt = qbt_ref[0]
    row = lax.broadcasted_iota(jnp.int32, (LANES, tq), 0)
    lane_k = lax.broadcasted_iota(jnp.int32, (tk, LANES), 1)
    q_cols = (lane_k >= BIAS_Q_LANE) & (lane_k < BIAS_Q_LANE + HEADS_PER_STEP * BIAS_PIECES)
    causal = (lax.broadcasted_iota(jnp.int32, (tk, tk), 0)
              <= lax.broadcasted_iota(jnp.int32, (tk, tk), 1))
    one = jnp.ones((), BF16)
    zero = jnp.zeros((), BF16)
    heads = range(HEADS_PER_STEP)
    qats = []
    for hh in heads:
        q_h = jnp.where((row >= HEAD_DIM * hh) & (row < HEAD_DIM * (hh + 1)), qt, zero)
        k_side = (row >= BIAS_PIECES * hh) & (row < BIAS_PIECES * (hh + 1))
        q_lo = BIAS_Q_LANE + BIAS_PIECES * hh
        q_side = (row >= q_lo) & (row < q_lo + BIAS_PIECES)
        qats.append(jnp.concatenate(
            [q_h, jnp.where(k_side, one, jnp.where(q_side, qbt, zero))], axis=0))

    def scores(hh, j, c0=0):
        off = pl.multiple_of(j * tk, tk)
        kbias = jnp.where(q_cols, one, kb_ref[0, pl.ds(off, tk), :])
        ka = jnp.concatenate([k_ref[0, pl.ds(off, tk), :], kbias], axis=1)
        return jnp.dot(ka, qats[hh][:, c0:], preferred_element_type=F32)

    if not online:
        j0 = n_sub * i
        def probs(hh, l, j, c0=0, diagonal=None):
            s = scores(hh, j, c0)
            if diagonal is not False:
                head = jnp.where(causal, s[:, :tk], -1e30)
                if diagonal is None:
                    head = jnp.where(j < j0, s[:, :tk], head)
                s = head if s.shape[1] == tk else jnp.concatenate([head, s[:, tk:]], axis=1)
            p = jnp.exp2(s)
            return l + jnp.sum(p, axis=0, keepdims=True), p.astype(BF16)

        def weigh(hh, acc, p, j):
            vt = vt_ref[0, j, HEAD_DIM * hh:HEAD_DIM * (hh + 1), :]
            return acc + jnp.dot(vt, p, preferred_element_type=F32)

        def fill(hh):
            head = HEADS_PER_STEP * pl.program_id(1) + hh
            jj0 = first_ref[(pl.program_id(0) * N_HEADS + head) * pl.num_programs(2) + i] // 2
            l, p0_scr[hh] = probs(hh, jnp.zeros((1, tq), F32), 2 * jj0)
            return jj0, l

        def off_diagonal(hh, jj0, l):
            def pairs(jj, c, n_pairs):
                l, acc = c
                for r in range(n_pairs):
                    a = 2 * (jj + r)
                    l, p1_scr[hh] = probs(hh, l, a + 1, diagonal=False)
                    acc = weigh(hh, acc, p0_scr[hh], a)
                    l, p0_scr[hh] = probs(hh, l, a + 2)
                    acc = weigh(hh, acc, p1_scr[hh], a + 1)
                return l, acc

            n_long = (j0 // 2 - jj0) // 2
            c = (l, jnp.zeros((HEAD_DIM, tq), F32))
            c = lax.fori_loop(0, n_long, lambda t, c: pairs(jj0 + 2 * t, c, 2), c)
            return lax.fori_loop(jj0 + 2 * n_long, j0 // 2, lambda jj, c: pairs(jj, c, 1), c)

        def diagonal(hh, l, acc):
            later = [probs(hh, l[:, tk * dd:], j0 + dd, tk * dd, diagonal=True)
                     for dd in range(1, n_sub)]
            acc = weigh(hh, acc, p0_scr[hh], j0)
            for dd in range(1, n_sub):
                c0 = tk * dd
                l_part, p = later[dd - 1]
                l = jnp.concatenate([l[:, :c0], l_part], axis=1)
                acc = jnp.concatenate([acc[:, :c0], weigh(hh, acc[:, c0:], p, j0 + dd)], axis=1)
            return acc / l

        outs = []
        jj0, l = fill(0)
        for hh in heads:
            l, acc = off_diagonal(hh, jj0, l)
            if hh + 1 < HEADS_PER_STEP:
                jj0, l_next = fill(hh + 1)
            outs.append(diagonal(hh, l, acc))
            l = l_next if hh + 1 < HEADS_PER_STEP else None
        o_ref[0] = jnp.concatenate(outs, axis=0).T.astype(BF16)
        return

    def consume(hh, carry, s, j, diagonal):
        if diagonal:
            s_tri = jnp.where(causal, s[:, :tk], -1e30)
            s = s_tri if s.shape[1] == tk else jnp.concatenate([s_tri, s[:, tk:]], axis=1)
        vt = vt_ref[0, j, HEAD_DIM * hh:HEAD_DIM * (hh + 1), :]
        if online:
            m, l, acc = carry
            m_new = jnp.maximum(m, jnp.max(s, axis=0, keepdims=True))
            alpha = jnp.exp2(m - m_new)
            p = jnp.exp2(s - m_new)
            l = alpha * l + jnp.sum(p, axis=0, keepdims=True)
            acc = alpha * acc + jnp.dot(vt, p.astype(BF16), preferred_element_type=F32)
            return m_new, l, acc
        l, acc = carry
        p = jnp.exp2(s)
        l = l + jnp.sum(p, axis=0, keepdims=True)
        acc = acc + jnp.dot(vt, p.astype(BF16), preferred_element_type=F32)
        return l, acc

    init = (jnp.zeros((1, tq), F32), jnp.zeros((HEAD_DIM, tq), F32))
    if online:
        init = (jnp.full((1, tq), -1e30, F32),) + init

    j0 = n_sub * i

    def first_scores(hh):
        if online:
            jj0 = 0
        else:
            head = HEADS_PER_STEP * pl.program_id(1) + hh
            jj0 = first_ref[(pl.program_id(0) * N_HEADS + head) * pl.num_programs(2) + i] // 2
        s0_scr[hh] = scores(hh, 2 * jj0)
        return jj0

    def off_diagonal(hh, jj0):
        def pairs(jj, c, n_pairs):
            for r in range(n_pairs):
                s1_scr[hh] = scores(hh, 2 * (jj + r) + 1)
                c = consume(hh, c, s0_scr[hh], 2 * (jj + r), False)
                s0_scr[hh] = scores(hh, 2 * (jj + r) + 2)
                c = consume(hh, c, s1_scr[hh], 2 * (jj + r) + 1, False)
            return c

        n_long = (j0 // 2 - jj0) // 2
        c = lax.fori_loop(0, n_long, lambda t, c: pairs(jj0 + 2 * t, c, 2), init)
        return lax.fori_loop(jj0 + 2 * n_long, j0 // 2, lambda jj, c: pairs(jj, c, 1), c)

    def diagonal(hh, carry):
        later = [scores(hh, j0 + dd, tk * dd) for dd in range(1, n_sub)]
        carry = consume(hh, carry, s0_scr[hh], j0, True)
        for dd in range(1, n_sub):
            c0 = tk * dd
            part = consume(hh, tuple(c[:, c0:] for c in carry), later[dd - 1], j0 + dd, True)
            carry = tuple(jnp.concatenate([c[:, :c0], pc], axis=1) for c, pc in zip(carry, part))
        return carry[-1] / carry[-2]

    outs = []
    jj0 = first_scores(0)
    for hh in heads:
        carry = off_diagonal(hh, jj0)
        if hh + 1 < HEADS_PER_STEP:
            jj0 = first_scores(hh + 1)
        outs.append(diagonal(hh, carry))
    o_t = jnp.concatenate(outs, axis=0)
    o_ref[0] = o_t.T.astype(BF16)


def _attention(first, qt, qbt, k, kb, vt, *, tq, tk):
    b, s, d = k.shape
    assert (tq // tk) % 2 == 0 and s % tq == 0
    n_pairs = d // LANES
    nk = s // tk
    qt_blk = pl.BlockSpec((1, LANES, tq), lambda bi, hp, i, first: (bi, hp, i))
    k_all = pl.BlockSpec((1, s, LANES), lambda bi, hp, i, first: (bi, 0, hp))
    return pl.pallas_call(
        functools.partial(_attn_kernel, tq=tq, tk=tk),
        grid_spec=pltpu.PrefetchScalarGridSpec(
            num_scalar_prefetch=1,
            grid=(b, n_pairs, s // tq),
            in_specs=[qt_blk, qt_blk, k_all, k_all,
                      pl.BlockSpec((1, nk, LANES, tk), lambda bi, hp, i, first: (bi, 0, hp, 0))],
            out_specs=pl.BlockSpec((1, tq, LANES), lambda bi, hp, i, first: (bi, i, hp)),
            scratch_shapes=[pltpu.VMEM((HEADS_PER_STEP, tk, tq), F32),
                            pltpu.VMEM((HEADS_PER_STEP, tk, tq), F32),
                            pltpu.VMEM((HEADS_PER_STEP, tk, tq), BF16),
                            pltpu.VMEM((HEADS_PER_STEP, tk, tq), BF16)]),
        out_shape=jax.ShapeDtypeStruct((b, s, d), BF16),
        compiler_params=pltpu.CompilerParams(
            dimension_semantics=("arbitrary", "arbitrary", "arbitrary"),
            vmem_limit_bytes=VMEM_LIMIT),
        name="fox_attention",
    )(first, qt, qbt, k, kb, vt)


def _permute_token_blocks(perm, a):
    w = perm.shape[0]
    return jnp.concatenate(
        [jnp.dot(perm, a[w * i:w * (i + 1)], preferred_element_type=F32).astype(BF16)
         for i in range(a.shape[0] // w)], axis=0)


def _mix_mlp_kernel(x_ref, a_ref, wmix_ref, g_ref, w1_ref, w2_ref, *rest, glu, d, ff_chunk):
    if glu:
        a = _permute_token_blocks(rest[0][...], a_ref[...])
        mix = jnp.dot(a, wmix_ref[...], preferred_element_type=F32)
        mix = mix[:, :d] * jax.nn.sigmoid(mix[:, d:])
    else:
        mix = jnp.dot(a_ref[...], wmix_ref[...], preferred_element_type=F32)
    x1 = x_ref[...] + mix
    h = _rms_norm(x1, g_ref[...]).astype(BF16)
    acc = x1
    for c in range(w1_ref.shape[1] // ff_chunk):
        sl = slice(ff_chunk * c, ff_chunk * (c + 1))
        hid = jnp.maximum(jnp.dot(h, w1_ref[:, sl], preferred_element_type=F32), 0.0)
        acc = acc + jnp.dot((hid * hid).astype(BF16), w2_ref[sl, :], preferred_element_type=F32)
    if glu:
        _, o_ref = rest
    else:
        perm_ref, g_next_ref, w_next_ref, o_ref, u_ref = rest
        h_next = _rms_norm(acc, g_next_ref[...]).astype(BF16)
        u = jnp.dot(h_next, w_next_ref[...], preferred_element_type=F32).astype(BF16)
        u_ref[...] = _permute_token_blocks(perm_ref[...], u)
    o_ref[...] = acc


def _mix_mlp(x, a, wmix, g, w1, w2, perm, next_proj=None, *, glu, tm, ff_chunk=1024):
    t, d = x.shape
    row = lambda i: (i, 0)
    single = pl.Buffered(1)
    wspec = lambda shape: pl.BlockSpec(shape, lambda i: (0, 0), pipeline_mode=single)
    operands = [x, a, wmix, g, w1, w2, perm]
    in_specs = [pl.BlockSpec((tm, d), row), pl.BlockSpec((tm, d), row),
                wspec(wmix.shape), wspec((1, d)), wspec(w1.shape), wspec(w2.shape),
                wspec(perm.shape)]
    out_specs = pl.BlockSpec((tm, d), row)
    out_shape = jax.ShapeDtypeStruct((t, d), F32)
    if next_proj is not None:
        operands += list(next_proj)
        in_specs += [wspec(w.shape) for w in next_proj]
        n = next_proj[1].shape[1]
        out_specs = [out_specs, pl.BlockSpec((tm, n), row)]
        out_shape = [out_shape, jax.ShapeDtypeStruct((t, n), BF16)]
    return pl.pallas_call(
        functools.partial(_mix_mlp_kernel, glu=glu, d=d, ff_chunk=ff_chunk),
        grid=(t // tm,),
        in_specs=in_specs,
        out_specs=out_specs,
        out_shape=out_shape,
        compiler_params=pltpu.CompilerParams(
            dimension_semantics=("arbitrary",), vmem_limit_bytes=VMEM_LIMIT),
        name="mix_glu_mlp" if glu else "mix_mlp",
    )(*operands)


GROUPS_PER_SLAB = LANES // SSM_GROUP


def _piece_transpose(arrs, piece):
    arrs = list(arrs)
    dist = GROUPS_PER_SLAB // 2
    while dist >= 1:
        keep = (piece & dist) == 0
        shift = SSM_GROUP * dist
        for i in range(GROUPS_PER_SLAB):
            if i & dist:
                continue
            a, b = arrs[i], arrs[i + dist]
            arrs[i] = jnp.where(keep, a, pltpu.roll(b, shift, axis=1))
            arrs[i + dist] = jnp.where(keep, pltpu.roll(a, LANES - shift, axis=1), b)
        dist //= 2
    return arrs


def _ssm_kernel(u_ref, cc_ref, bt_ref, pw1_ref, pw2_ref, zoh_ref, a1_ref, a2_ref,
                dv_ref, z_ref, ug_scr, zg_scr, *, n_chunks):
    L = SSM_CHUNK
    rows = u_ref.shape[0] // L
    w = L * SSM_GROUP
    half = SSM_STATE
    nb = math.gcd(rows // L, 8)
    piece = lax.broadcasted_iota(jnp.int32, (nb * L, LANES), 1) // SSM_GROUP

    def relayout_in(rb, carry):
        t0 = pl.multiple_of(rb * nb * w, nb * w)
        r0 = pl.multiple_of(rb * nb * L, nb * L)
        by_pos = u_ref[pl.ds(t0, nb * w), :].astype(F32)
        for hf in range(L // GROUPS_PER_SLAB):
            arrs = []
            for k in range(GROUPS_PER_SLAB):
                pos = GROUPS_PER_SLAB * hf + k
                arrs.append(jnp.concatenate(
                    [by_pos[w * blk + L * pos:w * blk + L * (pos + 1)] for blk in range(nb)],
                    axis=0))
            arrs = _piece_transpose(arrs, piece)
            for g in range(GROUPS_PER_SLAB):
                ug_scr[g, pl.ds(r0, nb * L), LANES * hf:LANES * (hf + 1)] = arrs[g].astype(BF16)
        return carry

    lax.fori_loop(0, rows // (nb * L), relayout_in, 0)

    chunk = lax.broadcasted_iota(jnp.int32, (rows, LANES), 0) % n_chunks
    lane_w = lax.broadcasted_iota(jnp.int32, (SSM_GROUP, w), 1)
    low = lax.broadcasted_iota(jnp.int32, (1, LANES), 1) < half
    sign = jnp.where(low, 1.0, -1.0)

    def operators(g):
        cc = cc_ref[g]
        bt = bt_ref[g]
        pw1 = pw1_ref[g]
        pw2 = pw2_ref[g]
        zoh = zoh_ref[g]
        bbar = bt * zoh[0:1] + pltpu.roll(bt, half, axis=1) * zoh[1:2]
        bbar_sw = pltpu.roll(bbar, half, axis=1)
        cc_sw = pltpu.roll(cc, half, axis=1)
        ca = [cc * pw1[k:k + 1] + cc_sw * pw2[k:k + 1] for k in range(L + 1)]
        k_all = lax.dot_general(bbar * sign, jnp.concatenate(ca[:L], axis=0), NT_DIMS,
                                precision=lax.Precision.HIGHEST, preferred_element_type=F32)
        m_intra, w_state, w_out_t = [], [], []
        for s in range(L):
            blk = k_all if s == 0 else jnp.where(
                lane_w >= SSM_GROUP * s, pltpu.roll(k_all, SSM_GROUP * s, axis=1), 0.0)
            m_intra.append(blk.astype(BF16))
            k = L - 1 - s
            w_state.append(bbar * pw1[k:k + 1] + bbar_sw * pw2[k:k + 1])
            w_out_t.append(ca[s + 1] * sign)
        return jnp.concatenate(m_intra, axis=0), w_state, w_out_t

    def halves(pieces, second):
        firsts, seconds = [], []
        for piece in pieces:
            swapped = pltpu.roll(piece, half, axis=1)
            if second:
                firsts.append(jnp.where(low, 0.0, swapped))
                seconds.append(jnp.where(low, 0.0, piece))
            else:
                firsts.append(jnp.where(low, piece, 0.0))
                seconds.append(jnp.where(low, swapped, 0.0))
        return firsts, seconds

    def group_pair(pr, carry):
        gs = (2 * pr, 2 * pr + 1)
        m_intra, ws_re, ws_im, wo_t = [], [], [], []
        for second, g in enumerate(gs):
            mi, w_state, w_out_t = operators(g)
            m_intra.append(mi)
            re, im = halves(w_state, second)
            ws_re += re
            ws_im += im
            re, im = halves(w_out_t, second)
            wo_t.append(jnp.concatenate(
                [jnp.concatenate([r, m], axis=1) for r, m in zip(re, im)], axis=0).astype(BF16))
        us = [ug_scr[g] for g in gs]
        uu = jnp.concatenate(us, axis=1)
        xr = jnp.dot(uu, jnp.concatenate(ws_re, axis=0).astype(BF16), preferred_element_type=F32)
        xi = jnp.dot(uu, jnp.concatenate(ws_im, axis=0).astype(BF16), preferred_element_type=F32)
        a1 = [a1_ref[g] for g in gs]
        a2 = [a2_ref[g] for g in gs]
        step = 1
        j = 0
        while step < n_chunks:
            ar = jnp.where(low, a1[0][j:j + 1], a1[1][j:j + 1])
            ai = jnp.where(low, -a2[0][j:j + 1], a2[1][j:j + 1])
            sr = jnp.where(chunk >= step, pltpu.roll(xr, step, axis=0), 0.0)
            si = jnp.where(chunk >= step, pltpu.roll(xi, step, axis=0), 0.0)
            xr = xr + sr * ar - si * ai
            xi = xi + si * ar + sr * ai
            step *= 2
            j += 1
        x_in = jnp.concatenate([jnp.where(chunk >= 1, pltpu.roll(x, 1, axis=0), 0.0)
                                for x in (xr, xi)], axis=1).astype(BF16)
        for second, g in enumerate(gs):
            y = jnp.dot(us[second], m_intra[second], preferred_element_type=F32)
            y = y + lax.dot_general(x_in, wo_t[second], NT_DIMS, preferred_element_type=F32)
            y = y + us[second].astype(F32) * dv_ref[g]
            zg_scr[g] = jax.nn.gelu(y).astype(BF16)
        return carry

    lax.fori_loop(0, GROUPS_PER_SLAB // 2, group_pair, 0)

    def relayout_out(rb, carry):
        t0 = pl.multiple_of(rb * nb * w, nb * w)
        r0 = pl.multiple_of(rb * nb * L, nb * L)
        by_pos = []
        for hf in range(L // GROUPS_PER_SLAB):
            arrs = [zg_scr[g, pl.ds(r0, nb * L), LANES * hf:LANES * (hf + 1)].astype(F32)
                    for g in range(GROUPS_PER_SLAB)]
            by_pos += _piece_transpose(arrs, piece)
        for blk in range(nb):
            z_ref[pl.ds(t0 + w * blk, w), :] = jnp.concatenate(
                [arr[L * blk:L * (blk + 1)] for arr in by_pos], axis=0).astype(BF16)
        return carry

    lax.fori_loop(0, rows // (nb * L), relayout_out, 0)


def _ssm_scan(u, cc, bt, pw1, pw2, zoh, a1, a2, dv, *, n_chunks):
    t, d = u.shape
    L = SSM_CHUNK
    rows = t // L
    w = L * SSM_GROUP
    gps = GROUPS_PER_SLAB
    slab = lambda j: (j, 0, 0)
    pspec = lambda arr: pl.BlockSpec((gps,) + arr.shape[1:], slab)
    return pl.pallas_call(
        functools.partial(_ssm_kernel, n_chunks=n_chunks),
        grid=(d // LANES,),
        in_specs=[pl.BlockSpec((t, LANES), lambda j: (0, j)),
                  pspec(cc), pspec(bt), pspec(pw1), pspec(pw2), pspec(zoh), pspec(a1), pspec(a2),
                  pspec(dv)],
        out_specs=pl.BlockSpec((t, LANES), lambda j: (0, j)),
        out_shape=jax.ShapeDtypeStruct((t, d), BF16),
        scratch_shapes=[pltpu.VMEM((gps, rows, w), BF16), pltpu.VMEM((gps, rows, w), BF16)],
        compiler_params=pltpu.CompilerParams(
            dimension_semantics=("arbitrary",), vmem_limit_bytes=VMEM_LIMIT),
        name="s5_scan",
    )(u, cc, bt, pw1, pw2, zoh, a1, a2, dv)


def _ssm_operators(a_re, a_im, b_re, b_im, c_re, c_im, log_dt, d_skip, n_chunks):
    L = SSM_CHUNK
    g, p = a_re.shape
    dt = jnp.exp(log_dt)[:, None]
    lam_re, lam_im = dt * a_re, dt * a_im

    def powers(ks):
        ks = jnp.asarray(ks, F32)[:, None, None]
        mag = jnp.exp(ks * lam_re)
        return mag * jnp.cos(ks * lam_im), mag * jnp.sin(ks * lam_im)

    def patterns(re, im):
        return (jnp.concatenate([re, re], axis=2).transpose(1, 0, 2),
                jnp.concatenate([-im, im], axis=2).transpose(1, 0, 2))

    pr, pi = powers(range(L + 1))
    num_re, num_im = pr[1] - 1.0, pi[1]
    den = a_re * a_re + a_im * a_im
    s_re = (num_re * a_re + num_im * a_im) / den
    s_im = (num_im * a_re - num_re * a_im) / den
    pw1, pw2 = patterns(pr, pi)
    zoh = jnp.concatenate(patterns(s_re[None], s_im[None]), axis=1)
    n_steps = max(1, int(math.log2(n_chunks)))
    a1, a2 = patterns(*powers([L * 2 ** j for j in range(n_steps)]))
    cc = jnp.concatenate([c_re, c_im], axis=2)
    bt = jnp.concatenate([b_re.transpose(0, 2, 1), b_im.transpose(0, 2, 1)], axis=2)
    dv = jnp.tile(d_skip.reshape(g, 1, SSM_GROUP), (1, L, 1)).reshape(g, 1, L * SSM_GROUP)
    return cc, bt, pw1, pw2, zoh, a1, a2, dv


def kernel(x, norm_mix_g, norm_mlp_g, attn_w_in, attn_b_f, attn_q_g, attn_k_g, attn_w_out,
           ssm_w_in, ssm_a_re, ssm_a_im, ssm_b_re, ssm_b_im, ssm_c_re, ssm_c_im, ssm_log_dt,
           ssm_d, ssm_w_glu, mlp_w1, mlp_w2):
    b, s, d = x.shape
    t = b * s
    tk = min(512, s // 2)
    tq = 2 * tk
    tm_proj = min(512, s)
    tm_mlp = min(512, t)

    w_in = attn_w_in[0]
    w_f = jnp.pad(w_in[:, 3 * d:], ((0, 0), (0, LANES - N_HEADS)))
    wkf = jnp.concatenate([w_in[:, d:2 * d], w_f], axis=1).astype(BF16)
    wqvt = jnp.concatenate([w_in[:, :d], w_in[:, 2 * d:3 * d]], axis=1).T.astype(BF16)
    blk = np.arange(MXU_DIM) // HEAD_DIM
    bd = jnp.asarray((blk[:, None] == blk[None, :]) * (1.0 / HEAD_DIM), BF16)
    gain_k = jnp.tile(attn_k_g[0], N_HEADS).reshape(1, d)
    gain_qt = jnp.broadcast_to(
        (jnp.tile(attn_q_g[0], N_HEADS) * (LOG2E / math.sqrt(HEAD_DIM)))[:, None], (d, LANES))
    bfp = jnp.pad(attn_b_f[0], (0, LANES - N_HEADS)).reshape(1, LANES)
    tri = jnp.asarray(np.arange(tm_proj)[:, None] >= np.arange(tm_proj)[None, :], BF16)
    heads = np.arange(N_HEADS)
    place_k = np.zeros((LANES, d), np.float32)
    for i in range(BIAS_PIECES):
        place_k[N_HEADS * i + heads, (heads // HEADS_PER_STEP) * LANES
                + BIAS_PIECES * (heads % HEADS_PER_STEP) + i] = 1.0
    place_qt = np.roll(place_k, BIAS_Q_LANE, axis=1).T
    shift = (LOG2E * math.sqrt(HEAD_DIM) * jnp.max(jnp.abs(attn_q_g[0]))
             * jnp.max(jnp.abs(attn_k_g[0])))
    qt, k, vt, kb, qbt, edge = _qkv_proj(
        x, norm_mix_g[0].reshape(1, d), wkf, wqvt, bd, gain_k, gain_qt, bfp, tri,
        jnp.asarray(place_k, BF16), jnp.asarray(place_qt, BF16), jnp.full((1, LANES), shift, F32),
        tm=tm_proj, tk=tk)
    assert tm_proj == tk
    c_first = edge[:, ::tq // tk, 0, :N_HEADS]
    c_last = edge[:, :, 1, :N_HEADS]
    bound = (c_first[:, :, None, :] - c_last[:, None, :, :]) * LOG2E
    key_blk = jnp.arange(s // tk)[None, None, :, None]
    first = jnp.min(jnp.where(bound < SKIP_LOGIT, s // tk, key_blk), axis=2)
    first = jnp.minimum(first, (tq // tk) * jnp.arange(s // tq)[None, :, None]) // 2 * 2
    first = jnp.concatenate([first.transpose(0, 2, 1).reshape(-1), (shift > MAX_SHIFT)[None]])
    o = _attention(first.astype(jnp.int32), qt, qbt, k, kb, vt, tq=tq, tk=tk)
    tok = np.arange(SSM_CHUNK * SSM_CHUNK)
    perm = jnp.asarray(
        tok[None, :] == (tok[:, None] % SSM_CHUNK) * SSM_CHUNK + tok[:, None] // SSM_CHUNK, BF16)
    x2, u = _mix_mlp(x.reshape(t, d), o.reshape(t, d), attn_w_out[0].astype(BF16),
                     norm_mlp_g[0].reshape(1, d), mlp_w1[0].astype(BF16), mlp_w2[0].astype(BF16),
                     perm, (norm_mix_g[1].reshape(1, d), ssm_w_in[0].astype(BF16)),
                     glu=False, tm=tm_mlp)

    n_chunks = s // SSM_CHUNK
    ops = _ssm_operators(ssm_a_re[0], ssm_a_im[0], ssm_b_re[0], ssm_b_im[0], ssm_c_re[0],
                         ssm_c_im[0], ssm_log_dt[0], ssm_d[0], n_chunks)
    z = _ssm_scan(u, *ops, n_chunks=n_chunks)
    x3 = _mix_mlp(x2, z, ssm_w_glu[0].astype(BF16), norm_mlp_g[1].reshape(1, d),
                  mlp_w1[1].astype(BF16), mlp_w2[1].astype(BF16), perm, glu=True, tm=tm_mlp)
    return x3.reshape(b, s, d)
```

```python
import functools
import math

import jax
import jax.numpy as jnp
import numpy as np
from jax import lax
from jax.experimental import pallas as pl
from jax.experimental.pallas import tpu as pltpu

F32 = jnp.float32
BF16 = jnp.bfloat16

N_HEADS = 16
HEAD_DIM = 64
SSM_GROUP = 16
SSM_STATE = 64
SSM_CHUNK = 16
EPS = 1e-6
LOG2E = 1.4426950408889634

LANES = 128
MXU_DIM = 256
HEADS_PER_STEP = LANES // HEAD_DIM
BIAS_PIECES = 3
VMEM_LIMIT = 56 * 1024 * 1024

NT_DIMS = (((1,), (1,)), ((), ()))


def _rms_norm(x, g):
    ms = jnp.mean(x * x, axis=-1, keepdims=True)
    return x * lax.rsqrt(ms + EPS) * g


def _const_spec(shape):
    zeros = (0,) * len(shape)
    return pl.BlockSpec(shape, lambda *_: zeros)


def _bf16_pieces(val):
    pieces = jnp.zeros_like(val)
    rem = val
    for n in range(BIAS_PIECES):
        piece = rem.astype(BF16).astype(F32)
        rem = rem - piece
        pieces = pieces + (piece if n == 0 else pltpu.roll(piece, N_HEADS * n, axis=1))
    return pieces


def _qkv_kernel(x_ref, g_ref, wkf_ref, wqvt_ref, bd_ref, gain_k_ref, gain_qt_ref, bf_ref, tri_ref,
                place_k_ref, place_qt_ref, shift_ref,
                qt_ref, k_ref, vt_ref, kb_ref, qbt_ref, edge_ref, carry_ref, *, tm, tk, d):
    @pl.when(pl.program_id(1) == 0)
    def _():
        carry_ref[...] = jnp.zeros_like(carry_ref)

    h = _rms_norm(x_ref[0], g_ref[...]).astype(BF16)

    y = jnp.dot(h, wkf_ref[...], preferred_element_type=F32)
    vt = lax.dot_general(wqvt_ref[d:, :], h, NT_DIMS, preferred_element_type=F32).astype(BF16)
    for j in range(tm // tk):
        vt_ref[0, j] = vt[:, tk * j:tk * (j + 1)]
    qt = lax.dot_general(wqvt_ref[:d, :], h, NT_DIMS, preferred_element_type=F32)

    for t in range(d // MXU_DIM):
        sl = slice(MXU_DIM * t, MXU_DIM * (t + 1))
        tile = y[:, sl]
        ms = jnp.dot((tile * tile).astype(BF16), bd_ref[...], preferred_element_type=F32)
        k_ref[0, :, sl] = (tile * lax.rsqrt(ms + EPS) * gain_k_ref[:, sl]).astype(BF16)
        tile = qt[sl, :]
        ms = jnp.dot(bd_ref[...], (tile * tile).astype(BF16), preferred_element_type=F32)
        gain = jnp.tile(gain_qt_ref[sl, :], (1, tm // LANES))
        qt_ref[0, sl, :] = (tile * lax.rsqrt(ms + EPS) * gain).astype(BF16)

    f = y[:, d:] + bf_ref[...]
    log_f = jnp.minimum(f, 0.0) - jnp.log1p(jnp.exp(-jnp.abs(f)))
    lane = lax.broadcasted_iota(jnp.int32, log_f.shape, 1)
    part = jnp.dot(tri_ref[...], _bf16_pieces(jnp.where(lane < N_HEADS, log_f, 0.0)).astype(BF16),
                   preferred_element_type=F32)
    cs = carry_ref[...] + part
    for n in range(1, BIAS_PIECES):
        cs = cs + pltpu.roll(part, LANES - N_HEADS * n, axis=1)
    carry_ref[...] = cs[tm - 1:tm, :]
    edge_ref[0, 0] = jnp.concatenate(
        [cs[0:1, :], cs[tm - 1:tm, :], jnp.zeros((6, LANES), F32)], axis=0)
    c2 = jnp.where(lane < N_HEADS, cs * LOG2E, 0.0)
    kb_ref[0] = jnp.dot(_bf16_pieces(-c2).astype(BF16), place_k_ref[...],
                        preferred_element_type=F32).astype(BF16)
    q_pieces = _bf16_pieces(jnp.where(lane < N_HEADS, c2 - shift_ref[...], 0.0))
    qbt_ref[0] = jnp.dot(place_qt_ref[...], q_pieces.T.astype(BF16),
                         preferred_element_type=F32).astype(BF16)


def _qkv_proj(x, g, wkf, wqvt, bd, gain_k, gain_qt, bfp, tri, place_k, place_qt, shift, *, tm, tk):
    b, s, d = x.shape
    row = lambda bi, si: (bi, si, 0)
    col = lambda bi, si: (bi, 0, si)
    consts = (g, wkf, wqvt, bd, gain_k, gain_qt, bfp, tri, place_k, place_qt, shift)
    return pl.pallas_call(
        functools.partial(_qkv_kernel, tm=tm, tk=tk, d=d),
        grid=(b, s // tm),
        in_specs=[pl.BlockSpec((1, tm, d), row)] + [_const_spec(c.shape) for c in consts],
        out_specs=[pl.BlockSpec((1, d, tm), col),
                   pl.BlockSpec((1, tm, d), row),
                   pl.BlockSpec((1, tm // tk, d, tk), lambda bi, si: (bi, si, 0, 0)),
                   pl.BlockSpec((1, tm, d), row),
                   pl.BlockSpec((1, d, tm), col),
                   pl.BlockSpec((1, 1, 8, LANES), lambda bi, si: (bi, si, 0, 0))],
        out_shape=[jax.ShapeDtypeStruct((b, d, s), BF16),
                   jax.ShapeDtypeStruct((b, s, d), BF16),
                   jax.ShapeDtypeStruct((b, s // tk, d, tk), BF16),
                   jax.ShapeDtypeStruct((b, s, d), BF16),
                   jax.ShapeDtypeStruct((b, d, s), BF16),
                   jax.ShapeDtypeStruct((b, s // tm, 8, LANES), F32)],
        scratch_shapes=[pltpu.VMEM((1, LANES), F32)],
        compiler_params=pltpu.CompilerParams(
            dimension_semantics=("arbitrary", "arbitrary"), vmem_limit_bytes=VMEM_LIMIT),
        name="qkv_proj",
    )(x, *consts)


BIAS_Q_LANE = 8
MAX_SHIFT = 48.0
SKIP_LOGIT = -200.0


def _attn_kernel(first_ref, *refs, tq, tk):
    *attn_in, w1_ref, w2_ref, o_ref, w1_bf_ref, w2_bf_ref = refs[:10]
    attn_refs = (*attn_in, o_ref, *refs[10:])
    w1_bf_ref[...] = w1_ref[...].astype(BF16)
    w2_bf_ref[...] = w2_ref[...].astype(BF16)

    use_online = first_ref[first_ref.shape[0] - 1]

    @pl.when(use_online == 0)
    def _():
        _attn_body(first_ref, *attn_refs, tq=tq, tk=tk, online=False)

    @pl.when(use_online != 0)
    def _():
        _attn_body(first_ref, *attn_refs, tq=tq, tk=tk, online=True)


def _attn_body(first_ref, qt_ref, qbt_ref, k_ref, kb_ref, vt_ref, o_ref, s0_scr, s1_scr, p0_scr,
               p1_scr, *, tq, tk, online):
    i = pl.program_id(2)
    n_sub = tq // tk
    qt = qt_ref[0]
    qbt = qbt_ref[0]
    row = lax.broadcasted_iota(jnp.int32, (LANES, tq), 0)
    lane_k = lax.broadcasted_iota(jnp.int32, (tk, LANES), 1)
    q_cols = (lane_k >= BIAS_Q_LANE) & (lane_k < BIAS_Q_LANE + HEADS_PER_STEP * BIAS_PIECES)
    causal = (lax.broadcasted_iota(jnp.int32, (tk, tk), 0)
              <= lax.broadcasted_iota(jnp.int32, (tk, tk), 1))
    one = jnp.ones((), BF16)
    zero = jnp.zeros((), BF16)
    heads = range(HEADS_PER_STEP)
    qats = []
    for hh in heads:
        q_h = jnp.where((row >= HEAD_DIM * hh) & (row < HEAD_DIM * (hh + 1)), qt, zero)
        k_side = (row >= BIAS_PIECES * hh) & (row < BIAS_PIECES * (hh + 1))
        q_lo = BIAS_Q_LANE + BIAS_PIECES * hh
        q_side = (row >= q_lo) & (row < q_lo + BIAS_PIECES)
        qats.append(jnp.concatenate(
            [q_h, jnp.where(k_side, one, jnp.where(q_side, qbt, zero))], axis=0))

    def scores(hh, j, c0=0):
        off = pl.multiple_of(j * tk, tk)
        kbias = jnp.where(q_cols, one, kb_ref[0, pl.ds(off, tk), :])
        ka = jnp.concatenate([k_ref[0, pl.ds(off, tk), :], kbias], axis=1)
        return jnp.dot(ka, qats[hh][:, c0:], preferred_element_type=F32)

    if not online:
        j0 = n_sub * i
        def probs(hh, l, j, c0=0, diagonal=None):
            s = scores(hh, j, c0)
            if diagonal is not False:
                head = jnp.where(causal, s[:, :tk], -1e30)
                if diagonal is None:
                    head = jnp.where(j < j0, s[:, :tk], head)
                s = head if s.shape[1] == tk else jnp.concatenate([head, s[:, tk:]], axis=1)
            p = jnp.exp2(s)
            return l + jnp.sum(p, axis=0, keepdims=True), p.astype(BF16)

        def weigh(hh, acc, p, j):
            vt = vt_ref[0, j, HEAD_DIM * hh:HEAD_DIM * (hh + 1), :]
            return acc + jnp.dot(vt, p, preferred_element_type=F32)

        def fill(hh):
            head = HEADS_PER_STEP * pl.program_id(1) + hh
            jj0 = first_ref[(pl.program_id(0) * N_HEADS + head) * pl.num_programs(2) + i] // 2
            l, p0_scr[hh] = probs(hh, jnp.zeros((1, tq), F32), 2 * jj0)
            return jj0, l

        def off_diagonal(hh, jj0, l):
            def pairs(jj, c, n_pairs):
                l, acc = c
                for r in range(n_pairs):
                    a = 2 * (jj + r)
                    l, p1_scr[hh] = probs(hh, l, a + 1, diagonal=False)
                    acc = weigh(hh, acc, p0_scr[hh], a)
                    l, p0_scr[hh] = probs(hh, l, a + 2)
                    acc = weigh(hh, acc, p1_scr[hh], a + 1)
                return l, acc

            n_long = (j0 // 2 - jj0) // 2
            c = (l, jnp.zeros((HEAD_DIM, tq), F32))
            c = lax.fori_loop(0, n_long, lambda t, c: pairs(jj0 + 2 * t, c, 2), c)
            return lax.fori_loop(jj0 + 2 * n_long, j0 // 2, lambda jj, c: pairs(jj, c, 1), c)

        def diagonal(hh, l, acc):
            later = [probs(hh, l[:, tk * dd:], j0 + dd, tk * dd, diagonal=True)
                     for dd in range(1, n_sub)]
            acc = weigh(hh, acc, p0_scr[hh], j0)
            for dd in range(1, n_sub):
                c0 = tk * dd
                l_part, p = later[dd - 1]
                l = jnp.concatenate([l[:, :c0], l_part], axis=1)
                acc = jnp.concatenate([acc[:, :c0], weigh(hh, acc[:, c0:], p, j0 + dd)], axis=1)
            return acc / l

        outs = []
        jj0, l = fill(0)
        for hh in heads:
            l, acc = off_diagonal(hh, jj0, l)
            if hh + 1 < HEADS_PER_STEP:
                jj0, l_next = fill(hh + 1)
            outs.append(diagonal(hh, l, acc))
            l = l_next if hh + 1 < HEADS_PER_STEP else None
        o_ref[0] = jnp.concatenate(outs, axis=0).T.astype(BF16)
        return

    def consume(hh, carry, s, j, diagonal):
        if diagonal:
            s_tri = jnp.where(causal, s[:, :tk], -1e30)
            s = s_tri if s.shape[1] == tk else jnp.concatenate([s_tri, s[:, tk:]], axis=1)
        vt = vt_ref[0, j, HEAD_DIM * hh:HEAD_DIM * (hh + 1), :]
        if online:
            m, l, acc = carry
            m_new = jnp.maximum(m, jnp.max(s, axis=0, keepdims=True))
            alpha = jnp.exp2(m - m_new)
            p = jnp.exp2(s - m_new)
            l = alpha * l + jnp.sum(p, axis=0, keepdims=True)
            acc = alpha * acc + jnp.dot(vt, p.astype(BF16), preferred_element_type=F32)
            return m_new, l, acc
        l, acc = carry
        p = jnp.exp2(s)
        l = l + jnp.sum(p, axis=0, keepdims=True)
        acc = acc + jnp.dot(vt, p.astype(BF16), preferred_element_type=F32)
        return l, acc

    init = (jnp.zeros((1, tq), F32), jnp.zeros((HEAD_DIM, tq), F32))
    if online:
        init = (jnp.full((1, tq), -1e30, F32),) + init

    j0 = n_sub * i

    def first_scores(hh):
        if online:
            jj0 = 0
        else:
            head = HEADS_PER_STEP * pl.program_id(1) + hh
            jj0 = first_ref[(pl.program_id(0) * N_HEADS + head) * pl.num_programs(2) + i] // 2
        s0_scr[hh] = scores(hh, 2 * jj0)
        return jj0

    def off_diagonal(hh, jj0):
        def pairs(jj, c, n_pairs):
            for r in range(n_pairs):
                s1_scr[hh] = scores(hh, 2 * (jj + r) + 1)
                c = consume(hh, c, s0_scr[hh], 2 * (jj + r), False)
                s0_scr[hh] = scores(hh, 2 * (jj + r) + 2)
                c = consume(hh, c, s1_scr[hh], 2 * (jj + r) + 1, False)
            return c

        n_long = (j0 // 2 - jj0) // 2
        c = lax.fori_loop(0, n_long, lambda t, c: pairs(jj0 + 2 * t, c, 2), init)
        return lax.fori_loop(jj0 + 2 * n_long, j0 // 2, lambda jj, c: pairs(jj, c, 1), c)

    def diagonal(hh, carry):
        later = [scores(hh, j0 + dd, tk * dd) for dd in range(1, n_sub)]
        carry = consume(hh, carry, s0_scr[hh], j0, True)
        for dd in range(1, n_sub):
            c0 = tk * dd
            part = consume(hh, tuple(c[:, c0:] for c in carry), later[dd - 1], j0 + dd, True)
            carry = tuple(jnp.concatenate([c[:, :c0], pc], axis=1) for c, pc in zip(carry, part))
        return carry[-1] / carry[-2]

    outs = []
    jj0 = first_scores(0)
    for hh in heads:
        carry = off_diagonal(hh, jj0)
        if hh + 1 < HEADS_PER_STEP:
            jj0 = first_scores(hh + 1)
        outs.append(diagonal(hh, carry))
    o_t = jnp.concatenate(outs, axis=0)
    o_ref[0] = o_t.T.astype(BF16)


def _attention(first, qt, qbt, k, kb, vt, w1, w2, *, tq, tk):
    b, s, d = k.shape
    assert (tq // tk) % 2 == 0 and s % tq == 0
    n_pairs = d // LANES
    nq = s // tq
    nk = s // tk
    n_steps = b * n_pairs * nq
    qt_blk = pl.BlockSpec((1, LANES, tq), lambda bi, hp, i, first: (bi, hp, i))
    k_all = pl.BlockSpec((1, s, LANES), lambda bi, hp, i, first: (bi, 0, hp))
    step = lambda bi, hp, i, first: ((bi * n_pairs + hp) * nq + i, 0)

    def slice_spec(w):
        assert w.shape[0] % (16 * n_steps) == 0
        return pl.BlockSpec((w.shape[0] // n_steps, w.shape[1]), step)

    return pl.pallas_call(
        functools.partial(_attn_kernel, tq=tq, tk=tk),
        grid_spec=pltpu.PrefetchScalarGridSpec(
            num_scalar_prefetch=1,
            grid=(b, n_pairs, nq),
            in_specs=[qt_blk, qt_blk, k_all, k_all,
                      pl.BlockSpec((1, nk, LANES, tk), lambda bi, hp, i, first: (bi, 0, hp, 0)),
                      slice_spec(w1), slice_spec(w2)],
            out_specs=[pl.BlockSpec((1, tq, LANES), lambda bi, hp, i, first: (bi, i, hp)),
                       slice_spec(w1), slice_spec(w2)],
            scratch_shapes=[pltpu.VMEM((HEADS_PER_STEP, tk, tq), F32),
                            pltpu.VMEM((HEADS_PER_STEP, tk, tq), F32),
                            pltpu.VMEM((HEADS_PER_STEP, tk, tq), BF16),
                            pltpu.VMEM((HEADS_PER_STEP, tk, tq), BF16)]),
        out_shape=[jax.ShapeDtypeStruct((b, s, d), BF16),
                   jax.ShapeDtypeStruct(w1.shape, BF16), jax.ShapeDtypeStruct(w2.shape, BF16)],
        compiler_params=pltpu.CompilerParams(
            dimension_semantics=("arbitrary", "arbitrary", "arbitrary"),
            vmem_limit_bytes=VMEM_LIMIT),
        name="fox_attention",
    )(first, qt, qbt, k, kb, vt, w1, w2)


def _permute_token_blocks(perm, a):
    w = perm.shape[0]
    return jnp.concatenate(
        [jnp.dot(perm, a[w * i:w * (i + 1)], preferred_element_type=F32).astype(BF16)
         for i in range(a.shape[0] // w)], axis=0)


def _mix_mlp_kernel(x_ref, a_ref, wmix_ref, g_ref, w1_ref, w2_ref, *rest, glu, d, ff_chunk):
    if glu:
        a = _permute_token_blocks(rest[0][...], a_ref[...])
        mix = jnp.dot(a, wmix_ref[...], preferred_element_type=F32)
        mix = mix[:, :d] * jax.nn.sigmoid(mix[:, d:])
    else:
        mix = jnp.dot(a_ref[...], wmix_ref[...], preferred_element_type=F32)
    x1 = x_ref[...] + mix
    h = _rms_norm(x1, g_ref[...]).astype(BF16)
    acc = x1
    for c in range(w1_ref.shape[1] // ff_chunk):
        sl = slice(ff_chunk * c, ff_chunk * (c + 1))
        hid = jnp.maximum(jnp.dot(h, w1_ref[:, sl], preferred_element_type=F32), 0.0)
        acc = acc + jnp.dot((hid * hid).astype(BF16), w2_ref[sl, :], preferred_element_type=F32)
    if glu:
        _, o_ref = rest
    else:
        perm_ref, g_next_ref, w_next_ref, o_ref, u_ref = rest
        h_next = _rms_norm(acc, g_next_ref[...]).astype(BF16)
        u = jnp.dot(h_next, w_next_ref[...], preferred_element_type=F32).astype(BF16)
        u_ref[...] = _permute_token_blocks(perm_ref[...], u)
    o_ref[...] = acc


def _mix_mlp(x, a, wmix, g, w1, w2, perm, next_proj=None, *, layer, glu, tm, ff_chunk=1024):
    t, d = x.shape
    row = lambda i: (i, 0)
    single = pl.Buffered(1)
    wspec = lambda shape: pl.BlockSpec(shape, lambda i: (0, 0), pipeline_mode=single)
    lspec = lambda w: pl.BlockSpec((None,) + w.shape[1:], lambda i: (layer, 0, 0),
                                   pipeline_mode=single)
    operands = [x, a, wmix, g, w1, w2, perm]
    in_specs = [pl.BlockSpec((tm, d), row), pl.BlockSpec((tm, d), row),
                wspec(wmix.shape), wspec((1, d)), lspec(w1), lspec(w2), wspec(perm.shape)]
    out_specs = pl.BlockSpec((tm, d), row)
    out_shape = jax.ShapeDtypeStruct((t, d), F32)
    if next_proj is not None:
        operands += list(next_proj)
        in_specs += [wspec(w.shape) for w in next_proj]
        n = next_proj[1].shape[1]
        out_specs = [out_specs, pl.BlockSpec((tm, n), row)]
        out_shape = [out_shape, jax.ShapeDtypeStruct((t, n), BF16)]
    return pl.pallas_call(
        functools.partial(_mix_mlp_kernel, glu=glu, d=d, ff_chunk=ff_chunk),
        grid=(t // tm,),
        in_specs=in_specs,
        out_specs=out_specs,
        out_shape=out_shape,
        compiler_params=pltpu.CompilerParams(
            dimension_semantics=("arbitrary",), vmem_limit_bytes=VMEM_LIMIT),
        name="mix_glu_mlp" if glu else "mix_mlp",
    )(*operands)


GROUPS_PER_SLAB = LANES // SSM_GROUP


def _piece_transpose(arrs, piece):
    arrs = list(arrs)
    dist = GROUPS_PER_SLAB // 2
    while dist >= 1:
        keep = (piece & dist) == 0
        shift = SSM_GROUP * dist
        for i in range(GROUPS_PER_SLAB):
            if i & dist:
                continue
            a, b = arrs[i], arrs[i + dist]
            arrs[i] = jnp.where(keep, a, pltpu.roll(b, shift, axis=1))
            arrs[i + dist] = jnp.where(keep, pltpu.roll(a, LANES - shift, axis=1), b)
        dist //= 2
    return arrs


def _ssm_kernel(u_ref, cc_ref, bt_ref, pw1_ref, pw2_ref, zoh_ref, a1_ref, a2_ref,
                dv_ref, z_ref, ug_scr, zg_scr, *, n_chunks):
    L = SSM_CHUNK
    rows = u_ref.shape[0] // L
    w = L * SSM_GROUP
    half = SSM_STATE
    nb = math.gcd(rows // L, 8)
    piece = lax.broadcasted_iota(jnp.int32, (nb * L, LANES), 1) // SSM_GROUP

    def relayout_in(rb, carry):
        t0 = pl.multiple_of(rb * nb * w, nb * w)
        r0 = pl.multiple_of(rb * nb * L, nb * L)
        by_pos = u_ref[pl.ds(t0, nb * w), :].astype(F32)
        for hf in range(L // GROUPS_PER_SLAB):
            arrs = []
            for k in range(GROUPS_PER_SLAB):
                pos = GROUPS_PER_SLAB * hf + k
                arrs.append(jnp.concatenate(
                    [by_pos[w * blk + L * pos:w * blk + L * (pos + 1)] for blk in range(nb)],
                    axis=0))
            arrs = _piece_transpose(arrs, piece)
            for g in range(GROUPS_PER_SLAB):
                ug_scr[g, pl.ds(r0, nb * L), LANES * hf:LANES * (hf + 1)] = arrs[g].astype(BF16)
        return carry

    lax.fori_loop(0, rows // (nb * L), relayout_in, 0)

    chunk = lax.broadcasted_iota(jnp.int32, (rows, LANES), 0) % n_chunks
    lane_w = lax.broadcasted_iota(jnp.int32, (SSM_GROUP, w), 1)
    low = lax.broadcasted_iota(jnp.int32, (1, LANES), 1) < half
    sign = jnp.where(low, 1.0, -1.0)

    def operators(g):
        cc = cc_ref[g]
        bt = bt_ref[g]
        pw1 = pw1_ref[g]
        pw2 = pw2_ref[g]
        zoh = zoh_ref[g]
        bbar = bt * zoh[0:1] + pltpu.roll(bt, half, axis=1) * zoh[1:2]
        bbar_sw = pltpu.roll(bbar, half, axis=1)
        cc_sw = pltpu.roll(cc, half, axis=1)
        ca = [cc * pw1[k:k + 1] + cc_sw * pw2[k:k + 1] for k in range(L + 1)]
        k_all = lax.dot_general(bbar * sign, jnp.concatenate(ca[:L], axis=0), NT_DIMS,
                                precision=lax.Precision.HIGHEST, preferred_element_type=F32)
        m_intra, w_state, w_out_t = [], [], []
        for s in range(L):
            blk = k_all if s == 0 else jnp.where(
                lane_w >= SSM_GROUP * s, pltpu.roll(k_all, SSM_GROUP * s, axis=1), 0.0)
            m_intra.append(blk.astype(BF16))
            k = L - 1 - s
            w_state.append(bbar * pw1[k:k + 1] + bbar_sw * pw2[k:k + 1])
            w_out_t.append(ca[s + 1] * sign)
        return jnp.concatenate(m_intra, axis=0), w_state, w_out_t

    def halves(pieces, second):
        firsts, seconds = [], []
        for piece in pieces:
            swapped = pltpu.roll(piece, half, axis=1)
            if second:
                firsts.append(jnp.where(low, 0.0, swapped))
                seconds.append(jnp.where(low, 0.0, piece))
            else:
                firsts.append(jnp.where(low, piece, 0.0))
                seconds.append(jnp.where(low, swapped, 0.0))
        return firsts, seconds

    def group_pair(pr, carry):
        gs = (2 * pr, 2 * pr + 1)
        m_intra, ws_re, ws_im, wo_t = [], [], [], []
        for second, g in enumerate(gs):
            mi, w_state, w_out_t = operators(g)
            m_intra.append(mi)
            re, im = halves(w_state, second)
            ws_re += re
            ws_im += im
            re, im = halves(w_out_t, second)
            wo_t.append(jnp.concatenate(
                [jnp.concatenate([r, m], axis=1) for r, m in zip(re, im)], axis=0).astype(BF16))
        us = [ug_scr[g] for g in gs]
        uu = jnp.concatenate(us, axis=1)
        xr = jnp.dot(uu, jnp.concatenate(ws_re, axis=0).astype(BF16), preferred_element_type=F32)
        xi = jnp.dot(uu, jnp.concatenate(ws_im, axis=0).astype(BF16), preferred_element_type=F32)
        a1 = [a1_ref[g] for g in gs]
        a2 = [a2_ref[g] for g in gs]
        step = 1
        j = 0
        while step < n_chunks:
            ar = jnp.where(low, a1[0][j:j + 1], a1[1][j:j + 1])
            ai = jnp.where(low, -a2[0][j:j + 1], a2[1][j:j + 1])
            sr = jnp.where(chunk >= step, pltpu.roll(xr, step, axis=0), 0.0)
            si = jnp.where(chunk >= step, pltpu.roll(xi, step, axis=0), 0.0)
            xr = xr + sr * ar - si * ai
            xi = xi + si * ar + sr * ai
            step *= 2
            j += 1
        x_in = jnp.concatenate([jnp.where(chunk >= 1, pltpu.roll(x, 1, axis=0), 0.0)
                                for x in (xr, xi)], axis=1).astype(BF16)
        for second, g in enumerate(gs):
            y = jnp.dot(us[second], m_intra[second], preferred_element_type=F32)
            y = y + lax.dot_general(x_in, wo_t[second], NT_DIMS, preferred_element_type=F32)
            y = y + us[second].astype(F32) * dv_ref[g]
            zg_scr[g] = jax.nn.gelu(y).astype(BF16)
        return carry

    lax.fori_loop(0, GROUPS_PER_SLAB // 2, group_pair, 0)

    def relayout_out(rb, carry):
        t0 = pl.multiple_of(rb * nb * w, nb * w)
        r0 = pl.multiple_of(rb * nb * L, nb * L)
        by_pos = []
        for hf in range(L // GROUPS_PER_SLAB):
            arrs = [zg_scr[g, pl.ds(r0, nb * L), LANES * hf:LANES * (hf + 1)].astype(F32)
                    for g in range(GROUPS_PER_SLAB)]
            by_pos += _piece_transpose(arrs, piece)
        for blk in range(nb):
            z_ref[pl.ds(t0 + w * blk, w), :] = jnp.concatenate(
                [arr[L * blk:L * (blk + 1)] for arr in by_pos], axis=0).astype(BF16)
        return carry

    lax.fori_loop(0, rows // (nb * L), relayout_out, 0)


def _ssm_scan(u, cc, bt, pw1, pw2, zoh, a1, a2, dv, *, n_chunks):
    t, d = u.shape
    L = SSM_CHUNK
    rows = t // L
    w = L * SSM_GROUP
    gps = GROUPS_PER_SLAB
    slab = lambda j: (j, 0, 0)
    pspec = lambda arr: pl.BlockSpec((gps,) + arr.shape[1:], slab)
    return pl.pallas_call(
        functools.partial(_ssm_kernel, n_chunks=n_chunks),
        grid=(d // LANES,),
        in_specs=[pl.BlockSpec((t, LANES), lambda j: (0, j)),
                  pspec(cc), pspec(bt), pspec(pw1), pspec(pw2), pspec(zoh), pspec(a1), pspec(a2),
                  pspec(dv)],
        out_specs=pl.BlockSpec((t, LANES), lambda j: (0, j)),
        out_shape=jax.ShapeDtypeStruct((t, d), BF16),
        scratch_shapes=[pltpu.VMEM((gps, rows, w), BF16), pltpu.VMEM((gps, rows, w), BF16)],
        compiler_params=pltpu.CompilerParams(
            dimension_semantics=("arbitrary",), vmem_limit_bytes=VMEM_LIMIT),
        name="s5_scan",
    )(u, cc, bt, pw1, pw2, zoh, a1, a2, dv)


def _ssm_operators(a_re, a_im, b_re, b_im, c_re, c_im, log_dt, d_skip, n_chunks):
    L = SSM_CHUNK
    g, p = a_re.shape
    dt = jnp.exp(log_dt)[:, None]
    lam_re, lam_im = dt * a_re, dt * a_im

    def powers(ks):
        ks = jnp.asarray(ks, F32)[:, None, None]
        mag = jnp.exp(ks * lam_re)
        return mag * jnp.cos(ks * lam_im), mag * jnp.sin(ks * lam_im)

    def patterns(re, im):
        return (jnp.concatenate([re, re], axis=2).transpose(1, 0, 2),
                jnp.concatenate([-im, im], axis=2).transpose(1, 0, 2))

    pr, pi = powers(range(L + 1))
    num_re, num_im = pr[1] - 1.0, pi[1]
    den = a_re * a_re + a_im * a_im
    s_re = (num_re * a_re + num_im * a_im) / den
    s_im = (num_im * a_re - num_re * a_im) / den
    pw1, pw2 = patterns(pr, pi)
    zoh = jnp.concatenate(patterns(s_re[None], s_im[None]), axis=1)
    n_steps = max(1, int(math.log2(n_chunks)))
    a1, a2 = patterns(*powers([L * 2 ** j for j in range(n_steps)]))
    cc = jnp.concatenate([c_re, c_im], axis=2)
    bt = jnp.concatenate([b_re.transpose(0, 2, 1), b_im.transpose(0, 2, 1)], axis=2)
    dv = jnp.tile(d_skip.reshape(g, 1, SSM_GROUP), (1, L, 1)).reshape(g, 1, L * SSM_GROUP)
    return cc, bt, pw1, pw2, zoh, a1, a2, dv


def kernel(x, norm_mix_g, norm_mlp_g, attn_w_in, attn_b_f, attn_q_g, attn_k_g, attn_w_out,
           ssm_w_in, ssm_a_re, ssm_a_im, ssm_b_re, ssm_b_im, ssm_c_re, ssm_c_im, ssm_log_dt,
           ssm_d, ssm_w_glu, mlp_w1, mlp_w2):
    b, s, d = x.shape
    t = b * s
    tk = min(512, s // 2)
    tq = 2 * tk
    tm_proj = min(512, s)
    tm_mlp = min(512, t)

    w_in = attn_w_in[0]
    w_f = jnp.pad(w_in[:, 3 * d:], ((0, 0), (0, LANES - N_HEADS)))
    wkf = jnp.concatenate([w_in[:, d:2 * d], w_f], axis=1).astype(BF16)
    wqvt = jnp.concatenate([w_in[:, :d], w_in[:, 2 * d:3 * d]], axis=1).T.astype(BF16)
    blk = np.arange(MXU_DIM) // HEAD_DIM
    bd = jnp.asarray((blk[:, None] == blk[None, :]) * (1.0 / HEAD_DIM), BF16)
    gain_k = jnp.tile(attn_k_g[0], N_HEADS).reshape(1, d)
    gain_qt = jnp.broadcast_to(
        (jnp.tile(attn_q_g[0], N_HEADS) * (LOG2E / math.sqrt(HEAD_DIM)))[:, None], (d, LANES))
    bfp = jnp.pad(attn_b_f[0], (0, LANES - N_HEADS)).reshape(1, LANES)
    tri = jnp.asarray(np.arange(tm_proj)[:, None] >= np.arange(tm_proj)[None, :], BF16)
    heads = np.arange(N_HEADS)
    place_k = np.zeros((LANES, d), np.float32)
    for i in range(BIAS_PIECES):
        place_k[N_HEADS * i + heads, (heads // HEADS_PER_STEP) * LANES
                + BIAS_PIECES * (heads % HEADS_PER_STEP) + i] = 1.0
    place_qt = np.roll(place_k, BIAS_Q_LANE, axis=1).T
    shift = (LOG2E * math.sqrt(HEAD_DIM) * jnp.max(jnp.abs(attn_q_g[0]))
             * jnp.max(jnp.abs(attn_k_g[0])))
    qt, k, vt, kb, qbt, edge = _qkv_proj(
        x, norm_mix_g[0].reshape(1, d), wkf, wqvt, bd, gain_k, gain_qt, bfp, tri,
        jnp.asarray(place_k, BF16), jnp.asarray(place_qt, BF16), jnp.full((1, LANES), shift, F32),
        tm=tm_proj, tk=tk)
    assert tm_proj == tk
    c_first = edge[:, ::tq // tk, 0, :N_HEADS]
    c_last = edge[:, :, 1, :N_HEADS]
    bound = (c_first[:, :, None, :] - c_last[:, None, :, :]) * LOG2E
    key_blk = jnp.arange(s // tk)[None, None, :, None]
    first = jnp.min(jnp.where(bound < SKIP_LOGIT, s // tk, key_blk), axis=2)
    first = jnp.minimum(first, (tq // tk) * jnp.arange(s // tq)[None, :, None]) // 2 * 2
    first = jnp.concatenate([first.transpose(0, 2, 1).reshape(-1), (shift > MAX_SHIFT)[None]])
    dff = mlp_w1.shape[2]
    o, w1, w2 = _attention(first.astype(jnp.int32), qt, qbt, k, kb, vt,
                           mlp_w1.reshape(-1, dff), mlp_w2.reshape(-1, d), tq=tq, tk=tk)
    w1 = w1.reshape(mlp_w1.shape)
    w2 = w2.reshape(mlp_w2.shape)
    tok = np.arange(SSM_CHUNK * SSM_CHUNK)
    perm = jnp.asarray(
        tok[None, :] == (tok[:, None] % SSM_CHUNK) * SSM_CHUNK + tok[:, None] // SSM_CHUNK, BF16)
    x2, u = _mix_mlp(x.reshape(t, d), o.reshape(t, d), attn_w_out[0].astype(BF16),
                     norm_mlp_g[0].reshape(1, d), w1, w2,
                     perm, (norm_mix_g[1].reshape(1, d), ssm_w_in[0].astype(BF16)),
                     layer=0, glu=False, tm=tm_mlp)

    n_chunks = s // SSM_CHUNK
    ops = _ssm_operators(ssm_a_re[0], ssm_a_im[0], ssm_b_re[0], ssm_b_im[0], ssm_c_re[0],
                         ssm_c_im[0], ssm_log_dt[0], ssm_d[0], n_chunks)
    z = _ssm_scan(u, *ops, n_chunks=n_chunks)
    x3 = _mix_mlp(x2, z, ssm_w_glu[0].astype(BF16), norm_mlp_g[1].reshape(1, d),
                  w1, w2, perm, layer=1, glu=True, tm=tm_mlp)
    return x3.reshape(b, s, d)
```

```python
import functools
import math

import jax
import jax.numpy as jnp
import numpy as np
from jax import lax
from jax.experimental import pallas as pl
from jax.experimental.pallas import tpu as pltpu

F32 = jnp.float32
BF16 = jnp.bfloat16

N_HEADS = 16
HEAD_DIM = 64
SSM_GROUP = 16
SSM_STATE = 64
SSM_CHUNK = 16
EPS = 1e-6
LOG2E = 1.4426950408889634

LANES = 128
MXU_DIM = 256
HEADS_PER_STEP = LANES // HEAD_DIM
BIAS_PIECES = 3
VMEM_LIMIT = 56 * 1024 * 1024

NT_DIMS = (((1,), (1,)), ((), ()))


def _rms_norm(x, g):
    ms = jnp.mean(x * x, axis=-1, keepdims=True)
    return x * lax.rsqrt(ms + EPS) * g


def _const_spec(shape):
    zeros = (0,) * len(shape)
    return pl.BlockSpec(shape, lambda *_: zeros)


def _bf16_pieces(val):
    pieces = jnp.zeros_like(val)
    rem = val
    for n in range(BIAS_PIECES):
        piece = rem.astype(BF16).astype(F32)
        rem = rem - piece
        pieces = pieces + (piece if n == 0 else pltpu.roll(piece, N_HEADS * n, axis=1))
    return pieces


def _qkv_kernel(x_ref, g_ref, wkf_ref, wqvt_ref, bd_ref, gain_k_ref, gain_qt_ref, bf_ref, tri_ref,
                place_k_ref, place_qt_ref, shift_ref,
                qt_ref, k_ref, vt_ref, kb_ref, qbt_ref, edge_ref, carry_ref, *, tm, tk, d):
    @pl.when(pl.program_id(1) == 0)
    def _():
        carry_ref[...] = jnp.zeros_like(carry_ref)

    h = _rms_norm(x_ref[0], g_ref[...]).astype(BF16)

    y = jnp.dot(h, wkf_ref[...], preferred_element_type=F32)
    vt = lax.dot_general(wqvt_ref[d:, :], h, NT_DIMS, preferred_element_type=F32).astype(BF16)
    for j in range(tm // tk):
        vt_ref[0, j] = vt[:, tk * j:tk * (j + 1)]
    qt = lax.dot_general(wqvt_ref[:d, :], h, NT_DIMS, preferred_element_type=F32)

    for t in range(d // MXU_DIM):
        sl = slice(MXU_DIM * t, MXU_DIM * (t + 1))
        tile = y[:, sl]
        ms = jnp.dot((tile * tile).astype(BF16), bd_ref[...], preferred_element_type=F32)
        k_ref[0, :, sl] = (tile * lax.rsqrt(ms + EPS) * gain_k_ref[:, sl]).astype(BF16)
        tile = qt[sl, :]
        ms = jnp.dot(bd_ref[...], (tile * tile).astype(BF16), preferred_element_type=F32)
        gain = jnp.tile(gain_qt_ref[sl, :], (1, tm // LANES))
        qt_ref[0, sl, :] = (tile * lax.rsqrt(ms + EPS) * gain).astype(BF16)

    f = y[:, d:] + bf_ref[...]
    log_f = jnp.minimum(f, 0.0) - jnp.log1p(jnp.exp(-jnp.abs(f)))
    lane = lax.broadcasted_iota(jnp.int32, log_f.shape, 1)
    part = jnp.dot(tri_ref[...], _bf16_pieces(jnp.where(lane < N_HEADS, log_f, 0.0)).astype(BF16),
                   preferred_element_type=F32)
    cs = carry_ref[...] + part
    for n in range(1, BIAS_PIECES):
        cs = cs + pltpu.roll(part, LANES - N_HEADS * n, axis=1)
    carry_ref[...] = cs[tm - 1:tm, :]
    edge_ref[0, 0] = jnp.concatenate(
        [cs[0:1, :], cs[tm - 1:tm, :], jnp.zeros((6, LANES), F32)], axis=0)
    c2 = jnp.where(lane < N_HEADS, cs * LOG2E, 0.0)
    kb_ref[0] = jnp.dot(_bf16_pieces(-c2).astype(BF16), place_k_ref[...],
                        preferred_element_type=F32).astype(BF16)
    q_pieces = _bf16_pieces(jnp.where(lane < N_HEADS, c2 - shift_ref[...], 0.0))
    qbt_ref[0] = jnp.dot(place_qt_ref[...], q_pieces.T.astype(BF16),
                         preferred_element_type=F32).astype(BF16)


def _qkv_proj(x, g, wkf, wqvt, bd, gain_k, gain_qt, bfp, tri, place_k, place_qt, shift, *, tm, tk):
    b, s, d = x.shape
    row = lambda bi, si: (bi, si, 0)
    col = lambda bi, si: (bi, 0, si)
    consts = (g, wkf, wqvt, bd, gain_k, gain_qt, bfp, tri, place_k, place_qt, shift)
    return pl.pallas_call(
        functools.partial(_qkv_kernel, tm=tm, tk=tk, d=d),
        grid=(b, s // tm),
        in_specs=[pl.BlockSpec((1, tm, d), row)] + [_const_spec(c.shape) for c in consts],
        out_specs=[pl.BlockSpec((1, d, tm), col),
                   pl.BlockSpec((1, tm, d), row),
                   pl.BlockSpec((1, tm // tk, d, tk), lambda bi, si: (bi, si, 0, 0)),
                   pl.BlockSpec((1, tm, d), row),
                   pl.BlockSpec((1, d, tm), col),
                   pl.BlockSpec((1, 1, 8, LANES), lambda bi, si: (bi, si, 0, 0))],
        out_shape=[jax.ShapeDtypeStruct((b, d, s), BF16),
                   jax.ShapeDtypeStruct((b, s, d), BF16),
                   jax.ShapeDtypeStruct((b, s // tk, d, tk), BF16),
                   jax.ShapeDtypeStruct((b, s, d), BF16),
                   jax.ShapeDtypeStruct((b, d, s), BF16),
                   jax.ShapeDtypeStruct((b, s // tm, 8, LANES), F32)],
        scratch_shapes=[pltpu.VMEM((1, LANES), F32)],
        compiler_params=pltpu.CompilerParams(
            dimension_semantics=("arbitrary", "arbitrary"), vmem_limit_bytes=VMEM_LIMIT),
        name="qkv_proj",
    )(x, *consts)


BIAS_Q_LANE = 8
MAX_SHIFT = 48.0
SKIP_LOGIT = -200.0


def _attn_kernel(first_ref, *refs, tq, tk, n_cast):
    n_in = 5
    attn_refs = (*refs[:n_in], refs[n_in + n_cast], *refs[n_in + 2 * n_cast + 1:])
    for w_ref, w_bf_ref in zip(refs[n_in:n_in + n_cast],
                               refs[n_in + n_cast + 1:n_in + 2 * n_cast + 1]):
        w_bf_ref[...] = w_ref[...].astype(BF16)

    use_online = first_ref[first_ref.shape[0] - 1]

    @pl.when(use_online == 0)
    def _():
        _attn_body(first_ref, *attn_refs, tq=tq, tk=tk, online=False)

    @pl.when(use_online != 0)
    def _():
        _attn_body(first_ref, *attn_refs, tq=tq, tk=tk, online=True)


def _attn_body(first_ref, qt_ref, qbt_ref, k_ref, kb_ref, vt_ref, o_ref, s0_scr, s1_scr, p0_scr,
               p1_scr, *, tq, tk, online):
    i = pl.program_id(2)
    n_sub = tq // tk
    qt = qt_ref[0]
    qbt = qbt_ref[0]
    row = lax.broadcasted_iota(jnp.int32, (LANES, tq), 0)
    lane_k = lax.broadcasted_iota(jnp.int32, (tk, LANES), 1)
    q_cols = (lane_k >= BIAS_Q_LANE) & (lane_k < BIAS_Q_LANE + HEADS_PER_STEP * BIAS_PIECES)
    causal = (lax.broadcasted_iota(jnp.int32, (tk, tk), 0)
              <= lax.broadcasted_iota(jnp.int32, (tk, tk), 1))
    one = jnp.ones((), BF16)
    zero = jnp.zeros((), BF16)
    heads = range(HEADS_PER_STEP)
    qats = []
    for hh in heads:
        q_h = jnp.where((row >= HEAD_DIM * hh) & (row < HEAD_DIM * (hh + 1)), qt, zero)
        k_side = (row >= BIAS_PIECES * hh) & (row < BIAS_PIECES * (hh + 1))
        q_lo = BIAS_Q_LANE + BIAS_PIECES * hh
        q_side = (row >= q_lo) & (row < q_lo + BIAS_PIECES)
        qats.append(jnp.concatenate(
            [q_h, jnp.where(k_side, one, jnp.where(q_side, qbt, zero))], axis=0))

    def scores(hh, j, c0=0):
        off = pl.multiple_of(j * tk, tk)
        kbias = jnp.where(q_cols, one, kb_ref[0, pl.ds(off, tk), :])
        ka = jnp.concatenate([k_ref[0, pl.ds(off, tk), :], kbias], axis=1)
        return jnp.dot(ka, qats[hh][:, c0:], preferred_element_type=F32)

    if not online:
        j0 = n_sub * i
        def probs(hh, l, j, c0=0, diagonal=None):
            s = scores(hh, j, c0)
            if diagonal is not False:
                head = jnp.where(causal, s[:, :tk], -1e30)
                if diagonal is None:
                    head = jnp.where(j < j0, s[:, :tk], head)
                s = head if s.shape[1] == tk else jnp.concatenate([head, s[:, tk:]], axis=1)
            p = jnp.exp2(s)
            return l + jnp.sum(p, axis=0, keepdims=True), p.astype(BF16)

        def weigh(hh, acc, p, j):
            vt = vt_ref[0, j, HEAD_DIM * hh:HEAD_DIM * (hh + 1), :]
            return acc + jnp.dot(vt, p, preferred_element_type=F32)

        def fill(hh):
            head = HEADS_PER_STEP * pl.program_id(1) + hh
            jj0 = first_ref[(pl.program_id(0) * N_HEADS + head) * pl.num_programs(2) + i] // 2
            l, p0_scr[hh] = probs(hh, jnp.zeros((1, tq), F32), 2 * jj0)
            return jj0, l

        def off_diagonal(hh, jj0, l):
            def pairs(jj, c, n_pairs):
                l, acc = c
                for r in range(n_pairs):
                    a = 2 * (jj + r)
                    l, p1_scr[hh] = probs(hh, l, a + 1, diagonal=False)
                    acc = weigh(hh, acc, p0_scr[hh], a)
                    l, p0_scr[hh] = probs(hh, l, a + 2)
                    acc = weigh(hh, acc, p1_scr[hh], a + 1)
                return l, acc

            n_long = (j0 // 2 - jj0) // 2
            c = (l, jnp.zeros((HEAD_DIM, tq), F32))
            c = lax.fori_loop(0, n_long, lambda t, c: pairs(jj0 + 2 * t, c, 2), c)
            return lax.fori_loop(jj0 + 2 * n_long, j0 // 2, lambda jj, c: pairs(jj, c, 1), c)

        def diagonal(hh, l, acc):
            later = [probs(hh, l[:, tk * dd:], j0 + dd, tk * dd, diagonal=True)
                     for dd in range(1, n_sub)]
            acc = weigh(hh, acc, p0_scr[hh], j0)
            for dd in range(1, n_sub):
                c0 = tk * dd
                l_part, p = later[dd - 1]
                l = jnp.concatenate([l[:, :c0], l_part], axis=1)
                acc = jnp.concatenate([acc[:, :c0], weigh(hh, acc[:, c0:], p, j0 + dd)], axis=1)
            return acc / l

        outs = []
        jj0, l = fill(0)
        for hh in heads:
            l, acc = off_diagonal(hh, jj0, l)
            if hh + 1 < HEADS_PER_STEP:
                jj0, l_next = fill(hh + 1)
            outs.append(diagonal(hh, l, acc))
            l = l_next if hh + 1 < HEADS_PER_STEP else None
        o_ref[0] = jnp.concatenate(outs, axis=0).T.astype(BF16)
        return

    def consume(hh, carry, s, j, diagonal):
        if diagonal:
            s_tri = jnp.where(causal, s[:, :tk], -1e30)
            s = s_tri if s.shape[1] == tk else jnp.concatenate([s_tri, s[:, tk:]], axis=1)
        vt = vt_ref[0, j, HEAD_DIM * hh:HEAD_DIM * (hh + 1), :]
        if online:
            m, l, acc = carry
            m_new = jnp.maximum(m, jnp.max(s, axis=0, keepdims=True))
            alpha = jnp.exp2(m - m_new)
            p = jnp.exp2(s - m_new)
            l = alpha * l + jnp.sum(p, axis=0, keepdims=True)
            acc = alpha * acc + jnp.dot(vt, p.astype(BF16), preferred_element_type=F32)
            return m_new, l, acc
        l, acc = carry
        p = jnp.exp2(s)
        l = l + jnp.sum(p, axis=0, keepdims=True)
        acc = acc + jnp.dot(vt, p.astype(BF16), preferred_element_type=F32)
        return l, acc

    init = (jnp.zeros((1, tq), F32), jnp.zeros((HEAD_DIM, tq), F32))
    if online:
        init = (jnp.full((1, tq), -1e30, F32),) + init

    j0 = n_sub * i

    def first_scores(hh):
        if online:
            jj0 = 0
        else:
            head = HEADS_PER_STEP * pl.program_id(1) + hh
            jj0 = first_ref[(pl.program_id(0) * N_HEADS + head) * pl.num_programs(2) + i] // 2
        s0_scr[hh] = scores(hh, 2 * jj0)
        return jj0

    def off_diagonal(hh, jj0):
        def pairs(jj, c, n_pairs):
            for r in range(n_pairs):
                s1_scr[hh] = scores(hh, 2 * (jj + r) + 1)
                c = consume(hh, c, s0_scr[hh], 2 * (jj + r), False)
                s0_scr[hh] = scores(hh, 2 * (jj + r) + 2)
                c = consume(hh, c, s1_scr[hh], 2 * (jj + r) + 1, False)
            return c

        n_long = (j0 // 2 - jj0) // 2
        c = lax.fori_loop(0, n_long, lambda t, c: pairs(jj0 + 2 * t, c, 2), init)
        return lax.fori_loop(jj0 + 2 * n_long, j0 // 2, lambda jj, c: pairs(jj, c, 1), c)

    def diagonal(hh, carry):
        later = [scores(hh, j0 + dd, tk * dd) for dd in range(1, n_sub)]
        carry = consume(hh, carry, s0_scr[hh], j0, True)
        for dd in range(1, n_sub):
            c0 = tk * dd
            part = consume(hh, tuple(c[:, c0:] for c in carry), later[dd - 1], j0 + dd, True)
            carry = tuple(jnp.concatenate([c[:, :c0], pc], axis=1) for c, pc in zip(carry, part))
        return carry[-1] / carry[-2]

    outs = []
    jj0 = first_scores(0)
    for hh in heads:
        carry = off_diagonal(hh, jj0)
        if hh + 1 < HEADS_PER_STEP:
            jj0 = first_scores(hh + 1)
        outs.append(diagonal(hh, carry))
    o_t = jnp.concatenate(outs, axis=0)
    o_ref[0] = o_t.T.astype(BF16)


def _attention(first, qt, qbt, k, kb, vt, weights, *, tq, tk):
    b, s, d = k.shape
    assert (tq // tk) % 2 == 0 and s % tq == 0
    n_pairs = d // LANES
    nq = s // tq
    nk = s // tk
    n_steps = b * n_pairs * nq
    qt_blk = pl.BlockSpec((1, LANES, tq), lambda bi, hp, i, first: (bi, hp, i))
    k_all = pl.BlockSpec((1, s, LANES), lambda bi, hp, i, first: (bi, 0, hp))
    step = lambda bi, hp, i, first: ((bi * n_pairs + hp) * nq + i, 0)

    flat = [w.reshape(16 * n_steps, -1) for w in weights]
    slice_specs = [pl.BlockSpec((16, w.shape[1]), step) for w in flat]
    assert all(w.shape[1] % LANES == 0 for w in flat)

    outs = pl.pallas_call(
        functools.partial(_attn_kernel, tq=tq, tk=tk, n_cast=len(flat)),
        grid_spec=pltpu.PrefetchScalarGridSpec(
            num_scalar_prefetch=1,
            grid=(b, n_pairs, nq),
            in_specs=[qt_blk, qt_blk, k_all, k_all,
                      pl.BlockSpec((1, nk, LANES, tk), lambda bi, hp, i, first: (bi, 0, hp, 0)),
                      *slice_specs],
            out_specs=[pl.BlockSpec((1, tq, LANES), lambda bi, hp, i, first: (bi, i, hp)),
                       *slice_specs],
            scratch_shapes=[pltpu.VMEM((HEADS_PER_STEP, tk, tq), F32),
                            pltpu.VMEM((HEADS_PER_STEP, tk, tq), F32),
                            pltpu.VMEM((HEADS_PER_STEP, tk, tq), BF16),
                            pltpu.VMEM((HEADS_PER_STEP, tk, tq), BF16)]),
        out_shape=[jax.ShapeDtypeStruct((b, s, d), BF16)]
        + [jax.ShapeDtypeStruct(w.shape, BF16) for w in flat],
        compiler_params=pltpu.CompilerParams(
            dimension_semantics=("arbitrary", "arbitrary", "arbitrary"),
            vmem_limit_bytes=VMEM_LIMIT),
        name="fox_attention",
    )(first, qt, qbt, k, kb, vt, *flat)
    return outs[0], [w_bf.reshape(w.shape) for w_bf, w in zip(outs[1:], weights)]


def _permute_token_blocks(perm, a):
    w = perm.shape[0]
    return jnp.concatenate(
        [jnp.dot(perm, a[w * i:w * (i + 1)], preferred_element_type=F32).astype(BF16)
         for i in range(a.shape[0] // w)], axis=0)


def _mix_mlp_kernel(x_ref, a_ref, wmix_ref, g_ref, w1_ref, w2_ref, *rest, glu, d, ff_chunk):
    if glu:
        a = _permute_token_blocks(rest[0][...], a_ref[...])
        mix = jnp.dot(a, wmix_ref[...], preferred_element_type=F32)
        mix = mix[:, :d] * jax.nn.sigmoid(mix[:, d:])
    else:
        mix = jnp.dot(a_ref[...], wmix_ref[...], preferred_element_type=F32)
    x1 = x_ref[...] + mix
    h = _rms_norm(x1, g_ref[...]).astype(BF16)
    acc = x1
    for c in range(w1_ref.shape[1] // ff_chunk):
        sl = slice(ff_chunk * c, ff_chunk * (c + 1))
        hid = jnp.maximum(jnp.dot(h, w1_ref[:, sl], preferred_element_type=F32), 0.0)
        acc = acc + jnp.dot((hid * hid).astype(BF16), w2_ref[sl, :], preferred_element_type=F32)
    if glu:
        _, o_ref = rest
    else:
        perm_ref, g_next_ref, w_next_ref, o_ref, u_ref = rest
        h_next = _rms_norm(acc, g_next_ref[...]).astype(BF16)
        u = jnp.dot(h_next, w_next_ref[...], preferred_element_type=F32).astype(BF16)
        u_ref[...] = _permute_token_blocks(perm_ref[...], u)
    o_ref[...] = acc


def _mix_mlp(x, a, wmix, g, w1, w2, perm, next_proj=None, *, layer, glu, tm, ff_chunk=1024):
    t, d = x.shape
    row = lambda i: (i, 0)
    single = pl.Buffered(1)
    wspec = lambda shape: pl.BlockSpec(shape, lambda i: (0, 0), pipeline_mode=single)
    lspec = lambda w: pl.BlockSpec((None,) + w.shape[1:], lambda i: (layer, 0, 0),
                                   pipeline_mode=single)
    operands = [x, a, wmix, g, w1, w2, perm]
    in_specs = [pl.BlockSpec((tm, d), row), pl.BlockSpec((tm, d), row),
                wspec(wmix.shape), wspec((1, d)), lspec(w1), lspec(w2), wspec(perm.shape)]
    out_specs = pl.BlockSpec((tm, d), row)
    out_shape = jax.ShapeDtypeStruct((t, d), F32)
    if next_proj is not None:
        operands += list(next_proj)
        in_specs += [wspec(w.shape) for w in next_proj]
        n = next_proj[1].shape[1]
        out_specs = [out_specs, pl.BlockSpec((tm, n), row)]
        out_shape = [out_shape, jax.ShapeDtypeStruct((t, n), BF16)]
    return pl.pallas_call(
        functools.partial(_mix_mlp_kernel, glu=glu, d=d, ff_chunk=ff_chunk),
        grid=(t // tm,),
        in_specs=in_specs,
        out_specs=out_specs,
        out_shape=out_shape,
        compiler_params=pltpu.CompilerParams(
            dimension_semantics=("arbitrary",), vmem_limit_bytes=VMEM_LIMIT),
        name="mix_glu_mlp" if glu else "mix_mlp",
    )(*operands)


GROUPS_PER_SLAB = LANES // SSM_GROUP


def _piece_transpose(arrs, piece):
    arrs = list(arrs)
    dist = GROUPS_PER_SLAB // 2
    while dist >= 1:
        keep = (piece & dist) == 0
        shift = SSM_GROUP * dist
        for i in range(GROUPS_PER_SLAB):
            if i & dist:
                continue
            a, b = arrs[i], arrs[i + dist]
            arrs[i] = jnp.where(keep, a, pltpu.roll(b, shift, axis=1))
            arrs[i + dist] = jnp.where(keep, pltpu.roll(a, LANES - shift, axis=1), b)
        dist //= 2
    return arrs


def _ssm_kernel(u_ref, cc_ref, bt_ref, pw1_ref, pw2_ref, zoh_ref, a1_ref, a2_ref,
                dv_ref, z_ref, ug_scr, zg_scr, *, n_chunks):
    L = SSM_CHUNK
    rows = u_ref.shape[0] // L
    w = L * SSM_GROUP
    half = SSM_STATE
    nb = math.gcd(rows // L, 8)
    piece = lax.broadcasted_iota(jnp.int32, (nb * L, LANES), 1) // SSM_GROUP

    def relayout_in(rb, carry):
        t0 = pl.multiple_of(rb * nb * w, nb * w)
        r0 = pl.multiple_of(rb * nb * L, nb * L)
        by_pos = u_ref[pl.ds(t0, nb * w), :].astype(F32)
        for hf in range(L // GROUPS_PER_SLAB):
            arrs = []
            for k in range(GROUPS_PER_SLAB):
                pos = GROUPS_PER_SLAB * hf + k
                arrs.append(jnp.concatenate(
                    [by_pos[w * blk + L * pos:w * blk + L * (pos + 1)] for blk in range(nb)],
                    axis=0))
            arrs = _piece_transpose(arrs, piece)
            for g in range(GROUPS_PER_SLAB):
                ug_scr[g, pl.ds(r0, nb * L), LANES * hf:LANES * (hf + 1)] = arrs[g].astype(BF16)
        return carry

    lax.fori_loop(0, rows // (nb * L), relayout_in, 0)

    chunk = lax.broadcasted_iota(jnp.int32, (rows, LANES), 0) % n_chunks
    lane_w = lax.broadcasted_iota(jnp.int32, (SSM_GROUP, w), 1)
    low = lax.broadcasted_iota(jnp.int32, (1, LANES), 1) < half
    sign = jnp.where(low, 1.0, -1.0)

    def operators(g):
        cc = cc_ref[g]
        bt = bt_ref[g]
        pw1 = pw1_ref[g]
        pw2 = pw2_ref[g]
        zoh = zoh_ref[g]
        bbar = bt * zoh[0:1] + pltpu.roll(bt, half, axis=1) * zoh[1:2]
        bbar_sw = pltpu.roll(bbar, half, axis=1)
        cc_sw = pltpu.roll(cc, half, axis=1)
        ca = [cc * pw1[k:k + 1] + cc_sw * pw2[k:k + 1] for k in range(L + 1)]
        k_all = lax.dot_general(bbar * sign, jnp.concatenate(ca[:L], axis=0), NT_DIMS,
                                precision=lax.Precision.HIGHEST, preferred_element_type=F32)
        m_intra, w_state, w_out_t = [], [], []
        for s in range(L):
            blk = k_all if s == 0 else jnp.where(
                lane_w >= SSM_GROUP * s, pltpu.roll(k_all, SSM_GROUP * s, axis=1), 0.0)
            m_intra.append(blk.astype(BF16))
            k = L - 1 - s
            w_state.append(bbar * pw1[k:k + 1] + bbar_sw * pw2[k:k + 1])
            w_out_t.append(ca[s + 1] * sign)
        return jnp.concatenate(m_intra, axis=0), w_state, w_out_t

    def halves(pieces, second):
        firsts, seconds = [], []
        for piece in pieces:
            swapped = pltpu.roll(piece, half, axis=1)
            if second:
                firsts.append(jnp.where(low, 0.0, swapped))
                seconds.append(jnp.where(low, 0.0, piece))
            else:
                firsts.append(jnp.where(low, piece, 0.0))
                seconds.append(jnp.where(low, swapped, 0.0))
        return firsts, seconds

    def group_pair(pr, carry):
        gs = (2 * pr, 2 * pr + 1)
        m_intra, ws_re, ws_im, wo_t = [], [], [], []
        for second, g in enumerate(gs):
            mi, w_state, w_out_t = operators(g)
            m_intra.append(mi)
            re, im = halves(w_state, second)
            ws_re += re
            ws_im += im
            re, im = halves(w_out_t, second)
            wo_t.append(jnp.concatenate(
                [jnp.concatenate([r, m], axis=1) for r, m in zip(re, im)], axis=0).astype(BF16))
        us = [ug_scr[g] for g in gs]
        uu = jnp.concatenate(us, axis=1)
        xr = jnp.dot(uu, jnp.concatenate(ws_re, axis=0).astype(BF16), preferred_element_type=F32)
        xi = jnp.dot(uu, jnp.concatenate(ws_im, axis=0).astype(BF16), preferred_element_type=F32)
        a1 = [a1_ref[g] for g in gs]
        a2 = [a2_ref[g] for g in gs]
        step = 1
        j = 0
        while step < n_chunks:
            ar = jnp.where(low, a1[0][j:j + 1], a1[1][j:j + 1])
            ai = jnp.where(low, -a2[0][j:j + 1], a2[1][j:j + 1])
            sr = jnp.where(chunk >= step, pltpu.roll(xr, step, axis=0), 0.0)
            si = jnp.where(chunk >= step, pltpu.roll(xi, step, axis=0), 0.0)
            xr = xr + sr * ar - si * ai
            xi = xi + si * ar + sr * ai
            step *= 2
            j += 1
        x_in = jnp.concatenate([jnp.where(chunk >= 1, pltpu.roll(x, 1, axis=0), 0.0)
                                for x in (xr, xi)], axis=1).astype(BF16)
        for second, g in enumerate(gs):
            y = jnp.dot(us[second], m_intra[second], preferred_element_type=F32)
            y = y + lax.dot_general(x_in, wo_t[second], NT_DIMS, preferred_element_type=F32)
            y = y + us[second].astype(F32) * dv_ref[g]
            zg_scr[g] = jax.nn.gelu(y).astype(BF16)
        return carry

    lax.fori_loop(0, GROUPS_PER_SLAB // 2, group_pair, 0)

    def relayout_out(rb, carry):
        t0 = pl.multiple_of(rb * nb * w, nb * w)
        r0 = pl.multiple_of(rb * nb * L, nb * L)
        by_pos = []
        for hf in range(L // GROUPS_PER_SLAB):
            arrs = [zg_scr[g, pl.ds(r0, nb * L), LANES * hf:LANES * (hf + 1)].astype(F32)
                    for g in range(GROUPS_PER_SLAB)]
            by_pos += _piece_transpose(arrs, piece)
        for blk in range(nb):
            z_ref[pl.ds(t0 + w * blk, w), :] = jnp.concatenate(
                [arr[L * blk:L * (blk + 1)] for arr in by_pos], axis=0).astype(BF16)
        return carry

    lax.fori_loop(0, rows // (nb * L), relayout_out, 0)


def _ssm_scan(u, cc, bt, pw1, pw2, zoh, a1, a2, dv, *, n_chunks):
    t, d = u.shape
    L = SSM_CHUNK
    rows = t // L
    w = L * SSM_GROUP
    gps = GROUPS_PER_SLAB
    slab = lambda j: (j, 0, 0)
    pspec = lambda arr: pl.BlockSpec((gps,) + arr.shape[1:], slab)
    return pl.pallas_call(
        functools.partial(_ssm_kernel, n_chunks=n_chunks),
        grid=(d // LANES,),
        in_specs=[pl.BlockSpec((t, LANES), lambda j: (0, j)),
                  pspec(cc), pspec(bt), pspec(pw1), pspec(pw2), pspec(zoh), pspec(a1), pspec(a2),
                  pspec(dv)],
        out_specs=pl.BlockSpec((t, LANES), lambda j: (0, j)),
        out_shape=jax.ShapeDtypeStruct((t, d), BF16),
        scratch_shapes=[pltpu.VMEM((gps, rows, w), BF16), pltpu.VMEM((gps, rows, w), BF16)],
        compiler_params=pltpu.CompilerParams(
            dimension_semantics=("arbitrary",), vmem_limit_bytes=VMEM_LIMIT),
        name="s5_scan",
    )(u, cc, bt, pw1, pw2, zoh, a1, a2, dv)


def _ssm_operators(a_re, a_im, b_re, b_im, c_re, c_im, log_dt, d_skip, n_chunks):
    L = SSM_CHUNK
    g, p = a_re.shape
    dt = jnp.exp(log_dt)[:, None]
    lam_re, lam_im = dt * a_re, dt * a_im

    def powers(ks):
        ks = jnp.asarray(ks, F32)[:, None, None]
        mag = jnp.exp(ks * lam_re)
        return mag * jnp.cos(ks * lam_im), mag * jnp.sin(ks * lam_im)

    def patterns(re, im):
        return (jnp.concatenate([re, re], axis=2).transpose(1, 0, 2),
                jnp.concatenate([-im, im], axis=2).transpose(1, 0, 2))

    pr, pi = powers(range(L + 1))
    num_re, num_im = pr[1] - 1.0, pi[1]
    den = a_re * a_re + a_im * a_im
    s_re = (num_re * a_re + num_im * a_im) / den
    s_im = (num_im * a_re - num_re * a_im) / den
    pw1, pw2 = patterns(pr, pi)
    zoh = jnp.concatenate(patterns(s_re[None], s_im[None]), axis=1)
    n_steps = max(1, int(math.log2(n_chunks)))
    a1, a2 = patterns(*powers([L * 2 ** j for j in range(n_steps)]))
    cc = jnp.concatenate([c_re, c_im], axis=2)
    bt = jnp.concatenate([b_re.transpose(0, 2, 1), b_im.transpose(0, 2, 1)], axis=2)
    dv = jnp.tile(d_skip.reshape(g, 1, SSM_GROUP), (1, L, 1)).reshape(g, 1, L * SSM_GROUP)
    return cc, bt, pw1, pw2, zoh, a1, a2, dv


def kernel(x, norm_mix_g, norm_mlp_g, attn_w_in, attn_b_f, attn_q_g, attn_k_g, attn_w_out,
           ssm_w_in, ssm_a_re, ssm_a_im, ssm_b_re, ssm_b_im, ssm_c_re, ssm_c_im, ssm_log_dt,
           ssm_d, ssm_w_glu, mlp_w1, mlp_w2):
    b, s, d = x.shape
    t = b * s
    tk = min(512, s // 2)
    tq = 2 * tk
    tm_proj = min(512, s)
    tm_mlp = min(512, t)

    w_in = attn_w_in[0]
    w_f = jnp.pad(w_in[:, 3 * d:], ((0, 0), (0, LANES - N_HEADS)))
    wkf = jnp.concatenate([w_in[:, d:2 * d], w_f], axis=1).astype(BF16)
    wqvt = jnp.concatenate([w_in[:, :d], w_in[:, 2 * d:3 * d]], axis=1).T.astype(BF16)
    blk = np.arange(MXU_DIM) // HEAD_DIM
    bd = jnp.asarray((blk[:, None] == blk[None, :]) * (1.0 / HEAD_DIM), BF16)
    gain_k = jnp.tile(attn_k_g[0], N_HEADS).reshape(1, d)
    gain_qt = jnp.broadcast_to(
        (jnp.tile(attn_q_g[0], N_HEADS) * (LOG2E / math.sqrt(HEAD_DIM)))[:, None], (d, LANES))
    bfp = jnp.pad(attn_b_f[0], (0, LANES - N_HEADS)).reshape(1, LANES)
    tri = jnp.asarray(np.arange(tm_proj)[:, None] >= np.arange(tm_proj)[None, :], BF16)
    heads = np.arange(N_HEADS)
    place_k = np.zeros((LANES, d), np.float32)
    for i in range(BIAS_PIECES):
        place_k[N_HEADS * i + heads, (heads // HEADS_PER_STEP) * LANES
                + BIAS_PIECES * (heads % HEADS_PER_STEP) + i] = 1.0
    place_qt = np.roll(place_k, BIAS_Q_LANE, axis=1).T
    shift = (LOG2E * math.sqrt(HEAD_DIM) * jnp.max(jnp.abs(attn_q_g[0]))
             * jnp.max(jnp.abs(attn_k_g[0])))
    qt, k, vt, kb, qbt, edge = _qkv_proj(
        x, norm_mix_g[0].reshape(1, d), wkf, wqvt, bd, gain_k, gain_qt, bfp, tri,
        jnp.asarray(place_k, BF16), jnp.asarray(place_qt, BF16), jnp.full((1, LANES), shift, F32),
        tm=tm_proj, tk=tk)
    assert tm_proj == tk
    c_first = edge[:, ::tq // tk, 0, :N_HEADS]
    c_last = edge[:, :, 1, :N_HEADS]
    bound = (c_first[:, :, None, :] - c_last[:, None, :, :]) * LOG2E
    key_blk = jnp.arange(s // tk)[None, None, :, None]
    first = jnp.min(jnp.where(bound < SKIP_LOGIT, s // tk, key_blk), axis=2)
    first = jnp.minimum(first, (tq // tk) * jnp.arange(s // tq)[None, :, None]) // 2 * 2
    first = jnp.concatenate([first.transpose(0, 2, 1).reshape(-1), (shift > MAX_SHIFT)[None]])
    o, (w1, w2, w_out, w_ssm_in, w_glu) = _attention(
        first.astype(jnp.int32), qt, qbt, k, kb, vt,
        [mlp_w1, mlp_w2, attn_w_out, ssm_w_in, ssm_w_glu], tq=tq, tk=tk)
    tok = np.arange(SSM_CHUNK * SSM_CHUNK)
    perm = jnp.asarray(
        tok[None, :] == (tok[:, None] % SSM_CHUNK) * SSM_CHUNK + tok[:, None] // SSM_CHUNK, BF16)
    x2, u = _mix_mlp(x.reshape(t, d), o.reshape(t, d), w_out[0],
                     norm_mlp_g[0].reshape(1, d), w1, w2,
                     perm, (norm_mix_g[1].reshape(1, d), w_ssm_in[0]),
                     layer=0, glu=False, tm=tm_mlp)

    n_chunks = s // SSM_CHUNK
    ops = _ssm_operators(ssm_a_re[0], ssm_a_im[0], ssm_b_re[0], ssm_b_im[0], ssm_c_re[0],
                         ssm_c_im[0], ssm_log_dt[0], ssm_d[0], n_chunks)
    z = _ssm_scan(u, *ops, n_chunks=n_chunks)
    x3 = _mix_mlp(x2, z, w_glu[0], norm_mlp_g[1].reshape(1, d),
                  w1, w2, perm, layer=1, glu=True, tm=tm_mlp)
    return x3.reshape(b, s, d)
```

```python
import functools
import math

import jax
import jax.numpy as jnp
import numpy as np
from jax import lax
from jax.experimental import pallas as pl
from jax.experimental.pallas import tpu as pltpu

F32 = jnp.float32
BF16 = jnp.bfloat16

N_HEADS = 16
HEAD_DIM = 64
SSM_GROUP = 16
SSM_STATE = 64
SSM_CHUNK = 16
EPS = 1e-6
LOG2E = 1.4426950408889634

LANES = 128
MXU_DIM = 256
HEADS_PER_STEP = LANES // HEAD_DIM
BIAS_PIECES = 3
VMEM_LIMIT = 56 * 1024 * 1024

NT_DIMS = (((1,), (1,)), ((), ()))


def _rms_norm(x, g):
    ms = jnp.mean(x * x, axis=-1, keepdims=True)
    return x * lax.rsqrt(ms + EPS) * g


def _const_spec(shape):
    zeros = (0,) * len(shape)
    return pl.BlockSpec(shape, lambda *_: zeros)


def _bf16_pieces(val):
    pieces = jnp.zeros_like(val)
    rem = val
    for n in range(BIAS_PIECES):
        piece = rem.astype(BF16).astype(F32)
        rem = rem - piece
        pieces = pieces + (piece if n == 0 else pltpu.roll(piece, N_HEADS * n, axis=1))
    return pieces


def _qkv_kernel(x_ref, g_ref, wkf_ref, wqvt_ref, bd_ref, gain_k_ref, gain_qt_ref, bf_ref, tri_ref,
                place_k_ref, place_qt_ref, shift_ref,
                qt_ref, k_ref, vt_ref, kb_ref, qbt_ref, edge_ref, carry_ref, *, tm, tk, d):
    @pl.when(pl.program_id(1) == 0)
    def _():
        carry_ref[...] = jnp.zeros_like(carry_ref)

    h = _rms_norm(x_ref[0], g_ref[...]).astype(BF16)

    y = jnp.dot(h, wkf_ref[...], preferred_element_type=F32)
    vt = lax.dot_general(wqvt_ref[d:, :], h, NT_DIMS, preferred_element_type=F32).astype(BF16)
    for j in range(tm // tk):
        vt_ref[0, j] = vt[:, tk * j:tk * (j + 1)]
    qt = lax.dot_general(wqvt_ref[:d, :], h, NT_DIMS, preferred_element_type=F32)

    for t in range(d // MXU_DIM):
        sl = slice(MXU_DIM * t, MXU_DIM * (t + 1))
        tile = y[:, sl]
        ms = jnp.dot((tile * tile).astype(BF16), bd_ref[...], preferred_element_type=F32)
        k_ref[0, :, sl] = (tile * lax.rsqrt(ms + EPS) * gain_k_ref[:, sl]).astype(BF16)
        tile = qt[sl, :]
        ms = jnp.dot(bd_ref[...], (tile * tile).astype(BF16), preferred_element_type=F32)
        gain = jnp.tile(gain_qt_ref[sl, :], (1, tm // LANES))
        qt_ref[0, sl, :] = (tile * lax.rsqrt(ms + EPS) * gain).astype(BF16)

    f = y[:, d:] + bf_ref[...]
    log_f = jnp.minimum(f, 0.0) - jnp.log1p(jnp.exp(-jnp.abs(f)))
    lane = lax.broadcasted_iota(jnp.int32, log_f.shape, 1)
    part = jnp.dot(tri_ref[...], _bf16_pieces(jnp.where(lane < N_HEADS, log_f, 0.0)).astype(BF16),
                   preferred_element_type=F32)
    cs = carry_ref[...] + part
    for n in range(1, BIAS_PIECES):
        cs = cs + pltpu.roll(part, LANES - N_HEADS * n, axis=1)
    carry_ref[...] = cs[tm - 1:tm, :]
    edge_ref[0, 0] = jnp.concatenate(
        [cs[0:1, :], cs[tm - 1:tm, :], jnp.zeros((6, LANES), F32)], axis=0)
    c2 = jnp.where(lane < N_HEADS, cs * LOG2E, 0.0)
    kb_ref[0] = jnp.dot(_bf16_pieces(-c2).astype(BF16), place_k_ref[...],
                        preferred_element_type=F32).astype(BF16)
    q_pieces = _bf16_pieces(jnp.where(lane < N_HEADS, c2 - shift_ref[...], 0.0))
    qbt_ref[0] = jnp.dot(place_qt_ref[...], q_pieces.T.astype(BF16),
                         preferred_element_type=F32).astype(BF16)


def _qkv_proj(x, g, wkf, wqvt, bd, gain_k, gain_qt, bfp, tri, place_k, place_qt, shift, *, tm, tk):
    b, s, d = x.shape
    row = lambda bi, si: (bi, si, 0)
    col = lambda bi, si: (bi, 0, si)
    consts = (g, wkf, wqvt, bd, gain_k, gain_qt, bfp, tri, place_k, place_qt, shift)
    return pl.pallas_call(
        functools.partial(_qkv_kernel, tm=tm, tk=tk, d=d),
        grid=(b, s // tm),
        in_specs=[pl.BlockSpec((1, tm, d), row)] + [_const_spec(c.shape) for c in consts],
        out_specs=[pl.BlockSpec((1, d, tm), col),
                   pl.BlockSpec((1, tm, d), row),
                   pl.BlockSpec((1, tm // tk, d, tk), lambda bi, si: (bi, si, 0, 0)),
                   pl.BlockSpec((1, tm, d), row),
                   pl.BlockSpec((1, d, tm), col),
                   pl.BlockSpec((1, 1, 8, LANES), lambda bi, si: (bi, si, 0, 0))],
        out_shape=[jax.ShapeDtypeStruct((b, d, s), BF16),
                   jax.ShapeDtypeStruct((b, s, d), BF16),
                   jax.ShapeDtypeStruct((b, s // tk, d, tk), BF16),
                   jax.ShapeDtypeStruct((b, s, d), BF16),
                   jax.ShapeDtypeStruct((b, d, s), BF16),
                   jax.ShapeDtypeStruct((b, s // tm, 8, LANES), F32)],
        scratch_shapes=[pltpu.VMEM((1, LANES), F32)],
        compiler_params=pltpu.CompilerParams(
            dimension_semantics=("arbitrary", "arbitrary"), vmem_limit_bytes=VMEM_LIMIT),
        name="qkv_proj",
    )(x, *consts)


BIAS_Q_LANE = 8
MAX_SHIFT = 48.0
SKIP_LOGIT = -200.0


def _attn_kernel(first_ref, *refs, tq, tk, n_cast):
    n_in = 5
    attn_refs = (*refs[:n_in], refs[n_in + n_cast], *refs[n_in + 2 * n_cast + 1:])
    for w_ref, w_bf_ref in zip(refs[n_in:n_in + n_cast],
                               refs[n_in + n_cast + 1:n_in + 2 * n_cast + 1]):
        w_bf_ref[...] = w_ref[...].astype(BF16)

    use_online = first_ref[first_ref.shape[0] - 1]

    @pl.when(use_online == 0)
    def _():
        _attn_body(first_ref, *attn_refs, tq=tq, tk=tk, online=False)

    @pl.when(use_online != 0)
    def _():
        _attn_body(first_ref, *attn_refs, tq=tq, tk=tk, online=True)


def _attn_body(first_ref, qt_ref, qbt_ref, k_ref, kb_ref, vt_ref, o_ref, s0_scr, s1_scr, p0_scr,
               p1_scr, *, tq, tk, online):
    i = pl.program_id(2)
    n_sub = tq // tk
    qt = qt_ref[0]
    qbt = qbt_ref[0]
    row = lax.broadcasted_iota(jnp.int32, (LANES, tq), 0)
    lane_k = lax.broadcasted_iota(jnp.int32, (tk, LANES), 1)
    q_cols = (lane_k >= BIAS_Q_LANE) & (lane_k < BIAS_Q_LANE + HEADS_PER_STEP * BIAS_PIECES)
    causal = (lax.broadcasted_iota(jnp.int32, (tk, tk), 0)
              <= lax.broadcasted_iota(jnp.int32, (tk, tk), 1))
    one = jnp.ones((), BF16)
    zero = jnp.zeros((), BF16)
    heads = range(HEADS_PER_STEP)
    qats = []
    for hh in heads:
        q_h = jnp.where((row >= HEAD_DIM * hh) & (row < HEAD_DIM * (hh + 1)), qt, zero)
        k_side = (row >= BIAS_PIECES * hh) & (row < BIAS_PIECES * (hh + 1))
        q_lo = BIAS_Q_LANE + BIAS_PIECES * hh
        q_side = (row >= q_lo) & (row < q_lo + BIAS_PIECES)
        qats.append(jnp.concatenate(
            [q_h, jnp.where(k_side, one, jnp.where(q_side, qbt, zero))], axis=0))

    def scores(hh, j, c0=0):
        off = pl.multiple_of(j * tk, tk)
        kbias = jnp.where(q_cols, one, kb_ref[0, pl.ds(off, tk), :])
        ka = jnp.concatenate([k_ref[0, pl.ds(off, tk), :], kbias], axis=1)
        return jnp.dot(ka, qats[hh][:, c0:], preferred_element_type=F32)

    if not online:
        j0 = n_sub * i
        def probs(hh, l, j, c0=0, diagonal=None):
            s = scores(hh, j, c0)
            if diagonal is not False:
                head = jnp.where(causal, s[:, :tk], -1e30)
                if diagonal is None:
                    head = jnp.where(j < j0, s[:, :tk], head)
                s = head if s.shape[1] == tk else jnp.concatenate([head, s[:, tk:]], axis=1)
            p = jnp.exp2(s)
            return l + jnp.sum(p, axis=0, keepdims=True), p.astype(BF16)

        def weigh(hh, acc, p, j):
            vt = vt_ref[0, j, HEAD_DIM * hh:HEAD_DIM * (hh + 1), :]
            return acc + jnp.dot(vt, p, preferred_element_type=F32)

        def fill(hh):
            head = HEADS_PER_STEP * pl.program_id(1) + hh
            jj0 = first_ref[(pl.program_id(0) * N_HEADS + head) * pl.num_programs(2) + i] // 2
            l, p0_scr[hh] = probs(hh, jnp.zeros((1, tq), F32), 2 * jj0)
            return jj0, l

        def off_diagonal(hh, jj0, l):
            def pairs(jj, c, n_pairs):
                l, acc = c
                for r in range(n_pairs):
                    a = 2 * (jj + r)
                    l, p1_scr[hh] = probs(hh, l, a + 1, diagonal=False)
                    acc = weigh(hh, acc, p0_scr[hh], a)
                    l, p0_scr[hh] = probs(hh, l, a + 2)
                    acc = weigh(hh, acc, p1_scr[hh], a + 1)
                return l, acc

            n_long = (j0 // 2 - jj0) // 2
            c = (l, jnp.zeros((HEAD_DIM, tq), F32))
            c = lax.fori_loop(0, n_long, lambda t, c: pairs(jj0 + 2 * t, c, 2), c)
            return lax.fori_loop(jj0 + 2 * n_long, j0 // 2, lambda jj, c: pairs(jj, c, 1), c)

        def diagonal(hh, l, acc):
            later = [probs(hh, l[:, tk * dd:], j0 + dd, tk * dd, diagonal=True)
                     for dd in range(1, n_sub)]
            acc = weigh(hh, acc, p0_scr[hh], j0)
            for dd in range(1, n_sub):
                c0 = tk * dd
                l_part, p = later[dd - 1]
                l = jnp.concatenate([l[:, :c0], l_part], axis=1)
                acc = jnp.concatenate([acc[:, :c0], weigh(hh, acc[:, c0:], p, j0 + dd)], axis=1)
            return acc / l

        outs = []
        jj0, l = fill(0)
        for hh in heads:
            l, acc = off_diagonal(hh, jj0, l)
            if hh + 1 < HEADS_PER_STEP:
                jj0, l_next = fill(hh + 1)
            outs.append(diagonal(hh, l, acc))
            l = l_next if hh + 1 < HEADS_PER_STEP else None
        o_ref[0] = jnp.concatenate(outs, axis=0).T.astype(BF16)
        return

    def consume(hh, carry, s, j, diagonal):
        if diagonal:
            s_tri = jnp.where(causal, s[:, :tk], -1e30)
            s = s_tri if s.shape[1] == tk else jnp.concatenate([s_tri, s[:, tk:]], axis=1)
        vt = vt_ref[0, j, HEAD_DIM * hh:HEAD_DIM * (hh + 1), :]
        if online:
            m, l, acc = carry
            m_new = jnp.maximum(m, jnp.max(s, axis=0, keepdims=True))
            alpha = jnp.exp2(m - m_new)
            p = jnp.exp2(s - m_new)
            l = alpha * l + jnp.sum(p, axis=0, keepdims=True)
            acc = alpha * acc + jnp.dot(vt, p.astype(BF16), preferred_element_type=F32)
            return m_new, l, acc
        l, acc = carry
        p = jnp.exp2(s)
        l = l + jnp.sum(p, axis=0, keepdims=True)
        acc = acc + jnp.dot(vt, p.astype(BF16), preferred_element_type=F32)
        return l, acc

    init = (jnp.zeros((1, tq), F32), jnp.zeros((HEAD_DIM, tq), F32))
    if online:
        init = (jnp.full((1, tq), -1e30, F32),) + init

    j0 = n_sub * i

    def first_scores(hh):
        if online:
            jj0 = 0
        else:
            head = HEADS_PER_STEP * pl.program_id(1) + hh
            jj0 = first_ref[(pl.program_id(0) * N_HEADS + head) * pl.num_programs(2) + i] // 2
        s0_scr[hh] = scores(hh, 2 * jj0)
        return jj0

    def off_diagonal(hh, jj0):
        def pairs(jj, c, n_pairs):
            for r in range(n_pairs):
                s1_scr[hh] = scores(hh, 2 * (jj + r) + 1)
                c = consume(hh, c, s0_scr[hh], 2 * (jj + r), False)
                s0_scr[hh] = scores(hh, 2 * (jj + r) + 2)
                c = consume(hh, c, s1_scr[hh], 2 * (jj + r) + 1, False)
            return c

        n_long = (j0 // 2 - jj0) // 2
        c = lax.fori_loop(0, n_long, lambda t, c: pairs(jj0 + 2 * t, c, 2), init)
        return lax.fori_loop(jj0 + 2 * n_long, j0 // 2, lambda jj, c: pairs(jj, c, 1), c)

    def diagonal(hh, carry):
        later = [scores(hh, j0 + dd, tk * dd) for dd in range(1, n_sub)]
        carry = consume(hh, carry, s0_scr[hh], j0, True)
        for dd in range(1, n_sub):
            c0 = tk * dd
            part = consume(hh, tuple(c[:, c0:] for c in carry), later[dd - 1], j0 + dd, True)
            carry = tuple(jnp.concatenate([c[:, :c0], pc], axis=1) for c, pc in zip(carry, part))
        return carry[-1] / carry[-2]

    outs = []
    jj0 = first_scores(0)
    for hh in heads:
        carry = off_diagonal(hh, jj0)
        if hh + 1 < HEADS_PER_STEP:
            jj0 = first_scores(hh + 1)
        outs.append(diagonal(hh, carry))
    o_t = jnp.concatenate(outs, axis=0)
    o_ref[0] = o_t.T.astype(BF16)


def _attention(first, qt, qbt, k, kb, vt, weights, *, tq, tk):
    b, s, d = k.shape
    assert (tq // tk) % 2 == 0 and s % tq == 0
    n_pairs = d // LANES
    nq = s // tq
    nk = s // tk
    n_steps = b * n_pairs * nq
    qt_blk = pl.BlockSpec((1, LANES, tq), lambda bi, hp, i, first: (bi, hp, i))
    k_all = pl.BlockSpec((1, s, LANES), lambda bi, hp, i, first: (bi, 0, hp))
    step = lambda bi, hp, i, first: ((bi * n_pairs + hp) * nq + i, 0)

    flat = [w.reshape(-1, w.shape[-1]) for w in weights]
    slice_specs = []
    for w in flat:
        rows = max(16, w.shape[0] // n_steps)
        assert w.shape[0] % rows == 0 and n_steps % (w.shape[0] // rows) == 0
        steps_per_block = n_steps // (w.shape[0] // rows)
        slice_specs.append(pl.BlockSpec(
            (rows, w.shape[1]),
            lambda bi, hp, i, first, spb=steps_per_block: (step(bi, hp, i, first)[0] // spb, 0)))

    outs = pl.pallas_call(
        functools.partial(_attn_kernel, tq=tq, tk=tk, n_cast=len(flat)),
        grid_spec=pltpu.PrefetchScalarGridSpec(
            num_scalar_prefetch=1,
            grid=(b, n_pairs, nq),
            in_specs=[qt_blk, qt_blk, k_all, k_all,
                      pl.BlockSpec((1, nk, LANES, tk), lambda bi, hp, i, first: (bi, 0, hp, 0)),
                      *slice_specs],
            out_specs=[pl.BlockSpec((1, tq, LANES), lambda bi, hp, i, first: (bi, i, hp)),
                       *slice_specs],
            scratch_shapes=[pltpu.VMEM((HEADS_PER_STEP, tk, tq), F32),
                            pltpu.VMEM((HEADS_PER_STEP, tk, tq), F32),
                            pltpu.VMEM((HEADS_PER_STEP, tk, tq), BF16),
                            pltpu.VMEM((HEADS_PER_STEP, tk, tq), BF16)]),
        out_shape=[jax.ShapeDtypeStruct((b, s, d), BF16)]
        + [jax.ShapeDtypeStruct(w.shape, BF16) for w in flat],
        compiler_params=pltpu.CompilerParams(
            dimension_semantics=("arbitrary", "arbitrary", "arbitrary"),
            vmem_limit_bytes=VMEM_LIMIT),
        name="fox_attention",
    )(first, qt, qbt, k, kb, vt, *flat)
    return outs[0], [w_bf.reshape(w.shape) for w_bf, w in zip(outs[1:], weights)]


def _permute_token_blocks(perm, a):
    w = perm.shape[0]
    return jnp.concatenate(
        [jnp.dot(perm, a[w * i:w * (i + 1)], preferred_element_type=F32).astype(BF16)
         for i in range(a.shape[0] // w)], axis=0)


def _mix_mlp_kernel(x_ref, a_ref, wmix_ref, g_ref, w1_ref, w2_ref, *rest, glu, d, ff_chunk):
    if glu:
        a = _permute_token_blocks(rest[0][...], a_ref[...])
        mix = jnp.dot(a, wmix_ref[...], preferred_element_type=F32)
        mix = mix[:, :d] * jax.nn.sigmoid(mix[:, d:])
    else:
        mix = jnp.dot(a_ref[...], wmix_ref[...], preferred_element_type=F32)
    x1 = x_ref[...] + mix
    h = _rms_norm(x1, g_ref[...]).astype(BF16)
    acc = x1
    for c in range(w1_ref.shape[1] // ff_chunk):
        sl = slice(ff_chunk * c, ff_chunk * (c + 1))
        hid = jnp.maximum(jnp.dot(h, w1_ref[:, sl], preferred_element_type=F32), 0.0)
        acc = acc + jnp.dot((hid * hid).astype(BF16), w2_ref[sl, :], preferred_element_type=F32)
    if glu:
        _, o_ref = rest
    else:
        perm_ref, g_next_ref, w_next_ref, o_ref, u_ref = rest
        h_next = _rms_norm(acc, g_next_ref[...]).astype(BF16)
        u = jnp.dot(h_next, w_next_ref[...], preferred_element_type=F32).astype(BF16)
        u_ref[...] = _permute_token_blocks(perm_ref[...], u)
    o_ref[...] = acc


def _mix_mlp(x, a, wmix, g, w1, w2, perm, next_proj=None, *, layer, glu, tm, ff_chunk=1024):
    t, d = x.shape
    row = lambda i: (i, 0)
    single = pl.Buffered(1)
    wspec = lambda shape: pl.BlockSpec(shape, lambda i: (0, 0), pipeline_mode=single)
    lspec = lambda w: pl.BlockSpec((None,) + w.shape[1:], lambda i: (layer, 0, 0),
                                   pipeline_mode=single)
    operands = [x, a, wmix, g, w1, w2, perm]
    in_specs = [pl.BlockSpec((tm, d), row), pl.BlockSpec((tm, d), row),
                wspec(wmix.shape), wspec((1, d)), lspec(w1), lspec(w2), wspec(perm.shape)]
    out_specs = pl.BlockSpec((tm, d), row)
    out_shape = jax.ShapeDtypeStruct((t, d), F32)
    if next_proj is not None:
        operands += list(next_proj)
        in_specs += [wspec(w.shape) for w in next_proj]
        n = next_proj[1].shape[1]
        out_specs = [out_specs, pl.BlockSpec((tm, n), row)]
        out_shape = [out_shape, jax.ShapeDtypeStruct((t, n), BF16)]
    return pl.pallas_call(
        functools.partial(_mix_mlp_kernel, glu=glu, d=d, ff_chunk=ff_chunk),
        grid=(t // tm,),
        in_specs=in_specs,
        out_specs=out_specs,
        out_shape=out_shape,
        compiler_params=pltpu.CompilerParams(
            dimension_semantics=("arbitrary",), vmem_limit_bytes=VMEM_LIMIT),
        name="mix_glu_mlp" if glu else "mix_mlp",
    )(*operands)


GROUPS_PER_SLAB = LANES // SSM_GROUP


def _piece_transpose(arrs, piece):
    arrs = list(arrs)
    dist = GROUPS_PER_SLAB // 2
    while dist >= 1:
        keep = (piece & dist) == 0
        shift = SSM_GROUP * dist
        for i in range(GROUPS_PER_SLAB):
            if i & dist:
                continue
            a, b = arrs[i], arrs[i + dist]
            arrs[i] = jnp.where(keep, a, pltpu.roll(b, shift, axis=1))
            arrs[i + dist] = jnp.where(keep, pltpu.roll(a, LANES - shift, axis=1), b)
        dist //= 2
    return arrs


def _ssm_kernel(u_ref, cc_ref, bt_ref, pw1_ref, pw2_ref, zoh_ref, a1_ref, a2_ref,
                dv_ref, z_ref, ug_scr, zg_scr, *, n_chunks):
    L = SSM_CHUNK
    rows = u_ref.shape[0] // L
    w = L * SSM_GROUP
    half = SSM_STATE
    nb = math.gcd(rows // L, 8)
    piece = lax.broadcasted_iota(jnp.int32, (nb * L, LANES), 1) // SSM_GROUP

    def relayout_in(rb, carry):
        t0 = pl.multiple_of(rb * nb * w, nb * w)
        r0 = pl.multiple_of(rb * nb * L, nb * L)
        by_pos = u_ref[pl.ds(t0, nb * w), :].astype(F32)
        for hf in range(L // GROUPS_PER_SLAB):
            arrs = []
            for k in range(GROUPS_PER_SLAB):
                pos = GROUPS_PER_SLAB * hf + k
                arrs.append(jnp.concatenate(
                    [by_pos[w * blk + L * pos:w * blk + L * (pos + 1)] for blk in range(nb)],
                    axis=0))
            arrs = _piece_transpose(arrs, piece)
            for g in range(GROUPS_PER_SLAB):
                ug_scr[g, pl.ds(r0, nb * L), LANES * hf:LANES * (hf + 1)] = arrs[g].astype(BF16)
        return carry

    lax.fori_loop(0, rows // (nb * L), relayout_in, 0)

    chunk = lax.broadcasted_iota(jnp.int32, (rows, LANES), 0) % n_chunks
    lane_w = lax.broadcasted_iota(jnp.int32, (SSM_GROUP, w), 1)
    low = lax.broadcasted_iota(jnp.int32, (1, LANES), 1) < half
    sign = jnp.where(low, 1.0, -1.0)

    def operators(g):
        cc = cc_ref[g]
        bt = bt_ref[g]
        pw1 = pw1_ref[g]
        pw2 = pw2_ref[g]
        zoh = zoh_ref[g]
        bbar = bt * zoh[0:1] + pltpu.roll(bt, half, axis=1) * zoh[1:2]
        bbar_sw = pltpu.roll(bbar, half, axis=1)
        cc_sw = pltpu.roll(cc, half, axis=1)
        ca = [cc * pw1[k:k + 1] + cc_sw * pw2[k:k + 1] for k in range(L + 1)]
        k_all = lax.dot_general(bbar * sign, jnp.concatenate(ca[:L], axis=0), NT_DIMS,
                                precision=lax.Precision.HIGHEST, preferred_element_type=F32)
        m_intra, w_state, w_out_t = [], [], []
        for s in range(L):
            blk = k_all if s == 0 else jnp.where(
                lane_w >= SSM_GROUP * s, pltpu.roll(k_all, SSM_GROUP * s, axis=1), 0.0)
            m_intra.append(blk.astype(BF16))
            k = L - 1 - s
            w_state.append(bbar * pw1[k:k + 1] + bbar_sw * pw2[k:k + 1])
            w_out_t.append(ca[s + 1] * sign)
        return jnp.concatenate(m_intra, axis=0), w_state, w_out_t

    def halves(pieces, second):
        firsts, seconds = [], []
        for piece in pieces:
            swapped = pltpu.roll(piece, half, axis=1)
            if second:
                firsts.append(jnp.where(low, 0.0, swapped))
                seconds.append(jnp.where(low, 0.0, piece))
            else:
                firsts.append(jnp.where(low, piece, 0.0))
                seconds.append(jnp.where(low, swapped, 0.0))
        return firsts, seconds

    def group_pair(pr, carry):
        gs = (2 * pr, 2 * pr + 1)
        m_intra, ws_re, ws_im, wo_t = [], [], [], []
        for second, g in enumerate(gs):
            mi, w_state, w_out_t = operators(g)
            m_intra.append(mi)
            re, im = halves(w_state, second)
            ws_re += re
            ws_im += im
            re, im = halves(w_out_t, second)
            wo_t.append(jnp.concatenate(
                [jnp.concatenate([r, m], axis=1) for r, m in zip(re, im)], axis=0).astype(BF16))
        us = [ug_scr[g] for g in gs]
        uu = jnp.concatenate(us, axis=1)
        xr = jnp.dot(uu, jnp.concatenate(ws_re, axis=0).astype(BF16), preferred_element_type=F32)
        xi = jnp.dot(uu, jnp.concatenate(ws_im, axis=0).astype(BF16), preferred_element_type=F32)
        a1 = [a1_ref[g] for g in gs]
        a2 = [a2_ref[g] for g in gs]
        step = 1
        j = 0
        while step < n_chunks:
            ar = jnp.where(low, a1[0][j:j + 1], a1[1][j:j + 1])
            ai = jnp.where(low, -a2[0][j:j + 1], a2[1][j:j + 1])
            sr = jnp.where(chunk >= step, pltpu.roll(xr, step, axis=0), 0.0)
            si = jnp.where(chunk >= step, pltpu.roll(xi, step, axis=0), 0.0)
            xr = xr + sr * ar - si * ai
            xi = xi + si * ar + sr * ai
            step *= 2
            j += 1
        x_in = jnp.concatenate([jnp.where(chunk >= 1, pltpu.roll(x, 1, axis=0), 0.0)
                                for x in (xr, xi)], axis=1).astype(BF16)
        for second, g in enumerate(gs):
            y = jnp.dot(us[second], m_intra[second], preferred_element_type=F32)
            y = y + lax.dot_general(x_in, wo_t[second], NT_DIMS, preferred_element_type=F32)
            y = y + us[second].astype(F32) * dv_ref[g]
            zg_scr[g] = jax.nn.gelu(y).astype(BF16)
        return carry

    lax.fori_loop(0, GROUPS_PER_SLAB // 2, group_pair, 0)

    def relayout_out(rb, carry):
        t0 = pl.multiple_of(rb * nb * w, nb * w)
        r0 = pl.multiple_of(rb * nb * L, nb * L)
        by_pos = []
        for hf in range(L // GROUPS_PER_SLAB):
            arrs = [zg_scr[g, pl.ds(r0, nb * L), LANES * hf:LANES * (hf + 1)].astype(F32)
                    for g in range(GROUPS_PER_SLAB)]
            by_pos += _piece_transpose(arrs, piece)
        for blk in range(nb):
            z_ref[pl.ds(t0 + w * blk, w), :] = jnp.concatenate(
                [arr[L * blk:L * (blk + 1)] for arr in by_pos], axis=0).astype(BF16)
        return carry

    lax.fori_loop(0, rows // (nb * L), relayout_out, 0)


def _ssm_scan(u, cc, bt, pw1, pw2, zoh, a1, a2, dv, *, n_chunks):
    t, d = u.shape
    L = SSM_CHUNK
    rows = t // L
    w = L * SSM_GROUP
    gps = GROUPS_PER_SLAB
    slab = lambda j: (j, 0, 0)
    pspec = lambda arr: pl.BlockSpec((gps,) + arr.shape[1:], slab)
    return pl.pallas_call(
        functools.partial(_ssm_kernel, n_chunks=n_chunks),
        grid=(d // LANES,),
        in_specs=[pl.BlockSpec((t, LANES), lambda j: (0, j)),
                  pspec(cc), pspec(bt), pspec(pw1), pspec(pw2), pspec(zoh), pspec(a1), pspec(a2),
                  pspec(dv)],
        out_specs=pl.BlockSpec((t, LANES), lambda j: (0, j)),
        out_shape=jax.ShapeDtypeStruct((t, d), BF16),
        scratch_shapes=[pltpu.VMEM((gps, rows, w), BF16), pltpu.VMEM((gps, rows, w), BF16)],
        compiler_params=pltpu.CompilerParams(
            dimension_semantics=("arbitrary",), vmem_limit_bytes=VMEM_LIMIT),
        name="s5_scan",
    )(u, cc, bt, pw1, pw2, zoh, a1, a2, dv)


def _ssm_operators(a_re, a_im, b_re, b_im, c_re, c_im, log_dt, d_skip, n_chunks):
    L = SSM_CHUNK
    g, p = a_re.shape
    dt = jnp.exp(log_dt)[:, None]
    lam_re, lam_im = dt * a_re, dt * a_im

    def powers(ks):
        ks = jnp.asarray(ks, F32)[:, None, None]
        mag = jnp.exp(ks * lam_re)
        return mag * jnp.cos(ks * lam_im), mag * jnp.sin(ks * lam_im)

    def patterns(re, im):
        return (jnp.concatenate([re, re], axis=2).transpose(1, 0, 2),
                jnp.concatenate([-im, im], axis=2).transpose(1, 0, 2))

    pr, pi = powers(range(L + 1))
    num_re, num_im = pr[1] - 1.0, pi[1]
    den = a_re * a_re + a_im * a_im
    s_re = (num_re * a_re + num_im * a_im) / den
    s_im = (num_im * a_re - num_re * a_im) / den
    pw1, pw2 = patterns(pr, pi)
    zoh = jnp.concatenate(patterns(s_re[None], s_im[None]), axis=1)
    n_steps = max(1, int(math.log2(n_chunks)))
    a1, a2 = patterns(*powers([L * 2 ** j for j in range(n_steps)]))
    cc = jnp.concatenate([c_re, c_im], axis=2)
    bt = jnp.concatenate([b_re.transpose(0, 2, 1), b_im.transpose(0, 2, 1)], axis=2)
    dv = jnp.tile(d_skip.reshape(g, 1, SSM_GROUP), (1, L, 1)).reshape(g, 1, L * SSM_GROUP)
    return cc, bt, pw1, pw2, zoh, a1, a2, dv


def kernel(x, norm_mix_g, norm_mlp_g, attn_w_in, attn_b_f, attn_q_g, attn_k_g, attn_w_out,
           ssm_w_in, ssm_a_re, ssm_a_im, ssm_b_re, ssm_b_im, ssm_c_re, ssm_c_im, ssm_log_dt,
           ssm_d, ssm_w_glu, mlp_w1, mlp_w2):
    b, s, d = x.shape
    t = b * s
    tk = min(512, s // 2)
    tq = 2 * tk
    tm_proj = min(512, s)
    tm_mlp = min(512, t)

    w_in = attn_w_in[0]
    w_f = jnp.pad(w_in[:, 3 * d:], ((0, 0), (0, LANES - N_HEADS)))
    wkf = jnp.concatenate([w_in[:, d:2 * d], w_f], axis=1).astype(BF16)
    wqvt = jnp.concatenate([w_in[:, :d], w_in[:, 2 * d:3 * d]], axis=1).T.astype(BF16)
    blk = np.arange(MXU_DIM) // HEAD_DIM
    bd = jnp.asarray((blk[:, None] == blk[None, :]) * (1.0 / HEAD_DIM), BF16)
    gain_k = jnp.tile(attn_k_g[0], N_HEADS).reshape(1, d)
    gain_qt = jnp.broadcast_to(
        (jnp.tile(attn_q_g[0], N_HEADS) * (LOG2E / math.sqrt(HEAD_DIM)))[:, None], (d, LANES))
    bfp = jnp.pad(attn_b_f[0], (0, LANES - N_HEADS)).reshape(1, LANES)
    tri = jnp.asarray(np.arange(tm_proj)[:, None] >= np.arange(tm_proj)[None, :], BF16)
    heads = np.arange(N_HEADS)
    place_k = np.zeros((LANES, d), np.float32)
    for i in range(BIAS_PIECES):
        place_k[N_HEADS * i + heads, (heads // HEADS_PER_STEP) * LANES
                + BIAS_PIECES * (heads % HEADS_PER_STEP) + i] = 1.0
    place_qt = np.roll(place_k, BIAS_Q_LANE, axis=1).T
    shift = (LOG2E * math.sqrt(HEAD_DIM) * jnp.max(jnp.abs(attn_q_g[0]))
             * jnp.max(jnp.abs(attn_k_g[0])))
    qt, k, vt, kb, qbt, edge = _qkv_proj(
        x, norm_mix_g[0].reshape(1, d), wkf, wqvt, bd, gain_k, gain_qt, bfp, tri,
        jnp.asarray(place_k, BF16), jnp.asarray(place_qt, BF16), jnp.full((1, LANES), shift, F32),
        tm=tm_proj, tk=tk)
    assert tm_proj == tk
    c_first = edge[:, ::tq // tk, 0, :N_HEADS]
    c_last = edge[:, :, 1, :N_HEADS]
    bound = (c_first[:, :, None, :] - c_last[:, None, :, :]) * LOG2E
    key_blk = jnp.arange(s // tk)[None, None, :, None]
    first = jnp.min(jnp.where(bound < SKIP_LOGIT, s // tk, key_blk), axis=2)
    first = jnp.minimum(first, (tq // tk) * jnp.arange(s // tq)[None, :, None]) // 2 * 2
    first = jnp.concatenate([first.transpose(0, 2, 1).reshape(-1), (shift > MAX_SHIFT)[None]])
    o, (w1, w2, w_out, w_ssm_in, w_glu) = _attention(
        first.astype(jnp.int32), qt, qbt, k, kb, vt,
        [mlp_w1, mlp_w2, attn_w_out, ssm_w_in, ssm_w_glu], tq=tq, tk=tk)
    tok = np.arange(SSM_CHUNK * SSM_CHUNK)
    perm = jnp.asarray(
        tok[None, :] == (tok[:, None] % SSM_CHUNK) * SSM_CHUNK + tok[:, None] // SSM_CHUNK, BF16)
    x2, u = _mix_mlp(x.reshape(t, d), o.reshape(t, d), w_out[0],
                     norm_mlp_g[0].reshape(1, d), w1, w2,
                     perm, (norm_mix_g[1].reshape(1, d), w_ssm_in[0]),
                     layer=0, glu=False, tm=tm_mlp)

    n_chunks = s // SSM_CHUNK
    ops = _ssm_operators(ssm_a_re[0], ssm_a_im[0], ssm_b_re[0], ssm_b_im[0], ssm_c_re[0],
                         ssm_c_im[0], ssm_log_dt[0], ssm_d[0], n_chunks)
    z = _ssm_scan(u, *ops, n_chunks=n_chunks)
    x3 = _mix_mlp(x2, z, w_glu[0], norm_mlp_g[1].reshape(1, d),
                  w1, w2, perm, layer=1, glu=True, tm=tm_mlp)
    return x3.reshape(b, s, d)
```

```python
import functools
import math

import jax
import jax.numpy as jnp
import numpy as np
from jax import lax
from jax.experimental import pallas as pl
from jax.experimental.pallas import tpu as pltpu

F32 = jnp.float32
BF16 = jnp.bfloat16

N_HEADS = 16
HEAD_DIM = 64
SSM_GROUP = 16
SSM_STATE = 64
SSM_CHUNK = 16
EPS = 1e-6
LOG2E = 1.4426950408889634

LANES = 128
MXU_DIM = 256
HEADS_PER_STEP = LANES // HEAD_DIM
BIAS_PIECES = 3
VMEM_LIMIT = 56 * 1024 * 1024

NT_DIMS = (((1,), (1,)), ((), ()))


def _rms_norm(x, g):
    ms = jnp.mean(x * x, axis=-1, keepdims=True)
    return x * lax.rsqrt(ms + EPS) * g


def _const_spec(shape):
    zeros = (0,) * len(shape)
    return pl.BlockSpec(shape, lambda *_: zeros)


def _bf16_pieces(val):
    pieces = jnp.zeros_like(val)
    rem = val
    for n in range(BIAS_PIECES):
        piece = rem.astype(BF16).astype(F32)
        rem = rem - piece
        pieces = pieces + (piece if n == 0 else pltpu.roll(piece, N_HEADS * n, axis=1))
    return pieces


def _qkv_kernel(x_ref, g_ref, wkf_ref, wqvt_ref, bd_ref, gain_k_ref, gain_qt_ref, bf_ref, tri_ref,
                place_k_ref, place_qt_ref, shift_ref,
                qt_ref, k_ref, vt_ref, kb_ref, qbt_ref, edge_ref, carry_ref, *, tm, tk, d):
    @pl.when(pl.program_id(1) == 0)
    def _():
        carry_ref[...] = jnp.zeros_like(carry_ref)

    h = _rms_norm(x_ref[0], g_ref[...]).astype(BF16)

    y = jnp.dot(h, wkf_ref[...], preferred_element_type=F32)
    vt = lax.dot_general(wqvt_ref[d:, :], h, NT_DIMS, preferred_element_type=F32).astype(BF16)
    for j in range(tm // tk):
        vt_ref[0, j] = vt[:, tk * j:tk * (j + 1)]
    qt = lax.dot_general(wqvt_ref[:d, :], h, NT_DIMS, preferred_element_type=F32)

    for t in range(d // MXU_DIM):
        sl = slice(MXU_DIM * t, MXU_DIM * (t + 1))
        tile = y[:, sl]
        ms = jnp.dot((tile * tile).astype(BF16), bd_ref[...], preferred_element_type=F32)
        k_ref[0, :, sl] = (tile * lax.rsqrt(ms + EPS) * gain_k_ref[:, sl]).astype(BF16)
        tile = qt[sl, :]
        ms = jnp.dot(bd_ref[...], (tile * tile).astype(BF16), preferred_element_type=F32)
        gain = jnp.tile(gain_qt_ref[sl, :], (1, tm // LANES))
        qt_ref[0, sl, :] = (tile * lax.rsqrt(ms + EPS) * gain).astype(BF16)

    f = y[:, d:] + bf_ref[...]
    log_f = jnp.minimum(f, 0.0) - jnp.log1p(jnp.exp(-jnp.abs(f)))
    lane = lax.broadcasted_iota(jnp.int32, log_f.shape, 1)
    part = jnp.dot(tri_ref[...], _bf16_pieces(jnp.where(lane < N_HEADS, log_f, 0.0)).astype(BF16),
                   preferred_element_type=F32)
    cs = carry_ref[...] + part
    for n in range(1, BIAS_PIECES):
        cs = cs + pltpu.roll(part, LANES - N_HEADS * n, axis=1)
    carry_ref[...] = cs[tm - 1:tm, :]
    edge_ref[0, 0] = jnp.concatenate(
        [cs[0:1, :], cs[tm - 1:tm, :], jnp.zeros((6, LANES), F32)], axis=0)
    c2 = jnp.where(lane < N_HEADS, cs * LOG2E, 0.0)
    kb_ref[0] = jnp.dot(_bf16_pieces(-c2).astype(BF16), place_k_ref[...],
                        preferred_element_type=F32).astype(BF16)
    q_pieces = _bf16_pieces(jnp.where(lane < N_HEADS, c2 - shift_ref[...], 0.0))
    qbt_ref[0] = jnp.dot(place_qt_ref[...], q_pieces.T.astype(BF16),
                         preferred_element_type=F32).astype(BF16)


def _qkv_proj(x, g, wkf, wqvt, bd, gain_k, gain_qt, bfp, tri, place_k, place_qt, shift, *, tm, tk):
    b, s, d = x.shape
    row = lambda bi, si: (bi, si, 0)
    col = lambda bi, si: (bi, 0, si)
    consts = (g, wkf, wqvt, bd, gain_k, gain_qt, bfp, tri, place_k, place_qt, shift)
    return pl.pallas_call(
        functools.partial(_qkv_kernel, tm=tm, tk=tk, d=d),
        grid=(b, s // tm),
        in_specs=[pl.BlockSpec((1, tm, d), row)] + [_const_spec(c.shape) for c in consts],
        out_specs=[pl.BlockSpec((1, d, tm), col),
                   pl.BlockSpec((1, tm, d), row),
                   pl.BlockSpec((1, tm // tk, d, tk), lambda bi, si: (bi, si, 0, 0)),
                   pl.BlockSpec((1, tm, d), row),
                   pl.BlockSpec((1, d, tm), col),
                   pl.BlockSpec((1, 1, 8, LANES), lambda bi, si: (bi, si, 0, 0))],
        out_shape=[jax.ShapeDtypeStruct((b, d, s), BF16),
                   jax.ShapeDtypeStruct((b, s, d), BF16),
                   jax.ShapeDtypeStruct((b, s // tk, d, tk), BF16),
                   jax.ShapeDtypeStruct((b, s, d), BF16),
                   jax.ShapeDtypeStruct((b, d, s), BF16),
                   jax.ShapeDtypeStruct((b, s // tm, 8, LANES), F32)],
        scratch_shapes=[pltpu.VMEM((1, LANES), F32)],
        compiler_params=pltpu.CompilerParams(
            dimension_semantics=("arbitrary", "arbitrary"), vmem_limit_bytes=VMEM_LIMIT),
        name="qkv_proj",
    )(x, *consts)


BIAS_Q_LANE = 8
MAX_SHIFT = 48.0
SKIP_LOGIT = -200.0


def _attn_kernel(first_ref, *refs, tq, tk):
    *attn_in, w1_ref, w2_ref, o_ref, w1_bf_ref, w2_bf_ref = refs[:10]
    attn_refs = (*attn_in, o_ref, *refs[10:])
    w1_bf_ref[...] = w1_ref[...].astype(BF16)
    w2_bf_ref[...] = w2_ref[...].astype(BF16)

    use_online = first_ref[first_ref.shape[0] - 1]

    @pl.when(use_online == 0)
    def _():
        _attn_body(first_ref, *attn_refs, tq=tq, tk=tk, online=False)

    @pl.when(use_online != 0)
    def _():
        _attn_body(first_ref, *attn_refs, tq=tq, tk=tk, online=True)


def _attn_body(first_ref, qt_ref, qbt_ref, k_ref, kb_ref, vt_ref, o_ref, s0_scr, s1_scr, p0_scr,
               p1_scr, *, tq, tk, online):
    i = pl.program_id(2)
    n_sub = tq // tk
    qt = qt_ref[0]
    qbt = qbt_ref[0]
    row = lax.broadcasted_iota(jnp.int32, (LANES, tq), 0)
    lane_k = lax.broadcasted_iota(jnp.int32, (tk, LANES), 1)
    q_cols = (lane_k >= BIAS_Q_LANE) & (lane_k < BIAS_Q_LANE + HEADS_PER_STEP * BIAS_PIECES)
    causal = (lax.broadcasted_iota(jnp.int32, (tk, tk), 0)
              <= lax.broadcasted_iota(jnp.int32, (tk, tk), 1))
    one = jnp.ones((), BF16)
    zero = jnp.zeros((), BF16)
    heads = range(HEADS_PER_STEP)
    qats = []
    for hh in heads:
        q_h = jnp.where((row >= HEAD_DIM * hh) & (row < HEAD_DIM * (hh + 1)), qt, zero)
        k_side = (row >= BIAS_PIECES * hh) & (row < BIAS_PIECES * (hh + 1))
        q_lo = BIAS_Q_LANE + BIAS_PIECES * hh
        q_side = (row >= q_lo) & (row < q_lo + BIAS_PIECES)
        qats.append(jnp.concatenate(
            [q_h, jnp.where(k_side, one, jnp.where(q_side, qbt, zero))], axis=0))

    def scores(hh, j, c0=0):
        off = pl.multiple_of(j * tk, tk)
        kbias = jnp.where(q_cols, one, kb_ref[0, pl.ds(off, tk), :])
        ka = jnp.concatenate([k_ref[0, pl.ds(off, tk), :], kbias], axis=1)
        return jnp.dot(ka, qats[hh][:, c0:], preferred_element_type=F32)

    if not online:
        j0 = n_sub * i
        def probs(hh, l, j, c0=0, diagonal=None):
            s = scores(hh, j, c0)
            if diagonal is not False:
                head = jnp.where(causal, s[:, :tk], -1e30)
                if diagonal is None:
                    head = jnp.where(j < j0, s[:, :tk], head)
                s = head if s.shape[1] == tk else jnp.concatenate([head, s[:, tk:]], axis=1)
            p = jnp.exp2(s)
            return l + jnp.sum(p, axis=0, keepdims=True), p.astype(BF16)

        def weigh(hh, acc, p, j):
            vt = vt_ref[0, j, HEAD_DIM * hh:HEAD_DIM * (hh + 1), :]
            return acc + jnp.dot(vt, p, preferred_element_type=F32)

        def fill(hh):
            head = HEADS_PER_STEP * pl.program_id(1) + hh
            jj0 = first_ref[(pl.program_id(0) * N_HEADS + head) * pl.num_programs(2) + i] // 2
            l, p0_scr[hh] = probs(hh, jnp.zeros((1, tq), F32), 2 * jj0)
            return jj0, l

        def off_diagonal(jj0s, ls):
            def pairs(jj, cs, n_pairs, hs):
                cs = list(cs)
                for r in range(n_pairs):
                    a = 2 * (jj + r)
                    for hh in hs:
                        l, acc = cs[hh]
                        l, p1_scr[hh] = probs(hh, l, a + 1, diagonal=False)
                        cs[hh] = (l, weigh(hh, acc, p0_scr[hh], a))
                    for hh in hs:
                        l, acc = cs[hh]
                        l, p0_scr[hh] = probs(hh, l, a + 2)
                        cs[hh] = (l, weigh(hh, acc, p1_scr[hh], a + 1))
                return tuple(cs)

            jj_both = functools.reduce(jnp.maximum, jj0s)
            cs = tuple((l, jnp.zeros((HEAD_DIM, tq), F32)) for l in ls)
            for hh in heads:
                cs = lax.fori_loop(jj0s[hh], jj_both,
                                   lambda jj, cs, hh=hh: pairs(jj, cs, 1, (hh,)), cs)
            n_long = (j0 // 2 - jj_both) // 2
            cs = lax.fori_loop(0, n_long, lambda t, cs: pairs(jj_both + 2 * t, cs, 2, heads), cs)
            return lax.fori_loop(jj_both + 2 * n_long, j0 // 2,
                                 lambda jj, cs: pairs(jj, cs, 1, heads), cs)

        def diagonal(hh, l, acc):
            later = [probs(hh, l[:, tk * dd:], j0 + dd, tk * dd, diagonal=True)
                     for dd in range(1, n_sub)]
            acc = weigh(hh, acc, p0_scr[hh], j0)
            for dd in range(1, n_sub):
                c0 = tk * dd
                l_part, p = later[dd - 1]
                l = jnp.concatenate([l[:, :c0], l_part], axis=1)
                acc = jnp.concatenate([acc[:, :c0], weigh(hh, acc[:, c0:], p, j0 + dd)], axis=1)
            return acc / l

        filled = [fill(hh) for hh in heads]
        cs = off_diagonal([f[0] for f in filled], [f[1] for f in filled])
        outs = [diagonal(hh, *cs[hh]) for hh in heads]
        o_ref[0] = jnp.concatenate(outs, axis=0).T.astype(BF16)
        return

    def consume(hh, carry, s, j, diagonal):
        if diagonal:
            s_tri = jnp.where(causal, s[:, :tk], -1e30)
            s = s_tri if s.shape[1] == tk else jnp.concatenate([s_tri, s[:, tk:]], axis=1)
        vt = vt_ref[0, j, HEAD_DIM * hh:HEAD_DIM * (hh + 1), :]
        if online:
            m, l, acc = carry
            m_new = jnp.maximum(m, jnp.max(s, axis=0, keepdims=True))
            alpha = jnp.exp2(m - m_new)
            p = jnp.exp2(s - m_new)
            l = alpha * l + jnp.sum(p, axis=0, keepdims=True)
            acc = alpha * acc + jnp.dot(vt, p.astype(BF16), preferred_element_type=F32)
            return m_new, l, acc
        l, acc = carry
        p = jnp.exp2(s)
        l = l + jnp.sum(p, axis=0, keepdims=True)
        acc = acc + jnp.dot(vt, p.astype(BF16), preferred_element_type=F32)
        return l, acc

    init = (jnp.zeros((1, tq), F32), jnp.zeros((HEAD_DIM, tq), F32))
    if online:
        init = (jnp.full((1, tq), -1e30, F32),) + init

    j0 = n_sub * i

    def first_scores(hh):
        if online:
            jj0 = 0
        else:
            head = HEADS_PER_STEP * pl.program_id(1) + hh
            jj0 = first_ref[(pl.program_id(0) * N_HEADS + head) * pl.num_programs(2) + i] // 2
        s0_scr[hh] = scores(hh, 2 * jj0)
        return jj0

    def off_diagonal(hh, jj0):
        def pairs(jj, c, n_pairs):
            for r in range(n_pairs):
                s1_scr[hh] = scores(hh, 2 * (jj + r) + 1)
                c = consume(hh, c, s0_scr[hh], 2 * (jj + r), False)
                s0_scr[hh] = scores(hh, 2 * (jj + r) + 2)
                c = consume(hh, c, s1_scr[hh], 2 * (jj + r) + 1, False)
            return c

        n_long = (j0 // 2 - jj0) // 2
        c = lax.fori_loop(0, n_long, lambda t, c: pairs(jj0 + 2 * t, c, 2), init)
        return lax.fori_loop(jj0 + 2 * n_long, j0 // 2, lambda jj, c: pairs(jj, c, 1), c)

    def diagonal(hh, carry):
        later = [scores(hh, j0 + dd, tk * dd) for dd in range(1, n_sub)]
        carry = consume(hh, carry, s0_scr[hh], j0, True)
        for dd in range(1, n_sub):
            c0 = tk * dd
            part = consume(hh, tuple(c[:, c0:] for c in carry), later[dd - 1], j0 + dd, True)
            carry = tuple(jnp.concatenate([c[:, :c0], pc], axis=1) for c, pc in zip(carry, part))
        return carry[-1] / carry[-2]

    outs = []
    jj0 = first_scores(0)
    for hh in heads:
        carry = off_diagonal(hh, jj0)
        if hh + 1 < HEADS_PER_STEP:
            jj0 = first_scores(hh + 1)
        outs.append(diagonal(hh, carry))
    o_t = jnp.concatenate(outs, axis=0)
    o_ref[0] = o_t.T.astype(BF16)


def _attention(first, qt, qbt, k, kb, vt, w1, w2, *, tq, tk):
    b, s, d = k.shape
    assert (tq // tk) % 2 == 0 and s % tq == 0
    n_pairs = d // LANES
    nq = s // tq
    nk = s // tk
    n_steps = b * n_pairs * nq
    qt_blk = pl.BlockSpec((1, LANES, tq), lambda bi, hp, i, first: (bi, hp, i))
    k_all = pl.BlockSpec((1, s, LANES), lambda bi, hp, i, first: (bi, 0, hp))
    step = lambda bi, hp, i, first: ((bi * n_pairs + hp) * nq + i, 0)

    def slice_spec(w):
        assert w.shape[0] % (16 * n_steps) == 0
        return pl.BlockSpec((w.shape[0] // n_steps, w.shape[1]), step)

    return pl.pallas_call(
        functools.partial(_attn_kernel, tq=tq, tk=tk),
        grid_spec=pltpu.PrefetchScalarGridSpec(
            num_scalar_prefetch=1,
            grid=(b, n_pairs, nq),
            in_specs=[qt_blk, qt_blk, k_all, k_all,
                      pl.BlockSpec((1, nk, LANES, tk), lambda bi, hp, i, first: (bi, 0, hp, 0)),
                      slice_spec(w1), slice_spec(w2)],
            out_specs=[pl.BlockSpec((1, tq, LANES), lambda bi, hp, i, first: (bi, i, hp)),
                       slice_spec(w1), slice_spec(w2)],
            scratch_shapes=[pltpu.VMEM((HEADS_PER_STEP, tk, tq), F32),
                            pltpu.VMEM((HEADS_PER_STEP, tk, tq), F32),
                            pltpu.VMEM((HEADS_PER_STEP, tk, tq), BF16),
                            pltpu.VMEM((HEADS_PER_STEP, tk, tq), BF16)]),
        out_shape=[jax.ShapeDtypeStruct((b, s, d), BF16),
                   jax.ShapeDtypeStruct(w1.shape, BF16), jax.ShapeDtypeStruct(w2.shape, BF16)],
        compiler_params=pltpu.CompilerParams(
            dimension_semantics=("arbitrary", "arbitrary", "arbitrary"),
            vmem_limit_bytes=VMEM_LIMIT),
        name="fox_attention",
    )(first, qt, qbt, k, kb, vt, w1, w2)


def _permute_token_blocks(perm, a):
    w = perm.shape[0]
    return jnp.concatenate(
        [jnp.dot(perm, a[w * i:w * (i + 1)], preferred_element_type=F32).astype(BF16)
         for i in range(a.shape[0] // w)], axis=0)


def _mix_mlp_kernel(x_ref, a_ref, wmix_ref, g_ref, w1_ref, w2_ref, *rest, glu, d, ff_chunk):
    if glu:
        a = _permute_token_blocks(rest[0][...], a_ref[...])
        mix = jnp.dot(a, wmix_ref[...], preferred_element_type=F32)
        mix = mix[:, :d] * jax.nn.sigmoid(mix[:, d:])
    else:
        mix = jnp.dot(a_ref[...], wmix_ref[...], preferred_element_type=F32)
    x1 = x_ref[...] + mix
    h = _rms_norm(x1, g_ref[...]).astype(BF16)
    acc = x1
    for c in range(w1_ref.shape[1] // ff_chunk):
        sl = slice(ff_chunk * c, ff_chunk * (c + 1))
        hid = jnp.maximum(jnp.dot(h, w1_ref[:, sl], preferred_element_type=F32), 0.0)
        acc = acc + jnp.dot((hid * hid).astype(BF16), w2_ref[sl, :], preferred_element_type=F32)
    if glu:
        _, o_ref = rest
    else:
        perm_ref, g_next_ref, w_next_ref, o_ref, u_ref = rest
        h_next = _rms_norm(acc, g_next_ref[...]).astype(BF16)
        u = jnp.dot(h_next, w_next_ref[...], preferred_element_type=F32).astype(BF16)
        u_ref[...] = _permute_token_blocks(perm_ref[...], u)
    o_ref[...] = acc


def _mix_mlp(x, a, wmix, g, w1, w2, perm, next_proj=None, *, layer, glu, tm, ff_chunk=1024):
    t, d = x.shape
    row = lambda i: (i, 0)
    single = pl.Buffered(1)
    wspec = lambda shape: pl.BlockSpec(shape, lambda i: (0, 0), pipeline_mode=single)
    lspec = lambda w: pl.BlockSpec((None,) + w.shape[1:], lambda i: (layer, 0, 0),
                                   pipeline_mode=single)
    operands = [x, a, wmix, g, w1, w2, perm]
    in_specs = [pl.BlockSpec((tm, d), row), pl.BlockSpec((tm, d), row),
                wspec(wmix.shape), wspec((1, d)), lspec(w1), lspec(w2), wspec(perm.shape)]
    out_specs = pl.BlockSpec((tm, d), row)
    out_shape = jax.ShapeDtypeStruct((t, d), F32)
    if next_proj is not None:
        operands += list(next_proj)
        in_specs += [wspec(w.shape) for w in next_proj]
        n = next_proj[1].shape[1]
        out_specs = [out_specs, pl.BlockSpec((tm, n), row)]
        out_shape = [out_shape, jax.ShapeDtypeStruct((t, n), BF16)]
    return pl.pallas_call(
        functools.partial(_mix_mlp_kernel, glu=glu, d=d, ff_chunk=ff_chunk),
        grid=(t // tm,),
        in_specs=in_specs,
        out_specs=out_specs,
        out_shape=out_shape,
        compiler_params=pltpu.CompilerParams(
            dimension_semantics=("arbitrary",), vmem_limit_bytes=VMEM_LIMIT),
        name="mix_glu_mlp" if glu else "mix_mlp",
    )(*operands)


GROUPS_PER_SLAB = LANES // SSM_GROUP


def _piece_transpose(arrs, piece):
    arrs = list(arrs)
    dist = GROUPS_PER_SLAB // 2
    while dist >= 1:
        keep = (piece & dist) == 0
        shift = SSM_GROUP * dist
        for i in range(GROUPS_PER_SLAB):
            if i & dist:
                continue
            a, b = arrs[i], arrs[i + dist]
            arrs[i] = jnp.where(keep, a, pltpu.roll(b, shift, axis=1))
            arrs[i + dist] = jnp.where(keep, pltpu.roll(a, LANES - shift, axis=1), b)
        dist //= 2
    return arrs


def _ssm_kernel(u_ref, cc_ref, bt_ref, pw1_ref, pw2_ref, zoh_ref, a1_ref, a2_ref,
                dv_ref, z_ref, ug_scr, zg_scr, *, n_chunks):
    L = SSM_CHUNK
    rows = u_ref.shape[0] // L
    w = L * SSM_GROUP
    half = SSM_STATE
    nb = math.gcd(rows // L, 8)
    piece = lax.broadcasted_iota(jnp.int32, (nb * L, LANES), 1) // SSM_GROUP

    def relayout_in(rb, carry):
        t0 = pl.multiple_of(rb * nb * w, nb * w)
        r0 = pl.multiple_of(rb * nb * L, nb * L)
        by_pos = u_ref[pl.ds(t0, nb * w), :].astype(F32)
        for hf in range(L // GROUPS_PER_SLAB):
            arrs = []
            for k in range(GROUPS_PER_SLAB):
                pos = GROUPS_PER_SLAB * hf + k
                arrs.append(jnp.concatenate(
                    [by_pos[w * blk + L * pos:w * blk + L * (pos + 1)] for blk in range(nb)],
                    axis=0))
            arrs = _piece_transpose(arrs, piece)
            for g in range(GROUPS_PER_SLAB):
                ug_scr[g, pl.ds(r0, nb * L), LANES * hf:LANES * (hf + 1)] = arrs[g].astype(BF16)
        return carry

    lax.fori_loop(0, rows // (nb * L), relayout_in, 0)

    chunk = lax.broadcasted_iota(jnp.int32, (rows, LANES), 0) % n_chunks
    lane_w = lax.broadcasted_iota(jnp.int32, (SSM_GROUP, w), 1)
    low = lax.broadcasted_iota(jnp.int32, (1, LANES), 1) < half
    sign = jnp.where(low, 1.0, -1.0)

    def operators(g):
        cc = cc_ref[g]
        bt = bt_ref[g]
        pw1 = pw1_ref[g]
        pw2 = pw2_ref[g]
        zoh = zoh_ref[g]
        bbar = bt * zoh[0:1] + pltpu.roll(bt, half, axis=1) * zoh[1:2]
        bbar_sw = pltpu.roll(bbar, half, axis=1)
        cc_sw = pltpu.roll(cc, half, axis=1)
        ca = [cc * pw1[k:k + 1] + cc_sw * pw2[k:k + 1] for k in range(L + 1)]
        k_all = lax.dot_general(bbar * sign, jnp.concatenate(ca[:L], axis=0), NT_DIMS,
                                precision=lax.Precision.HIGHEST, preferred_element_type=F32)
        m_intra, w_state, w_out_t = [], [], []
        for s in range(L):
            blk = k_all if s == 0 else jnp.where(
                lane_w >= SSM_GROUP * s, pltpu.roll(k_all, SSM_GROUP * s, axis=1), 0.0)
            m_intra.append(blk.astype(BF16))
            k = L - 1 - s
            w_state.append(bbar * pw1[k:k + 1] + bbar_sw * pw2[k:k + 1])
            w_out_t.append(ca[s + 1] * sign)
        return jnp.concatenate(m_intra, axis=0), w_state, w_out_t

    def halves(pieces, second):
        firsts, seconds = [], []
        for piece in pieces:
            swapped = pltpu.roll(piece, half, axis=1)
            if second:
                firsts.append(jnp.where(low, 0.0, swapped))
                seconds.append(jnp.where(low, 0.0, piece))
            else:
                firsts.append(jnp.where(low, piece, 0.0))
                seconds.append(jnp.where(low, swapped, 0.0))
        return firsts, seconds

    def group_pair(pr, carry):
        gs = (2 * pr, 2 * pr + 1)
        m_intra, ws_re, ws_im, wo_t = [], [], [], []
        for second, g in enumerate(gs):
            mi, w_state, w_out_t = operators(g)
            m_intra.append(mi)
            re, im = halves(w_state, second)
            ws_re += re
            ws_im += im
            re, im = halves(w_out_t, second)
            wo_t.append(jnp.concatenate(
                [jnp.concatenate([r, m], axis=1) for r, m in zip(re, im)], axis=0).astype(BF16))
        us = [ug_scr[g] for g in gs]
        uu = jnp.concatenate(us, axis=1)
        xr = jnp.dot(uu, jnp.concatenate(ws_re, axis=0).astype(BF16), preferred_element_type=F32)
        xi = jnp.dot(uu, jnp.concatenate(ws_im, axis=0).astype(BF16), preferred_element_type=F32)
        a1 = [a1_ref[g] for g in gs]
        a2 = [a2_ref[g] for g in gs]
        step = 1
        j = 0
        while step < n_chunks:
            ar = jnp.where(low, a1[0][j:j + 1], a1[1][j:j + 1])
            ai = jnp.where(low, -a2[0][j:j + 1], a2[1][j:j + 1])
            sr = jnp.where(chunk >= step, pltpu.roll(xr, step, axis=0), 0.0)
            si = jnp.where(chunk >= step, pltpu.roll(xi, step, axis=0), 0.0)
            xr = xr + sr * ar - si * ai
            xi = xi + si * ar + sr * ai
            step *= 2
            j += 1
        x_in = jnp.concatenate([jnp.where(chunk >= 1, pltpu.roll(x, 1, axis=0), 0.0)
                                for x in (xr, xi)], axis=1).astype(BF16)
        for second, g in enumerate(gs):
            y = jnp.dot(us[second], m_intra[second], preferred_element_type=F32)
            y = y + lax.dot_general(x_in, wo_t[second], NT_DIMS, preferred_element_type=F32)
            y = y + us[second].astype(F32) * dv_ref[g]
            zg_scr[g] = jax.nn.gelu(y).astype(BF16)
        return carry

    lax.fori_loop(0, GROUPS_PER_SLAB // 2, group_pair, 0)

    def relayout_out(rb, carry):
        t0 = pl.multiple_of(rb * nb * w, nb * w)
        r0 = pl.multiple_of(rb * nb * L, nb * L)
        by_pos = []
        for hf in range(L // GROUPS_PER_SLAB):
            arrs = [zg_scr[g, pl.ds(r0, nb * L), LANES * hf:LANES * (hf + 1)].astype(F32)
                    for g in range(GROUPS_PER_SLAB)]
            by_pos += _piece_transpose(arrs, piece)
        for blk in range(nb):
            z_ref[pl.ds(t0 + w * blk, w), :] = jnp.concatenate(
                [arr[L * blk:L * (blk + 1)] for arr in by_pos], axis=0).astype(BF16)
        return carry

    lax.fori_loop(0, rows // (nb * L), relayout_out, 0)


def _ssm_scan(u, cc, bt, pw1, pw2, zoh, a1, a2, dv, *, n_chunks):
    t, d = u.shape
    L = SSM_CHUNK
    rows = t // L
    w = L * SSM_GROUP
    gps = GROUPS_PER_SLAB
    slab = lambda j: (j, 0, 0)
    pspec = lambda arr: pl.BlockSpec((gps,) + arr.shape[1:], slab)
    return pl.pallas_call(
        functools.partial(_ssm_kernel, n_chunks=n_chunks),
        grid=(d // LANES,),
        in_specs=[pl.BlockSpec((t, LANES), lambda j: (0, j)),
                  pspec(cc), pspec(bt), pspec(pw1), pspec(pw2), pspec(zoh), pspec(a1), pspec(a2),
                  pspec(dv)],
        out_specs=pl.BlockSpec((t, LANES), lambda j: (0, j)),
        out_shape=jax.ShapeDtypeStruct((t, d), BF16),
        scratch_shapes=[pltpu.VMEM((gps, rows, w), BF16), pltpu.VMEM((gps, rows, w), BF16)],
        compiler_params=pltpu.CompilerParams(
            dimension_semantics=("arbitrary",), vmem_limit_bytes=VMEM_LIMIT),
        name="s5_scan",
    )(u, cc, bt, pw1, pw2, zoh, a1, a2, dv)


def _ssm_operators(a_re, a_im, b_re, b_im, c_re, c_im, log_dt, d_skip, n_chunks):
    L = SSM_CHUNK
    g, p = a_re.shape
    dt = jnp.exp(log_dt)[:, None]
    lam_re, lam_im = dt * a_re, dt * a_im

    def powers(ks):
        ks = jnp.asarray(ks, F32)[:, None, None]
        mag = jnp.exp(ks * lam_re)
        return mag * jnp.cos(ks * lam_im), mag * jnp.sin(ks * lam_im)

    def patterns(re, im):
        return (jnp.concatenate([re, re], axis=2).transpose(1, 0, 2),
                jnp.concatenate([-im, im], axis=2).transpose(1, 0, 2))

    pr, pi = powers(range(L + 1))
    num_re, num_im = pr[1] - 1.0, pi[1]
    den = a_re * a_re + a_im * a_im
    s_re = (num_re * a_re + num_im * a_im) / den
    s_im = (num_im * a_re - num_re * a_im) / den
    pw1, pw2 = patterns(pr, pi)
    zoh = jnp.concatenate(patterns(s_re[None], s_im[None]), axis=1)
    n_steps = max(1, int(math.log2(n_chunks)))
    a1, a2 = patterns(*powers([L * 2 ** j for j in range(n_steps)]))
    cc = jnp.concatenate([c_re, c_im], axis=2)
    bt = jnp.concatenate([b_re.transpose(0, 2, 1), b_im.transpose(0, 2, 1)], axis=2)
    dv = jnp.tile(d_skip.reshape(g, 1, SSM_GROUP), (1, L, 1)).reshape(g, 1, L * SSM_GROUP)
    return cc, bt, pw1, pw2, zoh, a1, a2, dv


def kernel(x, norm_mix_g, norm_mlp_g, attn_w_in, attn_b_f, attn_q_g, attn_k_g, attn_w_out,
           ssm_w_in, ssm_a_re, ssm_a_im, ssm_b_re, ssm_b_im, ssm_c_re, ssm_c_im, ssm_log_dt,
           ssm_d, ssm_w_glu, mlp_w1, mlp_w2):
    b, s, d = x.shape
    t = b * s
    tk = min(512, s // 2)
    tq = 2 * tk
    tm_proj = min(512, s)
    tm_mlp = min(512, t)

    w_in = attn_w_in[0]
    w_f = jnp.pad(w_in[:, 3 * d:], ((0, 0), (0, LANES - N_HEADS)))
    wkf = jnp.concatenate([w_in[:, d:2 * d], w_f], axis=1).astype(BF16)
    wqvt = jnp.concatenate([w_in[:, :d], w_in[:, 2 * d:3 * d]], axis=1).T.astype(BF16)
    blk = np.arange(MXU_DIM) // HEAD_DIM
    bd = jnp.asarray((blk[:, None] == blk[None, :]) * (1.0 / HEAD_DIM), BF16)
    gain_k = jnp.tile(attn_k_g[0], N_HEADS).reshape(1, d)
    gain_qt = jnp.broadcast_to(
        (jnp.tile(attn_q_g[0], N_HEADS) * (LOG2E / math.sqrt(HEAD_DIM)))[:, None], (d, LANES))
    bfp = jnp.pad(attn_b_f[0], (0, LANES - N_HEADS)).reshape(1, LANES)
    tri = jnp.asarray(np.arange(tm_proj)[:, None] >= np.arange(tm_proj)[None, :], BF16)
    heads = np.arange(N_HEADS)
    place_k = np.zeros((LANES, d), np.float32)
    for i in range(BIAS_PIECES):
        place_k[N_HEADS * i + heads, (heads // HEADS_PER_STEP) * LANES
                + BIAS_PIECES * (heads % HEADS_PER_STEP) + i] = 1.0
    place_qt = np.roll(place_k, BIAS_Q_LANE, axis=1).T
    shift = (LOG2E * math.sqrt(HEAD_DIM) * jnp.max(jnp.abs(attn_q_g[0]))
             * jnp.max(jnp.abs(attn_k_g[0])))
    qt, k, vt, kb, qbt, edge = _qkv_proj(
        x, norm_mix_g[0].reshape(1, d), wkf, wqvt, bd, gain_k, gain_qt, bfp, tri,
        jnp.asarray(place_k, BF16), jnp.asarray(place_qt, BF16), jnp.full((1, LANES), shift, F32),
        tm=tm_proj, tk=tk)
    assert tm_proj == tk
    c_first = edge[:, ::tq // tk, 0, :N_HEADS]
    c_last = edge[:, :, 1, :N_HEADS]
    bound = (c_first[:, :, None, :] - c_last[:, None, :, :]) * LOG2E
    key_blk = jnp.arange(s // tk)[None, None, :, None]
    first = jnp.min(jnp.where(bound < SKIP_LOGIT, s // tk, key_blk), axis=2)
    first = jnp.minimum(first, (tq // tk) * jnp.arange(s // tq)[None, :, None]) // 2 * 2
    first = jnp.concatenate([first.transpose(0, 2, 1).reshape(-1), (shift > MAX_SHIFT)[None]])
    dff = mlp_w1.shape[2]
    o, w1, w2 = _attention(first.astype(jnp.int32), qt, qbt, k, kb, vt,
                           mlp_w1.reshape(-1, dff), mlp_w2.reshape(-1, d), tq=tq, tk=tk)
    w1 = w1.reshape(mlp_w1.shape)
    w2 = w2.reshape(mlp_w2.shape)
    tok = np.arange(SSM_CHUNK * SSM_CHUNK)
    perm = jnp.asarray(
        tok[None, :] == (tok[:, None] % SSM_CHUNK) * SSM_CHUNK + tok[:, None] // SSM_CHUNK, BF16)
    x2, u = _mix_mlp(x.reshape(t, d), o.reshape(t, d), attn_w_out[0].astype(BF16),
                     norm_mlp_g[0].reshape(1, d), w1, w2,
                     perm, (norm_mix_g[1].reshape(1, d), ssm_w_in[0].astype(BF16)),
                     layer=0, glu=False, tm=tm_mlp)

    n_chunks = s // SSM_CHUNK
    ops = _ssm_operators(ssm_a_re[0], ssm_a_im[0], ssm_b_re[0], ssm_b_im[0], ssm_c_re[0],
                         ssm_c_im[0], ssm_log_dt[0], ssm_d[0], n_chunks)
    z = _ssm_scan(u, *ops, n_chunks=n_chunks)
    x3 = _mix_mlp(x2, z, ssm_w_glu[0].astype(BF16), norm_mlp_g[1].reshape(1, d),
                  w1, w2, perm, layer=1, glu=True, tm=tm_mlp)
    return x3.reshape(b, s, d)
```

```python
import functools
import math

import jax
import jax.numpy as jnp
import numpy as np
from jax import lax
from jax.experimental import pallas as pl
from jax.experimental.pallas import tpu as pltpu

F32 = jnp.float32
BF16 = jnp.bfloat16

N_HEADS = 16
HEAD_DIM = 64
SSM_GROUP = 16
SSM_STATE = 64
SSM_CHUNK = 16
EPS = 1e-6
LOG2E = 1.4426950408889634

LANES = 128
MXU_DIM = 256
HEADS_PER_STEP = LANES // HEAD_DIM
BIAS_PIECES = 3
VMEM_LIMIT = 56 * 1024 * 1024

NT_DIMS = (((1,), (1,)), ((), ()))


def _rms_norm(x, g):
    ms = jnp.mean(x * x, axis=-1, keepdims=True)
    return x * lax.rsqrt(ms + EPS) * g


def _const_spec(shape):
    zeros = (0,) * len(shape)
    return pl.BlockSpec(shape, lambda *_: zeros)


def _bf16_pieces(val):
    pieces = jnp.zeros_like(val)
    rem = val
    for n in range(BIAS_PIECES):
        piece = rem.astype(BF16).astype(F32)
        rem = rem - piece
        pieces = pieces + (piece if n == 0 else pltpu.roll(piece, N_HEADS * n, axis=1))
    return pieces


def _qkv_kernel(x_ref, g_ref, wkf_ref, wqvt_ref, bd_ref, gain_k_ref, gain_qt_ref, bf_ref, tri_ref,
                place_k_ref, place_qt_ref, shift_ref,
                qt_ref, k_ref, vt_ref, kb_ref, qbt_ref, edge_ref, carry_ref, *, tm, tk, d):
    @pl.when(pl.program_id(1) == 0)
    def _():
        carry_ref[...] = jnp.zeros_like(carry_ref)

    h = _rms_norm(x_ref[0], g_ref[...]).astype(BF16)

    y = jnp.dot(h, wkf_ref[...], preferred_element_type=F32)
    vt = lax.dot_general(wqvt_ref[d:, :], h, NT_DIMS, preferred_element_type=F32).astype(BF16)
    for j in range(tm // tk):
        vt_ref[0, j] = vt[:, tk * j:tk * (j + 1)]
    qt = lax.dot_general(wqvt_ref[:d, :], h, NT_DIMS, preferred_element_type=F32)

    for t in range(d // MXU_DIM):
        sl = slice(MXU_DIM * t, MXU_DIM * (t + 1))
        tile = y[:, sl]
        ms = jnp.dot((tile * tile).astype(BF16), bd_ref[...], preferred_element_type=F32)
        k_ref[0, :, sl] = (tile * lax.rsqrt(ms + EPS) * gain_k_ref[:, sl]).astype(BF16)
        tile = qt[sl, :]
        ms = jnp.dot(bd_ref[...], (tile * tile).astype(BF16), preferred_element_type=F32)
        gain = jnp.tile(gain_qt_ref[sl, :], (1, tm // LANES))
        qt_ref[0, sl, :] = (tile * lax.rsqrt(ms + EPS) * gain).astype(BF16)

    f = y[:, d:] + bf_ref[...]
    log_f = jnp.minimum(f, 0.0) - jnp.log1p(jnp.exp(-jnp.abs(f)))
    lane = lax.broadcasted_iota(jnp.int32, log_f.shape, 1)
    part = jnp.dot(tri_ref[...], _bf16_pieces(jnp.where(lane < N_HEADS, log_f, 0.0)).astype(BF16),
                   preferred_element_type=F32)
    cs = carry_ref[...] + part
    for n in range(1, BIAS_PIECES):
        cs = cs + pltpu.roll(part, LANES - N_HEADS * n, axis=1)
    carry_ref[...] = cs[tm - 1:tm, :]
    edge_ref[0, 0] = jnp.concatenate(
        [cs[0:1, :], cs[tm - 1:tm, :], jnp.zeros((6, LANES), F32)], axis=0)
    c2 = jnp.where(lane < N_HEADS, cs * LOG2E, 0.0)
    kb_ref[0] = jnp.dot(_bf16_pieces(-c2).astype(BF16), place_k_ref[...],
                        preferred_element_type=F32).astype(BF16)
    q_pieces = _bf16_pieces(jnp.where(lane < N_HEADS, c2 - shift_ref[...], 0.0))
    qbt_ref[0] = jnp.dot(place_qt_ref[...], q_pieces.T.astype(BF16),
                         preferred_element_type=F32).astype(BF16)


def _qkv_proj(x, g, wkf, wqvt, bd, gain_k, gain_qt, bfp, tri, place_k, place_qt, shift, *, tm, tk):
    b, s, d = x.shape
    row = lambda bi, si: (bi, si, 0)
    col = lambda bi, si: (bi, 0, si)
    consts = (g, wkf, wqvt, bd, gain_k, gain_qt, bfp, tri, place_k, place_qt, shift)
    return pl.pallas_call(
        functools.partial(_qkv_kernel, tm=tm, tk=tk, d=d),
        grid=(b, s // tm),
        in_specs=[pl.BlockSpec((1, tm, d), row)] + [_const_spec(c.shape) for c in consts],
        out_specs=[pl.BlockSpec((1, d, tm), col),
                   pl.BlockSpec((1, tm, d), row),
                   pl.BlockSpec((1, tm // tk, d, tk), lambda bi, si: (bi, si, 0, 0)),
                   pl.BlockSpec((1, tm, d), row),
                   pl.BlockSpec((1, d, tm), col),
                   pl.BlockSpec((1, 1, 8, LANES), lambda bi, si: (bi, si, 0, 0))],
        out_shape=[jax.ShapeDtypeStruct((b, d, s), BF16),
                   jax.ShapeDtypeStruct((b, s, d), BF16),
                   jax.ShapeDtypeStruct((b, s // tk, d, tk), BF16),
                   jax.ShapeDtypeStruct((b, s, d), BF16),
                   jax.ShapeDtypeStruct((b, d, s), BF16),
                   jax.ShapeDtypeStruct((b, s // tm, 8, LANES), F32)],
        scratch_shapes=[pltpu.VMEM((1, LANES), F32)],
        compiler_params=pltpu.CompilerParams(
            dimension_semantics=("arbitrary", "arbitrary"), vmem_limit_bytes=VMEM_LIMIT),
        name="qkv_proj",
    )(x, *consts)


BIAS_Q_LANE = 8
MAX_SHIFT = 48.0
SKIP_LOGIT = -200.0


def _attn_kernel(first_ref, *refs, tq, tk):
    *attn_in, w1_ref, w2_ref, o_ref, w1_bf_ref, w2_bf_ref = refs[:10]
    attn_refs = (*attn_in, o_ref, *refs[10:])
    w1_bf_ref[...] = w1_ref[...].astype(BF16)
    w2_bf_ref[...] = w2_ref[...].astype(BF16)

    use_online = first_ref[first_ref.shape[0] - 1]

    @pl.when(use_online == 0)
    def _():
        _attn_body(first_ref, *attn_refs, tq=tq, tk=tk, online=False)

    @pl.when(use_online != 0)
    def _():
        _attn_body(first_ref, *attn_refs, tq=tq, tk=tk, online=True)


def _attn_body(first_ref, qt_ref, qbt_ref, k_ref, kb_ref, vt_ref, o_ref, s0_scr, s1_scr, p0_scr,
               p1_scr, *, tq, tk, online):
    i = pl.program_id(2)
    n_sub = tq // tk
    qt = qt_ref[0]
    qbt = qbt_ref[0]
    row = lax.broadcasted_iota(jnp.int32, (LANES, tq), 0)
    lane_k = lax.broadcasted_iota(jnp.int32, (tk, LANES), 1)
    q_cols = (lane_k >= BIAS_Q_LANE) & (lane_k < BIAS_Q_LANE + HEADS_PER_STEP * BIAS_PIECES)
    causal = (lax.broadcasted_iota(jnp.int32, (tk, tk), 0)
              <= lax.broadcasted_iota(jnp.int32, (tk, tk), 1))
    one = jnp.ones((), BF16)
    zero = jnp.zeros((), BF16)
    heads = range(HEADS_PER_STEP)
    qats = []
    for hh in heads:
        q_h = jnp.where((row >= HEAD_DIM * hh) & (row < HEAD_DIM * (hh + 1)), qt, zero)
        k_side = (row >= BIAS_PIECES * hh) & (row < BIAS_PIECES * (hh + 1))
        q_lo = BIAS_Q_LANE + BIAS_PIECES * hh
        q_side = (row >= q_lo) & (row < q_lo + BIAS_PIECES)
        qats.append(jnp.concatenate(
            [q_h, jnp.where(k_side, one, jnp.where(q_side, qbt, zero))], axis=0))

    def scores(hh, j, c0=0):
        off = pl.multiple_of(j * tk, tk)
        kbias = jnp.where(q_cols, one, kb_ref[0, pl.ds(off, tk), :])
        ka = jnp.concatenate([k_ref[0, pl.ds(off, tk), :], kbias], axis=1)
        return jnp.dot(ka, qats[hh][:, c0:], preferred_element_type=F32)

    if not online:
        j0 = n_sub * i
        def probs(hh, l, j, c0=0, diagonal=None):
            s = scores(hh, j, c0)
            if diagonal is not False:
                head = jnp.where(causal, s[:, :tk], -1e30)
                if diagonal is None:
                    head = jnp.where(j < j0, s[:, :tk], head)
                s = head if s.shape[1] == tk else jnp.concatenate([head, s[:, tk:]], axis=1)
            p = jnp.exp2(s)
            return l + jnp.sum(p, axis=0, keepdims=True), p.astype(BF16)

        def weigh(hh, acc, p, j):
            vt = vt_ref[0, j, HEAD_DIM * hh:HEAD_DIM * (hh + 1), :]
            return acc + jnp.dot(vt, p, preferred_element_type=F32)

        def fill(hh):
            head = HEADS_PER_STEP * pl.program_id(1) + hh
            jj0 = first_ref[(pl.program_id(0) * N_HEADS + head) * pl.num_programs(2) + i] // 2
            l, p0_scr[hh] = probs(hh, jnp.zeros((1, tq), F32), 2 * jj0)
            return jj0, l

        def off_diagonal(jj0s, ls):
            def pairs(jj, cs, n_pairs, hs):
                cs = list(cs)
                for r in range(n_pairs):
                    a = 2 * (jj + r)
                    for hh in hs:
                        l, acc = cs[hh]
                        l, p1_scr[hh] = probs(hh, l, a + 1, diagonal=False)
                        cs[hh] = (l, weigh(hh, acc, p0_scr[hh], a))
                    for hh in hs:
                        l, acc = cs[hh]
                        l, p0_scr[hh] = probs(hh, l, a + 2)
                        cs[hh] = (l, weigh(hh, acc, p1_scr[hh], a + 1))
                return tuple(cs)

            jj_both = functools.reduce(jnp.maximum, jj0s)
            cs = tuple((l, jnp.zeros((HEAD_DIM, tq), F32)) for l in ls)
            for hh in heads:
                cs = lax.fori_loop(jj0s[hh], jj_both,
                                   lambda jj, cs, hh=hh: pairs(jj, cs, 1, (hh,)), cs)
            n_long = (j0 // 2 - jj_both) // 2
            cs = lax.fori_loop(0, n_long, lambda t, cs: pairs(jj_both + 2 * t, cs, 2, heads), cs)
            return lax.fori_loop(jj_both + 2 * n_long, j0 // 2,
                                 lambda jj, cs: pairs(jj, cs, 1, heads), cs)

        def diagonal(hh, l, acc):
            later = [probs(hh, l[:, tk * dd:], j0 + dd, tk * dd, diagonal=True)
                     for dd in range(1, n_sub)]
            acc = weigh(hh, acc, p0_scr[hh], j0)
            for dd in range(1, n_sub):
                c0 = tk * dd
                l_part, p = later[dd - 1]
                l = jnp.concatenate([l[:, :c0], l_part], axis=1)
                acc = jnp.concatenate([acc[:, :c0], weigh(hh, acc[:, c0:], p, j0 + dd)], axis=1)
            return acc / l

        filled = [fill(hh) for hh in heads]
        cs = off_diagonal([f[0] for f in filled], [f[1] for f in filled])
        outs = [diagonal(hh, *cs[hh]) for hh in heads]
        o_ref[0] = jnp.concatenate(outs, axis=0).T.astype(BF16)
        return

    def consume(hh, carry, s, j, diagonal):
        if diagonal:
            s_tri = jnp.where(causal, s[:, :tk], -1e30)
            s = s_tri if s.shape[1] == tk else jnp.concatenate([s_tri, s[:, tk:]], axis=1)
        vt = vt_ref[0, j, HEAD_DIM * hh:HEAD_DIM * (hh + 1), :]
        if online:
            m, l, acc = carry
            m_new = jnp.maximum(m, jnp.max(s, axis=0, keepdims=True))
            alpha = jnp.exp2(m - m_new)
            p = jnp.exp2(s - m_new)
            l = alpha * l + jnp.sum(p, axis=0, keepdims=True)
            acc = alpha * acc + jnp.dot(vt, p.astype(BF16), preferred_element_type=F32)
            return m_new, l, acc
        l, acc = carry
        p = jnp.exp2(s)
        l = l + jnp.sum(p, axis=0, keepdims=True)
        acc = acc + jnp.dot(vt, p.astype(BF16), preferred_element_type=F32)
        return l, acc

    init = (jnp.zeros((1, tq), F32), jnp.zeros((HEAD_DIM, tq), F32))
    if online:
        init = (jnp.full((1, tq), -1e30, F32),) + init

    j0 = n_sub * i

    def first_scores(hh):
        if online:
            jj0 = 0
        else:
            head = HEADS_PER_STEP * pl.program_id(1) + hh
            jj0 = first_ref[(pl.program_id(0) * N_HEADS + head) * pl.num_programs(2) + i] // 2
        s0_scr[hh] = scores(hh, 2 * jj0)
        return jj0

    def off_diagonal(hh, jj0):
        def pairs(jj, c, n_pairs):
            for r in range(n_pairs):
                s1_scr[hh] = scores(hh, 2 * (jj + r) + 1)
                c = consume(hh, c, s0_scr[hh], 2 * (jj + r), False)
                s0_scr[hh] = scores(hh, 2 * (jj + r) + 2)
                c = consume(hh, c, s1_scr[hh], 2 * (jj + r) + 1, False)
            return c

        n_long = (j0 // 2 - jj0) // 2
        c = lax.fori_loop(0, n_long, lambda t, c: pairs(jj0 + 2 * t, c, 2), init)
        return lax.fori_loop(jj0 + 2 * n_long, j0 // 2, lambda jj, c: pairs(jj, c, 1), c)

    def diagonal(hh, carry):
        later = [scores(hh, j0 + dd, tk * dd) for dd in range(1, n_sub)]
        carry = consume(hh, carry, s0_scr[hh], j0, True)
        for dd in range(1, n_sub):
            c0 = tk * dd
            part = consume(hh, tuple(c[:, c0:] for c in carry), later[dd - 1], j0 + dd, True)
            carry = tuple(jnp.concatenate([c[:, :c0], pc], axis=1) for c, pc in zip(carry, part))
        return carry[-1] / carry[-2]

    outs = []
    jj0 = first_scores(0)
    for hh in heads:
        carry = off_diagonal(hh, jj0)
        if hh + 1 < HEADS_PER_STEP:
            jj0 = first_scores(hh + 1)
        outs.append(diagonal(hh, carry))
    o_t = jnp.concatenate(outs, axis=0)
    o_ref[0] = o_t.T.astype(BF16)


def _attention(first, qt, qbt, k, kb, vt, w1, w2, *, tq, tk):
    b, s, d = k.shape
    assert (tq // tk) % 2 == 0 and s % tq == 0
    n_pairs = d // LANES
    nq = s // tq
    nk = s // tk
    n_steps = b * n_pairs * nq
    qt_blk = pl.BlockSpec((1, LANES, tq), lambda bi, hp, i, first: (bi, hp, i))
    k_all = pl.BlockSpec((1, s, LANES), lambda bi, hp, i, first: (bi, 0, hp))
    step = lambda bi, hp, i, first: ((bi * n_pairs + hp) * nq + i, 0)

    def slice_spec(w):
        assert w.shape[0] % (16 * n_steps) == 0
        return pl.BlockSpec((w.shape[0] // n_steps, w.shape[1]), step)

    return pl.pallas_call(
        functools.partial(_attn_kernel, tq=tq, tk=tk),
        grid_spec=pltpu.PrefetchScalarGridSpec(
            num_scalar_prefetch=1,
            grid=(b, n_pairs, nq),
            in_specs=[qt_blk, qt_blk, k_all, k_all,
                      pl.BlockSpec((1, nk, LANES, tk), lambda bi, hp, i, first: (bi, 0, hp, 0)),
                      slice_spec(w1), slice_spec(w2)],
            out_specs=[pl.BlockSpec((1, tq, LANES), lambda bi, hp, i, first: (bi, i, hp)),
                       slice_spec(w1), slice_spec(w2)],
            scratch_shapes=[pltpu.VMEM((HEADS_PER_STEP, tk, tq), F32),
                            pltpu.VMEM((HEADS_PER_STEP, tk, tq), F32),
                            pltpu.VMEM((HEADS_PER_STEP, tk, tq), BF16),
                            pltpu.VMEM((HEADS_PER_STEP, tk, tq), BF16)]),
        out_shape=[jax.ShapeDtypeStruct((b, s, d), BF16),
                   jax.ShapeDtypeStruct(w1.shape, BF16), jax.ShapeDtypeStruct(w2.shape, BF16)],
        compiler_params=pltpu.CompilerParams(
            dimension_semantics=("arbitrary", "arbitrary", "arbitrary"),
            vmem_limit_bytes=VMEM_LIMIT),
        name="fox_attention",
    )(first, qt, qbt, k, kb, vt, w1, w2)


def _permute_token_blocks(perm, a):
    w = perm.shape[0]
    return jnp.concatenate(
        [jnp.dot(perm, a[w * i:w * (i + 1)], preferred_element_type=F32).astype(BF16)
         for i in range(a.shape[0] // w)], axis=0)


def _mix_mlp_kernel(x_ref, a_ref, wmix_ref, g_ref, w1_ref, w2_ref, *rest, glu, d, ff_chunk):
    if glu:
        a = _permute_token_blocks(rest[0][...], a_ref[...])
        mix = jnp.dot(a, wmix_ref[...], preferred_element_type=F32)
        mix = mix[:, :d] * jax.nn.sigmoid(mix[:, d:])
    else:
        mix = jnp.dot(a_ref[...], wmix_ref[...], preferred_element_type=F32)
    x1 = x_ref[...] + mix
    h = _rms_norm(x1, g_ref[...]).astype(BF16)
    acc = x1
    for c in range(w1_ref.shape[1] // ff_chunk):
        sl = slice(ff_chunk * c, ff_chunk * (c + 1))
        hid = jnp.maximum(jnp.dot(h, w1_ref[:, sl], preferred_element_type=F32), 0.0)
        acc = acc + jnp.dot((hid * hid).astype(BF16), w2_ref[sl, :], preferred_element_type=F32)
    if glu:
        _, o_ref = rest
    else:
        perm_ref, g_next_ref, w_next_ref, o_ref, u_ref = rest
        h_next = _rms_norm(acc, g_next_ref[...]).astype(BF16)
        u = jnp.dot(h_next, w_next_ref[...], preferred_element_type=F32).astype(BF16)
        u_ref[...] = _permute_token_blocks(perm_ref[...], u)
    o_ref[...] = acc


def _mix_mlp(x, a, wmix, g, w1, w2, perm, next_proj=None, *, layer, glu, tm, ff_chunk=1024):
    t, d = x.shape
    row = lambda i: (i, 0)
    single = pl.Buffered(1)
    wspec = lambda shape: pl.BlockSpec(shape, lambda i: (0, 0), pipeline_mode=single)
    lspec = lambda w: pl.BlockSpec((None,) + w.shape[1:], lambda i: (layer, 0, 0),
                                   pipeline_mode=single)
    operands = [x, a, wmix, g, w1, w2, perm]
    in_specs = [pl.BlockSpec((tm, d), row), pl.BlockSpec((tm, d), row),
                wspec(wmix.shape), wspec((1, d)), lspec(w1), lspec(w2), wspec(perm.shape)]
    out_specs = pl.BlockSpec((tm, d), row)
    out_shape = jax.ShapeDtypeStruct((t, d), F32)
    if next_proj is not None:
        operands += list(next_proj)
        in_specs += [wspec(w.shape) for w in next_proj]
        n = next_proj[1].shape[1]
        out_specs = [out_specs, pl.BlockSpec((tm, n), row)]
        out_shape = [out_shape, jax.ShapeDtypeStruct((t, n), BF16)]
    return pl.pallas_call(
        functools.partial(_mix_mlp_kernel, glu=glu, d=d, ff_chunk=ff_chunk),
        grid=(t // tm,),
        in_specs=in_specs,
        out_specs=out_specs,
        out_shape=out_shape,
        compiler_params=pltpu.CompilerParams(
            dimension_semantics=("arbitrary",), vmem_limit_bytes=VMEM_LIMIT),
        name="mix_glu_mlp" if glu else "mix_mlp",
    )(*operands)


GROUPS_PER_SLAB = LANES // SSM_GROUP


def _piece_transpose(arrs, piece):
    arrs = list(arrs)
    dist = GROUPS_PER_SLAB // 2
    while dist >= 1:
        keep = (piece & dist) == 0
        shift = SSM_GROUP * dist
        for i in range(GROUPS_PER_SLAB):
            if i & dist:
                continue
            a, b = arrs[i], arrs[i + dist]
            arrs[i] = jnp.where(keep, a, pltpu.roll(b, shift, axis=1))
            arrs[i + dist] = jnp.where(keep, pltpu.roll(a, LANES - shift, axis=1), b)
        dist //= 2
    return arrs


def _ssm_kernel(u_ref, cc_ref, bt_ref, pw1_ref, pw2_ref, zoh_ref, a1_ref, a2_ref,
                dv_ref, z_ref, ug0_scr, ug1_scr, zg0_scr, zg1_scr, *, n_chunks, n_slabs):
    ug_scr = (ug0_scr, ug1_scr)
    zg_scr = (zg0_scr, zg1_scr)
    L = SSM_CHUNK
    rows = u_ref.shape[0] // L
    w = L * SSM_GROUP
    half = SSM_STATE
    nb = math.gcd(rows // L, 8)
    piece = lax.broadcasted_iota(jnp.int32, (nb * L, LANES), 1) // SSM_GROUP
    step_id = pl.program_id(0)

    def relayout_in(cur, rb):
        t0 = pl.multiple_of(rb * nb * w, nb * w)
        r0 = pl.multiple_of(rb * nb * L, nb * L)
        by_pos = u_ref[pl.ds(t0, nb * w), :].astype(F32)
        for hf in range(L // GROUPS_PER_SLAB):
            arrs = []
            for k in range(GROUPS_PER_SLAB):
                pos = GROUPS_PER_SLAB * hf + k
                arrs.append(jnp.concatenate(
                    [by_pos[w * blk + L * pos:w * blk + L * (pos + 1)] for blk in range(nb)],
                    axis=0))
            arrs = _piece_transpose(arrs, piece)
            for g in range(GROUPS_PER_SLAB):
                ug_scr[cur][g, pl.ds(r0, nb * L), LANES * hf:LANES * (hf + 1)] = (
                    arrs[g].astype(BF16))

    chunk = lax.broadcasted_iota(jnp.int32, (rows, LANES), 0) % n_chunks
    lane_w = lax.broadcasted_iota(jnp.int32, (SSM_GROUP, w), 1)
    low = lax.broadcasted_iota(jnp.int32, (1, LANES), 1) < half
    sign = jnp.where(low, 1.0, -1.0)

    def operators(g):
        cc = cc_ref[g]
        bt = bt_ref[g]
        pw1 = pw1_ref[g]
        pw2 = pw2_ref[g]
        zoh = zoh_ref[g]
        bbar = bt * zoh[0:1] + pltpu.roll(bt, half, axis=1) * zoh[1:2]
        bbar_sw = pltpu.roll(bbar, half, axis=1)
        cc_sw = pltpu.roll(cc, half, axis=1)
        ca = [cc * pw1[k:k + 1] + cc_sw * pw2[k:k + 1] for k in range(L + 1)]
        k_all = lax.dot_general(bbar * sign, jnp.concatenate(ca[:L], axis=0), NT_DIMS,
                                precision=lax.Precision.HIGHEST, preferred_element_type=F32)
        m_intra, w_state, w_out_t = [], [], []
        for s in range(L):
            blk = k_all if s == 0 else jnp.where(
                lane_w >= SSM_GROUP * s, pltpu.roll(k_all, SSM_GROUP * s, axis=1), 0.0)
            m_intra.append(blk.astype(BF16))
            k = L - 1 - s
            w_state.append(bbar * pw1[k:k + 1] + bbar_sw * pw2[k:k + 1])
            w_out_t.append(ca[s + 1] * sign)
        return jnp.concatenate(m_intra, axis=0), w_state, w_out_t

    def halves(pieces, second):
        firsts, seconds = [], []
        for piece in pieces:
            swapped = pltpu.roll(piece, half, axis=1)
            if second:
                firsts.append(jnp.where(low, 0.0, swapped))
                seconds.append(jnp.where(low, 0.0, piece))
            else:
                firsts.append(jnp.where(low, piece, 0.0))
                seconds.append(jnp.where(low, swapped, 0.0))
        return firsts, seconds

    def group_pair(cur, pr):
        prev = 1 - cur
        gs = (2 * pr, 2 * pr + 1)
        m_intra, ws_re, ws_im, wo_t = [], [], [], []
        for second, g in enumerate(gs):
            mi, w_state, w_out_t = operators(g)
            m_intra.append(mi)
            re, im = halves(w_state, second)
            ws_re += re
            ws_im += im
            re, im = halves(w_out_t, second)
            wo_t.append(jnp.concatenate(
                [jnp.concatenate([r, m], axis=1) for r, m in zip(re, im)], axis=0).astype(BF16))
        us = [ug_scr[prev][g] for g in gs]
        uu = jnp.concatenate(us, axis=1)
        xr = jnp.dot(uu, jnp.concatenate(ws_re, axis=0).astype(BF16), preferred_element_type=F32)
        xi = jnp.dot(uu, jnp.concatenate(ws_im, axis=0).astype(BF16), preferred_element_type=F32)
        a1 = [a1_ref[g] for g in gs]
        a2 = [a2_ref[g] for g in gs]
        step = 1
        j = 0
        while step < n_chunks:
            ar = jnp.where(low, a1[0][j:j + 1], a1[1][j:j + 1])
            ai = jnp.where(low, -a2[0][j:j + 1], a2[1][j:j + 1])
            sr = jnp.where(chunk >= step, pltpu.roll(xr, step, axis=0), 0.0)
            si = jnp.where(chunk >= step, pltpu.roll(xi, step, axis=0), 0.0)
            xr = xr + sr * ar - si * ai
            xi = xi + si * ar + sr * ai
            step *= 2
            j += 1
        x_in = jnp.concatenate([jnp.where(chunk >= 1, pltpu.roll(x, 1, axis=0), 0.0)
                                for x in (xr, xi)], axis=1).astype(BF16)
        for second, g in enumerate(gs):
            y = jnp.dot(us[second], m_intra[second], preferred_element_type=F32)
            y = y + lax.dot_general(x_in, wo_t[second], NT_DIMS, preferred_element_type=F32)
            y = y + us[second].astype(F32) * dv_ref[g]
            zg_scr[prev][g] = jax.nn.gelu(y).astype(BF16)

    def relayout_out(cur, rb):
        t0 = pl.multiple_of(rb * nb * w, nb * w)
        r0 = pl.multiple_of(rb * nb * L, nb * L)
        by_pos = []
        for hf in range(L // GROUPS_PER_SLAB):
            arrs = [zg_scr[cur][g, pl.ds(r0, nb * L), LANES * hf:LANES * (hf + 1)].astype(F32)
                    for g in range(GROUPS_PER_SLAB)]
            by_pos += _piece_transpose(arrs, piece)
        for blk in range(nb):
            z_ref[pl.ds(t0 + w * blk, w), :] = jnp.concatenate(
                [arr[L * blk:L * (blk + 1)] for arr in by_pos], axis=0).astype(BF16)

    def loop(n, *stages):
        def body(i, carry):
            for stage, per_trip in stages:
                for r in range(per_trip):
                    stage(per_trip * i + r)
            return carry
        lax.fori_loop(0, n, body, 0)

    n_rb = rows // (nb * L)
    n_pr = GROUPS_PER_SLAB // 2

    @pl.when(step_id == 0)
    def _():
        zg_scr[1][...] = jnp.zeros(zg_scr[1].shape, BF16)
        loop(n_rb, (functools.partial(relayout_in, 0), 1))

    def inner(cur):
        stage_in = functools.partial(relayout_in, cur)
        stage_scan = functools.partial(group_pair, cur)
        stage_out = functools.partial(relayout_out, cur)
        if n_rb % n_pr == 0:
            per = n_rb // n_pr
            loop(n_pr, (stage_scan, 1), (stage_in, per), (stage_out, per))
        else:
            loop(n_rb, (stage_in, 1))
            loop(n_pr, (stage_scan, 1))
            loop(n_rb, (stage_out, 1))

    for cur in range(2):
        pl.when((step_id > 0) & (step_id <= n_slabs) & (step_id % 2 == cur))(
            functools.partial(inner, cur))

    @pl.when(step_id == n_slabs + 1)
    def _():
        loop(n_rb, (functools.partial(relayout_out, (n_slabs + 1) % 2), 1))


def _ssm_scan(u, cc, bt, pw1, pw2, zoh, a1, a2, dv, *, n_chunks):
    t, d = u.shape
    L = SSM_CHUNK
    rows = t // L
    w = L * SSM_GROUP
    gps = GROUPS_PER_SLAB
    n_slabs = d // LANES
    clamp = lambda j: jnp.clip(j, 0, n_slabs - 1)
    pspec = lambda arr: pl.BlockSpec((gps,) + arr.shape[1:], lambda j: (clamp(j - 1), 0, 0))
    return pl.pallas_call(
        functools.partial(_ssm_kernel, n_chunks=n_chunks, n_slabs=n_slabs),
        grid=(n_slabs + 2,),
        in_specs=[pl.BlockSpec((t, LANES), lambda j: (0, clamp(j))),
                  pspec(cc), pspec(bt), pspec(pw1), pspec(pw2), pspec(zoh), pspec(a1), pspec(a2),
                  pspec(dv)],
        out_specs=pl.BlockSpec((t, LANES), lambda j: (0, clamp(j - 2))),
        out_shape=jax.ShapeDtypeStruct((t, d), BF16),
        scratch_shapes=[pltpu.VMEM((gps, rows, w), BF16)] * 4,
        compiler_params=pltpu.CompilerParams(
            dimension_semantics=("arbitrary",), vmem_limit_bytes=VMEM_LIMIT),
        name="s5_scan",
    )(u, cc, bt, pw1, pw2, zoh, a1, a2, dv)


def _ssm_operators(a_re, a_im, b_re, b_im, c_re, c_im, log_dt, d_skip, n_chunks):
    L = SSM_CHUNK
    g, p = a_re.shape
    dt = jnp.exp(log_dt)[:, None]
    lam_re, lam_im = dt * a_re, dt * a_im

    def powers(ks):
        ks = jnp.asarray(ks, F32)[:, None, None]
        mag = jnp.exp(ks * lam_re)
        return mag * jnp.cos(ks * lam_im), mag * jnp.sin(ks * lam_im)

    def patterns(re, im):
        return (jnp.concatenate([re, re], axis=2).transpose(1, 0, 2),
                jnp.concatenate([-im, im], axis=2).transpose(1, 0, 2))

    pr, pi = powers(range(L + 1))
    num_re, num_im = pr[1] - 1.0, pi[1]
    den = a_re * a_re + a_im * a_im
    s_re = (num_re * a_re + num_im * a_im) / den
    s_im = (num_im * a_re - num_re * a_im) / den
    pw1, pw2 = patterns(pr, pi)
    zoh = jnp.concatenate(patterns(s_re[None], s_im[None]), axis=1)
    n_steps = max(1, int(math.log2(n_chunks)))
    a1, a2 = patterns(*powers([L * 2 ** j for j in range(n_steps)]))
    cc = jnp.concatenate([c_re, c_im], axis=2)
    bt = jnp.concatenate([b_re.transpose(0, 2, 1), b_im.transpose(0, 2, 1)], axis=2)
    dv = jnp.tile(d_skip.reshape(g, 1, SSM_GROUP), (1, L, 1)).reshape(g, 1, L * SSM_GROUP)
    return cc, bt, pw1, pw2, zoh, a1, a2, dv


def kernel(x, norm_mix_g, norm_mlp_g, attn_w_in, attn_b_f, attn_q_g, attn_k_g, attn_w_out,
           ssm_w_in, ssm_a_re, ssm_a_im, ssm_b_re, ssm_b_im, ssm_c_re, ssm_c_im, ssm_log_dt,
           ssm_d, ssm_w_glu, mlp_w1, mlp_w2):
    b, s, d = x.shape
    t = b * s
    tk = min(512, s // 2)
    tq = 2 * tk
    tm_proj = min(512, s)
    tm_mlp = min(512, t)

    w_in = attn_w_in[0]
    w_f = jnp.pad(w_in[:, 3 * d:], ((0, 0), (0, LANES - N_HEADS)))
    wkf = jnp.concatenate([w_in[:, d:2 * d], w_f], axis=1).astype(BF16)
    wqvt = jnp.concatenate([w_in[:, :d], w_in[:, 2 * d:3 * d]], axis=1).T.astype(BF16)
    blk = np.arange(MXU_DIM) // HEAD_DIM
    bd = jnp.asarray((blk[:, None] == blk[None, :]) * (1.0 / HEAD_DIM), BF16)
    gain_k = jnp.tile(attn_k_g[0], N_HEADS).reshape(1, d)
    gain_qt = jnp.broadcast_to(
        (jnp.tile(attn_q_g[0], N_HEADS) * (LOG2E / math.sqrt(HEAD_DIM)))[:, None], (d, LANES))
    bfp = jnp.pad(attn_b_f[0], (0, LANES - N_HEADS)).reshape(1, LANES)
    tri = jnp.asarray(np.arange(tm_proj)[:, None] >= np.arange(tm_proj)[None, :], BF16)
    heads = np.arange(N_HEADS)
    place_k = np.zeros((LANES, d), np.float32)
    for i in range(BIAS_PIECES):
        place_k[N_HEADS * i + heads, (heads // HEADS_PER_STEP) * LANES
                + BIAS_PIECES * (heads % HEADS_PER_STEP) + i] = 1.0
    place_qt = np.roll(place_k, BIAS_Q_LANE, axis=1).T
    shift = (LOG2E * math.sqrt(HEAD_DIM) * jnp.max(jnp.abs(attn_q_g[0]))
             * jnp.max(jnp.abs(attn_k_g[0])))
    qt, k, vt, kb, qbt, edge = _qkv_proj(
        x, norm_mix_g[0].reshape(1, d), wkf, wqvt, bd, gain_k, gain_qt, bfp, tri,
        jnp.asarray(place_k, BF16), jnp.asarray(place_qt, BF16), jnp.full((1, LANES), shift, F32),
        tm=tm_proj, tk=tk)
    assert tm_proj == tk
    c_first = edge[:, ::tq // tk, 0, :N_HEADS]
    c_last = edge[:, :, 1, :N_HEADS]
    bound = (c_first[:, :, None, :] - c_last[:, None, :, :]) * LOG2E
    key_blk = jnp.arange(s // tk)[None, None, :, None]
    first = jnp.min(jnp.where(bound < SKIP_LOGIT, s // tk, key_blk), axis=2)
    first = jnp.minimum(first, (tq // tk) * jnp.arange(s // tq)[None, :, None]) // 2 * 2
    first = jnp.concatenate([first.transpose(0, 2, 1).reshape(-1), (shift > MAX_SHIFT)[None]])
    dff = mlp_w1.shape[2]
    o, w1, w2 = _attention(first.astype(jnp.int32), qt, qbt, k, kb, vt,
                           mlp_w1.reshape(-1, dff), mlp_w2.reshape(-1, d), tq=tq, tk=tk)
    w1 = w1.reshape(mlp_w1.shape)
    w2 = w2.reshape(mlp_w2.shape)
    tok = np.arange(SSM_CHUNK * SSM_CHUNK)
    perm = jnp.asarray(
        tok[None, :] == (tok[:, None] % SSM_CHUNK) * SSM_CHUNK + tok[:, None] // SSM_CHUNK, BF16)
    x2, u = _mix_mlp(x.reshape(t, d), o.reshape(t, d), attn_w_out[0].astype(BF16),
                     norm_mlp_g[0].reshape(1, d), w1, w2,
                     perm, (norm_mix_g[1].reshape(1, d), ssm_w_in[0].astype(BF16)),
                     layer=0, glu=False, tm=tm_mlp)

    n_chunks = s // SSM_CHUNK
    ops = _ssm_operators(ssm_a_re[0], ssm_a_im[0], ssm_b_re[0], ssm_b_im[0], ssm_c_re[0],
                         ssm_c_im[0], ssm_log_dt[0], ssm_d[0], n_chunks)
    z = _ssm_scan(u, *ops, n_chunks=n_chunks)
    x3 = _mix_mlp(x2, z, ssm_w_glu[0].astype(BF16), norm_mlp_g[1].reshape(1, d),
                  w1, w2, perm, layer=1, glu=True, tm=tm_mlp)
    return x3.reshape(b, s, d)
```

```python
import functools
import math

import jax
import jax.numpy as jnp
import numpy as np
from jax import lax
from jax.experimental import pallas as pl
from jax.experimental.pallas import tpu as pltpu

F32 = jnp.float32
BF16 = jnp.bfloat16

N_HEADS = 16
HEAD_DIM = 64
SSM_GROUP = 16
SSM_STATE = 64
SSM_CHUNK = 16
EPS = 1e-6
LOG2E = 1.4426950408889634

LANES = 128
MXU_DIM = 256
HEADS_PER_STEP = LANES // HEAD_DIM
BIAS_PIECES = 3
VMEM_LIMIT = 56 * 1024 * 1024

NT_DIMS = (((1,), (1,)), ((), ()))


def _rms_norm(x, g):
    ms = jnp.mean(x * x, axis=-1, keepdims=True)
    return x * lax.rsqrt(ms + EPS) * g


def _const_spec(shape):
    zeros = (0,) * len(shape)
    return pl.BlockSpec(shape, lambda *_: zeros)


def _bf16_pieces(val):
    pieces = jnp.zeros_like(val)
    rem = val
    for n in range(BIAS_PIECES):
        piece = rem.astype(BF16).astype(F32)
        rem = rem - piece
        pieces = pieces + (piece if n == 0 else pltpu.roll(piece, N_HEADS * n, axis=1))
    return pieces


def _qkv_kernel(x_ref, g_ref, wkf_ref, wqvt_ref, bd_ref, gain_k_ref, gain_qt_ref, bf_ref, tri_ref,
                place_k_ref, place_qt_ref, shift_ref,
                qt_ref, k_ref, vt_ref, kb_ref, qbt_ref, edge_ref, carry_ref, *, tm, tk, d):
    @pl.when(pl.program_id(1) == 0)
    def _():
        carry_ref[...] = jnp.zeros_like(carry_ref)

    h = _rms_norm(x_ref[0], g_ref[...]).astype(BF16)

    y = jnp.dot(h, wkf_ref[...], preferred_element_type=F32)
    vt = lax.dot_general(wqvt_ref[d:, :], h, NT_DIMS, preferred_element_type=F32).astype(BF16)
    for j in range(tm // tk):
        vt_ref[0, j] = vt[:, tk * j:tk * (j + 1)]
    qt = lax.dot_general(wqvt_ref[:d, :], h, NT_DIMS, preferred_element_type=F32)

    for t in range(d // MXU_DIM):
        sl = slice(MXU_DIM * t, MXU_DIM * (t + 1))
        tile = y[:, sl]
        ms = jnp.dot((tile * tile).astype(BF16), bd_ref[...], preferred_element_type=F32)
        k_ref[0, :, sl] = (tile * lax.rsqrt(ms + EPS) * gain_k_ref[:, sl]).astype(BF16)
        tile = qt[sl, :]
        ms = jnp.dot(bd_ref[...], (tile * tile).astype(BF16), preferred_element_type=F32)
        gain = jnp.tile(gain_qt_ref[sl, :], (1, tm // LANES))
        qt_ref[0, sl, :] = (tile * lax.rsqrt(ms + EPS) * gain).astype(BF16)

    f = y[:, d:] + bf_ref[...]
    log_f = jnp.minimum(f, 0.0) - jnp.log1p(jnp.exp(-jnp.abs(f)))
    lane = lax.broadcasted_iota(jnp.int32, log_f.shape, 1)
    part = jnp.dot(tri_ref[...], _bf16_pieces(jnp.where(lane < N_HEADS, log_f, 0.0)).astype(BF16),
                   preferred_element_type=F32)
    cs = carry_ref[...] + part
    for n in range(1, BIAS_PIECES):
        cs = cs + pltpu.roll(part, LANES - N_HEADS * n, axis=1)
    carry_ref[...] = cs[tm - 1:tm, :]
    edge_ref[0, 0] = jnp.concatenate(
        [cs[0:1, :], cs[tm - 1:tm, :], jnp.zeros((6, LANES), F32)], axis=0)
    c2 = jnp.where(lane < N_HEADS, cs * LOG2E, 0.0)
    kb_ref[0] = jnp.dot(_bf16_pieces(-c2).astype(BF16), place_k_ref[...],
                        preferred_element_type=F32).astype(BF16)
    q_pieces = _bf16_pieces(jnp.where(lane < N_HEADS, c2 - shift_ref[...], 0.0))
    qbt_ref[0] = jnp.dot(place_qt_ref[...], q_pieces.T.astype(BF16),
                         preferred_element_type=F32).astype(BF16)


def _qkv_proj(x, g, wkf, wqvt, bd, gain_k, gain_qt, bfp, tri, place_k, place_qt, shift, *, tm, tk):
    b, s, d = x.shape
    row = lambda bi, si: (bi, si, 0)
    col = lambda bi, si: (bi, 0, si)
    consts = (g, wkf, wqvt, bd, gain_k, gain_qt, bfp, tri, place_k, place_qt, shift)
    return pl.pallas_call(
        functools.partial(_qkv_kernel, tm=tm, tk=tk, d=d),
        grid=(b, s // tm),
        in_specs=[pl.BlockSpec((1, tm, d), row)] + [_const_spec(c.shape) for c in consts],
        out_specs=[pl.BlockSpec((1, d, tm), col),
                   pl.BlockSpec((1, tm, d), row),
                   pl.BlockSpec((1, tm // tk, d, tk), lambda bi, si: (bi, si, 0, 0)),
                   pl.BlockSpec((1, tm, d), row),
                   pl.BlockSpec((1, d, tm), col),
                   pl.BlockSpec((1, 1, 8, LANES), lambda bi, si: (bi, si, 0, 0))],
        out_shape=[jax.ShapeDtypeStruct((b, d, s), BF16),
                   jax.ShapeDtypeStruct((b, s, d), BF16),
                   jax.ShapeDtypeStruct((b, s // tk, d, tk), BF16),
                   jax.ShapeDtypeStruct((b, s, d), BF16),
                   jax.ShapeDtypeStruct((b, d, s), BF16),
                   jax.ShapeDtypeStruct((b, s // tm, 8, LANES), F32)],
        scratch_shapes=[pltpu.VMEM((1, LANES), F32)],
        compiler_params=pltpu.CompilerParams(
            dimension_semantics=("arbitrary", "arbitrary"), vmem_limit_bytes=VMEM_LIMIT),
        name="qkv_proj",
    )(x, *consts)


BIAS_Q_LANE = 8
MAX_SHIFT = 48.0
SKIP_LOGIT = -200.0


def _attn_kernel(first_ref, *refs, tq, tk):
    *attn_in, w1_ref, w2_ref, o_ref, w1_bf_ref, w2_bf_ref = refs[:10]
    attn_refs = (*attn_in, o_ref, *refs[10:])
    w1_bf_ref[...] = w1_ref[...].astype(BF16)
    w2_bf_ref[...] = w2_ref[...].astype(BF16)

    use_online = first_ref[first_ref.shape[0] - 1]

    @pl.when(use_online == 0)
    def _():
        _attn_body(first_ref, *attn_refs, tq=tq, tk=tk, online=False)

    @pl.when(use_online != 0)
    def _():
        _attn_body(first_ref, *attn_refs, tq=tq, tk=tk, online=True)


def _attn_body(first_ref, qt_ref, qbt_ref, k_ref, kb_ref, vt_ref, o_ref, s0_scr, s1_scr, p0_scr,
               p1_scr, *, tq, tk, online):
    i = pl.program_id(2)
    n_sub = tq // tk
    qt = qt_ref[0]
    qbt = qbt_ref[0]
    row = lax.broadcasted_iota(jnp.int32, (LANES, tq), 0)
    lane_k = lax.broadcasted_iota(jnp.int32, (tk, LANES), 1)
    q_cols = (lane_k >= BIAS_Q_LANE) & (lane_k < BIAS_Q_LANE + HEADS_PER_STEP * BIAS_PIECES)
    causal = (lax.broadcasted_iota(jnp.int32, (tk, tk), 0)
              <= lax.broadcasted_iota(jnp.int32, (tk, tk), 1))
    one = jnp.ones((), BF16)
    zero = jnp.zeros((), BF16)
    heads = range(HEADS_PER_STEP)
    qats = []
    for hh in heads:
        q_h = jnp.where((row >= HEAD_DIM * hh) & (row < HEAD_DIM * (hh + 1)), qt, zero)
        k_side = (row >= BIAS_PIECES * hh) & (row < BIAS_PIECES * (hh + 1))
        q_lo = BIAS_Q_LANE + BIAS_PIECES * hh
        q_side = (row >= q_lo) & (row < q_lo + BIAS_PIECES)
        qats.append(jnp.concatenate(
            [q_h, jnp.where(k_side, one, jnp.where(q_side, qbt, zero))], axis=0))

    def scores(hh, j, c0=0):
        off = pl.multiple_of(j * tk, tk)
        kbias = jnp.where(q_cols, one, kb_ref[0, pl.ds(off, tk), :])
        ka = jnp.concatenate([k_ref[0, pl.ds(off, tk), :], kbias], axis=1)
        return jnp.dot(ka, qats[hh][:, c0:], preferred_element_type=F32)

    if not online:
        j0 = n_sub * i
        def probs(hh, l, j, c0=0, diagonal=None):
            s = scores(hh, j, c0)
            if diagonal is not False:
                head = jnp.where(causal, s[:, :tk], -1e30)
                if diagonal is None:
                    head = jnp.where(j < j0, s[:, :tk], head)
                s = head if s.shape[1] == tk else jnp.concatenate([head, s[:, tk:]], axis=1)
            p = jnp.exp2(s)
            return l + jnp.sum(p, axis=0, keepdims=True), p.astype(BF16)

        def weigh(hh, acc, p, j):
            vt = vt_ref[0, j, HEAD_DIM * hh:HEAD_DIM * (hh + 1), :]
            return acc + jnp.dot(vt, p, preferred_element_type=F32)

        def first_pair(hh):
            head = HEADS_PER_STEP * pl.program_id(1) + hh
            return first_ref[(pl.program_id(0) * N_HEADS + head) * pl.num_programs(2) + i] // 2

        def fill(hh):
            jj0 = first_pair(hh)
            l, p0_scr[hh] = probs(hh, jnp.zeros((1, tq), F32), 2 * jj0)
            return jj0, l

        def off_diagonal(jj0s, ls, static_pairs=None):
            def pairs(jj, cs, n_pairs, hs):
                cs = list(cs)
                for r in range(n_pairs):
                    a = 2 * (jj + r)
                    for hh in hs:
                        l, acc = cs[hh]
                        l, p1_scr[hh] = probs(hh, l, a + 1, diagonal=False)
                        cs[hh] = (l, weigh(hh, acc, p0_scr[hh], a))
                    for hh in hs:
                        l, acc = cs[hh]
                        l, p0_scr[hh] = probs(hh, l, a + 2)
                        cs[hh] = (l, weigh(hh, acc, p1_scr[hh], a + 1))
                return tuple(cs)

            jj_both = functools.reduce(jnp.maximum, jj0s)
            cs = tuple((l, jnp.zeros((HEAD_DIM, tq), F32)) for l in ls)
            if static_pairs is not None:
                return pairs(0, cs, static_pairs, heads)
            for hh in heads:
                cs = lax.fori_loop(jj0s[hh], jj_both,
                                   lambda jj, cs, hh=hh: pairs(jj, cs, 1, (hh,)), cs)
            n_long = (j0 // 2 - jj_both) // 2
            cs = lax.fori_loop(0, n_long, lambda t, cs: pairs(jj_both + 2 * t, cs, 2, heads), cs)
            return lax.fori_loop(jj_both + 2 * n_long, j0 // 2,
                                 lambda jj, cs: pairs(jj, cs, 1, heads), cs)

        def diagonal(hh, l, acc):
            later = [probs(hh, l[:, tk * dd:], j0 + dd, tk * dd, diagonal=True)
                     for dd in range(1, n_sub)]
            acc = weigh(hh, acc, p0_scr[hh], j0)
            for dd in range(1, n_sub):
                c0 = tk * dd
                l_part, p = later[dd - 1]
                l = jnp.concatenate([l[:, :c0], l_part], axis=1)
                acc = jnp.concatenate([acc[:, :c0], weigh(hh, acc[:, c0:], p, j0 + dd)], axis=1)
            return acc / l

        def run(static_pairs):
            filled = [fill(hh) for hh in heads]
            cs = off_diagonal([f[0] for f in filled], [f[1] for f in filled], static_pairs)
            outs = [diagonal(hh, *cs[hh]) for hh in heads]
            o_ref[0] = jnp.concatenate(outs, axis=0).T.astype(BF16)

        unskipped = functools.reduce(jnp.logical_and, [first_pair(hh) == 0 for hh in heads])
        n_static = 3
        static = [(i == n) & unskipped for n in range(n_static)]
        for n in range(n_static):
            pl.when(static[n])(functools.partial(run, n * n_sub // 2))
        pl.when(jnp.logical_not(functools.reduce(jnp.logical_or, static)))(
            functools.partial(run, None))
        return

    def consume(hh, carry, s, j, diagonal):
        if diagonal:
            s_tri = jnp.where(causal, s[:, :tk], -1e30)
            s = s_tri if s.shape[1] == tk else jnp.concatenate([s_tri, s[:, tk:]], axis=1)
        vt = vt_ref[0, j, HEAD_DIM * hh:HEAD_DIM * (hh + 1), :]
        if online:
            m, l, acc = carry
            m_new = jnp.maximum(m, jnp.max(s, axis=0, keepdims=True))
            alpha = jnp.exp2(m - m_new)
            p = jnp.exp2(s - m_new)
            l = alpha * l + jnp.sum(p, axis=0, keepdims=True)
            acc = alpha * acc + jnp.dot(vt, p.astype(BF16), preferred_element_type=F32)
            return m_new, l, acc
        l, acc = carry
        p = jnp.exp2(s)
        l = l + jnp.sum(p, axis=0, keepdims=True)
        acc = acc + jnp.dot(vt, p.astype(BF16), preferred_element_type=F32)
        return l, acc

    init = (jnp.zeros((1, tq), F32), jnp.zeros((HEAD_DIM, tq), F32))
    if online:
        init = (jnp.full((1, tq), -1e30, F32),) + init

    j0 = n_sub * i

    def first_scores(hh):
        if online:
            jj0 = 0
        else:
            head = HEADS_PER_STEP * pl.program_id(1) + hh
            jj0 = first_ref[(pl.program_id(0) * N_HEADS + head) * pl.num_programs(2) + i] // 2
        s0_scr[hh] = scores(hh, 2 * jj0)
        return jj0

    def off_diagonal(hh, jj0):
        def pairs(jj, c, n_pairs):
            for r in range(n_pairs):
                s1_scr[hh] = scores(hh, 2 * (jj + r) + 1)
                c = consume(hh, c, s0_scr[hh], 2 * (jj + r), False)
                s0_scr[hh] = scores(hh, 2 * (jj + r) + 2)
                c = consume(hh, c, s1_scr[hh], 2 * (jj + r) + 1, False)
            return c

        n_long = (j0 // 2 - jj0) // 2
        c = lax.fori_loop(0, n_long, lambda t, c: pairs(jj0 + 2 * t, c, 2), init)
        return lax.fori_loop(jj0 + 2 * n_long, j0 // 2, lambda jj, c: pairs(jj, c, 1), c)

    def diagonal(hh, carry):
        later = [scores(hh, j0 + dd, tk * dd) for dd in range(1, n_sub)]
        carry = consume(hh, carry, s0_scr[hh], j0, True)
        for dd in range(1, n_sub):
            c0 = tk * dd
            part = consume(hh, tuple(c[:, c0:] for c in carry), later[dd - 1], j0 + dd, True)
            carry = tuple(jnp.concatenate([c[:, :c0], pc], axis=1) for c, pc in zip(carry, part))
        return carry[-1] / carry[-2]

    outs = []
    jj0 = first_scores(0)
    for hh in heads:
        carry = off_diagonal(hh, jj0)
        if hh + 1 < HEADS_PER_STEP:
            jj0 = first_scores(hh + 1)
        outs.append(diagonal(hh, carry))
    o_t = jnp.concatenate(outs, axis=0)
    o_ref[0] = o_t.T.astype(BF16)


def _attention(first, qt, qbt, k, kb, vt, w1, w2, *, tq, tk):
    b, s, d = k.shape
    assert (tq // tk) % 2 == 0 and s % tq == 0
    n_pairs = d // LANES
    nq = s // tq
    nk = s // tk
    n_steps = b * n_pairs * nq
    qt_blk = pl.BlockSpec((1, LANES, tq), lambda bi, hp, i, first: (bi, hp, i))
    k_all = pl.BlockSpec((1, s, LANES), lambda bi, hp, i, first: (bi, 0, hp))
    step = lambda bi, hp, i, first: ((bi * n_pairs + hp) * nq + i, 0)

    def slice_spec(w):
        assert w.shape[0] % (16 * n_steps) == 0
        return pl.BlockSpec((w.shape[0] // n_steps, w.shape[1]), step)

    return pl.pallas_call(
        functools.partial(_attn_kernel, tq=tq, tk=tk),
        grid_spec=pltpu.PrefetchScalarGridSpec(
            num_scalar_prefetch=1,
            grid=(b, n_pairs, nq),
            in_specs=[qt_blk, qt_blk, k_all, k_all,
                      pl.BlockSpec((1, nk, LANES, tk), lambda bi, hp, i, first: (bi, 0, hp, 0)),
                      slice_spec(w1), slice_spec(w2)],
            out_specs=[pl.BlockSpec((1, tq, LANES), lambda bi, hp, i, first: (bi, i, hp)),
                       slice_spec(w1), slice_spec(w2)],
            scratch_shapes=[pltpu.VMEM((HEADS_PER_STEP, tk, tq), F32),
                            pltpu.VMEM((HEADS_PER_STEP, tk, tq), F32),
                            pltpu.VMEM((HEADS_PER_STEP, tk, tq), BF16),
                            pltpu.VMEM((HEADS_PER_STEP, tk, tq), BF16)]),
        out_shape=[jax.ShapeDtypeStruct((b, s, d), BF16),
                   jax.ShapeDtypeStruct(w1.shape, BF16), jax.ShapeDtypeStruct(w2.shape, BF16)],
        compiler_params=pltpu.CompilerParams(
            dimension_semantics=("arbitrary", "arbitrary", "arbitrary"),
            vmem_limit_bytes=VMEM_LIMIT),
        name="fox_attention",
    )(first, qt, qbt, k, kb, vt, w1, w2)


def _permute_token_blocks(perm, a):
    w = perm.shape[0]
    return jnp.concatenate(
        [jnp.dot(perm, a[w * i:w * (i + 1)], preferred_element_type=F32).astype(BF16)
         for i in range(a.shape[0] // w)], axis=0)


def _mix_mlp_kernel(x_ref, a_ref, wmix_ref, g_ref, w1_ref, w2_ref, *rest, glu, d, ff_chunk):
    if glu:
        a = _permute_token_blocks(rest[0][...], a_ref[...])
        mix = jnp.dot(a, wmix_ref[...], preferred_element_type=F32)
        mix = mix[:, :d] * jax.nn.sigmoid(mix[:, d:])
    else:
        mix = jnp.dot(a_ref[...], wmix_ref[...], preferred_element_type=F32)
    x1 = x_ref[...] + mix
    h = _rms_norm(x1, g_ref[...]).astype(BF16)
    acc = x1
    for c in range(w1_ref.shape[1] // ff_chunk):
        sl = slice(ff_chunk * c, ff_chunk * (c + 1))
        hid = jnp.maximum(jnp.dot(h, w1_ref[:, sl], preferred_element_type=F32), 0.0)
        acc = acc + jnp.dot((hid * hid).astype(BF16), w2_ref[sl, :], preferred_element_type=F32)
    if glu:
        _, o_ref = rest
    else:
        perm_ref, g_next_ref, w_next_ref, o_ref, u_ref = rest
        h_next = _rms_norm(acc, g_next_ref[...]).astype(BF16)
        u = jnp.dot(h_next, w_next_ref[...], preferred_element_type=F32).astype(BF16)
        u_ref[...] = _permute_token_blocks(perm_ref[...], u)
    o_ref[...] = acc


def _mix_mlp(x, a, wmix, g, w1, w2, perm, next_proj=None, *, layer, glu, tm, ff_chunk=1024):
    t, d = x.shape
    row = lambda i: (i, 0)
    single = pl.Buffered(1)
    wspec = lambda shape: pl.BlockSpec(shape, lambda i: (0, 0), pipeline_mode=single)
    lspec = lambda w: pl.BlockSpec((None,) + w.shape[1:], lambda i: (layer, 0, 0),
                                   pipeline_mode=single)
    operands = [x, a, wmix, g, w1, w2, perm]
    in_specs = [pl.BlockSpec((tm, d), row), pl.BlockSpec((tm, d), row),
                wspec(wmix.shape), wspec((1, d)), lspec(w1), lspec(w2), wspec(perm.shape)]
    out_specs = pl.BlockSpec((tm, d), row)
    out_shape = jax.ShapeDtypeStruct((t, d), F32)
    if next_proj is not None:
        operands += list(next_proj)
        in_specs += [wspec(w.shape) for w in next_proj]
        n = next_proj[1].shape[1]
        out_specs = [out_specs, pl.BlockSpec((tm, n), row)]
        out_shape = [out_shape, jax.ShapeDtypeStruct((t, n), BF16)]
    return pl.pallas_call(
        functools.partial(_mix_mlp_kernel, glu=glu, d=d, ff_chunk=ff_chunk),
        grid=(t // tm,),
        in_specs=in_specs,
        out_specs=out_specs,
        out_shape=out_shape,
        compiler_params=pltpu.CompilerParams(
            dimension_semantics=("arbitrary",), vmem_limit_bytes=VMEM_LIMIT),
        name="mix_glu_mlp" if glu else "mix_mlp",
    )(*operands)


GROUPS_PER_SLAB = LANES // SSM_GROUP


def _piece_transpose(arrs, piece):
    arrs = list(arrs)
    dist = GROUPS_PER_SLAB // 2
    while dist >= 1:
        keep = (piece & dist) == 0
        shift = SSM_GROUP * dist
        for i in range(GROUPS_PER_SLAB):
            if i & dist:
                continue
            a, b = arrs[i], arrs[i + dist]
            arrs[i] = jnp.where(keep, a, pltpu.roll(b, shift, axis=1))
            arrs[i + dist] = jnp.where(keep, pltpu.roll(a, LANES - shift, axis=1), b)
        dist //= 2
    return arrs


def _ssm_kernel(u_ref, cc_ref, bt_ref, pw1_ref, pw2_ref, zoh_ref, a1_ref, a2_ref,
                dv_ref, z_ref, ug0_scr, ug1_scr, zg0_scr, zg1_scr, *, n_chunks, n_slabs):
    ug_scr = (ug0_scr, ug1_scr)
    zg_scr = (zg0_scr, zg1_scr)
    L = SSM_CHUNK
    rows = u_ref.shape[0] // L
    w = L * SSM_GROUP
    half = SSM_STATE
    nb = math.gcd(rows // L, 8)
    piece = lax.broadcasted_iota(jnp.int32, (nb * L, LANES), 1) // SSM_GROUP
    step_id = pl.program_id(0)

    def relayout_in(cur, rb):
        t0 = pl.multiple_of(rb * nb * w, nb * w)
        r0 = pl.multiple_of(rb * nb * L, nb * L)
        by_pos = u_ref[pl.ds(t0, nb * w), :].astype(F32)
        for hf in range(L // GROUPS_PER_SLAB):
            arrs = []
            for k in range(GROUPS_PER_SLAB):
                pos = GROUPS_PER_SLAB * hf + k
                arrs.append(jnp.concatenate(
                    [by_pos[w * blk + L * pos:w * blk + L * (pos + 1)] for blk in range(nb)],
                    axis=0))
            arrs = _piece_transpose(arrs, piece)
            for g in range(GROUPS_PER_SLAB):
                ug_scr[cur][g, pl.ds(r0, nb * L), LANES * hf:LANES * (hf + 1)] = (
                    arrs[g].astype(BF16))

    chunk = lax.broadcasted_iota(jnp.int32, (rows, LANES), 0) % n_chunks
    lane_w = lax.broadcasted_iota(jnp.int32, (SSM_GROUP, w), 1)
    low = lax.broadcasted_iota(jnp.int32, (1, LANES), 1) < half
    sign = jnp.where(low, 1.0, -1.0)

    def operators(g):
        cc = cc_ref[g]
        bt = bt_ref[g]
        pw1 = pw1_ref[g]
        pw2 = pw2_ref[g]
        zoh = zoh_ref[g]
        bbar = bt * zoh[0:1] + pltpu.roll(bt, half, axis=1) * zoh[1:2]
        bbar_sw = pltpu.roll(bbar, half, axis=1)
        cc_sw = pltpu.roll(cc, half, axis=1)
        ca = [cc * pw1[k:k + 1] + cc_sw * pw2[k:k + 1] for k in range(L + 1)]
        k_all = lax.dot_general(bbar * sign, jnp.concatenate(ca[:L], axis=0), NT_DIMS,
                                precision=lax.Precision.HIGHEST, preferred_element_type=F32)
        m_intra, w_state, w_out_t = [], [], []
        for s in range(L):
            blk = k_all if s == 0 else jnp.where(
                lane_w >= SSM_GROUP * s, pltpu.roll(k_all, SSM_GROUP * s, axis=1), 0.0)
            m_intra.append(blk.astype(BF16))
            k = L - 1 - s
            w_state.append(bbar * pw1[k:k + 1] + bbar_sw * pw2[k:k + 1])
            w_out_t.append(ca[s + 1] * sign)
        return jnp.concatenate(m_intra, axis=0), w_state, w_out_t

    def halves(pieces, second):
        firsts, seconds = [], []
        for piece in pieces:
            swapped = pltpu.roll(piece, half, axis=1)
            if second:
                firsts.append(jnp.where(low, 0.0, swapped))
                seconds.append(jnp.where(low, 0.0, piece))
            else:
                firsts.append(jnp.where(low, piece, 0.0))
                seconds.append(jnp.where(low, swapped, 0.0))
        return firsts, seconds

    def group_pair(cur, pr):
        prev = 1 - cur
        gs = (2 * pr, 2 * pr + 1)
        m_intra, ws_re, ws_im, wo_t = [], [], [], []
        for second, g in enumerate(gs):
            mi, w_state, w_out_t = operators(g)
            m_intra.append(mi)
            re, im = halves(w_state, second)
            ws_re += re
            ws_im += im
            re, im = halves(w_out_t, second)
            wo_t.append(jnp.concatenate(
                [jnp.concatenate([r, m], axis=1) for r, m in zip(re, im)], axis=0).astype(BF16))
        us = [ug_scr[prev][g] for g in gs]
        uu = jnp.concatenate(us, axis=1)
        xr = jnp.dot(uu, jnp.concatenate(ws_re, axis=0).astype(BF16), preferred_element_type=F32)
        xi = jnp.dot(uu, jnp.concatenate(ws_im, axis=0).astype(BF16), preferred_element_type=F32)
        a1 = [a1_ref[g] for g in gs]
        a2 = [a2_ref[g] for g in gs]
        step = 1
        j = 0
        while step < n_chunks:
            ar = jnp.where(low, a1[0][j:j + 1], a1[1][j:j + 1])
            ai = jnp.where(low, -a2[0][j:j + 1], a2[1][j:j + 1])
            sr = jnp.where(chunk >= step, pltpu.roll(xr, step, axis=0), 0.0)
            si = jnp.where(chunk >= step, pltpu.roll(xi, step, axis=0), 0.0)
            xr = xr + sr * ar - si * ai
            xi = xi + si * ar + sr * ai
            step *= 2
            j += 1
        x_in = jnp.concatenate([jnp.where(chunk >= 1, pltpu.roll(x, 1, axis=0), 0.0)
                                for x in (xr, xi)], axis=1).astype(BF16)
        for second, g in enumerate(gs):
            y = jnp.dot(us[second], m_intra[second], preferred_element_type=F32)
            y = y + lax.dot_general(x_in, wo_t[second], NT_DIMS, preferred_element_type=F32)
            y = y + us[second].astype(F32) * dv_ref[g]
            zg_scr[prev][g] = jax.nn.gelu(y).astype(BF16)

    def relayout_out(cur, rb):
        t0 = pl.multiple_of(rb * nb * w, nb * w)
        r0 = pl.multiple_of(rb * nb * L, nb * L)
        by_pos = []
        for hf in range(L // GROUPS_PER_SLAB):
            arrs = [zg_scr[cur][g, pl.ds(r0, nb * L), LANES * hf:LANES * (hf + 1)].astype(F32)
                    for g in range(GROUPS_PER_SLAB)]
            by_pos += _piece_transpose(arrs, piece)
        for blk in range(nb):
            z_ref[pl.ds(t0 + w * blk, w), :] = jnp.concatenate(
                [arr[L * blk:L * (blk + 1)] for arr in by_pos], axis=0).astype(BF16)

    def loop(n, *stages):
        def body(i, carry):
            for stage, per_trip in stages:
                for r in range(per_trip):
                    stage(per_trip * i + r)
            return carry
        lax.fori_loop(0, n, body, 0)

    n_rb = rows // (nb * L)
    n_pr = GROUPS_PER_SLAB // 2

    @pl.when(step_id == 0)
    def _():
        zg_scr[1][...] = jnp.zeros(zg_scr[1].shape, BF16)
        loop(n_rb, (functools.partial(relayout_in, 0), 1))

    def inner(cur):
        stage_in = functools.partial(relayout_in, cur)
        stage_scan = functools.partial(group_pair, cur)
        stage_out = functools.partial(relayout_out, cur)
        if n_rb % n_pr == 0:
            per = n_rb // n_pr
            loop(n_pr, (stage_scan, 1), (stage_in, per), (stage_out, per))
        else:
            loop(n_rb, (stage_in, 1))
            loop(n_pr, (stage_scan, 1))
            loop(n_rb, (stage_out, 1))

    for cur in range(2):
        pl.when((step_id > 0) & (step_id <= n_slabs) & (step_id % 2 == cur))(
            functools.partial(inner, cur))

    @pl.when(step_id == n_slabs + 1)
    def _():
        loop(n_rb, (functools.partial(relayout_out, (n_slabs + 1) % 2), 1))


def _ssm_scan(u, cc, bt, pw1, pw2, zoh, a1, a2, dv, *, n_chunks):
    t, d = u.shape
    L = SSM_CHUNK
    rows = t // L
    w = L * SSM_GROUP
    gps = GROUPS_PER_SLAB
    n_slabs = d // LANES
    clamp = lambda j: jnp.clip(j, 0, n_slabs - 1)
    pspec = lambda arr: pl.BlockSpec((gps,) + arr.shape[1:], lambda j: (clamp(j - 1), 0, 0))
    return pl.pallas_call(
        functools.partial(_ssm_kernel, n_chunks=n_chunks, n_slabs=n_slabs),
        grid=(n_slabs + 2,),
        in_specs=[pl.BlockSpec((t, LANES), lambda j: (0, clamp(j))),
                  pspec(cc), pspec(bt), pspec(pw1), pspec(pw2), pspec(zoh), pspec(a1), pspec(a2),
                  pspec(dv)],
        out_specs=pl.BlockSpec((t, LANES), lambda j: (0, clamp(j - 2))),
        out_shape=jax.ShapeDtypeStruct((t, d), BF16),
        scratch_shapes=[pltpu.VMEM((gps, rows, w), BF16)] * 4,
        compiler_params=pltpu.CompilerParams(
            dimension_semantics=("arbitrary",), vmem_limit_bytes=VMEM_LIMIT),
        name="s5_scan",
    )(u, cc, bt, pw1, pw2, zoh, a1, a2, dv)


def _ssm_operators(a_re, a_im, b_re, b_im, c_re, c_im, log_dt, d_skip, n_chunks):
    L = SSM_CHUNK
    g, p = a_re.shape
    dt = jnp.exp(log_dt)[:, None]
    lam_re, lam_im = dt * a_re, dt * a_im

    def powers(ks):
        ks = jnp.asarray(ks, F32)[:, None, None]
        mag = jnp.exp(ks * lam_re)
        return mag * jnp.cos(ks * lam_im), mag * jnp.sin(ks * lam_im)

    def patterns(re, im):
        return (jnp.concatenate([re, re], axis=2).transpose(1, 0, 2),
                jnp.concatenate([-im, im], axis=2).transpose(1, 0, 2))

    pr, pi = powers(range(L + 1))
    num_re, num_im = pr[1] - 1.0, pi[1]
    den = a_re * a_re + a_im * a_im
    s_re = (num_re * a_re + num_im * a_im) / den
    s_im = (num_im * a_re - num_re * a_im) / den
    pw1, pw2 = patterns(pr, pi)
    zoh = jnp.concatenate(patterns(s_re[None], s_im[None]), axis=1)
    n_steps = max(1, int(math.log2(n_chunks)))
    a1, a2 = patterns(*powers([L * 2 ** j for j in range(n_steps)]))
    cc = jnp.concatenate([c_re, c_im], axis=2)
    bt = jnp.concatenate([b_re.transpose(0, 2, 1), b_im.transpose(0, 2, 1)], axis=2)
    dv = jnp.tile(d_skip.reshape(g, 1, SSM_GROUP), (1, L, 1)).reshape(g, 1, L * SSM_GROUP)
    return cc, bt, pw1, pw2, zoh, a1, a2, dv


def kernel(x, norm_mix_g, norm_mlp_g, attn_w_in, attn_b_f, attn_q_g, attn_k_g, attn_w_out,
           ssm_w_in, ssm_a_re, ssm_a_im, ssm_b_re, ssm_b_im, ssm_c_re, ssm_c_im, ssm_log_dt,
           ssm_d, ssm_w_glu, mlp_w1, mlp_w2):
    b, s, d = x.shape
    t = b * s
    tk = min(512, s // 2)
    tq = 2 * tk
    tm_proj = min(512, s)
    tm_mlp = min(512, t)

    w_in = attn_w_in[0]
    w_f = jnp.pad(w_in[:, 3 * d:], ((0, 0), (0, LANES - N_HEADS)))
    wkf = jnp.concatenate([w_in[:, d:2 * d], w_f], axis=1).astype(BF16)
    wqvt = jnp.concatenate([w_in[:, :d], w_in[:, 2 * d:3 * d]], axis=1).T.astype(BF16)
    blk = np.arange(MXU_DIM) // HEAD_DIM
    bd = jnp.asarray((blk[:, None] == blk[None, :]) * (1.0 / HEAD_DIM), BF16)
    gain_k = jnp.tile(attn_k_g[0], N_HEADS).reshape(1, d)
    gain_qt = jnp.broadcast_to(
        (jnp.tile(attn_q_g[0], N_HEADS) * (LOG2E / math.sqrt(HEAD_DIM)))[:, None], (d, LANES))
    bfp = jnp.pad(attn_b_f[0], (0, LANES - N_HEADS)).reshape(1, LANES)
    tri = jnp.asarray(np.arange(tm_proj)[:, None] >= np.arange(tm_proj)[None, :], BF16)
    heads = np.arange(N_HEADS)
    place_k = np.zeros((LANES, d), np.float32)
    for i in range(BIAS_PIECES):
        place_k[N_HEADS * i + heads, (heads // HEADS_PER_STEP) * LANES
                + BIAS_PIECES * (heads % HEADS_PER_STEP) + i] = 1.0
    place_qt = np.roll(place_k, BIAS_Q_LANE, axis=1).T
    shift = (LOG2E * math.sqrt(HEAD_DIM) * jnp.max(jnp.abs(attn_q_g[0]))
             * jnp.max(jnp.abs(attn_k_g[0])))
    qt, k, vt, kb, qbt, edge = _qkv_proj(
        x, norm_mix_g[0].reshape(1, d), wkf, wqvt, bd, gain_k, gain_qt, bfp, tri,
        jnp.asarray(place_k, BF16), jnp.asarray(place_qt, BF16), jnp.full((1, LANES), shift, F32),
        tm=tm_proj, tk=tk)
    assert tm_proj == tk
    c_first = edge[:, ::tq // tk, 0, :N_HEADS]
    c_last = edge[:, :, 1, :N_HEADS]
    bound = (c_first[:, :, None, :] - c_last[:, None, :, :]) * LOG2E
    key_blk = jnp.arange(s // tk)[None, None, :, None]
    first = jnp.min(jnp.where(bound < SKIP_LOGIT, s // tk, key_blk), axis=2)
    first = jnp.minimum(first, (tq // tk) * jnp.arange(s // tq)[None, :, None]) // 2 * 2
    first = jnp.concatenate([first.transpose(0, 2, 1).reshape(-1), (shift > MAX_SHIFT)[None]])
    dff = mlp_w1.shape[2]
    o, w1, w2 = _attention(first.astype(jnp.int32), qt, qbt, k, kb, vt,
                           mlp_w1.reshape(-1, dff), mlp_w2.reshape(-1, d), tq=tq, tk=tk)
    w1 = w1.reshape(mlp_w1.shape)
    w2 = w2.reshape(mlp_w2.shape)
    tok = np.arange(SSM_CHUNK * SSM_CHUNK)
    perm = jnp.asarray(
        tok[None, :] == (tok[:, None] % SSM_CHUNK) * SSM_CHUNK + tok[:, None] // SSM_CHUNK, BF16)
    x2, u = _mix_mlp(x.reshape(t, d), o.reshape(t, d), attn_w_out[0].astype(BF16),
                     norm_mlp_g[0].reshape(1, d), w1, w2,
                     perm, (norm_mix_g[1].reshape(1, d), ssm_w_in[0].astype(BF16)),
                     layer=0, glu=False, tm=tm_mlp)

    n_chunks = s // SSM_CHUNK
    ops = _ssm_operators(ssm_a_re[0], ssm_a_im[0], ssm_b_re[0], ssm_b_im[0], ssm_c_re[0],
                         ssm_c_im[0], ssm_log_dt[0], ssm_d[0], n_chunks)
    z = _ssm_scan(u, *ops, n_chunks=n_chunks)
    x3 = _mix_mlp(x2, z, ssm_w_glu[0].astype(BF16), norm_mlp_g[1].reshape(1, d),
                  w1, w2, perm, layer=1, glu=True, tm=tm_mlp)
    return x3.reshape(b, s, d)
```

```python
import functools
import math

import jax
import jax.numpy as jnp
import numpy as np
from jax import lax
from jax.experimental import pallas as pl
from jax.experimental.pallas import tpu as pltpu

F32 = jnp.float32
BF16 = jnp.bfloat16

N_HEADS = 16
HEAD_DIM = 64
SSM_GROUP = 16
SSM_STATE = 64
SSM_CHUNK = 16
EPS = 1e-6
LOG2E = 1.4426950408889634

LANES = 128
MXU_DIM = 256
HEADS_PER_STEP = LANES // HEAD_DIM
BIAS_PIECES = 3
VMEM_LIMIT = 56 * 1024 * 1024

NT_DIMS = (((1,), (1,)), ((), ()))


def _rms_norm(x, g):
    ms = jnp.mean(x * x, axis=-1, keepdims=True)
    return x * lax.rsqrt(ms + EPS) * g


def _const_spec(shape):
    zeros = (0,) * len(shape)
    return pl.BlockSpec(shape, lambda *_: zeros)


def _bf16_pieces(val):
    pieces = jnp.zeros_like(val)
    rem = val
    for n in range(BIAS_PIECES):
        piece = rem.astype(BF16).astype(F32)
        rem = rem - piece
        pieces = pieces + (piece if n == 0 else pltpu.roll(piece, N_HEADS * n, axis=1))
    return pieces


def _qkv_kernel(x_ref, g_ref, wkf_ref, wqvt_ref, bd_ref, gain_k_ref, gain_qt_ref, bf_ref, tri_ref,
                place_k_ref, place_qt_ref, shift_ref,
                qt_ref, k_ref, vt_ref, kb_ref, qbt_ref, edge_ref, carry_ref, *, tm, tk, d):
    @pl.when(pl.program_id(1) == 0)
    def _():
        carry_ref[...] = jnp.zeros_like(carry_ref)

    h = _rms_norm(x_ref[0], g_ref[...]).astype(BF16)

    y = jnp.dot(h, wkf_ref[...], preferred_element_type=F32)
    vt = lax.dot_general(wqvt_ref[d:, :], h, NT_DIMS, preferred_element_type=F32).astype(BF16)
    for j in range(tm // tk):
        vt_ref[0, j] = vt[:, tk * j:tk * (j + 1)]
    qt = lax.dot_general(wqvt_ref[:d, :], h, NT_DIMS, preferred_element_type=F32)

    for t in range(d // MXU_DIM):
        sl = slice(MXU_DIM * t, MXU_DIM * (t + 1))
        tile = y[:, sl]
        ms = jnp.dot((tile * tile).astype(BF16), bd_ref[...], preferred_element_type=F32)
        k_ref[0, :, sl] = (tile * lax.rsqrt(ms + EPS) * gain_k_ref[:, sl]).astype(BF16)
        tile = qt[sl, :]
        ms = jnp.dot(bd_ref[...], (tile * tile).astype(BF16), preferred_element_type=F32)
        gain = jnp.tile(gain_qt_ref[sl, :], (1, tm // LANES))
        qt_ref[0, sl, :] = (tile * lax.rsqrt(ms + EPS) * gain).astype(BF16)

    f = y[:, d:] + bf_ref[...]
    log_f = jnp.minimum(f, 0.0) - jnp.log1p(jnp.exp(-jnp.abs(f)))
    lane = lax.broadcasted_iota(jnp.int32, log_f.shape, 1)
    part = jnp.dot(tri_ref[...], _bf16_pieces(jnp.where(lane < N_HEADS, log_f, 0.0)).astype(BF16),
                   preferred_element_type=F32)
    cs = carry_ref[...] + part
    for n in range(1, BIAS_PIECES):
        cs = cs + pltpu.roll(part, LANES - N_HEADS * n, axis=1)
    carry_ref[...] = cs[tm - 1:tm, :]
    edge_ref[0, 0] = jnp.concatenate(
        [cs[0:1, :], cs[tm - 1:tm, :], jnp.zeros((6, LANES), F32)], axis=0)
    c2 = jnp.where(lane < N_HEADS, cs * LOG2E, 0.0)
    kb_ref[0] = jnp.dot(_bf16_pieces(-c2).astype(BF16), place_k_ref[...],
                        preferred_element_type=F32).astype(BF16)
    q_pieces = _bf16_pieces(jnp.where(lane < N_HEADS, c2 - shift_ref[...], 0.0))
    qbt_ref[0] = jnp.dot(place_qt_ref[...], q_pieces.T.astype(BF16),
                         preferred_element_type=F32).astype(BF16)


def _qkv_proj(x, g, wkf, wqvt, bd, gain_k, gain_qt, bfp, tri, place_k, place_qt, shift, *, tm, tk):
    b, s, d = x.shape
    row = lambda bi, si: (bi, si, 0)
    col = lambda bi, si: (bi, 0, si)
    consts = (g, wkf, wqvt, bd, gain_k, gain_qt, bfp, tri, place_k, place_qt, shift)
    return pl.pallas_call(
        functools.partial(_qkv_kernel, tm=tm, tk=tk, d=d),
        grid=(b, s // tm),
        in_specs=[pl.BlockSpec((1, tm, d), row)] + [_const_spec(c.shape) for c in consts],
        out_specs=[pl.BlockSpec((1, d, tm), col),
                   pl.BlockSpec((1, tm, d), row),
                   pl.BlockSpec((1, tm // tk, d, tk), lambda bi, si: (bi, si, 0, 0)),
                   pl.BlockSpec((1, tm, d), row),
                   pl.BlockSpec((1, d, tm), col),
                   pl.BlockSpec((1, 1, 8, LANES), lambda bi, si: (bi, si, 0, 0))],
        out_shape=[jax.ShapeDtypeStruct((b, d, s), BF16),
                   jax.ShapeDtypeStruct((b, s, d), BF16),
                   jax.ShapeDtypeStruct((b, s // tk, d, tk), BF16),
                   jax.ShapeDtypeStruct((b, s, d), BF16),
                   jax.ShapeDtypeStruct((b, d, s), BF16),
                   jax.ShapeDtypeStruct((b, s // tm, 8, LANES), F32)],
        scratch_shapes=[pltpu.VMEM((1, LANES), F32)],
        compiler_params=pltpu.CompilerParams(
            dimension_semantics=("arbitrary", "arbitrary"), vmem_limit_bytes=VMEM_LIMIT),
        name="qkv_proj",
    )(x, *consts)


BIAS_Q_LANE = 8
MAX_SHIFT = 48.0
SKIP_LOGIT = -200.0


def _attn_kernel(first_ref, *refs, tq, tk):
    *attn_in, w1_ref, w2_ref, o_ref, w1_bf_ref, w2_bf_ref = refs[:10]
    attn_refs = (*attn_in, o_ref, *refs[10:])
    w1_bf_ref[...] = w1_ref[...].astype(BF16)
    w2_bf_ref[...] = w2_ref[...].astype(BF16)

    use_online = first_ref[first_ref.shape[0] - 1]

    @pl.when(use_online == 0)
    def _():
        _attn_body(first_ref, *attn_refs, tq=tq, tk=tk, online=False)

    @pl.when(use_online != 0)
    def _():
        _attn_body(first_ref, *attn_refs, tq=tq, tk=tk, online=True)


def _attn_body(first_ref, qt_ref, qbt_ref, k_ref, kb_ref, vt_ref, o_ref, s0_scr, s1_scr, p0_scr,
               p1_scr, *, tq, tk, online):
    i = pl.program_id(2)
    n_sub = tq // tk
    qt = qt_ref[0]
    qbt = qbt_ref[0]
    row = lax.broadcasted_iota(jnp.int32, (LANES, tq), 0)
    lane_k = lax.broadcasted_iota(jnp.int32, (tk, LANES), 1)
    q_cols = (lane_k >= BIAS_Q_LANE) & (lane_k < BIAS_Q_LANE + HEADS_PER_STEP * BIAS_PIECES)
    causal = (lax.broadcasted_iota(jnp.int32, (tk, tk), 0)
              <= lax.broadcasted_iota(jnp.int32, (tk, tk), 1))
    one = jnp.ones((), BF16)
    zero = jnp.zeros((), BF16)
    heads = range(HEADS_PER_STEP)
    qats = []
    for hh in heads:
        q_h = jnp.where((row >= HEAD_DIM * hh) & (row < HEAD_DIM * (hh + 1)), qt, zero)
        k_side = (row >= BIAS_PIECES * hh) & (row < BIAS_PIECES * (hh + 1))
        q_lo = BIAS_Q_LANE + BIAS_PIECES * hh
        q_side = (row >= q_lo) & (row < q_lo + BIAS_PIECES)
        qats.append(jnp.concatenate(
            [q_h, jnp.where(k_side, one, jnp.where(q_side, qbt, zero))], axis=0))

    def scores(hh, j, c0=0):
        off = pl.multiple_of(j * tk, tk)
        kbias = jnp.where(q_cols, one, kb_ref[0, pl.ds(off, tk), :])
        ka = jnp.concatenate([k_ref[0, pl.ds(off, tk), :], kbias], axis=1)
        return jnp.dot(ka, qats[hh][:, c0:], preferred_element_type=F32)

    if not online:
        j0 = n_sub * i
        def probs(hh, l, j, c0=0, diagonal=None):
            s = scores(hh, j, c0)
            if diagonal is not False:
                head = jnp.where(causal, s[:, :tk], -1e30)
                if diagonal is None:
                    head = jnp.where(j < j0, s[:, :tk], head)
                s = head if s.shape[1] == tk else jnp.concatenate([head, s[:, tk:]], axis=1)
            p = jnp.exp2(s)
            return l + jnp.sum(p, axis=0, keepdims=True), p.astype(BF16)

        def weigh(hh, acc, p, j):
            vt = vt_ref[0, j, HEAD_DIM * hh:HEAD_DIM * (hh + 1), :]
            return acc + jnp.dot(vt, p, preferred_element_type=F32)

        def first_pair(hh):
            head = HEADS_PER_STEP * pl.program_id(1) + hh
            return first_ref[(pl.program_id(0) * N_HEADS + head) * pl.num_programs(2) + i] // 2

        def fill(hh):
            jj0 = first_pair(hh)
            l, p0_scr[hh] = probs(hh, jnp.zeros((1, tq), F32), 2 * jj0)
            return jj0, l

        def off_diagonal(jj0s, ls, static_pairs=None):
            def pairs(jj, cs, n_pairs, hs):
                cs = list(cs)
                for r in range(n_pairs):
                    a = 2 * (jj + r)
                    for hh in hs:
                        l, acc = cs[hh]
                        l, p1_scr[hh] = probs(hh, l, a + 1, diagonal=False)
                        cs[hh] = (l, weigh(hh, acc, p0_scr[hh], a))
                    for hh in hs:
                        l, acc = cs[hh]
                        l, p0_scr[hh] = probs(hh, l, a + 2)
                        cs[hh] = (l, weigh(hh, acc, p1_scr[hh], a + 1))
                return tuple(cs)

            jj_both = functools.reduce(jnp.maximum, jj0s)
            cs = tuple((l, jnp.zeros((HEAD_DIM, tq), F32)) for l in ls)
            if static_pairs is not None:
                return pairs(jj0s[0], cs, static_pairs, heads)
            for hh in heads:
                cs = lax.fori_loop(jj0s[hh], jj_both,
                                   lambda jj, cs, hh=hh: pairs(jj, cs, 1, (hh,)), cs)
            n_long = (j0 // 2 - jj_both) // 2
            cs = lax.fori_loop(0, n_long, lambda t, cs: pairs(jj_both + 2 * t, cs, 2, heads), cs)
            return lax.fori_loop(jj_both + 2 * n_long, j0 // 2,
                                 lambda jj, cs: pairs(jj, cs, 1, heads), cs)

        def diagonal(hh, l, acc):
            later = [probs(hh, l[:, tk * dd:], j0 + dd, tk * dd, diagonal=True)
                     for dd in range(1, n_sub)]
            acc = weigh(hh, acc, p0_scr[hh], j0)
            for dd in range(1, n_sub):
                c0 = tk * dd
                l_part, p = later[dd - 1]
                l = jnp.concatenate([l[:, :c0], l_part], axis=1)
                acc = jnp.concatenate([acc[:, :c0], weigh(hh, acc[:, c0:], p, j0 + dd)], axis=1)
            return acc / l

        def run(static_pairs):
            filled = [fill(hh) for hh in heads]
            cs = off_diagonal([f[0] for f in filled], [f[1] for f in filled], static_pairs)
            outs = [diagonal(hh, *cs[hh]) for hh in heads]
            o_ref[0] = jnp.concatenate(outs, axis=0).T.astype(BF16)

        starts = [first_pair(hh) for hh in heads]
        together = functools.reduce(jnp.logical_and, [s == starts[0] for s in starts])
        n_static = 4
        static = [(j0 // 2 - starts[0] == n) & together for n in range(n_static)]
        for n in range(n_static):
            pl.when(static[n])(functools.partial(run, n))
        pl.when(jnp.logical_not(functools.reduce(jnp.logical_or, static)))(
            functools.partial(run, None))
        return

    def consume(hh, carry, s, j, diagonal):
        if diagonal:
            s_tri = jnp.where(causal, s[:, :tk], -1e30)
            s = s_tri if s.shape[1] == tk else jnp.concatenate([s_tri, s[:, tk:]], axis=1)
        vt = vt_ref[0, j, HEAD_DIM * hh:HEAD_DIM * (hh + 1), :]
        if online:
            m, l, acc = carry
            m_new = jnp.maximum(m, jnp.max(s, axis=0, keepdims=True))
            alpha = jnp.exp2(m - m_new)
            p = jnp.exp2(s - m_new)
            l = alpha * l + jnp.sum(p, axis=0, keepdims=True)
            acc = alpha * acc + jnp.dot(vt, p.astype(BF16), preferred_element_type=F32)
            return m_new, l, acc
        l, acc = carry
        p = jnp.exp2(s)
        l = l + jnp.sum(p, axis=0, keepdims=True)
        acc = acc + jnp.dot(vt, p.astype(BF16), preferred_element_type=F32)
        return l, acc

    init = (jnp.zeros((1, tq), F32), jnp.zeros((HEAD_DIM, tq), F32))
    if online:
        init = (jnp.full((1, tq), -1e30, F32),) + init

    j0 = n_sub * i

    def first_scores(hh):
        if online:
            jj0 = 0
        else:
            head = HEADS_PER_STEP * pl.program_id(1) + hh
            jj0 = first_ref[(pl.program_id(0) * N_HEADS + head) * pl.num_programs(2) + i] // 2
        s0_scr[hh] = scores(hh, 2 * jj0)
        return jj0

    def off_diagonal(hh, jj0):
        def pairs(jj, c, n_pairs):
            for r in range(n_pairs):
                s1_scr[hh] = scores(hh, 2 * (jj + r) + 1)
                c = consume(hh, c, s0_scr[hh], 2 * (jj + r), False)
                s0_scr[hh] = scores(hh, 2 * (jj + r) + 2)
                c = consume(hh, c, s1_scr[hh], 2 * (jj + r) + 1, False)
            return c

        n_long = (j0 // 2 - jj0) // 2
        c = lax.fori_loop(0, n_long, lambda t, c: pairs(jj0 + 2 * t, c, 2), init)
        return lax.fori_loop(jj0 + 2 * n_long, j0 // 2, lambda jj, c: pairs(jj, c, 1), c)

    def diagonal(hh, carry):
        later = [scores(hh, j0 + dd, tk * dd) for dd in range(1, n_sub)]
        carry = consume(hh, carry, s0_scr[hh], j0, True)
        for dd in range(1, n_sub):
            c0 = tk * dd
            part = consume(hh, tuple(c[:, c0:] for c in carry), later[dd - 1], j0 + dd, True)
            carry = tuple(jnp.concatenate([c[:, :c0], pc], axis=1) for c, pc in zip(carry, part))
        return carry[-1] / carry[-2]

    outs = []
    jj0 = first_scores(0)
    for hh in heads:
        carry = off_diagonal(hh, jj0)
        if hh + 1 < HEADS_PER_STEP:
            jj0 = first_scores(hh + 1)
        outs.append(diagonal(hh, carry))
    o_t = jnp.concatenate(outs, axis=0)
    o_ref[0] = o_t.T.astype(BF16)


def _attention(first, qt, qbt, k, kb, vt, w1, w2, *, tq, tk):
    b, s, d = k.shape
    assert (tq // tk) % 2 == 0 and s % tq == 0
    n_pairs = d // LANES
    nq = s // tq
    nk = s // tk
    n_steps = b * n_pairs * nq
    qt_blk = pl.BlockSpec((1, LANES, tq), lambda bi, hp, i, first: (bi, hp, i))
    k_all = pl.BlockSpec((1, s, LANES), lambda bi, hp, i, first: (bi, 0, hp))
    step = lambda bi, hp, i, first: ((bi * n_pairs + hp) * nq + i, 0)

    def slice_spec(w):
        assert w.shape[0] % (16 * n_steps) == 0
        return pl.BlockSpec((w.shape[0] // n_steps, w.shape[1]), step)

    return pl.pallas_call(
        functools.partial(_attn_kernel, tq=tq, tk=tk),
        grid_spec=pltpu.PrefetchScalarGridSpec(
            num_scalar_prefetch=1,
            grid=(b, n_pairs, nq),
            in_specs=[qt_blk, qt_blk, k_all, k_all,
                      pl.BlockSpec((1, nk, LANES, tk), lambda bi, hp, i, first: (bi, 0, hp, 0)),
                      slice_spec(w1), slice_spec(w2)],
            out_specs=[pl.BlockSpec((1, tq, LANES), lambda bi, hp, i, first: (bi, i, hp)),
                       slice_spec(w1), slice_spec(w2)],
            scratch_shapes=[pltpu.VMEM((HEADS_PER_STEP, tk, tq), F32),
                            pltpu.VMEM((HEADS_PER_STEP, tk, tq), F32),
                            pltpu.VMEM((HEADS_PER_STEP, tk, tq), BF16),
                            pltpu.VMEM((HEADS_PER_STEP, tk, tq), BF16)]),
        out_shape=[jax.ShapeDtypeStruct((b, s, d), BF16),
                   jax.ShapeDtypeStruct(w1.shape, BF16), jax.ShapeDtypeStruct(w2.shape, BF16)],
        compiler_params=pltpu.CompilerParams(
            dimension_semantics=("arbitrary", "arbitrary", "arbitrary"),
            vmem_limit_bytes=VMEM_LIMIT),
        name="fox_attention",
    )(first, qt, qbt, k, kb, vt, w1, w2)


def _permute_token_blocks(perm, a):
    w = perm.shape[0]
    return jnp.concatenate(
        [jnp.dot(perm, a[w * i:w * (i + 1)], preferred_element_type=F32).astype(BF16)
         for i in range(a.shape[0] // w)], axis=0)


def _mix_mlp_kernel(x_ref, a_ref, wmix_ref, g_ref, w1_ref, w2_ref, *rest, glu, d, ff_chunk):
    if glu:
        a = _permute_token_blocks(rest[0][...], a_ref[...])
        mix = jnp.dot(a, wmix_ref[...], preferred_element_type=F32)
        mix = mix[:, :d] * jax.nn.sigmoid(mix[:, d:])
    else:
        mix = jnp.dot(a_ref[...], wmix_ref[...], preferred_element_type=F32)
    x1 = x_ref[...] + mix
    h = _rms_norm(x1, g_ref[...]).astype(BF16)
    acc = x1
    for c in range(w1_ref.shape[1] // ff_chunk):
        sl = slice(ff_chunk * c, ff_chunk * (c + 1))
        hid = jnp.maximum(jnp.dot(h, w1_ref[:, sl], preferred_element_type=F32), 0.0)
        acc = acc + jnp.dot((hid * hid).astype(BF16), w2_ref[sl, :], preferred_element_type=F32)
    if glu:
        _, o_ref = rest
    else:
        perm_ref, g_next_ref, w_next_ref, o_ref, u_ref = rest
        h_next = _rms_norm(acc, g_next_ref[...]).astype(BF16)
        u = jnp.dot(h_next, w_next_ref[...], preferred_element_type=F32).astype(BF16)
        u_ref[...] = _permute_token_blocks(perm_ref[...], u)
    o_ref[...] = acc


def _mix_mlp(x, a, wmix, g, w1, w2, perm, next_proj=None, *, layer, glu, tm, ff_chunk=1024):
    t, d = x.shape
    row = lambda i: (i, 0)
    single = pl.Buffered(1)
    wspec = lambda shape: pl.BlockSpec(shape, lambda i: (0, 0), pipeline_mode=single)
    lspec = lambda w: pl.BlockSpec((None,) + w.shape[1:], lambda i: (layer, 0, 0),
                                   pipeline_mode=single)
    operands = [x, a, wmix, g, w1, w2, perm]
    in_specs = [pl.BlockSpec((tm, d), row), pl.BlockSpec((tm, d), row),
                wspec(wmix.shape), wspec((1, d)), lspec(w1), lspec(w2), wspec(perm.shape)]
    out_specs = pl.BlockSpec((tm, d), row)
    out_shape = jax.ShapeDtypeStruct((t, d), F32)
    if next_proj is not None:
        operands += list(next_proj)
        in_specs += [wspec(w.shape) for w in next_proj]
        n = next_proj[1].shape[1]
        out_specs = [out_specs, pl.BlockSpec((tm, n), row)]
        out_shape = [out_shape, jax.ShapeDtypeStruct((t, n), BF16)]
    return pl.pallas_call(
        functools.partial(_mix_mlp_kernel, glu=glu, d=d, ff_chunk=ff_chunk),
        grid=(t // tm,),
        in_specs=in_specs,
        out_specs=out_specs,
        out_shape=out_shape,
        compiler_params=pltpu.CompilerParams(
            dimension_semantics=("arbitrary",), vmem_limit_bytes=VMEM_LIMIT),
        name="mix_glu_mlp" if glu else "mix_mlp",
    )(*operands)


GROUPS_PER_SLAB = LANES // SSM_GROUP


def _piece_transpose(arrs, piece):
    arrs = list(arrs)
    dist = GROUPS_PER_SLAB // 2
    while dist >= 1:
        keep = (piece & dist) == 0
        shift = SSM_GROUP * dist
        for i in range(GROUPS_PER_SLAB):
            if i & dist:
                continue
            a, b = arrs[i], arrs[i + dist]
            arrs[i] = jnp.where(keep, a, pltpu.roll(b, shift, axis=1))
            arrs[i + dist] = jnp.where(keep, pltpu.roll(a, LANES - shift, axis=1), b)
        dist //= 2
    return arrs


def _ssm_kernel(u_ref, cc_ref, bt_ref, pw1_ref, pw2_ref, zoh_ref, a1_ref, a2_ref,
                dv_ref, z_ref, ug0_scr, ug1_scr, zg0_scr, zg1_scr, *, n_chunks, n_slabs):
    ug_scr = (ug0_scr, ug1_scr)
    zg_scr = (zg0_scr, zg1_scr)
    L = SSM_CHUNK
    rows = u_ref.shape[0] // L
    w = L * SSM_GROUP
    half = SSM_STATE
    nb = math.gcd(rows // L, 8)
    piece = lax.broadcasted_iota(jnp.int32, (nb * L, LANES), 1) // SSM_GROUP
    step_id = pl.program_id(0)

    def relayout_in(cur, rb):
        t0 = pl.multiple_of(rb * nb * w, nb * w)
        r0 = pl.multiple_of(rb * nb * L, nb * L)
        by_pos = u_ref[pl.ds(t0, nb * w), :].astype(F32)
        for hf in range(L // GROUPS_PER_SLAB):
            arrs = []
            for k in range(GROUPS_PER_SLAB):
                pos = GROUPS_PER_SLAB * hf + k
                arrs.append(jnp.concatenate(
                    [by_pos[w * blk + L * pos:w * blk + L * (pos + 1)] for blk in range(nb)],
                    axis=0))
            arrs = _piece_transpose(arrs, piece)
            for g in range(GROUPS_PER_SLAB):
                ug_scr[cur][g, pl.ds(r0, nb * L), LANES * hf:LANES * (hf + 1)] = (
                    arrs[g].astype(BF16))

    chunk = lax.broadcasted_iota(jnp.int32, (rows, LANES), 0) % n_chunks
    lane_w = lax.broadcasted_iota(jnp.int32, (SSM_GROUP, w), 1)
    low = lax.broadcasted_iota(jnp.int32, (1, LANES), 1) < half
    sign = jnp.where(low, 1.0, -1.0)

    def operators(g):
        cc = cc_ref[g]
        bt = bt_ref[g]
        pw1 = pw1_ref[g]
        pw2 = pw2_ref[g]
        zoh = zoh_ref[g]
        bbar = bt * zoh[0:1] + pltpu.roll(bt, half, axis=1) * zoh[1:2]
        bbar_sw = pltpu.roll(bbar, half, axis=1)
        cc_sw = pltpu.roll(cc, half, axis=1)
        ca = [cc * pw1[k:k + 1] + cc_sw * pw2[k:k + 1] for k in range(L + 1)]
        k_all = lax.dot_general(bbar * sign, jnp.concatenate(ca[:L], axis=0), NT_DIMS,
                                precision=lax.Precision.HIGHEST, preferred_element_type=F32)
        m_intra, w_state, w_out_t = [], [], []
        for s in range(L):
            blk = k_all if s == 0 else jnp.where(
                lane_w >= SSM_GROUP * s, pltpu.roll(k_all, SSM_GROUP * s, axis=1), 0.0)
            m_intra.append(blk.astype(BF16))
            k = L - 1 - s
            w_state.append(bbar * pw1[k:k + 1] + bbar_sw * pw2[k:k + 1])
            w_out_t.append(ca[s + 1] * sign)
        return jnp.concatenate(m_intra, axis=0), w_state, w_out_t

    def halves(pieces, second):
        firsts, seconds = [], []
        for piece in pieces:
            swapped = pltpu.roll(piece, half, axis=1)
            if second:
                firsts.append(jnp.where(low, 0.0, swapped))
                seconds.append(jnp.where(low, 0.0, piece))
            else:
                firsts.append(jnp.where(low, piece, 0.0))
                seconds.append(jnp.where(low, swapped, 0.0))
        return firsts, seconds

    def group_pair(cur, pr):
        prev = 1 - cur
        gs = (2 * pr, 2 * pr + 1)
        m_intra, ws_re, ws_im, wo_t = [], [], [], []
        for second, g in enumerate(gs):
            mi, w_state, w_out_t = operators(g)
            m_intra.append(mi)
            re, im = halves(w_state, second)
            ws_re += re
            ws_im += im
            re, im = halves(w_out_t, second)
            wo_t.append(jnp.concatenate(
                [jnp.concatenate([r, m], axis=1) for r, m in zip(re, im)], axis=0).astype(BF16))
        us = [ug_scr[prev][g] for g in gs]
        uu = jnp.concatenate(us, axis=1)
        xr = jnp.dot(uu, jnp.concatenate(ws_re, axis=0).astype(BF16), preferred_element_type=F32)
        xi = jnp.dot(uu, jnp.concatenate(ws_im, axis=0).astype(BF16), preferred_element_type=F32)
        a1 = [a1_ref[g] for g in gs]
        a2 = [a2_ref[g] for g in gs]
        step = 1
        j = 0
        while step < n_chunks:
            ar = jnp.where(low, a1[0][j:j + 1], a1[1][j:j + 1])
            ai = jnp.where(low, -a2[0][j:j + 1], a2[1][j:j + 1])
            sr = jnp.where(chunk >= step, pltpu.roll(xr, step, axis=0), 0.0)
            si = jnp.where(chunk >= step, pltpu.roll(xi, step, axis=0), 0.0)
            xr = xr + sr * ar - si * ai
            xi = xi + si * ar + sr * ai
            step *= 2
            j += 1
        x_in = jnp.concatenate([jnp.where(chunk >= 1, pltpu.roll(x, 1, axis=0), 0.0)
                                for x in (xr, xi)], axis=1).astype(BF16)
        for second, g in enumerate(gs):
            y = jnp.dot(us[second], m_intra[second], preferred_element_type=F32)
            y = y + lax.dot_general(x_in, wo_t[second], NT_DIMS, preferred_element_type=F32)
            y = y + us[second].astype(F32) * dv_ref[g]
            zg_scr[prev][g] = jax.nn.gelu(y).astype(BF16)

    def relayout_out(cur, rb):
        t0 = pl.multiple_of(rb * nb * w, nb * w)
        r0 = pl.multiple_of(rb * nb * L, nb * L)
        by_pos = []
        for hf in range(L // GROUPS_PER_SLAB):
            arrs = [zg_scr[cur][g, pl.ds(r0, nb * L), LANES * hf:LANES * (hf + 1)].astype(F32)
                    for g in range(GROUPS_PER_SLAB)]
            by_pos += _piece_transpose(arrs, piece)
        for blk in range(nb):
            z_ref[pl.ds(t0 + w * blk, w), :] = jnp.concatenate(
                [arr[L * blk:L * (blk + 1)] for arr in by_pos], axis=0).astype(BF16)

    def loop(n, *stages):
        def body(i, carry):
            for stage, per_trip in stages:
                for r in range(per_trip):
                    stage(per_trip * i + r)
            return carry
        lax.fori_loop(0, n, body, 0)

    n_rb = rows // (nb * L)
    n_pr = GROUPS_PER_SLAB // 2

    @pl.when(step_id == 0)
    def _():
        zg_scr[1][...] = jnp.zeros(zg_scr[1].shape, BF16)
        loop(n_rb, (functools.partial(relayout_in, 0), 1))

    def inner(cur):
        stage_in = functools.partial(relayout_in, cur)
        stage_scan = functools.partial(group_pair, cur)
        stage_out = functools.partial(relayout_out, cur)
        if n_rb % n_pr == 0:
            per = n_rb // n_pr
            loop(n_pr, (stage_scan, 1), (stage_in, per), (stage_out, per))
        else:
            loop(n_rb, (stage_in, 1))
            loop(n_pr, (stage_scan, 1))
            loop(n_rb, (stage_out, 1))

    for cur in range(2):
        pl.when((step_id > 0) & (step_id <= n_slabs) & (step_id % 2 == cur))(
            functools.partial(inner, cur))

    @pl.when(step_id == n_slabs + 1)
    def _():
        loop(n_rb, (functools.partial(relayout_out, (n_slabs + 1) % 2), 1))


def _ssm_scan(u, cc, bt, pw1, pw2, zoh, a1, a2, dv, *, n_chunks):
    t, d = u.shape
    L = SSM_CHUNK
    rows = t // L
    w = L * SSM_GROUP
    gps = GROUPS_PER_SLAB
    n_slabs = d // LANES
    clamp = lambda j: jnp.clip(j, 0, n_slabs - 1)
    pspec = lambda arr: pl.BlockSpec((gps,) + arr.shape[1:], lambda j: (clamp(j - 1), 0, 0))
    return pl.pallas_call(
        functools.partial(_ssm_kernel, n_chunks=n_chunks, n_slabs=n_slabs),
        grid=(n_slabs + 2,),
        in_specs=[pl.BlockSpec((t, LANES), lambda j: (0, clamp(j))),
                  pspec(cc), pspec(bt), pspec(pw1), pspec(pw2), pspec(zoh), pspec(a1), pspec(a2),
                  pspec(dv)],
        out_specs=pl.BlockSpec((t, LANES), lambda j: (0, clamp(j - 2))),
        out_shape=jax.ShapeDtypeStruct((t, d), BF16),
        scratch_shapes=[pltpu.VMEM((gps, rows, w), BF16)] * 4,
        compiler_params=pltpu.CompilerParams(
            dimension_semantics=("arbitrary",), vmem_limit_bytes=VMEM_LIMIT),
        name="s5_scan",
    )(u, cc, bt, pw1, pw2, zoh, a1, a2, dv)


def _ssm_operators(a_re, a_im, b_re, b_im, c_re, c_im, log_dt, d_skip, n_chunks):
    L = SSM_CHUNK
    g, p = a_re.shape
    dt = jnp.exp(log_dt)[:, None]
    lam_re, lam_im = dt * a_re, dt * a_im

    def powers(ks):
        ks = jnp.asarray(ks, F32)[:, None, None]
        mag = jnp.exp(ks * lam_re)
        return mag * jnp.cos(ks * lam_im), mag * jnp.sin(ks * lam_im)

    def patterns(re, im):
        return (jnp.concatenate([re, re], axis=2).transpose(1, 0, 2),
                jnp.concatenate([-im, im], axis=2).transpose(1, 0, 2))

    pr, pi = powers(range(L + 1))
    num_re, num_im = pr[1] - 1.0, pi[1]
    den = a_re * a_re + a_im * a_im
    s_re = (num_re * a_re + num_im * a_im) / den
    s_im = (num_im * a_re - num_re * a_im) / den
    pw1, pw2 = patterns(pr, pi)
    zoh = jnp.concatenate(patterns(s_re[None], s_im[None]), axis=1)
    n_steps = max(1, int(math.log2(n_chunks)))
    a1, a2 = patterns(*powers([L * 2 ** j for j in range(n_steps)]))
    cc = jnp.concatenate([c_re, c_im], axis=2)
    bt = jnp.concatenate([b_re.transpose(0, 2, 1), b_im.transpose(0, 2, 1)], axis=2)
    dv = jnp.tile(d_skip.reshape(g, 1, SSM_GROUP), (1, L, 1)).reshape(g, 1, L * SSM_GROUP)
    return cc, bt, pw1, pw2, zoh, a1, a2, dv


def kernel(x, norm_mix_g, norm_mlp_g, attn_w_in, attn_b_f, attn_q_g, attn_k_g, attn_w_out,
           ssm_w_in, ssm_a_re, ssm_a_im, ssm_b_re, ssm_b_im, ssm_c_re, ssm_c_im, ssm_log_dt,
           ssm_d, ssm_w_glu, mlp_w1, mlp_w2):
    b, s, d = x.shape
    t = b * s
    tk = min(512, s // 2)
    tq = 2 * tk
    tm_proj = min(512, s)
    tm_mlp = min(512, t)

    w_in = attn_w_in[0]
    w_f = jnp.pad(w_in[:, 3 * d:], ((0, 0), (0, LANES - N_HEADS)))
    wkf = jnp.concatenate([w_in[:, d:2 * d], w_f], axis=1).astype(BF16)
    wqvt = jnp.concatenate([w_in[:, :d], w_in[:, 2 * d:3 * d]], axis=1).T.astype(BF16)
    blk = np.arange(MXU_DIM) // HEAD_DIM
    bd = jnp.asarray((blk[:, None] == blk[None, :]) * (1.0 / HEAD_DIM), BF16)
    gain_k = jnp.tile(attn_k_g[0], N_HEADS).reshape(1, d)
    gain_qt = jnp.broadcast_to(
        (jnp.tile(attn_q_g[0], N_HEADS) * (LOG2E / math.sqrt(HEAD_DIM)))[:, None], (d, LANES))
    bfp = jnp.pad(attn_b_f[0], (0, LANES - N_HEADS)).reshape(1, LANES)
    tri = jnp.asarray(np.arange(tm_proj)[:, None] >= np.arange(tm_proj)[None, :], BF16)
    heads = np.arange(N_HEADS)
    place_k = np.zeros((LANES, d), np.float32)
    for i in range(BIAS_PIECES):
        place_k[N_HEADS * i + heads, (heads // HEADS_PER_STEP) * LANES
                + BIAS_PIECES * (heads % HEADS_PER_STEP) + i] = 1.0
    place_qt = np.roll(place_k, BIAS_Q_LANE, axis=1).T
    shift = (LOG2E * math.sqrt(HEAD_DIM) * jnp.max(jnp.abs(attn_q_g[0]))
             * jnp.max(jnp.abs(attn_k_g[0])))
    qt, k, vt, kb, qbt, edge = _qkv_proj(
        x, norm_mix_g[0].reshape(1, d), wkf, wqvt, bd, gain_k, gain_qt, bfp, tri,
        jnp.asarray(place_k, BF16), jnp.asarray(place_qt, BF16), jnp.full((1, LANES), shift, F32),
        tm=tm_proj, tk=tk)
    assert tm_proj == tk
    c_first = edge[:, ::tq // tk, 0, :N_HEADS]
    c_last = edge[:, :, 1, :N_HEADS]
    bound = (c_first[:, :, None, :] - c_last[:, None, :, :]) * LOG2E
    key_blk = jnp.arange(s // tk)[None, None, :, None]
    first = jnp.min(jnp.where(bound < SKIP_LOGIT, s // tk, key_blk), axis=2)
    first = jnp.minimum(first, (tq // tk) * jnp.arange(s // tq)[None, :, None]) // 2 * 2
    first = jnp.concatenate([first.transpose(0, 2, 1).reshape(-1), (shift > MAX_SHIFT)[None]])
    dff = mlp_w1.shape[2]
    o, w1, w2 = _attention(first.astype(jnp.int32), qt, qbt, k, kb, vt,
                           mlp_w1.reshape(-1, dff), mlp_w2.reshape(-1, d), tq=tq, tk=tk)
    w1 = w1.reshape(mlp_w1.shape)
    w2 = w2.reshape(mlp_w2.shape)
    tok = np.arange(SSM_CHUNK * SSM_CHUNK)
    perm = jnp.asarray(
        tok[None, :] == (tok[:, None] % SSM_CHUNK) * SSM_CHUNK + tok[:, None] // SSM_CHUNK, BF16)
    x2, u = _mix_mlp(x.reshape(t, d), o.reshape(t, d), attn_w_out[0].astype(BF16),
                     norm_mlp_g[0].reshape(1, d), w1, w2,
                     perm, (norm_mix_g[1].reshape(1, d), ssm_w_in[0].astype(BF16)),
                     layer=0, glu=False, tm=tm_mlp)

    n_chunks = s // SSM_CHUNK
    ops = _ssm_operators(ssm_a_re[0], ssm_a_im[0], ssm_b_re[0], ssm_b_im[0], ssm_c_re[0],
                         ssm_c_im[0], ssm_log_dt[0], ssm_d[0], n_chunks)
    z = _ssm_scan(u, *ops, n_chunks=n_chunks)
    x3 = _mix_mlp(x2, z, ssm_w_glu[0].astype(BF16), norm_mlp_g[1].reshape(1, d),
                  w1, w2, perm, layer=1, glu=True, tm=tm_mlp)
    return x3.reshape(b, s, d)
```

```python
import functools
import math

import jax
import jax.numpy as jnp
import numpy as np
from jax import lax
from jax.experimental import pallas as pl
from jax.experimental.pallas import tpu as pltpu

F32 = jnp.float32
BF16 = jnp.bfloat16

N_HEADS = 16
HEAD_DIM = 64
SSM_GROUP = 16
SSM_STATE = 64
SSM_CHUNK = 16
EPS = 1e-6
LOG2E = 1.4426950408889634

LANES = 128
MXU_DIM = 256
HEADS_PER_STEP = LANES // HEAD_DIM
BIAS_PIECES = 3
VMEM_LIMIT = 56 * 1024 * 1024

NT_DIMS = (((1,), (1,)), ((), ()))


def _rms_norm(x, g):
    ms = jnp.mean(x * x, axis=-1, keepdims=True)
    return x * lax.rsqrt(ms + EPS) * g


def _const_spec(shape):
    zeros = (0,) * len(shape)
    return pl.BlockSpec(shape, lambda *_: zeros)


def _bf16_pieces(val):
    pieces = jnp.zeros_like(val)
    rem = val
    for n in range(BIAS_PIECES):
        piece = rem.astype(BF16).astype(F32)
        rem = rem - piece
        pieces = pieces + (piece if n == 0 else pltpu.roll(piece, N_HEADS * n, axis=1))
    return pieces


def _qkv_kernel(x_ref, g_ref, wkf_ref, wqvt_ref, bd_ref, gain_k_ref, gain_qt_ref, bf_ref, tri_ref,
                place_k_ref, place_qt_ref, shift_ref,
                qt_ref, k_ref, vt_ref, kb_ref, qbt_ref, edge_ref, carry_ref, *, tm, tk, d):
    @pl.when(pl.program_id(1) == 0)
    def _():
        carry_ref[...] = jnp.zeros_like(carry_ref)

    h = _rms_norm(x_ref[0], g_ref[...]).astype(BF16)

    y = jnp.dot(h, wkf_ref[...], preferred_element_type=F32)
    vt = lax.dot_general(wqvt_ref[d:, :], h, NT_DIMS, preferred_element_type=F32).astype(BF16)
    for j in range(tm // tk):
        vt_ref[0, j] = vt[:, tk * j:tk * (j + 1)]
    qt = lax.dot_general(wqvt_ref[:d, :], h, NT_DIMS, preferred_element_type=F32)

    for t in range(d // MXU_DIM):
        sl = slice(MXU_DIM * t, MXU_DIM * (t + 1))
        tile = y[:, sl]
        ms = jnp.dot((tile * tile).astype(BF16), bd_ref[...], preferred_element_type=F32)
        k_ref[0, :, sl] = (tile * lax.rsqrt(ms + EPS) * gain_k_ref[:, sl]).astype(BF16)
        tile = qt[sl, :]
        ms = jnp.dot(bd_ref[...], (tile * tile).astype(BF16), preferred_element_type=F32)
        gain = jnp.tile(gain_qt_ref[sl, :], (1, tm // LANES))
        qt_ref[0, sl, :] = (tile * lax.rsqrt(ms + EPS) * gain).astype(BF16)

    f = y[:, d:] + bf_ref[...]
    log_f = jnp.minimum(f, 0.0) - jnp.log1p(jnp.exp(-jnp.abs(f)))
    lane = lax.broadcasted_iota(jnp.int32, log_f.shape, 1)
    part = jnp.dot(tri_ref[...], _bf16_pieces(jnp.where(lane < N_HEADS, log_f, 0.0)).astype(BF16),
                   preferred_element_type=F32)
    cs = carry_ref[...] + part
    for n in range(1, BIAS_PIECES):
        cs = cs + pltpu.roll(part, LANES - N_HEADS * n, axis=1)
    carry_ref[...] = cs[tm - 1:tm, :]
    edge_ref[0, 0] = jnp.concatenate(
        [cs[0:1, :], cs[tm - 1:tm, :], jnp.zeros((6, LANES), F32)], axis=0)
    c2 = jnp.where(lane < N_HEADS, cs * LOG2E, 0.0)
    kb_ref[0] = jnp.dot(_bf16_pieces(-c2).astype(BF16), place_k_ref[...],
                        preferred_element_type=F32).astype(BF16)
    q_pieces = _bf16_pieces(jnp.where(lane < N_HEADS, c2 - shift_ref[...], 0.0))
    qbt_ref[0] = jnp.dot(place_qt_ref[...], q_pieces.T.astype(BF16),
                         preferred_element_type=F32).astype(BF16)


def _qkv_proj(x, g, wkf, wqvt, bd, gain_k, gain_qt, bfp, tri, place_k, place_qt, shift, *, tm, tk):
    b, s, d = x.shape
    row = lambda bi, si: (bi, si, 0)
    col = lambda bi, si: (bi, 0, si)
    consts = (g, wkf, wqvt, bd, gain_k, gain_qt, bfp, tri, place_k, place_qt, shift)
    return pl.pallas_call(
        functools.partial(_qkv_kernel, tm=tm, tk=tk, d=d),
        grid=(b, s // tm),
        in_specs=[pl.BlockSpec((1, tm, d), row)] + [_const_spec(c.shape) for c in consts],
        out_specs=[pl.BlockSpec((1, d, tm), col),
                   pl.BlockSpec((1, tm, d), row),
                   pl.BlockSpec((1, tm // tk, d, tk), lambda bi, si: (bi, si, 0, 0)),
                   pl.BlockSpec((1, tm, d), row),
                   pl.BlockSpec((1, d, tm), col),
                   pl.BlockSpec((1, 1, 8, LANES), lambda bi, si: (bi, si, 0, 0))],
        out_shape=[jax.ShapeDtypeStruct((b, d, s), BF16),
                   jax.ShapeDtypeStruct((b, s, d), BF16),
                   jax.ShapeDtypeStruct((b, s // tk, d, tk), BF16),
                   jax.ShapeDtypeStruct((b, s, d), BF16),
                   jax.ShapeDtypeStruct((b, d, s), BF16),
                   jax.ShapeDtypeStruct((b, s // tm, 8, LANES), F32)],
        scratch_shapes=[pltpu.VMEM((1, LANES), F32)],
        compiler_params=pltpu.CompilerParams(
            dimension_semantics=("arbitrary", "arbitrary"), vmem_limit_bytes=VMEM_LIMIT),
        name="qkv_proj",
    )(x, *consts)


BIAS_Q_LANE = 8
MAX_SHIFT = 48.0
SKIP_LOGIT = -200.0


def _attn_kernel(first_ref, *refs, tq, tk):
    *attn_in, w1_ref, w2_ref, o_ref, w1_bf_ref, w2_bf_ref = refs[:10]
    attn_refs = (*attn_in, o_ref, *refs[10:])
    w1_bf_ref[...] = w1_ref[...].astype(BF16)
    w2_bf_ref[...] = w2_ref[...].astype(BF16)

    use_online = first_ref[first_ref.shape[0] - 1]

    @pl.when(use_online == 0)
    def _():
        _attn_body(first_ref, *attn_refs, tq=tq, tk=tk, online=False)

    @pl.when(use_online != 0)
    def _():
        _attn_body(first_ref, *attn_refs, tq=tq, tk=tk, online=True)


def _attn_body(first_ref, qt_ref, qbt_ref, k_ref, kb_ref, vt_ref, o_ref, s0_scr, s1_scr, p0_scr,
               p1_scr, *, tq, tk, online):
    i = pl.program_id(2)
    n_sub = tq // tk
    qt = qt_ref[0]
    qbt = qbt_ref[0]
    row = lax.broadcasted_iota(jnp.int32, (LANES, tq), 0)
    lane_k = lax.broadcasted_iota(jnp.int32, (tk, LANES), 1)
    q_cols = (lane_k >= BIAS_Q_LANE) & (lane_k < BIAS_Q_LANE + HEADS_PER_STEP * BIAS_PIECES)
    causal = (lax.broadcasted_iota(jnp.int32, (tk, tk), 0)
              <= lax.broadcasted_iota(jnp.int32, (tk, tk), 1))
    one = jnp.ones((), BF16)
    zero = jnp.zeros((), BF16)
    heads = range(HEADS_PER_STEP)
    qats = []
    for hh in heads:
        q_h = jnp.where((row >= HEAD_DIM * hh) & (row < HEAD_DIM * (hh + 1)), qt, zero)
        k_side = (row >= BIAS_PIECES * hh) & (row < BIAS_PIECES * (hh + 1))
        q_lo = BIAS_Q_LANE + BIAS_PIECES * hh
        q_side = (row >= q_lo) & (row < q_lo + BIAS_PIECES)
        qats.append(jnp.concatenate(
            [q_h, jnp.where(k_side, one, jnp.where(q_side, qbt, zero))], axis=0))

    def scores(hh, j, c0=0):
        off = pl.multiple_of(j * tk, tk)
        kbias = jnp.where(q_cols, one, kb_ref[0, pl.ds(off, tk), :])
        ka = jnp.concatenate([k_ref[0, pl.ds(off, tk), :], kbias], axis=1)
        return jnp.dot(ka, qats[hh][:, c0:], preferred_element_type=F32)

    if not online:
        j0 = n_sub * i
        def probs(hh, l, j, c0=0, diagonal=None):
            s = scores(hh, j, c0)
            if diagonal is not False:
                head = jnp.where(causal, s[:, :tk], -1e30)
                if diagonal is None:
                    head = jnp.where(j < j0, s[:, :tk], head)
                s = head if s.shape[1] == tk else jnp.concatenate([head, s[:, tk:]], axis=1)
            p = jnp.exp2(s)
            return l + jnp.sum(p, axis=0, keepdims=True), p.astype(BF16)

        def weigh(hh, acc, p, j):
            vt = vt_ref[0, j, HEAD_DIM * hh:HEAD_DIM * (hh + 1), :]
            return acc + jnp.dot(vt, p, preferred_element_type=F32)

        def first_pair(hh):
            head = HEADS_PER_STEP * pl.program_id(1) + hh
            return first_ref[(pl.program_id(0) * N_HEADS + head) * pl.num_programs(2) + i] // 2

        def fill(hh):
            jj0 = first_pair(hh)
            l, p0_scr[hh] = probs(hh, jnp.zeros((1, tq), F32), 2 * jj0)
            return jj0, l

        def off_diagonal(jj0s, ls, static_pairs=None):
            def pairs(jj, cs, n_pairs, hs):
                cs = list(cs)
                for r in range(n_pairs):
                    a = 2 * (jj + r)
                    for hh in hs:
                        l, acc = cs[hh]
                        l, p1_scr[hh] = probs(hh, l, a + 1, diagonal=False)
                        cs[hh] = (l, weigh(hh, acc, p0_scr[hh], a))
                    for hh in hs:
                        l, acc = cs[hh]
                        l, p0_scr[hh] = probs(hh, l, a + 2)
                        cs[hh] = (l, weigh(hh, acc, p1_scr[hh], a + 1))
                return tuple(cs)

            jj_both = functools.reduce(jnp.maximum, jj0s)
            cs = tuple((l, jnp.zeros((HEAD_DIM, tq), F32)) for l in ls)
            if static_pairs is not None:
                return pairs(jj0s[0], cs, static_pairs, heads)
            for hh in heads:
                cs = lax.fori_loop(jj0s[hh], jj_both,
                                   lambda jj, cs, hh=hh: pairs(jj, cs, 1, (hh,)), cs)
            n_long = (j0 // 2 - jj_both) // 2
            cs = lax.fori_loop(0, n_long, lambda t, cs: pairs(jj_both + 2 * t, cs, 2, heads), cs)
            return lax.fori_loop(jj_both + 2 * n_long, j0 // 2,
                                 lambda jj, cs: pairs(jj, cs, 1, heads), cs)

        def diagonal(hh, l, acc):
            later = [probs(hh, l[:, tk * dd:], j0 + dd, tk * dd, diagonal=True)
                     for dd in range(1, n_sub)]
            acc = weigh(hh, acc, p0_scr[hh], j0)
            for dd in range(1, n_sub):
                c0 = tk * dd
                l_part, p = later[dd - 1]
                l = jnp.concatenate([l[:, :c0], l_part], axis=1)
                acc = jnp.concatenate([acc[:, :c0], weigh(hh, acc[:, c0:], p, j0 + dd)], axis=1)
            return acc / l

        def run(static_pairs):
            filled = [fill(hh) for hh in heads]
            cs = off_diagonal([f[0] for f in filled], [f[1] for f in filled], static_pairs)
            outs = [diagonal(hh, *cs[hh]) for hh in heads]
            o_ref[0] = jnp.concatenate(outs, axis=0).T.astype(BF16)

        starts = [first_pair(hh) for hh in heads]
        together = functools.reduce(jnp.logical_and, [s == starts[0] for s in starts])
        n_static = 3
        static = [(j0 // 2 - starts[0] == n) & together for n in range(n_static)]
        for n in range(n_static):
            pl.when(static[n])(functools.partial(run, n))
        pl.when(jnp.logical_not(functools.reduce(jnp.logical_or, static)))(
            functools.partial(run, None))
        return

    def consume(hh, carry, s, j, diagonal):
        if diagonal:
            s_tri = jnp.where(causal, s[:, :tk], -1e30)
            s = s_tri if s.shape[1] == tk else jnp.concatenate([s_tri, s[:, tk:]], axis=1)
        vt = vt_ref[0, j, HEAD_DIM * hh:HEAD_DIM * (hh + 1), :]
        if online:
            m, l, acc = carry
            m_new = jnp.maximum(m, jnp.max(s, axis=0, keepdims=True))
            alpha = jnp.exp2(m - m_new)
            p = jnp.exp2(s - m_new)
            l = alpha * l + jnp.sum(p, axis=0, keepdims=True)
            acc = alpha * acc + jnp.dot(vt, p.astype(BF16), preferred_element_type=F32)
            return m_new, l, acc
        l, acc = carry
        p = jnp.exp2(s)
        l = l + jnp.sum(p, axis=0, keepdims=True)
        acc = acc + jnp.dot(vt, p.astype(BF16), preferred_element_type=F32)
        return l, acc

    init = (jnp.zeros((1, tq), F32), jnp.zeros((HEAD_DIM, tq), F32))
    if online:
        init = (jnp.full((1, tq), -1e30, F32),) + init

    j0 = n_sub * i

    def first_scores(hh):
        if online:
            jj0 = 0
        else:
            head = HEADS_PER_STEP * pl.program_id(1) + hh
            jj0 = first_ref[(pl.program_id(0) * N_HEADS + head) * pl.num_programs(2) + i] // 2
        s0_scr[hh] = scores(hh, 2 * jj0)
        return jj0

    def off_diagonal(hh, jj0):
        def pairs(jj, c, n_pairs):
            for r in range(n_pairs):
                s1_scr[hh] = scores(hh, 2 * (jj + r) + 1)
                c = consume(hh, c, s0_scr[hh], 2 * (jj + r), False)
                s0_scr[hh] = scores(hh, 2 * (jj + r) + 2)
                c = consume(hh, c, s1_scr[hh], 2 * (jj + r) + 1, False)
            return c

        n_long = (j0 // 2 - jj0) // 2
        c = lax.fori_loop(0, n_long, lambda t, c: pairs(jj0 + 2 * t, c, 2), init)
        return lax.fori_loop(jj0 + 2 * n_long, j0 // 2, lambda jj, c: pairs(jj, c, 1), c)

    def diagonal(hh, carry):
        later = [scores(hh, j0 + dd, tk * dd) for dd in range(1, n_sub)]
        carry = consume(hh, carry, s0_scr[hh], j0, True)
        for dd in range(1, n_sub):
            c0 = tk * dd
            part = consume(hh, tuple(c[:, c0:] for c in carry), later[dd - 1], j0 + dd, True)
            carry = tuple(jnp.concatenate([c[:, :c0], pc], axis=1) for c, pc in zip(carry, part))
        return carry[-1] / carry[-2]

    outs = []
    jj0 = first_scores(0)
    for hh in heads:
        carry = off_diagonal(hh, jj0)
        if hh + 1 < HEADS_PER_STEP:
            jj0 = first_scores(hh + 1)
        outs.append(diagonal(hh, carry))
    o_t = jnp.concatenate(outs, axis=0)
    o_ref[0] = o_t.T.astype(BF16)


def _attention(first, qt, qbt, k, kb, vt, w1, w2, *, tq, tk):
    b, s, d = k.shape
    assert (tq // tk) % 2 == 0 and s % tq == 0
    n_pairs = d // LANES
    nq = s // tq
    nk = s // tk
    n_steps = b * n_pairs * nq
    qt_blk = pl.BlockSpec((1, LANES, tq), lambda bi, hp, i, first: (bi, hp, i))
    k_all = pl.BlockSpec((1, s, LANES), lambda bi, hp, i, first: (bi, 0, hp))
    step = lambda bi, hp, i, first: ((bi * n_pairs + hp) * nq + i, 0)

    def slice_spec(w):
        assert w.shape[0] % (16 * n_steps) == 0
        return pl.BlockSpec((w.shape[0] // n_steps, w.shape[1]), step)

    return pl.pallas_call(
        functools.partial(_attn_kernel, tq=tq, tk=tk),
        grid_spec=pltpu.PrefetchScalarGridSpec(
            num_scalar_prefetch=1,
            grid=(b, n_pairs, nq),
            in_specs=[qt_blk, qt_blk, k_all, k_all,
                      pl.BlockSpec((1, nk, LANES, tk), lambda bi, hp, i, first: (bi, 0, hp, 0)),
                      slice_spec(w1), slice_spec(w2)],
            out_specs=[pl.BlockSpec((1, tq, LANES), lambda bi, hp, i, first: (bi, i, hp)),
                       slice_spec(w1), slice_spec(w2)],
            scratch_shapes=[pltpu.VMEM((HEADS_PER_STEP, tk, tq), F32),
                            pltpu.VMEM((HEADS_PER_STEP, tk, tq), F32),
                            pltpu.VMEM((HEADS_PER_STEP, tk, tq), BF16),
                            pltpu.VMEM((HEADS_PER_STEP, tk, tq), BF16)]),
        out_shape=[jax.ShapeDtypeStruct((b, s, d), BF16),
                   jax.ShapeDtypeStruct(w1.shape, BF16), jax.ShapeDtypeStruct(w2.shape, BF16)],
        compiler_params=pltpu.CompilerParams(
            dimension_semantics=("arbitrary", "arbitrary", "arbitrary"),
            vmem_limit_bytes=VMEM_LIMIT),
        name="fox_attention",
    )(first, qt, qbt, k, kb, vt, w1, w2)


def _permute_token_blocks(perm, a):
    w = perm.shape[0]
    return jnp.concatenate(
        [jnp.dot(perm, a[w * i:w * (i + 1)], preferred_element_type=F32).astype(BF16)
         for i in range(a.shape[0] // w)], axis=0)


def _mix_mlp_kernel(x_ref, a_ref, wmix_ref, g_ref, w1_ref, w2_ref, *rest, glu, d, ff_chunk):
    if glu:
        a = _permute_token_blocks(rest[0][...], a_ref[...])
        mix = jnp.dot(a, wmix_ref[...], preferred_element_type=F32)
        mix = mix[:, :d] * jax.nn.sigmoid(mix[:, d:])
    else:
        mix = jnp.dot(a_ref[...], wmix_ref[...], preferred_element_type=F32)
    x1 = x_ref[...] + mix
    h = _rms_norm(x1, g_ref[...]).astype(BF16)
    acc = x1
    for c in range(w1_ref.shape[1] // ff_chunk):
        sl = slice(ff_chunk * c, ff_chunk * (c + 1))
        hid = jnp.maximum(jnp.dot(h, w1_ref[:, sl], preferred_element_type=F32), 0.0)
        acc = acc + jnp.dot((hid * hid).astype(BF16), w2_ref[sl, :], preferred_element_type=F32)
    if glu:
        _, o_ref = rest
    else:
        perm_ref, g_next_ref, w_next_ref, o_ref, u_ref = rest
        h_next = _rms_norm(acc, g_next_ref[...]).astype(BF16)
        u = jnp.dot(h_next, w_next_ref[...], preferred_element_type=F32).astype(BF16)
        u_ref[...] = _permute_token_blocks(perm_ref[...], u)
    o_ref[...] = acc


def _mix_mlp(x, a, wmix, g, w1, w2, perm, next_proj=None, *, layer, glu, tm, ff_chunk=1024):
    t, d = x.shape
    row = lambda i: (i, 0)
    single = pl.Buffered(1)
    wspec = lambda shape: pl.BlockSpec(shape, lambda i: (0, 0), pipeline_mode=single)
    lspec = lambda w: pl.BlockSpec((None,) + w.shape[1:], lambda i: (layer, 0, 0),
                                   pipeline_mode=single)
    operands = [x, a, wmix, g, w1, w2, perm]
    in_specs = [pl.BlockSpec((tm, d), row), pl.BlockSpec((tm, d), row),
                wspec(wmix.shape), wspec((1, d)), lspec(w1), lspec(w2), wspec(perm.shape)]
    out_specs = pl.BlockSpec((tm, d), row)
    out_shape = jax.ShapeDtypeStruct((t, d), F32)
    if next_proj is not None:
        operands += list(next_proj)
        in_specs += [wspec(w.shape) for w in next_proj]
        n = next_proj[1].shape[1]
        out_specs = [out_specs, pl.BlockSpec((tm, n), row)]
        out_shape = [out_shape, jax.ShapeDtypeStruct((t, n), BF16)]
    return pl.pallas_call(
        functools.partial(_mix_mlp_kernel, glu=glu, d=d, ff_chunk=ff_chunk),
        grid=(t // tm,),
        in_specs=in_specs,
        out_specs=out_specs,
        out_shape=out_shape,
        compiler_params=pltpu.CompilerParams(
            dimension_semantics=("arbitrary",), vmem_limit_bytes=VMEM_LIMIT),
        name="mix_glu_mlp" if glu else "mix_mlp",
    )(*operands)


GROUPS_PER_SLAB = LANES // SSM_GROUP


def _piece_transpose(arrs, piece):
    arrs = list(arrs)
    dist = GROUPS_PER_SLAB // 2
    while dist >= 1:
        keep = (piece & dist) == 0
        shift = SSM_GROUP * dist
        for i in range(GROUPS_PER_SLAB):
            if i & dist:
                continue
            a, b = arrs[i], arrs[i + dist]
            arrs[i] = jnp.where(keep, a, pltpu.roll(b, shift, axis=1))
            arrs[i + dist] = jnp.where(keep, pltpu.roll(a, LANES - shift, axis=1), b)
        dist //= 2
    return arrs


def _ssm_kernel(u_ref, cc_ref, bt_ref, pw1_ref, pw2_ref, zoh_ref, a1_ref, a2_ref,
                dv_ref, z_ref, ug0_scr, ug1_scr, zg0_scr, zg1_scr, *, n_chunks, n_slabs):
    ug_scr = (ug0_scr, ug1_scr)
    zg_scr = (zg0_scr, zg1_scr)
    L = SSM_CHUNK
    rows = u_ref.shape[0] // L
    w = L * SSM_GROUP
    half = SSM_STATE
    nb = math.gcd(rows // L, 8)
    piece = lax.broadcasted_iota(jnp.int32, (nb * L, LANES), 1) // SSM_GROUP
    step_id = pl.program_id(0)

    def relayout_in(cur, rb):
        t0 = pl.multiple_of(rb * nb * w, nb * w)
        r0 = pl.multiple_of(rb * nb * L, nb * L)
        by_pos = u_ref[pl.ds(t0, nb * w), :].astype(F32)
        for hf in range(L // GROUPS_PER_SLAB):
            arrs = []
            for k in range(GROUPS_PER_SLAB):
                pos = GROUPS_PER_SLAB * hf + k
                arrs.append(jnp.concatenate(
                    [by_pos[w * blk + L * pos:w * blk + L * (pos + 1)] for blk in range(nb)],
                    axis=0))
            arrs = _piece_transpose(arrs, piece)
            for g in range(GROUPS_PER_SLAB):
                ug_scr[cur][g, pl.ds(r0, nb * L), LANES * hf:LANES * (hf + 1)] = (
                    arrs[g].astype(BF16))

    chunk = lax.broadcasted_iota(jnp.int32, (rows, LANES), 0) % n_chunks
    lane_w = lax.broadcasted_iota(jnp.int32, (SSM_GROUP, w), 1)
    low = lax.broadcasted_iota(jnp.int32, (1, LANES), 1) < half
    sign = jnp.where(low, 1.0, -1.0)

    def operators(g):
        cc = cc_ref[g]
        bt = bt_ref[g]
        pw1 = pw1_ref[g]
        pw2 = pw2_ref[g]
        zoh = zoh_ref[g]
        bbar = bt * zoh[0:1] + pltpu.roll(bt, half, axis=1) * zoh[1:2]
        bbar_sw = pltpu.roll(bbar, half, axis=1)
        cc_sw = pltpu.roll(cc, half, axis=1)
        ca = [cc * pw1[k:k + 1] + cc_sw * pw2[k:k + 1] for k in range(L + 1)]
        k_all = lax.dot_general(bbar * sign, jnp.concatenate(ca[:L], axis=0), NT_DIMS,
                                precision=lax.Precision.HIGHEST, preferred_element_type=F32)
        m_intra, w_state, w_out_t = [], [], []
        for s in range(L):
            blk = k_all if s == 0 else jnp.where(
                lane_w >= SSM_GROUP * s, pltpu.roll(k_all, SSM_GROUP * s, axis=1), 0.0)
            m_intra.append(blk.astype(BF16))
            k = L - 1 - s
            w_state.append(bbar * pw1[k:k + 1] + bbar_sw * pw2[k:k + 1])
            w_out_t.append(ca[s + 1] * sign)
        return jnp.concatenate(m_intra, axis=0), w_state, w_out_t

    def halves(pieces, second):
        firsts, seconds = [], []
        for piece in pieces:
            swapped = pltpu.roll(piece, half, axis=1)
            if second:
                firsts.append(jnp.where(low, 0.0, swapped))
                seconds.append(jnp.where(low, 0.0, piece))
            else:
                firsts.append(jnp.where(low, piece, 0.0))
                seconds.append(jnp.where(low, swapped, 0.0))
        return firsts, seconds

    def group_pair(cur, pr):
        prev = 1 - cur
        gs = (2 * pr, 2 * pr + 1)
        m_intra, ws_re, ws_im, wo_t = [], [], [], []
        for second, g in enumerate(gs):
            mi, w_state, w_out_t = operators(g)
            m_intra.append(mi)
            re, im = halves(w_state, second)
            ws_re += re
            ws_im += im
            re, im = halves(w_out_t, second)
            wo_t.append(jnp.concatenate(
                [jnp.concatenate([r, m], axis=1) for r, m in zip(re, im)], axis=0).astype(BF16))
        us = [ug_scr[prev][g] for g in gs]
        uu = jnp.concatenate(us, axis=1)
        xr = jnp.dot(uu, jnp.concatenate(ws_re, axis=0).astype(BF16), preferred_element_type=F32)
        xi = jnp.dot(uu, jnp.concatenate(ws_im, axis=0).astype(BF16), preferred_element_type=F32)
        a1 = [a1_ref[g] for g in gs]
        a2 = [a2_ref[g] for g in gs]
        step = 1
        j = 0
        while step < n_chunks:
            ar = jnp.where(low, a1[0][j:j + 1], a1[1][j:j + 1])
            ai = jnp.where(low, -a2[0][j:j + 1], a2[1][j:j + 1])
            sr = jnp.where(chunk >= step, pltpu.roll(xr, step, axis=0), 0.0)
            si = jnp.where(chunk >= step, pltpu.roll(xi, step, axis=0), 0.0)
            xr = xr + sr * ar - si * ai
            xi = xi + si * ar + sr * ai
            step *= 2
            j += 1
        x_in = jnp.concatenate([jnp.where(chunk >= 1, pltpu.roll(x, 1, axis=0), 0.0)
                                for x in (xr, xi)], axis=1).astype(BF16)
        for second, g in enumerate(gs):
            y = jnp.dot(us[second], m_intra[second], preferred_element_type=F32)
            y = y + lax.dot_general(x_in, wo_t[second], NT_DIMS, preferred_element_type=F32)
            y = y + us[second].astype(F32) * dv_ref[g]
            zg_scr[prev][g] = jax.nn.gelu(y).astype(BF16)

    def relayout_out(cur, rb):
        t0 = pl.multiple_of(rb * nb * w, nb * w)
        r0 = pl.multiple_of(rb * nb * L, nb * L)
        by_pos = []
        for hf in range(L // GROUPS_PER_SLAB):
            arrs = [zg_scr[cur][g, pl.ds(r0, nb * L), LANES * hf:LANES * (hf + 1)].astype(F32)
                    for g in range(GROUPS_PER_SLAB)]
            by_pos += _piece_transpose(arrs, piece)
        for blk in range(nb):
            z_ref[pl.ds(t0 + w * blk, w), :] = jnp.concatenate(
                [arr[L * blk:L * (blk + 1)] for arr in by_pos], axis=0).astype(BF16)

    def loop(n, *stages):
        def body(i, carry):
            for stage, per_trip in stages:
                for r in range(per_trip):
                    stage(per_trip * i + r)
            return carry
        lax.fori_loop(0, n, body, 0)

    n_rb = rows // (nb * L)
    n_pr = GROUPS_PER_SLAB // 2

    @pl.when(step_id == 0)
    def _():
        zg_scr[1][...] = jnp.zeros(zg_scr[1].shape, BF16)
        loop(n_rb, (functools.partial(relayout_in, 0), 1))

    def inner(cur):
        stage_in = functools.partial(relayout_in, cur)
        stage_scan = functools.partial(group_pair, cur)
        stage_out = functools.partial(relayout_out, cur)
        if n_rb % n_pr == 0:
            per = n_rb // n_pr
            loop(n_pr, (stage_scan, 1), (stage_in, per), (stage_out, per))
        else:
            loop(n_rb, (stage_in, 1))
            loop(n_pr, (stage_scan, 1))
            loop(n_rb, (stage_out, 1))

    for cur in range(2):
        pl.when((step_id > 0) & (step_id <= n_slabs) & (step_id % 2 == cur))(
            functools.partial(inner, cur))

    @pl.when(step_id == n_slabs + 1)
    def _():
        loop(n_rb, (functools.partial(relayout_out, (n_slabs + 1) % 2), 1))


def _ssm_scan(u, cc, bt, pw1, pw2, zoh, a1, a2, dv, *, n_chunks):
    t, d = u.shape
    L = SSM_CHUNK
    rows = t // L
    w = L * SSM_GROUP
    gps = GROUPS_PER_SLAB
    n_slabs = d // LANES
    clamp = lambda j: jnp.clip(j, 0, n_slabs - 1)
    pspec = lambda arr: pl.BlockSpec((gps,) + arr.shape[1:], lambda j: (clamp(j - 1), 0, 0))
    return pl.pallas_call(
        functools.partial(_ssm_kernel, n_chunks=n_chunks, n_slabs=n_slabs),
        grid=(n_slabs + 2,),
        in_specs=[pl.BlockSpec((t, LANES), lambda j: (0, clamp(j))),
                  pspec(cc), pspec(bt), pspec(pw1), pspec(pw2), pspec(zoh), pspec(a1), pspec(a2),
                  pspec(dv)],
        out_specs=pl.BlockSpec((t, LANES), lambda j: (0, clamp(j - 2))),
        out_shape=jax.ShapeDtypeStruct((t, d), BF16),
        scratch_shapes=[pltpu.VMEM((gps, rows, w), BF16)] * 4,
        compiler_params=pltpu.CompilerParams(
            dimension_semantics=("arbitrary",), vmem_limit_bytes=VMEM_LIMIT),
        name="s5_scan",
    )(u, cc, bt, pw1, pw2, zoh, a1, a2, dv)


def _ssm_operators(a_re, a_im, b_re, b_im, c_re, c_im, log_dt, d_skip, n_chunks):
    L = SSM_CHUNK
    g, p = a_re.shape
    dt = jnp.exp(log_dt)[:, None]
    lam_re, lam_im = dt * a_re, dt * a_im

    def powers(ks):
        ks = jnp.asarray(ks, F32)[:, None, None]
        mag = jnp.exp(ks * lam_re)
        return mag * jnp.cos(ks * lam_im), mag * jnp.sin(ks * lam_im)

    def patterns(re, im):
        return (jnp.concatenate([re, re], axis=2).transpose(1, 0, 2),
                jnp.concatenate([-im, im], axis=2).transpose(1, 0, 2))

    pr, pi = powers(range(L + 1))
    num_re, num_im = pr[1] - 1.0, pi[1]
    den = a_re * a_re + a_im * a_im
    s_re = (num_re * a_re + num_im * a_im) / den
    s_im = (num_im * a_re - num_re * a_im) / den
    pw1, pw2 = patterns(pr, pi)
    zoh = jnp.concatenate(patterns(s_re[None], s_im[None]), axis=1)
    n_steps = max(1, int(math.log2(n_chunks)))
    a1, a2 = patterns(*powers([L * 2 ** j for j in range(n_steps)]))
    cc = jnp.concatenate([c_re, c_im], axis=2)
    bt = jnp.concatenate([b_re.transpose(0, 2, 1), b_im.transpose(0, 2, 1)], axis=2)
    dv = jnp.tile(d_skip.reshape(g, 1, SSM_GROUP), (1, L, 1)).reshape(g, 1, L * SSM_GROUP)
    return cc, bt, pw1, pw2, zoh, a1, a2, dv


def kernel(x, norm_mix_g, norm_mlp_g, attn_w_in, attn_b_f, attn_q_g, attn_k_g, attn_w_out,
           ssm_w_in, ssm_a_re, ssm_a_im, ssm_b_re, ssm_b_im, ssm_c_re, ssm_c_im, ssm_log_dt,
           ssm_d, ssm_w_glu, mlp_w1, mlp_w2):
    b, s, d = x.shape
    t = b * s
    tk = min(512, s // 2)
    tq = 2 * tk
    tm_proj = min(512, s)
    tm_mlp = min(512, t)

    w_in = attn_w_in[0]
    w_f = jnp.pad(w_in[:, 3 * d:], ((0, 0), (0, LANES - N_HEADS)))
    wkf = jnp.concatenate([w_in[:, d:2 * d], w_f], axis=1).astype(BF16)
    wqvt = jnp.concatenate([w_in[:, :d], w_in[:, 2 * d:3 * d]], axis=1).T.astype(BF16)
    blk = np.arange(MXU_DIM) // HEAD_DIM
    bd = jnp.asarray((blk[:, None] == blk[None, :]) * (1.0 / HEAD_DIM), BF16)
    gain_k = jnp.tile(attn_k_g[0], N_HEADS).reshape(1, d)
    gain_qt = jnp.broadcast_to(
        (jnp.tile(attn_q_g[0], N_HEADS) * (LOG2E / math.sqrt(HEAD_DIM)))[:, None], (d, LANES))
    bfp = jnp.pad(attn_b_f[0], (0, LANES - N_HEADS)).reshape(1, LANES)
    tri = jnp.asarray(np.arange(tm_proj)[:, None] >= np.arange(tm_proj)[None, :], BF16)
    heads = np.arange(N_HEADS)
    place_k = np.zeros((LANES, d), np.float32)
    for i in range(BIAS_PIECES):
        place_k[N_HEADS * i + heads, (heads // HEADS_PER_STEP) * LANES
                + BIAS_PIECES * (heads % HEADS_PER_STEP) + i] = 1.0
    place_qt = np.roll(place_k, BIAS_Q_LANE, axis=1).T
    shift = (LOG2E * math.sqrt(HEAD_DIM) * jnp.max(jnp.abs(attn_q_g[0]))
             * jnp.max(jnp.abs(attn_k_g[0])))
    qt, k, vt, kb, qbt, edge = _qkv_proj(
        x, norm_mix_g[0].reshape(1, d), wkf, wqvt, bd, gain_k, gain_qt, bfp, tri,
        jnp.asarray(place_k, BF16), jnp.asarray(place_qt, BF16), jnp.full((1, LANES), shift, F32),
        tm=tm_proj, tk=tk)
    assert tm_proj == tk
    c_first = edge[:, ::tq // tk, 0, :N_HEADS]
    c_last = edge[:, :, 1, :N_HEADS]
    bound = (c_first[:, :, None, :] - c_last[:, None, :, :]) * LOG2E
    key_blk = jnp.arange(s // tk)[None, None, :, None]
    first = jnp.min(jnp.where(bound < SKIP_LOGIT, s // tk, key_blk), axis=2)
    first = jnp.minimum(first, (tq // tk) * jnp.arange(s // tq)[None, :, None]) // 2 * 2
    first = jnp.concatenate([first.transpose(0, 2, 1).reshape(-1), (shift > MAX_SHIFT)[None]])
    dff = mlp_w1.shape[2]
    o, w1, w2 = _attention(first.astype(jnp.int32), qt, qbt, k, kb, vt,
                           mlp_w1.reshape(-1, dff), mlp_w2.reshape(-1, d), tq=tq, tk=tk)
    w1 = w1.reshape(mlp_w1.shape)
    w2 = w2.reshape(mlp_w2.shape)
    tok = np.arange(SSM_CHUNK * SSM_CHUNK)
    perm = jnp.asarray(
        tok[None, :] == (tok[:, None] % SSM_CHUNK) * SSM_CHUNK + tok[:, None] // SSM_CHUNK, BF16)
    x2, u = _mix_mlp(x.reshape(t, d), o.reshape(t, d), attn_w_out[0].astype(BF16),
                     norm_mlp_g[0].reshape(1, d), w1, w2,
                     perm, (norm_mix_g[1].reshape(1, d), ssm_w_in[0].astype(BF16)),
                     layer=0, glu=False, tm=tm_mlp)

    n_chunks = s // SSM_CHUNK
    ops = _ssm_operators(ssm_a_re[0], ssm_a_im[0], ssm_b_re[0], ssm_b_im[0], ssm_c_re[0],
                         ssm_c_im[0], ssm_log_dt[0], ssm_d[0], n_chunks)
    z = _ssm_scan(u, *ops, n_chunks=n_chunks)
    x3 = _mix_mlp(x2, z, ssm_w_glu[0].astype(BF16), norm_mlp_g[1].reshape(1, d),
                  w1, w2, perm, layer=1, glu=True, tm=tm_mlp)
    return x3.reshape(b, s, d)
```
